```python
import math
import jax, jax.numpy as jnp
from jax import lax
import numpy as np

D_MODEL = 1024
BATCH = 16
SEQ = 256
DEPTH = 1
DEC_BATCH = 4
DEC_SEQ = 2048
PAST_LEN = 512

GRID_W = 64
N_HEADS = 8
HEAD_DIM = D_MODEL // N_HEADS
D_MLSTM = N_HEADS * HEAD_DIM
CHUNK = 64
N_FGROUPS = 4
FGROUP_DIM = D_MODEL // 8
D_FOURIER = N_FGROUPS * FGROUP_DIM
N_GATES = 4 * N_HEADS
N_IN = 4 * D_MLSTM + N_GATES + D_FOURIER + 2 * D_MODEL
N_GROUPS = 4
EXPERTS_PER_GROUP = 8
N_EXPERTS = N_GROUPS * EXPERTS_PER_GROUP
D_EXPERT = D_MODEL // 4
TOP_K_INNER = 2
POS_BASE = 10000.0
EPS = 1e-6

kernel_name = 'hybrid_mlstm_fnet_hmoe_diffusion_step'


def rmsnorm(x, g):
    xf = x.astype(jnp.float32)
    y = xf * lax.rsqrt(jnp.mean(xf * xf, axis=-1, keepdims=True) + EPS)
    return (y * g.astype(jnp.float32)).astype(x.dtype)


def grid_pos(n_tokens, dtype):
    rows = n_tokens // GRID_W
    row = jnp.repeat(jnp.arange(rows, dtype=jnp.float32), GRID_W)
    col = jnp.tile(jnp.arange(GRID_W, dtype=jnp.float32), rows)
    n_freq = D_MODEL // 4
    freq = jnp.exp(-math.log(POS_BASE) * jnp.arange(n_freq, dtype=jnp.float32) / n_freq)

    def enc(p):
        a = p[:, None] * freq[None, :]
        return jnp.concatenate([jnp.sin(a), jnp.cos(a)], axis=-1)

    return jnp.concatenate([enc(row), enc(col)], axis=-1).astype(dtype)


def mlstm_chunkwise(q, k, v, ig, lf, C0, n0, m0):
    n_chunks = q.shape[-2] // CHUNK

    def chunk_vec(t):
        t = t.reshape(t.shape[:-2] + (n_chunks, CHUNK, t.shape[-1]))
        return jnp.moveaxis(t, -3, 0)

    def chunk_gate(t):
        t = t.reshape(t.shape[:-1] + (n_chunks, CHUNK))
        return jnp.moveaxis(t, -2, 0)

    causal = jnp.tril(jnp.ones((CHUNK, CHUNK), dtype=bool))

    def step(carry, inp):
        C, n, m = carry
        qc, kc, vc, ic, fc = inp
        b = jnp.cumsum(fc, axis=-1)
        d_log = jnp.where(causal, b[..., :, None] - b[..., None, :] + ic[..., None, :], -jnp.inf)
        inter = b + m[..., None]
        m_t = jnp.maximum(inter, jnp.max(d_log, axis=-1))
        s = jnp.einsum('...td,...sd->...ts', qc, kc) * jnp.exp(d_log - m_t[..., None])
        decay = jnp.exp(inter - m_t)
        num = decay[..., None] * jnp.einsum('...td,...de->...te', qc, C) + jnp.einsum('...ts,...se->...te', s, vc)
        den = decay * jnp.einsum('...td,...d->...t', qc, n) + jnp.sum(s, axis=-1)
        h = num / jnp.maximum(jnp.abs(den), jnp.exp(-m_t))[..., None]
        b_end = b[..., -1]
        w_log = b_end[..., None] - b + ic
        m_new = jnp.maximum(b_end + m, jnp.max(w_log, axis=-1))
        a = jnp.exp(b_end + m - m_new)
        w = jnp.exp(w_log - m_new[..., None])
        C_new = a[..., None, None] * C + jnp.einsum('...s,...sd,...se->...de', w, kc, vc)
        n_new = a[..., None] * n + jnp.einsum('...s,...sd->...d', w, kc)
        return (C_new, n_new, m_new), h

    xs = (chunk_vec(q), chunk_vec(k), chunk_vec(v), chunk_gate(ig), chunk_gate(lf))
    (C, n, m), h = lax.scan(step, (C0, n0, m0), xs)
    h = jnp.moveaxis(h, 0, -3)
    return h.reshape(h.shape[:-3] + (n_chunks * CHUNK, h.shape[-1])), (C, n, m)


def token_mixer(h, w_in, b_gates, g_hn, w_a, w_f, w_out, C0, n0, m0):
    B, T, _ = h.shape
    f32 = jnp.float32
    z = h @ w_in
    cuts = np.cumsum([D_MLSTM] * 4 + [N_GATES, D_FOURIER, D_MODEL]).tolist()
    q, k, v, o, gts, u, ga, gb = jnp.split(z, cuts, axis=-1)

    def heads(t):
        return t.reshape(B, T, N_HEADS, HEAD_DIM).transpose(0, 2, 1, 3).astype(f32)

    qh, kh, vh = heads(q), heads(k) * (HEAD_DIM ** -0.5), heads(v)
    gts = (gts.astype(f32) + b_gates.astype(f32)).reshape(B, T, 4, N_HEADS).transpose(2, 0, 3, 1)
    i_f, f_f, i_b, f_b = gts[0], gts[1], gts[2], gts[3]

    def both(t):
        return jnp.stack([t, jnp.flip(t, axis=2)])

    ig = jnp.stack([i_f, jnp.flip(i_b, axis=-1)])
    lf = jax.nn.log_sigmoid(jnp.stack([f_f, jnp.flip(f_b, axis=-1)]))
    hd, state = mlstm_chunkwise(both(qh), both(kh), both(vh), ig, lf, C0, n0, m0)
    ha = hd[0] + jnp.flip(hd[1], axis=2)
    ha = ha * lax.rsqrt(jnp.mean(ha * ha, axis=-1, keepdims=True) + EPS)
    ha = ha.transpose(0, 2, 1, 3).reshape(B, T, D_MLSTM) * g_hn.astype(f32) * jax.nn.sigmoid(o.astype(f32))
    ya = ha.astype(h.dtype) @ w_a

    uf = u.astype(f32).reshape(B, T, N_FGROUPS, FGROUP_DIM)
    yf = jnp.real(jnp.fft.fft2(uf, axes=(1, 3), norm='ortho')).reshape(B, T, D_FOURIER)
    yf = yf.astype(h.dtype) @ w_f

    mix = (jax.nn.sigmoid(ga) * ya + jax.nn.sigmoid(gb) * yf) @ w_out
    return mix, state


def hier_moe(h, w_rg, b_rg, w_re, b_re, w_e1, w_e3, w_e2):
    B, T, _ = h.shape
    f32 = jnp.float32
    lg = (h @ w_rg).astype(f32) + b_rg.astype(f32)
    pg = jax.nn.softmax(lg, axis=-1)
    gsel = jnp.argmax(lg, axis=-1)
    psel = jnp.max(pg, axis=-1)
    le = ((h @ w_re).astype(f32) + b_re.astype(f32)).reshape(B, T, N_GROUPS, EXPERTS_PER_GROUP)
    le = jnp.einsum('btge,btg->bte', le, jax.nn.one_hot(gsel, N_GROUPS, dtype=f32))
    vals, idx = lax.top_k(le, TOP_K_INNER)
    wi = jax.nn.softmax(vals, axis=-1) * psel[..., None]
    eidx = gsel[..., None] * EXPERTS_PER_GROUP + idx
    gates = jnp.sum(jax.nn.one_hot(eidx, N_EXPERTS, dtype=f32) * wi[..., None], axis=-2)
    hid = jax.nn.silu(h @ w_e1) * (h @ w_e3)
    hid = (hid.reshape(B, T, N_EXPERTS, D_EXPERT) * gates[..., None].astype(hid.dtype)).reshape(B, T, N_EXPERTS * D_EXPERT)
    return hid @ w_e2


def block(x, mod, lp, C0, n0, m0):
    (g_pre1, w_in, b_gates, g_hn, w_a, w_f, w_out, g_post1,
     g_pre2, w_rg, b_rg, w_re, b_re, w_e1, w_e3, w_e2, g_post2) = lp
    sh1, sc1, gt1, sh2, sc2, gt2 = jnp.split(mod[:, None, :], 6, axis=-1)
    h = rmsnorm(x, g_pre1) * (1 + sc1) + sh1
    mix, state = token_mixer(h, w_in, b_gates, g_hn, w_a, w_f, w_out, C0, n0, m0)
    x = x + gt1 * rmsnorm(mix, g_post1)
    h = rmsnorm(x, g_pre2) * (1 + sc2) + sh2
    x = x + gt2 * rmsnorm(hier_moe(h, w_rg, b_rg, w_re, b_re, w_e1, w_e3, w_e2), g_post2)
    return x, state


def setup_inputs(seed: int = 0) -> dict:
    key = jax.random.key(seed)
    ks = jax.random.split(key, 32)
    f32 = jnp.float32
    nrm = lambda k, s: jax.random.normal(k, s, dtype=f32)
    base_gate = jnp.stack([jnp.zeros((N_HEADS,), f32), jnp.linspace(3.0, 6.0, N_HEADS),
                           jnp.zeros((N_HEADS,), f32), jnp.linspace(3.0, 6.0, N_HEADS)])
    b_gates = (base_gate[None] + 0.1 * nrm(ks[11], (DEPTH, 4, N_HEADS))).reshape(DEPTH, N_GATES)
    return {
        'x_prompt': nrm(ks[0], (BATCH, SEQ, D_MODEL)),
        'x_sample': nrm(ks[1], (DEC_BATCH, DEC_SEQ, D_MODEL)),
        'c': nrm(ks[2], (DEC_BATCH, D_MODEL)),
        'state_C': 0.1 * nrm(ks[3], (DEC_BATCH, DEPTH, 2, N_HEADS, HEAD_DIM, HEAD_DIM)),
        'state_n': 0.1 * nrm(ks[4], (DEC_BATCH, DEPTH, 2, N_HEADS, HEAD_DIM)),
        'state_m': nrm(ks[5], (DEC_BATCH, DEPTH, 2, N_HEADS)),
        'c_ctx': nrm(ks[6], (D_MODEL,)),
        'w_ada': 0.5 * D_MODEL ** -0.5 * nrm(ks[7], (DEPTH, D_MODEL, 6 * D_MODEL)),
        'b_ada': 0.02 * nrm(ks[8], (DEPTH, 6 * D_MODEL)),
        'g_pre1': 1.0 + 0.05 * nrm(ks[9], (DEPTH, D_MODEL)),
        'w_in': D_MODEL ** -0.5 * nrm(ks[10], (DEPTH, D_MODEL, N_IN)),
        'b_gates': b_gates,
        'g_hn': 1.0 + 0.05 * nrm(ks[12], (DEPTH, D_MLSTM)),
        'w_a': D_MLSTM ** -0.5 * nrm(ks[13], (DEPTH, D_MLSTM, D_MODEL)),
        'w_f': D_FOURIER ** -0.5 * nrm(ks[14], (DEPTH, D_FOURIER, D_MODEL)),
        'w_out': D_MODEL ** -0.5 * nrm(ks[15], (DEPTH, D_MODEL, D_MODEL)),
        'g_post1': 1.0 + 0.05 * nrm(ks[16], (DEPTH, D_MODEL)),
        'g_pre2': 1.0 + 0.05 * nrm(ks[17], (DEPTH, D_MODEL)),
        'w_rg': D_MODEL ** -0.5 * nrm(ks[18], (DEPTH, D_MODEL, N_GROUPS)),
        'b_rg': 0.01 * nrm(ks[19], (DEPTH, N_GROUPS)),
        'w_re': D_MODEL ** -0.5 * nrm(ks[20], (DEPTH, D_MODEL, N_EXPERTS)),
        'b_re': 0.01 * nrm(ks[21], (DEPTH, N_EXPERTS)),
        'w_e1': D_MODEL ** -0.5 * nrm(ks[22], (DEPTH, D_MODEL, N_EXPERTS * D_EXPERT)),
        'w_e3': D_MODEL ** -0.5 * nrm(ks[23], (DEPTH, D_MODEL, N_EXPERTS * D_EXPERT)),
        'w_e2': D_EXPERT ** -0.5 * nrm(ks[24], (DEPTH, N_EXPERTS * D_EXPERT, D_MODEL)),
        'g_post2': 1.0 + 0.05 * nrm(ks[25], (DEPTH, D_MODEL)),
    }


def reference(x_prompt, x_sample, c, state_C, state_n, state_m, c_ctx, w_ada, b_ada,
              g_pre1, w_in, b_gates, g_hn, w_a, w_f, w_out, g_post1, g_pre2,
              w_rg, b_rg, w_re, b_re, w_e1, w_e3, w_e2, g_post2):
    f32 = jnp.float32
    xp = x_prompt
    xs = x_sample + grid_pos(x_sample.shape[1], x_sample.dtype)[None]
    bp = xp.shape[0]
    new_C, new_n, new_m = [], [], []
    for l in range(DEPTH):
        lp = (g_pre1[l], w_in[l], b_gates[l], g_hn[l], w_a[l], w_f[l], w_out[l], g_post1[l],
              g_pre2[l], w_rg[l], b_rg[l], w_re[l], b_re[l], w_e1[l], w_e3[l], w_e2[l], g_post2[l])
        mod_ctx = jax.nn.silu(c_ctx)[None, :] @ w_ada[l] + b_ada[l]
        C0 = jnp.zeros((2, bp, N_HEADS, HEAD_DIM, HEAD_DIM), f32)
        n0 = jnp.zeros((2, bp, N_HEADS, HEAD_DIM), f32)
        m0 = jnp.zeros((2, bp, N_HEADS), f32)
        xp, (Cc, nc, mc) = block(xp, mod_ctx, lp, C0, n0, m0)
        new_C.append(jnp.moveaxis(Cc, 0, 1))
        new_n.append(jnp.moveaxis(nc, 0, 1))
        new_m.append(jnp.moveaxis(mc, 0, 1))
        mod_lat = jax.nn.silu(c) @ w_ada[l] + b_ada[l]
        xs, _ = block(xs, mod_lat, lp,
                      jnp.moveaxis(state_C[:, l], 1, 0).astype(f32),
                      jnp.moveaxis(state_n[:, l], 1, 0).astype(f32),
                      jnp.moveaxis(state_m[:, l], 1, 0).astype(f32))
    new_state_C = jnp.stack(new_C, axis=1)
    new_state_n = jnp.stack(new_n, axis=1)
    new_state_m = jnp.stack(new_m, axis=1)
    return (xp, xs, new_state_C, new_state_n, new_state_m)
```

```python
import functools
import math

import numpy as np
import jax
import jax.numpy as jnp
from jax import lax
from jax.experimental import pallas as pl
from jax.experimental.pallas import tpu as pltpu

F32 = jnp.float32
BF16 = jnp.bfloat16

N_HEADS = 8
HEAD_DIM = 128
N_FGROUPS = 4
FGROUP_DIM = 128
N_GROUPS = 4
EXPERTS_PER_GROUP = 8
N_EXPERTS = N_GROUPS * EXPERTS_PER_GROUP
GRID_W = 64
POS_BASE = 10000.0
EPS = 1e-6

V7X_VMEM_LIMIT_BYTES = 56 * 1024 * 1024
LANES = 128
CHUNK = 256
TOKEN_TILE = CHUNK


def _params(sem):
    return pltpu.CompilerParams(dimension_semantics=sem, vmem_limit_bytes=V7X_VMEM_LIMIT_BYTES)


def _const_spec(shape):
    n = len(shape)
    return pl.BlockSpec(shape, lambda *_: (0,) * n, pipeline_mode=pl.Buffered(1))


def _split2(x):
    hi = x.astype(BF16)
    lo = (x - hi.astype(F32)).astype(BF16)
    return hi, lo


def _split3(x):
    hi = x.astype(BF16)
    r = x - hi.astype(F32)
    mid = r.astype(BF16)
    lo = (r - mid.astype(F32)).astype(BF16)
    return hi, mid, lo


def _dot(a, b):
    return jnp.dot(a, b, preferred_element_type=F32)


def _dot_nt(a, b):
    return lax.dot_general(a, b, (((1,), (1,)), ((), ())), preferred_element_type=F32)


def _dot_f32(a, b, nt=False):
    d = _dot_nt if nt else _dot
    a1, a2 = _split2(a)
    b1, b2 = _split2(b)
    return d(a1, b1) + (d(a1, b2) + d(a2, b1))


def _dot_exact_rhs(a, b_bf16, nt=False):
    d = _dot_nt if nt else _dot
    a1, a2, a3 = _split3(a)
    return d(a1, b_bf16) + (d(a2, b_bf16) + d(a3, b_bf16))


def _rms(x, g):
    return x * lax.rsqrt(jnp.mean(x * x, axis=-1, keepdims=True) + EPS) * g


def _sigmoid(x):
    return 1.0 / (1.0 + jnp.exp(-x))


def _log_sigmoid(x):
    return jnp.minimum(x, 0.0) - jnp.log(1.0 + jnp.exp(-jnp.abs(x)))


def _ada_kernel(c_ref, w_ref, b_ref, o_ref):
    c = c_ref[...]
    s = (c * _sigmoid(c)).astype(BF16)
    o_ref[...] = _dot(s, w_ref[...].astype(BF16)) + b_ref[...]


def _ada(cs, w_ada, b_ada):
    rows, d = cs.shape
    n = w_ada.shape[1]
    tn = 1024
    return pl.pallas_call(
        _ada_kernel,
        grid=(n // tn,),
        in_specs=[pl.BlockSpec((rows, d), lambda j: (0, 0)),
                  pl.BlockSpec((d, tn), lambda j: (0, j)),
                  pl.BlockSpec((1, tn), lambda j: (0, j))],
        out_specs=pl.BlockSpec((rows, tn), lambda j: (0, j)),
        out_shape=jax.ShapeDtypeStruct((rows, n), F32),
        compiler_params=_params(("arbitrary",)),
        name="ada",
    )(cs, w_ada, b_ada.reshape(1, n))


def _proj_kernel(has_pos, *refs):
    if has_pos:
        x_ref, pos_ref = refs[0], refs[1]
        refs = refs[2:]
    else:
        x_ref = refs[0]
        refs = refs[1:]
    (mod_ref, g_ref, wq_ref, wkt_ref, wv_ref, wo_ref, wu_ref, wga_ref, wgb_ref, wgt_ref, bg_ref,
     tri_ref, q_ref, kt_ref, v_ref, o_ref, u_ref, ga_ref, gb_ref, gr_ref) = refs
    x = x_ref[...]
    if has_pos:
        x = x + pos_ref[...]
    sh = mod_ref[0:1, :]
    sc = mod_ref[1:2, :]
    h = _rms(x, g_ref[...]) * (1.0 + sc) + sh
    hb = h.astype(BF16)
    q_ref[...] = _dot(hb, wq_ref[...]).astype(BF16)
    kt_ref[...] = (_dot_nt(wkt_ref[...], hb) * (HEAD_DIM ** -0.5)).astype(BF16)
    v_ref[...] = _dot(hb, wv_ref[...]).astype(BF16)
    o_ref[...] = _dot(hb, wo_ref[...])
    u_ref[...] = _dot(hb, wu_ref[...]).astype(BF16)
    ga_ref[...] = _dot(hb, wga_ref[...])
    gb_ref[...] = _dot(hb, wgb_ref[...])
    g = _dot_f32(wgt_ref[...], h, nt=True) + bg_ref[...]
    nh = N_HEADS
    lf_f = _log_sigmoid(g[nh:2 * nh])
    lf_b = _log_sigmoid(g[3 * nh:4 * nh])
    b_f = _dot_exact_rhs(lf_f, tri_ref[0])
    b_b = _dot_exact_rhs(lf_b, tri_ref[1])
    gr_ref[0:nh, :] = g[0:nh]
    gr_ref[nh:2 * nh, :] = b_f
    gr_ref[2 * nh:3 * nh, :] = g[2 * nh:3 * nh]
    gr_ref[3 * nh:4 * nh, :] = b_b


def _mod_spec(d, tm, seq_len, mod_rows):
    row0, stride = mod_rows
    return pl.BlockSpec((None, 6, d), lambda i, *_: (row0 + stride * ((i * tm) // seq_len), 0, 0))


def _proj(x, pos, mod, mod_rows, seq_len, g_pre1, wts):
    n_tok, d = x.shape
    tm = TOKEN_TILE
    n_tiles = n_tok // tm
    tiles_per_seq = seq_len // tm
    has_pos = pos is not None
    (wq, wkt, wv, wo, wu, wga, wgb, wgt, bg, tri) = wts
    row = pl.BlockSpec((tm, d), lambda i: (i, 0))
    in_specs = [row]
    args = [x]
    if has_pos:
        in_specs.append(pl.BlockSpec((tm, d), lambda i: (i % tiles_per_seq, 0)))
        args.append(pos)
    in_specs += [_mod_spec(d, tm, seq_len, mod_rows), _const_spec((1, d))]
    in_specs += [_const_spec(w.shape) for w in wts]
    args += [mod, g_pre1.reshape(1, d)] + list(wts)
    n_gate_rows = 4 * N_HEADS
    du = wu.shape[1]
    out_shape = [jax.ShapeDtypeStruct((n_tok, d), BF16),
                 jax.ShapeDtypeStruct((n_tiles, d, tm), BF16),
                 jax.ShapeDtypeStruct((n_tok, d), BF16),
                 jax.ShapeDtypeStruct((n_tok, d), F32),
                 jax.ShapeDtypeStruct((n_tok, du), BF16),
                 jax.ShapeDtypeStruct((n_tok, d), F32),
                 jax.ShapeDtypeStruct((n_tok, d), F32),
                 jax.ShapeDtypeStruct((n_tiles, n_gate_rows, tm), F32)]
    out_specs = [row,
                 pl.BlockSpec((None, d, tm), lambda i: (i, 0, 0)),
                 row, row,
                 pl.BlockSpec((tm, du), lambda i: (i, 0)),
                 row, row,
                 pl.BlockSpec((None, n_gate_rows, tm), lambda i: (i, 0, 0))]
    return pl.pallas_call(
        functools.partial(_proj_kernel, has_pos),
        grid=(n_tiles,),
        in_specs=in_specs,
        out_specs=out_specs,
        out_shape=out_shape,
        compiler_params=_params(("parallel",)),
        name="proj",
    )(*args)


def _mlstm_kernel(n_chunks, has_state, emit_state, *refs):
    it = iter(refs)
    if has_state:
        m0_ref = next(it)
    q_ref, kt_ref, v_ref, o_ref, ghn_ref, gr_ref, eye_ref = (next(it) for _ in range(7))
    if has_state:
        c0_ref, n0_ref = next(it), next(it)
    ha_ref = next(it)
    if emit_state:
        cout_ref, nout_ref, mout_ref = next(it), next(it), next(it)
    hs_ref, c_ref, n_ref, m_ref = (next(it) for _ in range(4))

    L = CHUNK
    h_idx = pl.program_id(1)
    nh = N_HEADS

    for d in range(2):
        if has_state:
            c_ref[d] = c0_ref[d]
            n_ref[d] = jnp.broadcast_to(n0_ref[d], (8, HEAD_DIM))
            m0 = m0_ref[pl.program_id(0) * (2 * nh) + d * nh + h_idx]
            m_ref[d] = jnp.full((8, LANES), m0, F32)
        else:
            c_ref[d] = jnp.zeros((HEAD_DIM, HEAD_DIM), F32)
            n_ref[d] = jnp.zeros((8, HEAD_DIM), F32)
            m_ref[d] = jnp.zeros((8, LANES), F32)

    row_id = lax.broadcasted_iota(jnp.int32, (L, L), 0)
    col_id = lax.broadcasted_iota(jnp.int32, (L, L), 1)

    def chunk(d, c):
        t0 = pl.multiple_of(c * L, L)
        qc = q_ref[pl.ds(t0, L), :]
        ktc = kt_ref[c]
        vc = v_ref[pl.ds(t0, L), :]
        ig_row = gr_ref[c, pl.ds(2 * nh * d + h_idx, 1), :]
        b_row = gr_ref[c, pl.ds(2 * nh * d + nh + h_idx, 1), :]
        sub = lax.broadcasted_iota(jnp.int32, (8, L), 0)
        rows = jnp.where(sub == 0, ig_row, jnp.where(sub == 1, b_row, 0.0))
        cols = _dot_exact_rhs_lhs_eye(eye_ref[...], rows)
        ig_col = cols[:, 0:1]
        b_col = cols[:, 1:2]
        m_prev = m_ref[d][0:1, 0:1]
        if d == 0:
            b_end = b_row[:, L - 1:L]
            mask = col_id <= row_id
        else:
            b_end = b_row[:, 0:1]
            mask = col_id >= row_id
        a_row = ig_row - b_row
        dlog = jnp.where(mask, b_col + a_row, -jnp.inf)
        inter = b_col + m_prev
        m_t = jnp.maximum(inter, jnp.max(dlog, axis=1, keepdims=True))
        s = _dot(qc, ktc) * jnp.exp(dlog - m_t)
        decay = jnp.exp(inter - m_t)
        q_c = _dot(qc, c_ref[d].astype(BF16))
        q_n = _dot_nt(qc, n_ref[d].astype(BF16))[:, 0:1]
        num = decay * q_c + _dot(s.astype(BF16), vc)
        den = decay * q_n + jnp.sum(s, axis=1, keepdims=True)
        hch = num / jnp.maximum(jnp.abs(den), jnp.exp(-m_t))
        w_log_row = b_end - b_row + ig_row
        w_log_col = b_end - b_col + ig_col
        m_new = jnp.maximum(b_end + m_prev, jnp.max(w_log_row, axis=1, keepdims=True))
        a = jnp.exp(b_end + m_prev - m_new)
        w_col = jnp.exp(w_log_col - m_new)
        w_row = jnp.exp(w_log_row - m_new)
        c_ref[d] = a * c_ref[d] + _dot(ktc, (w_col * vc.astype(F32)).astype(BF16))
        w_rows = jnp.broadcast_to(w_row, (8, L)).astype(BF16)
        n_ref[d] = a * n_ref[d] + _dot_nt(w_rows, ktc)
        m_ref[d] = jnp.broadcast_to(m_new, (8, LANES))
        return t0, hch

    if n_chunks == 1:
        _, h_f = chunk(0, 0)
        _, h_b = chunk(1, 0)
        hs_ref[...] = h_f + h_b
    else:
        half = n_chunks // 2

        def first(i, carry):
            t_f, h_f = chunk(0, i)
            t_b, h_b = chunk(1, n_chunks - 1 - i)
            hs_ref[pl.ds(t_f, L), :] = h_f
            hs_ref[pl.ds(t_b, L), :] = h_b
            return carry

        def second(i, carry):
            t_f, h_f = chunk(0, i)
            t_b, h_b = chunk(1, n_chunks - 1 - i)
            hs_ref[pl.ds(t_f, L), :] += h_f
            hs_ref[pl.ds(t_b, L), :] += h_b
            return carry

        lax.fori_loop(0, half, first, 0)
        lax.fori_loop(half, n_chunks, second, 0)

    def finish(c, carry):
        t0 = pl.multiple_of(c * L, L)
        ha = hs_ref[pl.ds(t0, L), :]
        ha = ha * lax.rsqrt(jnp.mean(ha * ha, axis=-1, keepdims=True) + EPS)
        ha = ha * ghn_ref[...] * _sigmoid(o_ref[pl.ds(t0, L), :])
        ha_ref[pl.ds(t0, L), :] = ha.astype(BF16)
        return carry

    if n_chunks == 1:
        finish(0, 0)
    else:
        lax.fori_loop(0, n_chunks, finish, 0)

    if emit_state:
        for d in range(2):
            cout_ref[d] = c_ref[d]
            nout_ref[d] = n_ref[d][0:1, :]
            mout_ref[d] = m_ref[d][0:1, :]


def _dot_exact_rhs_lhs_eye(eye_bf16, rows):
    r1, r2, r3 = _split3(rows)
    return _dot_nt(eye_bf16, r1) + (_dot_nt(eye_bf16, r2) + _dot_nt(eye_bf16, r3))


def _mlstm(q, kt, v, o, g_hn, gr, eye, n_seq, seq_len, states):
    n_tok, d = q.shape
    L = CHUNK
    n_chunks = seq_len // L
    has_state = states is not None
    emit_state = not has_state
    nh, dh = N_HEADS, HEAD_DIM
    half = max(n_chunks // 2, 1)
    assert n_chunks == 1 or n_chunks % 2 == 0

    in_specs = []
    args = []
    if has_state:
        state_c, state_n, state_m = states
        in_specs.append(pl.BlockSpec(memory_space=pltpu.SMEM))
        args.append(state_m.reshape(-1))
    in_specs += [pl.BlockSpec((seq_len, dh), lambda b, h: (b, h)),
                 pl.BlockSpec((n_chunks, dh, L), lambda b, h: (b, h, 0)),
                 pl.BlockSpec((seq_len, dh), lambda b, h: (b, h)),
                 pl.BlockSpec((seq_len, dh), lambda b, h: (b, h)),
                 pl.BlockSpec((1, dh), lambda b, h: (0, h)),
                 pl.BlockSpec((n_chunks, 4 * nh, L), lambda b, h: (b, 0, 0)),
                 _const_spec((L, L))]
    args += [q, kt, v, o, g_hn.reshape(1, d), gr, eye]
    if has_state:
        in_specs += [pl.BlockSpec((None, None, 2, None, dh, dh), lambda b, h: (b, 0, 0, h, 0, 0)),
                     pl.BlockSpec((None, None, 2, None, 1, dh), lambda b, h: (b, 0, 0, h, 0, 0))]
        args += [state_c, state_n.reshape(n_seq, 1, 2, nh, 1, dh)]
    out_shape = [jax.ShapeDtypeStruct((n_tok, d), BF16)]
    out_specs = [pl.BlockSpec((seq_len, dh), lambda b, h: (b, h))]
    if emit_state:
        out_shape += [jax.ShapeDtypeStruct((n_seq, 1, 2, nh, dh, dh), F32),
                      jax.ShapeDtypeStruct((n_seq, 1, 2, nh, 1, dh), F32),
                      jax.ShapeDtypeStruct((n_seq, 1, 2, nh, 1, LANES), F32)]
        out_specs += [pl.BlockSpec((None, None, 2, None, dh, dh), lambda b, h: (b, 0, 0, h, 0, 0)),
                      pl.BlockSpec((None, None, 2, None, 1, dh), lambda b, h: (b, 0, 0, h, 0, 0)),
                      pl.BlockSpec((None, None, 2, None, 1, LANES), lambda b, h: (b, 0, 0, h, 0, 0))]
    scratch = [pltpu.VMEM((seq_len, dh), F32),
               pltpu.VMEM((2, dh, dh), F32),
               pltpu.VMEM((2, 8, dh), F32),
               pltpu.VMEM((2, 8, LANES), F32)]
    return pl.pallas_call(
        functools.partial(_mlstm_kernel, n_chunks, has_state, emit_state),
        grid=(n_seq, nh),
        in_specs=in_specs,
        out_specs=out_specs,
        out_shape=out_shape,
        scratch_shapes=scratch,
        compiler_params=_params(("parallel", "parallel")),
        name="mlstm",
    )(*args)


def _fnet_kernel(seq_len, u_ref, f_ref, cs_ref, y_ref, ab_ref):
    T = seq_len

    @pl.when(pl.program_id(1) == 0)
    def _():
        for g in range(N_FGROUPS):
            ug = u_ref[:, g * FGROUP_DIM:(g + 1) * FGROUP_DIM]
            ab_ref[0:T, g * FGROUP_DIM:(g + 1) * FGROUP_DIM] = _dot(ug, cs_ref[0]).astype(BF16)
            ab_ref[T:2 * T, g * FGROUP_DIM:(g + 1) * FGROUP_DIM] = _dot(ug, cs_ref[1]).astype(BF16)

    scale = 1.0 / math.sqrt(T * FGROUP_DIM)
    y_ref[...] = (_dot(f_ref[...], ab_ref[...]) * scale).astype(BF16)


def _fnet(u, f_mat, cs_mat, n_seq, seq_len):
    n_tok, du = u.shape
    tr = min(seq_len, 512)
    return pl.pallas_call(
        functools.partial(_fnet_kernel, seq_len),
        grid=(n_seq, seq_len // tr),
        in_specs=[pl.BlockSpec((seq_len, du), lambda b, r: (b, 0)),
                  pl.BlockSpec((tr, 2 * seq_len), lambda b, r: (r, 0)),
                  _const_spec(cs_mat.shape)],
        out_specs=pl.BlockSpec((tr, du), lambda b, r: (b * (seq_len // tr) + r, 0)),
        out_shape=jax.ShapeDtypeStruct((n_tok, du), BF16),
        scratch_shapes=[pltpu.VMEM((2 * seq_len, du), BF16)],
        compiler_params=_params(("parallel", "arbitrary")),
        name="fnet",
    )(u, f_mat, cs_mat)


def _post_kernel(has_pos, *refs):
    if has_pos:
        x_ref, pos_ref = refs[0], refs[1]
        refs = refs[2:]
    else:
        x_ref = refs[0]
        refs = refs[1:]
    (ha_ref, yf_ref, ga_ref, gb_ref, mod_ref, gp1_ref, gp2_ref, wa_ref, wf_ref, wout_ref, wr_ref, br_ref,
     x1_ref, h2_ref, lg_ref) = refs
    x = x_ref[...]
    if has_pos:
        x = x + pos_ref[...]
    ya = _dot(ha_ref[...], wa_ref[...])
    yf = _dot(yf_ref[...], wf_ref[...])
    mix_in = _sigmoid(ga_ref[...]) * ya + _sigmoid(gb_ref[...]) * yf
    mix = _dot(mix_in.astype(BF16), wout_ref[...])
    gt1 = mod_ref[2:3, :]
    sh2 = mod_ref[3:4, :]
    sc2 = mod_ref[4:5, :]
    x1 = x + gt1 * _rms(mix, gp1_ref[...])
    h2 = _rms(x1, gp2_ref[...]) * (1.0 + sc2) + sh2
    x1_ref[...] = x1
    h2_ref[...] = h2.astype(BF16)
    lg_ref[...] = _dot_f32(h2, wr_ref[...]) + br_ref[...]


def _post(x, pos, ha, yf, ga, gb, mod, mod_rows, seq_len, g_post1, g_pre2, wts):
    n_tok, d = x.shape
    tm = 256
    tiles_per_seq = seq_len // tm
    has_pos = pos is not None
    (wa, wf, wout, wr, br) = wts
    du = yf.shape[1]
    row = pl.BlockSpec((tm, d), lambda i: (i, 0))
    in_specs = [row]
    args = [x]
    if has_pos:
        in_specs.append(pl.BlockSpec((tm, d), lambda i: (i % tiles_per_seq, 0)))
        args.append(pos)
    in_specs += [row, pl.BlockSpec((tm, du), lambda i: (i, 0)), row, row,
                 _mod_spec(d, tm, seq_len, mod_rows),
                 _const_spec((1, d)), _const_spec((1, d))]
    in_specs += [_const_spec(w.shape) for w in wts]
    args += [ha, yf, ga, gb, mod, g_post1.reshape(1, d), g_pre2.reshape(1, d)] + list(wts)
    return pl.pallas_call(
        functools.partial(_post_kernel, has_pos),
        grid=(n_tok // tm,),
        in_specs=in_specs,
        out_specs=[row, row, pl.BlockSpec((tm, LANES), lambda i: (i, 0))],
        out_shape=[jax.ShapeDtypeStruct((n_tok, d), F32),
                   jax.ShapeDtypeStruct((n_tok, d), BF16),
                   jax.ShapeDtypeStruct((n_tok, LANES), F32)],
        compiler_params=_params(("parallel",)),
        name="post",
    )(*args)


def _route(logits):
    tm = logits.shape[0]
    lane = lax.broadcasted_iota(jnp.int32, (tm, LANES), 1)
    lane_f = lane.astype(F32)
    big = float(LANES)
    is_g = lane < N_GROUPS
    lg = jnp.where(is_g, logits, -jnp.inf)
    mx = jnp.max(lg, axis=1, keepdims=True)
    z = jnp.sum(jnp.where(is_g, jnp.exp(lg - mx), 0.0), axis=1, keepdims=True)
    p_sel = 1.0 / z
    g_sel = jnp.min(jnp.where(lg == mx, lane_f, big), axis=1, keepdims=True)
    lo = N_GROUPS + EXPERTS_PER_GROUP * g_sel
    in_grp = (lane_f >= lo) & (lane_f < lo + EXPERTS_PER_GROUP)
    le = jnp.where(in_grp, logits, -jnp.inf)
    v1 = jnp.max(le, axis=1, keepdims=True)
    i1 = jnp.min(jnp.where(le == v1, lane_f, big), axis=1, keepdims=True)
    le2 = jnp.where(lane_f == i1, -jnp.inf, le)
    v2 = jnp.max(le2, axis=1, keepdims=True)
    i2 = jnp.min(jnp.where(le2 == v2, lane_f, big), axis=1, keepdims=True)
    e2 = jnp.exp(v2 - v1)
    p1 = 1.0 / (1.0 + e2)
    p2 = e2 / (1.0 + e2)
    return jnp.where(lane_f == i1, p1 * p_sel, 0.0) + jnp.where(lane_f == i2, p2 * p_sel, 0.0)


def _moe_kernel(h_ref, lg_ref, x1_ref, mod_ref, gp_ref, w1_ref, w3_ref, w2_ref, y_ref, acc_ref, gate_ref):
    e = pl.program_id(1)

    @pl.when(e == 0)
    def _():
        gate_ref[...] = _route(lg_ref[...])
        acc_ref[...] = jnp.zeros_like(acc_ref)

    lane = lax.broadcasted_iota(jnp.int32, gate_ref.shape, 1)
    g_e = jnp.sum(jnp.where(lane == e + N_GROUPS, gate_ref[...], 0.0), axis=1, keepdims=True)
    h = h_ref[...]
    a = _dot(h, w1_ref[...])
    b = _dot(h, w3_ref[...])
    hid = (a * _sigmoid(a)) * b * g_e
    acc_ref[...] += _dot(hid.astype(BF16), w2_ref[...])

    @pl.when(e == pl.num_programs(1) - 1)
    def _():
        gt2 = mod_ref[5:6, :]
        y_ref[...] = x1_ref[...] + gt2 * _rms(acc_ref[...], gp_ref[...])


def _moe(h2, logits, x1, mod, mod_rows, seq_len, g_post2, w1, w3, w2):
    n_tok, d = x1.shape
    tm = 1024
    assert mod_rows[1] == 0 or seq_len % tm == 0
    de = w1.shape[1] // N_EXPERTS
    row = pl.BlockSpec((tm, d), lambda i, e: (i, 0))
    return pl.pallas_call(
        _moe_kernel,
        grid=(n_tok // tm, N_EXPERTS),
        in_specs=[row,
                  pl.BlockSpec((tm, LANES), lambda i, e: (i, 0)),
                  row,
                  _mod_spec(d, tm, seq_len, mod_rows),
                  pl.BlockSpec((1, d), lambda i, e: (0, 0)),
                  pl.BlockSpec((d, de), lambda i, e: (0, e)),
                  pl.BlockSpec((d, de), lambda i, e: (0, e)),
                  pl.BlockSpec((de, d), lambda i, e: (e, 0))],
        out_specs=row,
        out_shape=jax.ShapeDtypeStruct((n_tok, d), F32),
        scratch_shapes=[pltpu.VMEM((tm, d), F32), pltpu.VMEM((tm, LANES), F32)],
        compiler_params=_params(("parallel", "arbitrary")),
        name="moe",
    )(h2, logits, x1, mod, g_post2.reshape(1, d), w1, w3, w2)


def _grid_pos(n_tokens, d_model):
    rows = n_tokens // GRID_W
    row = jnp.repeat(jnp.arange(rows, dtype=F32), GRID_W)
    col = jnp.tile(jnp.arange(GRID_W, dtype=F32), rows)
    n_freq = d_model // 4
    freq = jnp.exp(-math.log(POS_BASE) * jnp.arange(n_freq, dtype=F32) / n_freq)

    def enc(p):
        a = p[:, None] * freq[None, :]
        return jnp.concatenate([jnp.sin(a), jnp.cos(a)], axis=-1)

    return jnp.concatenate([enc(row), enc(col)], axis=-1)


def _dft_cos_sin(n):
    k = np.arange(n, dtype=np.int64)
    ang = 2.0 * np.pi * ((k[:, None] * k[None, :]) % n).astype(np.float64) / n
    return np.cos(ang), np.sin(ang)


def _fnet_consts(seq_len):
    ct, st = _dft_cos_sin(seq_len)
    f_mat = jnp.asarray(np.concatenate([ct, -st], axis=1), dtype=F32).astype(BF16)
    cc, sc = _dft_cos_sin(FGROUP_DIM)
    cs_mat = jnp.asarray(np.stack([cc, sc]), dtype=F32).astype(BF16)
    return f_mat, cs_mat


def _tri_consts():
    i = np.arange(CHUNK)
    prefix = (i[:, None] <= i[None, :]).astype(np.float32)
    suffix = (i[:, None] >= i[None, :]).astype(np.float32)
    tri = jnp.asarray(np.stack([prefix, suffix]), dtype=BF16)
    eye = jnp.asarray(np.eye(CHUNK, dtype=np.float32), dtype=BF16)
    return tri, eye


def _run_path(x, pos, n_seq, seq_len, mod, mod_rows, states, lw, consts):
    d = x.shape[-1]
    (g_pre1, proj_w, g_hn, post_w, g_post1, g_pre2, w_e1, w_e3, w_e2, g_post2) = lw
    tri, eye = consts
    x2 = x.reshape(n_seq * seq_len, d)
    q, kt, v, o, u, ga, gb, gr = _proj(x2, pos, mod, mod_rows, seq_len, g_pre1, proj_w + (tri,))
    outs = _mlstm(q, kt, v, o, g_hn, gr, eye, n_seq, seq_len, states)
    ha = outs[0]
    f_mat, cs_mat = _fnet_consts(seq_len)
    yf = _fnet(u, f_mat, cs_mat, n_seq, seq_len)
    x1, h2, logits = _post(x2, pos, ha, yf, ga, gb, mod, mod_rows, seq_len, g_post1, g_pre2, post_w)
    y = _moe(h2, logits, x1, mod, mod_rows, seq_len, g_post2, w_e1, w_e3, w_e2)
    return y.reshape(n_seq, seq_len, d), outs[1:]


def kernel(x_prompt, x_sample, c, state_C, state_n, state_m, c_ctx, w_ada, b_ada, g_pre1, w_in, b_gates, g_hn,
           w_a, w_f, w_out, g_post1, g_pre2, w_rg, b_rg, w_re, b_re, w_e1, w_e3, w_e2, g_post2):
    bp, sp, d = x_prompt.shape
    bs, ss, _ = x_sample.shape
    depth = w_in.shape[0]
    assert depth == 1
    l = 0
    dm = N_HEADS * HEAD_DIM
    du = N_FGROUPS * FGROUP_DIM
    ng = 4 * N_HEADS

    n_rows = 8
    assert 1 + bs <= n_rows
    cs = jnp.concatenate([c_ctx[None, :], c, jnp.zeros((n_rows - 1 - bs, d), F32)], axis=0)
    mod = _ada(cs, w_ada[l], b_ada[l]).reshape(n_rows, 6, d)

    wi = w_in[l]
    cuts = np.cumsum([dm] * 4 + [ng, du, d]).tolist()
    wq = wi[:, :cuts[0]].astype(BF16)
    wkt = wi[:, cuts[0]:cuts[1]].T.astype(BF16)
    wv = wi[:, cuts[1]:cuts[2]].astype(BF16)
    wo = wi[:, cuts[2]:cuts[3]].astype(BF16)
    wgt = wi[:, cuts[3]:cuts[4]].T
    wu = wi[:, cuts[4]:cuts[5]].astype(BF16)
    wga = wi[:, cuts[5]:cuts[6]].astype(BF16)
    wgb = wi[:, cuts[6]:].astype(BF16)
    bg = b_gates[l].reshape(ng, 1)
    proj_w = (wq, wkt, wv, wo, wu, wga, wgb, wgt, bg)
    n_r = N_GROUPS + N_EXPERTS
    wr = jnp.concatenate([w_rg[l], w_re[l], jnp.zeros((d, LANES - n_r), F32)], axis=1)
    br = jnp.concatenate([b_rg[l], b_re[l], jnp.zeros((LANES - n_r,), F32)]).reshape(1, LANES)
    post_w = (w_a[l].astype(BF16), w_f[l].astype(BF16), w_out[l].astype(BF16), wr, br)
    lw = (g_pre1[l], proj_w, g_hn[l], post_w, g_post1[l], g_pre2[l],
          w_e1[l].astype(BF16), w_e3[l].astype(BF16), w_e2[l].astype(BF16), g_post2[l])
    consts = _tri_consts()

    y_prompt, (new_c, new_n, new_m) = _run_path(x_prompt, None, bp, sp, mod, (0, 0), None, lw, consts)
    pos = _grid_pos(ss, d)
    y_sample, _ = _run_path(x_sample, pos, bs, ss, mod, (1, 1), (state_C, state_n, state_m), lw, consts)

    new_state_c = new_c
    new_state_n = new_n.reshape(bp, depth, 2, N_HEADS, HEAD_DIM)
    new_state_m = new_m[..., 0, 0]
    return (y_prompt, y_sample, new_state_c, new_state_n, new_state_m)
```

```python
import functools
import math

import numpy as np
import jax
import jax.numpy as jnp
from jax import lax
from jax.experimental import pallas as pl
from jax.experimental.pallas import tpu as pltpu

F32 = jnp.float32
BF16 = jnp.bfloat16

N_HEADS = 8
HEAD_DIM = 128
N_FGROUPS = 4
FGROUP_DIM = 128
N_GROUPS = 4
EXPERTS_PER_GROUP = 8
N_EXPERTS = N_GROUPS * EXPERTS_PER_GROUP
GRID_W = 64
POS_BASE = 10000.0
EPS = 1e-6

V7X_VMEM_LIMIT_BYTES = 56 * 1024 * 1024
LANES = 128
CHUNK = 256
TOKEN_TILE = CHUNK


def _params(sem):
    return pltpu.CompilerParams(dimension_semantics=sem, vmem_limit_bytes=V7X_VMEM_LIMIT_BYTES)


def _const_spec(shape):
    n = len(shape)
    return pl.BlockSpec(shape, lambda *_: (0,) * n, pipeline_mode=pl.Buffered(1))


def _split2(x):
    hi = x.astype(BF16)
    lo = (x - hi.astype(F32)).astype(BF16)
    return hi, lo


def _split3(x):
    hi = x.astype(BF16)
    r = x - hi.astype(F32)
    mid = r.astype(BF16)
    lo = (r - mid.astype(F32)).astype(BF16)
    return hi, mid, lo


def _dot(a, b):
    return jnp.dot(a, b, preferred_element_type=F32)


def _dot_nt(a, b):
    return lax.dot_general(a, b, (((1,), (1,)), ((), ())), preferred_element_type=F32)


def _dot_f32(a, b, nt=False):
    d = _dot_nt if nt else _dot
    a1, a2 = _split2(a)
    b1, b2 = _split2(b)
    return d(a1, b1) + (d(a1, b2) + d(a2, b1))


def _dot_exact_rhs(a, b_bf16, nt=False):
    d = _dot_nt if nt else _dot
    a1, a2, a3 = _split3(a)
    return d(a1, b_bf16) + (d(a2, b_bf16) + d(a3, b_bf16))


def _rms(x, g):
    return x * lax.rsqrt(jnp.mean(x * x, axis=-1, keepdims=True) + EPS) * g


def _sigmoid(x):
    return 1.0 / (1.0 + jnp.exp(-x))


def _log_sigmoid(x):
    return jnp.minimum(x, 0.0) - jnp.log(1.0 + jnp.exp(-jnp.abs(x)))


def _ada_kernel(c_ref, w_ref, b_ref, o_ref):
    c = c_ref[...]
    s = (c * _sigmoid(c)).astype(BF16)
    o_ref[...] = _dot(s, w_ref[...].astype(BF16)) + b_ref[...]


def _ada(cs, w_ada, b_ada):
    rows, d = cs.shape
    n = w_ada.shape[1]
    tn = 1024
    return pl.pallas_call(
        _ada_kernel,
        grid=(n // tn,),
        in_specs=[pl.BlockSpec((rows, d), lambda j: (0, 0)),
                  pl.BlockSpec((d, tn), lambda j: (0, j)),
                  pl.BlockSpec((1, tn), lambda j: (0, j))],
        out_specs=pl.BlockSpec((rows, tn), lambda j: (0, j)),
        out_shape=jax.ShapeDtypeStruct((rows, n), F32),
        compiler_params=_params(("arbitrary",)),
        name="ada",
    )(cs, w_ada, b_ada.reshape(1, n))


def _proj_kernel(has_pos, *refs):
    if has_pos:
        x_ref, pos_ref = refs[0], refs[1]
        refs = refs[2:]
    else:
        x_ref = refs[0]
        refs = refs[1:]
    (mod_ref, g_ref, wq_ref, wkt_ref, wv_ref, wo_ref, wu_ref, wga_ref, wgb_ref, wgt_ref, bg_ref,
     tri_ref, q_ref, kt_ref, v_ref, o_ref, u_ref, ga_ref, gb_ref, gr_ref) = refs
    x = x_ref[...]
    if has_pos:
        x = x + pos_ref[...]
    sh = mod_ref[0:1, :]
    sc = mod_ref[1:2, :]
    h = _rms(x, g_ref[...]) * (1.0 + sc) + sh
    hb = h.astype(BF16)
    q_ref[...] = _dot(hb, wq_ref[...]).astype(BF16)
    kt_ref[...] = (_dot_nt(wkt_ref[...], hb) * (HEAD_DIM ** -0.5)).astype(BF16)
    v_ref[...] = _dot(hb, wv_ref[...]).astype(BF16)
    o_ref[...] = _dot(hb, wo_ref[...])
    u_ref[...] = _dot(hb, wu_ref[...]).astype(BF16)
    ga_ref[...] = _dot(hb, wga_ref[...])
    gb_ref[...] = _dot(hb, wgb_ref[...])
    g = _dot_f32(wgt_ref[...], h, nt=True) + bg_ref[...]
    nh = N_HEADS
    lf_f = _log_sigmoid(g[nh:2 * nh])
    lf_b = _log_sigmoid(g[3 * nh:4 * nh])
    b_f = _dot_exact_rhs(lf_f, tri_ref[0])
    b_b = _dot_exact_rhs(lf_b, tri_ref[1])
    gr_ref[0:nh, :] = g[0:nh]
    gr_ref[nh:2 * nh, :] = b_f
    gr_ref[2 * nh:3 * nh, :] = g[2 * nh:3 * nh]
    gr_ref[3 * nh:4 * nh, :] = b_b


def _mod_spec(d, tm, seq_len, mod_rows):
    row0, stride = mod_rows
    return pl.BlockSpec((None, 6, d), lambda i, *_: (row0 + stride * ((i * tm) // seq_len), 0, 0))


def _proj(x, pos, mod, mod_rows, seq_len, g_pre1, wts):
    n_tok, d = x.shape
    tm = TOKEN_TILE
    n_tiles = n_tok // tm
    tiles_per_seq = seq_len // tm
    has_pos = pos is not None
    (wq, wkt, wv, wo, wu, wga, wgb, wgt, bg, tri) = wts
    row = pl.BlockSpec((tm, d), lambda i: (i, 0))
    in_specs = [row]
    args = [x]
    if has_pos:
        in_specs.append(pl.BlockSpec((tm, d), lambda i: (i % tiles_per_seq, 0)))
        args.append(pos)
    in_specs += [_mod_spec(d, tm, seq_len, mod_rows), _const_spec((1, d))]
    in_specs += [_const_spec(w.shape) for w in wts]
    args += [mod, g_pre1.reshape(1, d)] + list(wts)
    n_gate_rows = 4 * N_HEADS
    du = wu.shape[1]
    out_shape = [jax.ShapeDtypeStruct((n_tok, d), BF16),
                 jax.ShapeDtypeStruct((n_tiles, d, tm), BF16),
                 jax.ShapeDtypeStruct((n_tok, d), BF16),
                 jax.ShapeDtypeStruct((n_tok, d), F32),
                 jax.ShapeDtypeStruct((n_tok, du), BF16),
                 jax.ShapeDtypeStruct((n_tok, d), F32),
                 jax.ShapeDtypeStruct((n_tok, d), F32),
                 jax.ShapeDtypeStruct((n_tiles, n_gate_rows, tm), F32)]
    out_specs = [row,
                 pl.BlockSpec((None, d, tm), lambda i: (i, 0, 0)),
                 row, row,
                 pl.BlockSpec((tm, du), lambda i: (i, 0)),
                 row, row,
                 pl.BlockSpec((None, n_gate_rows, tm), lambda i: (i, 0, 0))]
    return pl.pallas_call(
        functools.partial(_proj_kernel, has_pos),
        grid=(n_tiles,),
        in_specs=in_specs,
        out_specs=out_specs,
        out_shape=out_shape,
        compiler_params=_params(("parallel",)),
        name="proj",
    )(*args)


def _mlstm_kernel(n_chunks, hp, has_state, emit_state, *refs):
    it = iter(refs)
    if has_state:
        m0_ref = next(it)
    q_ref, kt_ref, v_ref, o_ref, ghn_ref, gr_ref = (next(it) for _ in range(6))
    if has_state:
        c0_ref, n0_ref = next(it), next(it)
    ha_ref = next(it)
    if emit_state:
        cout_ref, nout_ref, mout_ref = next(it), next(it), next(it)
    hs_ref, cn_ref, m_ref = (next(it) for _ in range(3))

    L = CHUNK
    nh, dh = N_HEADS, HEAD_DIM
    head0 = pl.program_id(1) * hp
    neg_inf = -jnp.inf

    sq_r = lax.broadcasted_iota(jnp.int32, (dh, dh), 0)
    sq_c = lax.broadcasted_iota(jnp.int32, (dh, dh), 1)
    for j in range(hp):
        for d in range(2):
            if has_state:
                n_col = jnp.sum(jnp.where(sq_r == sq_c, n0_ref[d, j], 0.0), axis=1, keepdims=True)
                cn_ref[j, d, :, 0:dh] = c0_ref[d, j]
                cn_ref[j, d, :, dh:2 * dh] = jnp.broadcast_to(n_col, (dh, dh))
                m0 = m0_ref[pl.program_id(0) * (2 * nh) + d * nh + head0 + j]
                m_ref[j, d] = jnp.full((8, LANES), m0, F32)
            else:
                cn_ref[j, d] = jnp.zeros((dh, 2 * dh), F32)
                m_ref[j, d] = jnp.zeros((8, LANES), F32)

    row_id = lax.broadcasted_iota(jnp.int32, (L, L), 0)
    col_id = lax.broadcasted_iota(jnp.int32, (L, L), 1)
    ones_blk = jnp.ones((L, dh), BF16)

    def chunk(j, d, c):
        t0 = pl.multiple_of(c * L, L)
        lanes = slice(j * dh, (j + 1) * dh)
        qc = q_ref[pl.ds(t0, L), lanes]
        ktc = kt_ref[c, lanes, :]
        vc = v_ref[pl.ds(t0, L), lanes]
        ig_row = gr_ref[c, pl.ds(2 * nh * d + head0 + j, 1), :]
        b_row = gr_ref[c, pl.ds(2 * nh * d + nh + head0 + j, 1), :]
        m_prev = m_ref[j, d][0:1, 0:1]
        if d == 0:
            b_end = b_row[:, L - 1:L]
            mask = col_id <= row_id
        else:
            b_end = b_row[:, 0:1]
            mask = col_id >= row_id
        a_row = ig_row - b_row
        a_max = jnp.max(jnp.where(mask, a_row, neg_inf), axis=1, keepdims=True)
        g = jnp.maximum(jnp.broadcast_to(a_max, (L, dh)), m_prev)
        b_col = jnp.broadcast_to(
            jnp.sum(jnp.where(row_id == col_id, b_row, 0.0), axis=1, keepdims=True), (L, dh))
        g_full = jnp.concatenate([g] * (L // dh), axis=1)
        s = _dot(qc, ktc) * jnp.exp(jnp.where(mask, a_row - g_full, neg_inf))
        decay = jnp.exp(m_prev - g)
        qcn = _dot(qc, cn_ref[j, d].astype(BF16))
        num = decay * qcn[:, 0:dh] + _dot(s.astype(BF16), vc)
        den = decay * qcn[:, dh:2 * dh] + jnp.sum(s, axis=1, keepdims=True)
        hch = num / jnp.maximum(jnp.abs(den), jnp.exp(-(b_col + g)))
        w_log = b_end + a_row
        m_new = jnp.maximum(b_end + m_prev, jnp.max(w_log, axis=1, keepdims=True))
        kw = (ktc.astype(F32) * jnp.exp(w_log - m_new)).astype(BF16)
        v_aug = jnp.concatenate([vc, ones_blk], axis=1)
        cn_ref[j, d] = jnp.exp(b_end + m_prev - m_new) * cn_ref[j, d] + _dot(kw, v_aug)
        m_ref[j, d] = jnp.broadcast_to(m_new, (8, LANES))
        return t0, hch

    if n_chunks == 1:
        for j in range(hp):
            _, h_f = chunk(j, 0, 0)
            _, h_b = chunk(j, 1, 0)
            hs_ref[:, j * dh:(j + 1) * dh] = h_f + h_b
    else:
        half = n_chunks // 2

        def first(i, carry):
            for j in range(hp):
                lanes = slice(j * dh, (j + 1) * dh)
                t_f, h_f = chunk(j, 0, i)
                t_b, h_b = chunk(j, 1, n_chunks - 1 - i)
                hs_ref[pl.ds(t_f, L), lanes] = h_f
                hs_ref[pl.ds(t_b, L), lanes] = h_b
            return carry

        def second(i, carry):
            for j in range(hp):
                lanes = slice(j * dh, (j + 1) * dh)
                t_f, h_f = chunk(j, 0, i)
                t_b, h_b = chunk(j, 1, n_chunks - 1 - i)
                hs_ref[pl.ds(t_f, L), lanes] += h_f
                hs_ref[pl.ds(t_b, L), lanes] += h_b
            return carry

        lax.fori_loop(0, half, first, 0)
        lax.fori_loop(half, n_chunks, second, 0)

    def finish(c, carry):
        t0 = pl.multiple_of(c * L, L)
        for j in range(hp):
            lanes = slice(j * dh, (j + 1) * dh)
            ha = hs_ref[pl.ds(t0, L), lanes]
            ha = ha * lax.rsqrt(jnp.mean(ha * ha, axis=-1, keepdims=True) + EPS)
            ha = ha * ghn_ref[:, lanes] * _sigmoid(o_ref[pl.ds(t0, L), lanes])
            ha_ref[pl.ds(t0, L), lanes] = ha.astype(BF16)
        return carry

    if n_chunks == 1:
        finish(0, 0)
    else:
        lax.fori_loop(0, n_chunks, finish, 0)

    if emit_state:
        for j in range(hp):
            for d in range(2):
                cout_ref[d, j] = cn_ref[j, d, :, 0:dh]
                n_rep = cn_ref[j, d, :, dh:2 * dh]
                nout_ref[d, j] = jnp.sum(jnp.where(sq_r == sq_c, n_rep, 0.0), axis=0, keepdims=True)
                mout_ref[d, j] = m_ref[j, d][0:1, :]


def _mlstm(q, kt, v, o, g_hn, gr, n_seq, seq_len, states, hp):
    n_tok, d = q.shape
    L = CHUNK
    n_chunks = seq_len // L
    has_state = states is not None
    emit_state = not has_state
    nh, dh = N_HEADS, HEAD_DIM
    assert n_chunks == 1 or n_chunks % 2 == 0
    assert nh % hp == 0
    wd = hp * dh

    in_specs = []
    args = []
    if has_state:
        state_c, state_n, state_m = states
        in_specs.append(pl.BlockSpec(memory_space=pltpu.SMEM))
        args.append(state_m.reshape(-1))
    tok = pl.BlockSpec((seq_len, wd), lambda b, h: (b, h))
    in_specs += [tok,
                 pl.BlockSpec((n_chunks, wd, L), lambda b, h: (b, h, 0)),
                 tok, tok,
                 pl.BlockSpec((1, wd), lambda b, h: (0, h)),
                 pl.BlockSpec((n_chunks, 4 * nh, L), lambda b, h: (b, 0, 0))]
    args += [q, kt, v, o, g_hn.reshape(1, d), gr]
    st_c = pl.BlockSpec((None, None, 2, hp, dh, dh), lambda b, h: (b, 0, 0, h, 0, 0))
    st_v = pl.BlockSpec((None, None, 2, hp, 1, dh), lambda b, h: (b, 0, 0, h, 0, 0))
    if has_state:
        in_specs += [st_c, st_v]
        args += [state_c, state_n.reshape(n_seq, 1, 2, nh, 1, dh)]
    out_shape = [jax.ShapeDtypeStruct((n_tok, d), BF16)]
    out_specs = [tok]
    if emit_state:
        out_shape += [jax.ShapeDtypeStruct((n_seq, 1, 2, nh, dh, dh), F32),
                      jax.ShapeDtypeStruct((n_seq, 1, 2, nh, 1, dh), F32),
                      jax.ShapeDtypeStruct((n_seq, 1, 2, nh, 1, LANES), F32)]
        out_specs += [st_c, st_v, st_v]
    scratch = [pltpu.VMEM((seq_len, wd), F32),
               pltpu.VMEM((hp, 2, dh, 2 * dh), F32),
               pltpu.VMEM((hp, 2, 8, LANES), F32)]
    return pl.pallas_call(
        functools.partial(_mlstm_kernel, n_chunks, hp, has_state, emit_state),
        grid=(n_seq, nh // hp),
        in_specs=in_specs,
        out_specs=out_specs,
        out_shape=out_shape,
        scratch_shapes=scratch,
        compiler_params=_params(("parallel", "parallel")),
        name="mlstm",
    )(*args)


def _fnet_kernel(seq_len, u_ref, f_ref, cs_ref, y_ref, ab_ref):
    T = seq_len

    @pl.when(pl.program_id(1) == 0)
    def _():
        for g in range(N_FGROUPS):
            ug = u_ref[:, g * FGROUP_DIM:(g + 1) * FGROUP_DIM]
            ab_ref[0:T, g * FGROUP_DIM:(g + 1) * FGROUP_DIM] = _dot(ug, cs_ref[0]).astype(BF16)
            ab_ref[T:2 * T, g * FGROUP_DIM:(g + 1) * FGROUP_DIM] = _dot(ug, cs_ref[1]).astype(BF16)

    scale = 1.0 / math.sqrt(T * FGROUP_DIM)
    y_ref[...] = (_dot(f_ref[...], ab_ref[...]) * scale).astype(BF16)


def _fnet(u, f_mat, cs_mat, n_seq, seq_len):
    n_tok, du = u.shape
    tr = min(seq_len, 512)
    return pl.pallas_call(
        functools.partial(_fnet_kernel, seq_len),
        grid=(n_seq, seq_len // tr),
        in_specs=[pl.BlockSpec((seq_len, du), lambda b, r: (b, 0)),
                  pl.BlockSpec((tr, 2 * seq_len), lambda b, r: (r, 0)),
                  _const_spec(cs_mat.shape)],
        out_specs=pl.BlockSpec((tr, du), lambda b, r: (b * (seq_len // tr) + r, 0)),
        out_shape=jax.ShapeDtypeStruct((n_tok, du), BF16),
        scratch_shapes=[pltpu.VMEM((2 * seq_len, du), BF16)],
        compiler_params=_params(("parallel", "arbitrary")),
        name="fnet",
    )(u, f_mat, cs_mat)


def _post_kernel(has_pos, *refs):
    if has_pos:
        x_ref, pos_ref = refs[0], refs[1]
        refs = refs[2:]
    else:
        x_ref = refs[0]
        refs = refs[1:]
    (ha_ref, yf_ref, ga_ref, gb_ref, mod_ref, gp1_ref, gp2_ref, wa_ref, wf_ref, wout_ref, wr_ref, br_ref,
     x1_ref, h2_ref, lg_ref) = refs
    x = x_ref[...]
    if has_pos:
        x = x + pos_ref[...]
    ya = _dot(ha_ref[...], wa_ref[...])
    yf = _dot(yf_ref[...], wf_ref[...])
    mix_in = _sigmoid(ga_ref[...]) * ya + _sigmoid(gb_ref[...]) * yf
    mix = _dot(mix_in.astype(BF16), wout_ref[...])
    gt1 = mod_ref[2:3, :]
    sh2 = mod_ref[3:4, :]
    sc2 = mod_ref[4:5, :]
    x1 = x + gt1 * _rms(mix, gp1_ref[...])
    h2 = _rms(x1, gp2_ref[...]) * (1.0 + sc2) + sh2
    x1_ref[...] = x1
    h2_ref[...] = h2.astype(BF16)
    lg_ref[...] = _dot_f32(h2, wr_ref[...]) + br_ref[...]


def _post(x, pos, ha, yf, ga, gb, mod, mod_rows, seq_len, g_post1, g_pre2, wts):
    n_tok, d = x.shape
    tm = 256
    tiles_per_seq = seq_len // tm
    has_pos = pos is not None
    (wa, wf, wout, wr, br) = wts
    du = yf.shape[1]
    row = pl.BlockSpec((tm, d), lambda i: (i, 0))
    in_specs = [row]
    args = [x]
    if has_pos:
        in_specs.append(pl.BlockSpec((tm, d), lambda i: (i % tiles_per_seq, 0)))
        args.append(pos)
    in_specs += [row, pl.BlockSpec((tm, du), lambda i: (i, 0)), row, row,
                 _mod_spec(d, tm, seq_len, mod_rows),
                 _const_spec((1, d)), _const_spec((1, d))]
    in_specs += [_const_spec(w.shape) for w in wts]
    args += [ha, yf, ga, gb, mod, g_post1.reshape(1, d), g_pre2.reshape(1, d)] + list(wts)
    return pl.pallas_call(
        functools.partial(_post_kernel, has_pos),
        grid=(n_tok // tm,),
        in_specs=in_specs,
        out_specs=[row, row, pl.BlockSpec((tm, LANES), lambda i: (i, 0))],
        out_shape=[jax.ShapeDtypeStruct((n_tok, d), F32),
                   jax.ShapeDtypeStruct((n_tok, d), BF16),
                   jax.ShapeDtypeStruct((n_tok, LANES), F32)],
        compiler_params=_params(("parallel",)),
        name="post",
    )(*args)


def _route(logits):
    tm = logits.shape[0]
    lane = lax.broadcasted_iota(jnp.int32, (tm, LANES), 1)
    lane_f = lane.astype(F32)
    big = float(LANES)
    is_g = lane < N_GROUPS
    lg = jnp.where(is_g, logits, -jnp.inf)
    mx = jnp.max(lg, axis=1, keepdims=True)
    z = jnp.sum(jnp.where(is_g, jnp.exp(lg - mx), 0.0), axis=1, keepdims=True)
    p_sel = 1.0 / z
    g_sel = jnp.min(jnp.where(lg == mx, lane_f, big), axis=1, keepdims=True)
    lo = N_GROUPS + EXPERTS_PER_GROUP * g_sel
    in_grp = (lane_f >= lo) & (lane_f < lo + EXPERTS_PER_GROUP)
    le = jnp.where(in_grp, logits, -jnp.inf)
    v1 = jnp.max(le, axis=1, keepdims=True)
    i1 = jnp.min(jnp.where(le == v1, lane_f, big), axis=1, keepdims=True)
    le2 = jnp.where(lane_f == i1, -jnp.inf, le)
    v2 = jnp.max(le2, axis=1, keepdims=True)
    i2 = jnp.min(jnp.where(le2 == v2, lane_f, big), axis=1, keepdims=True)
    e2 = jnp.exp(v2 - v1)
    p1 = 1.0 / (1.0 + e2)
    p2 = e2 / (1.0 + e2)
    return jnp.where(lane_f == i1, p1 * p_sel, 0.0) + jnp.where(lane_f == i2, p2 * p_sel, 0.0)


def _moe_kernel(h_ref, lg_ref, x1_ref, mod_ref, gp_ref, w1_ref, w3_ref, w2_ref, y_ref, acc_ref, gate_ref):
    e = pl.program_id(1)

    @pl.when(e == 0)
    def _():
        gate_ref[...] = _route(lg_ref[...])
        acc_ref[...] = jnp.zeros_like(acc_ref)

    lane = lax.broadcasted_iota(jnp.int32, gate_ref.shape, 1)
    g_e = jnp.sum(jnp.where(lane == e + N_GROUPS, gate_ref[...], 0.0), axis=1, keepdims=True)
    h = h_ref[...]
    a = _dot(h, w1_ref[...])
    b = _dot(h, w3_ref[...])
    hid = (a * _sigmoid(a)) * b * g_e
    acc_ref[...] += _dot(hid.astype(BF16), w2_ref[...])

    @pl.when(e == pl.num_programs(1) - 1)
    def _():
        gt2 = mod_ref[5:6, :]
        y_ref[...] = x1_ref[...] + gt2 * _rms(acc_ref[...], gp_ref[...])


def _moe(h2, logits, x1, mod, mod_rows, seq_len, g_post2, w1, w3, w2):
    n_tok, d = x1.shape
    tm = 1024
    assert mod_rows[1] == 0 or seq_len % tm == 0
    de = w1.shape[1] // N_EXPERTS
    row = pl.BlockSpec((tm, d), lambda i, e: (i, 0))
    return pl.pallas_call(
        _moe_kernel,
        grid=(n_tok // tm, N_EXPERTS),
        in_specs=[row,
                  pl.BlockSpec((tm, LANES), lambda i, e: (i, 0)),
                  row,
                  _mod_spec(d, tm, seq_len, mod_rows),
                  pl.BlockSpec((1, d), lambda i, e: (0, 0)),
                  pl.BlockSpec((d, de), lambda i, e: (0, e)),
                  pl.BlockSpec((d, de), lambda i, e: (0, e)),
                  pl.BlockSpec((de, d), lambda i, e: (e, 0))],
        out_specs=row,
        out_shape=jax.ShapeDtypeStruct((n_tok, d), F32),
        scratch_shapes=[pltpu.VMEM((tm, d), F32), pltpu.VMEM((tm, LANES), F32)],
        compiler_params=_params(("parallel", "arbitrary")),
        name="moe",
    )(h2, logits, x1, mod, g_post2.reshape(1, d), w1, w3, w2)


def _grid_pos(n_tokens, d_model):
    rows = n_tokens // GRID_W
    row = jnp.repeat(jnp.arange(rows, dtype=F32), GRID_W)
    col = jnp.tile(jnp.arange(GRID_W, dtype=F32), rows)
    n_freq = d_model // 4
    freq = jnp.exp(-math.log(POS_BASE) * jnp.arange(n_freq, dtype=F32) / n_freq)

    def enc(p):
        a = p[:, None] * freq[None, :]
        return jnp.concatenate([jnp.sin(a), jnp.cos(a)], axis=-1)

    return jnp.concatenate([enc(row), enc(col)], axis=-1)


def _dft_cos_sin(n):
    k = np.arange(n, dtype=np.int64)
    ang = 2.0 * np.pi * ((k[:, None] * k[None, :]) % n).astype(np.float64) / n
    return np.cos(ang), np.sin(ang)


def _fnet_consts(seq_len):
    ct, st = _dft_cos_sin(seq_len)
    f_mat = jnp.asarray(np.concatenate([ct, -st], axis=1), dtype=F32).astype(BF16)
    cc, sc = _dft_cos_sin(FGROUP_DIM)
    cs_mat = jnp.asarray(np.stack([cc, sc]), dtype=F32).astype(BF16)
    return f_mat, cs_mat


def _tri_consts():
    i = np.arange(CHUNK)
    prefix = (i[:, None] <= i[None, :]).astype(np.float32)
    suffix = (i[:, None] >= i[None, :]).astype(np.float32)
    return jnp.asarray(np.stack([prefix, suffix]), dtype=BF16)


def _run_path(x, pos, n_seq, seq_len, mod, mod_rows, states, lw, consts):
    d = x.shape[-1]
    (g_pre1, proj_w, g_hn, post_w, g_post1, g_pre2, w_e1, w_e3, w_e2, g_post2) = lw
    tri = consts
    x2 = x.reshape(n_seq * seq_len, d)
    q, kt, v, o, u, ga, gb, gr = _proj(x2, pos, mod, mod_rows, seq_len, g_pre1, proj_w + (tri,))
    heads_per_step = N_HEADS if seq_len == CHUNK else 2
    outs = _mlstm(q, kt, v, o, g_hn, gr, n_seq, seq_len, states, heads_per_step)
    ha = outs[0]
    f_mat, cs_mat = _fnet_consts(seq_len)
    yf = _fnet(u, f_mat, cs_mat, n_seq, seq_len)
    x1, h2, logits = _post(x2, pos, ha, yf, ga, gb, mod, mod_rows, seq_len, g_post1, g_pre2, post_w)
    y = _moe(h2, logits, x1, mod, mod_rows, seq_len, g_post2, w_e1, w_e3, w_e2)
    return y.reshape(n_seq, seq_len, d), outs[1:]


def kernel(x_prompt, x_sample, c, state_C, state_n, state_m, c_ctx, w_ada, b_ada, g_pre1, w_in, b_gates, g_hn,
           w_a, w_f, w_out, g_post1, g_pre2, w_rg, b_rg, w_re, b_re, w_e1, w_e3, w_e2, g_post2):
    bp, sp, d = x_prompt.shape
    bs, ss, _ = x_sample.shape
    depth = w_in.shape[0]
    assert depth == 1
    l = 0
    dm = N_HEADS * HEAD_DIM
    du = N_FGROUPS * FGROUP_DIM
    ng = 4 * N_HEADS

    n_rows = 8
    assert 1 + bs <= n_rows
    cs = jnp.concatenate([c_ctx[None, :], c, jnp.zeros((n_rows - 1 - bs, d), F32)], axis=0)
    mod = _ada(cs, w_ada[l], b_ada[l]).reshape(n_rows, 6, d)

    wi = w_in[l]
    cuts = np.cumsum([dm] * 4 + [ng, du, d]).tolist()
    wq = wi[:, :cuts[0]].astype(BF16)
    wkt = wi[:, cuts[0]:cuts[1]].T.astype(BF16)
    wv = wi[:, cuts[1]:cuts[2]].astype(BF16)
    wo = wi[:, cuts[2]:cuts[3]].astype(BF16)
    wgt = wi[:, cuts[3]:cuts[4]].T
    wu = wi[:, cuts[4]:cuts[5]].astype(BF16)
    wga = wi[:, cuts[5]:cuts[6]].astype(BF16)
    wgb = wi[:, cuts[6]:].astype(BF16)
    bg = b_gates[l].reshape(ng, 1)
    proj_w = (wq, wkt, wv, wo, wu, wga, wgb, wgt, bg)
    n_r = N_GROUPS + N_EXPERTS
    wr = jnp.concatenate([w_rg[l], w_re[l], jnp.zeros((d, LANES - n_r), F32)], axis=1)
    br = jnp.concatenate([b_rg[l], b_re[l], jnp.zeros((LANES - n_r,), F32)]).reshape(1, LANES)
    post_w = (w_a[l].astype(BF16), w_f[l].astype(BF16), w_out[l].astype(BF16), wr, br)
    lw = (g_pre1[l], proj_w, g_hn[l], post_w, g_post1[l], g_pre2[l],
          w_e1[l].astype(BF16), w_e3[l].astype(BF16), w_e2[l].astype(BF16), g_post2[l])
    consts = _tri_consts()

    y_prompt, (new_c, new_n, new_m) = _run_path(x_prompt, None, bp, sp, mod, (0, 0), None, lw, consts)
    pos = _grid_pos(ss, d)
    y_sample, _ = _run_path(x_sample, pos, bs, ss, mod, (1, 1), (state_C, state_n, state_m), lw, consts)

    new_state_c = new_c
    new_state_n = new_n.reshape(bp, depth, 2, N_HEADS, HEAD_DIM)
    new_state_m = new_m[..., 0, 0]
    return (y_prompt, y_sample, new_state_c, new_state_n, new_state_m)
```

```python
import functools
import math

import numpy as np
import jax
import jax.numpy as jnp
from jax import lax
from jax.experimental import pallas as pl
from jax.experimental.pallas import tpu as pltpu

F32 = jnp.float32
BF16 = jnp.bfloat16

N_HEADS = 8
HEAD_DIM = 128
N_FGROUPS = 4
FGROUP_DIM = 128
N_GROUPS = 4
EXPERTS_PER_GROUP = 8
N_EXPERTS = N_GROUPS * EXPERTS_PER_GROUP
GRID_W = 64
POS_BASE = 10000.0
EPS = 1e-6

V7X_VMEM_LIMIT_BYTES = 56 * 1024 * 1024
LANES = 128
CHUNK = 256
TOKEN_TILE = CHUNK


def _params(sem):
    return pltpu.CompilerParams(dimension_semantics=sem, vmem_limit_bytes=V7X_VMEM_LIMIT_BYTES)


def _const_spec(shape):
    n = len(shape)
    return pl.BlockSpec(shape, lambda *_: (0,) * n, pipeline_mode=pl.Buffered(1))


def _split2(x):
    hi = x.astype(BF16)
    lo = (x - hi.astype(F32)).astype(BF16)
    return hi, lo


def _split3(x):
    hi = x.astype(BF16)
    r = x - hi.astype(F32)
    mid = r.astype(BF16)
    lo = (r - mid.astype(F32)).astype(BF16)
    return hi, mid, lo


def _dot(a, b):
    return jnp.dot(a, b, preferred_element_type=F32)


def _dot_nt(a, b):
    return lax.dot_general(a, b, (((1,), (1,)), ((), ())), preferred_element_type=F32)


def _dot_f32(a, b, nt=False):
    d = _dot_nt if nt else _dot
    a1, a2 = _split2(a)
    b1, b2 = _split2(b)
    return d(a1, b1) + (d(a1, b2) + d(a2, b1))


def _dot_exact_rhs(a, b_bf16, nt=False):
    d = _dot_nt if nt else _dot
    a1, a2, a3 = _split3(a)
    return d(a1, b_bf16) + (d(a2, b_bf16) + d(a3, b_bf16))


def _rms(x, g):
    return x * lax.rsqrt(jnp.mean(x * x, axis=-1, keepdims=True) + EPS) * g


def _sigmoid(x):
    return 1.0 / (1.0 + jnp.exp(-x))


def _log_sigmoid(x):
    return jnp.minimum(x, 0.0) - jnp.log(1.0 + jnp.exp(-jnp.abs(x)))


def _ada_kernel(c_ref, w_ref, b_ref, o_ref):
    c = c_ref[...]
    s = (c * _sigmoid(c)).astype(BF16)
    o_ref[...] = _dot(s, w_ref[...].astype(BF16)) + b_ref[...]


def _ada(cs, w_ada, b_ada):
    rows, d = cs.shape
    n = w_ada.shape[1]
    tn = 1024
    return pl.pallas_call(
        _ada_kernel,
        grid=(n // tn,),
        in_specs=[pl.BlockSpec((rows, d), lambda j: (0, 0)),
                  pl.BlockSpec((d, tn), lambda j: (0, j)),
                  pl.BlockSpec((1, tn), lambda j: (0, j))],
        out_specs=pl.BlockSpec((rows, tn), lambda j: (0, j)),
        out_shape=jax.ShapeDtypeStruct((rows, n), F32),
        compiler_params=_params(("arbitrary",)),
        name="ada",
    )(cs, w_ada, b_ada.reshape(1, n))


def _proj_kernel(has_pos, *refs):
    if has_pos:
        x_ref, pos_ref = refs[0], refs[1]
        refs = refs[2:]
    else:
        x_ref = refs[0]
        refs = refs[1:]
    (mod_ref, g_ref, wq_ref, wkt_ref, wv_ref, wo_ref, wu_ref, wga_ref, wgb_ref, wgt_ref, bg_ref,
     tri_ref, q_ref, kt_ref, v_ref, o_ref, u_ref, ga_ref, gb_ref, gr_ref) = refs
    x = x_ref[...]
    if has_pos:
        x = x + pos_ref[...]
    sh = mod_ref[0:1, :]
    sc = mod_ref[1:2, :]
    h = _rms(x, g_ref[...]) * (1.0 + sc) + sh
    hb = h.astype(BF16)
    q_ref[...] = _dot(hb, wq_ref[...]).astype(BF16)
    kt_ref[...] = (_dot_nt(wkt_ref[...], hb) * (HEAD_DIM ** -0.5)).astype(BF16)
    v_ref[...] = _dot(hb, wv_ref[...]).astype(BF16)
    o_ref[...] = _dot(hb, wo_ref[...])
    u_ref[...] = _dot(hb, wu_ref[...]).astype(BF16)
    ga_ref[...] = _dot(hb, wga_ref[...])
    gb_ref[...] = _dot(hb, wgb_ref[...])
    g = _dot_f32(wgt_ref[...], h, nt=True) + bg_ref[...]
    nh = N_HEADS
    lf_f = _log_sigmoid(g[nh:2 * nh])
    lf_b = _log_sigmoid(g[3 * nh:4 * nh])
    b_f = _dot_exact_rhs(lf_f, tri_ref[0])
    b_b = _dot_exact_rhs(lf_b, tri_ref[1])
    gr_ref[0:nh, :] = g[0:nh]
    gr_ref[nh:2 * nh, :] = b_f
    gr_ref[2 * nh:3 * nh, :] = g[2 * nh:3 * nh]
    gr_ref[3 * nh:4 * nh, :] = b_b


def _mod_spec(d, tm, seq_len, mod_rows):
    row0, stride = mod_rows
    return pl.BlockSpec((None, 6, d), lambda i, *_: (row0 + stride * ((i * tm) // seq_len), 0, 0))


def _proj(x, pos, mod, mod_rows, seq_len, g_pre1, wts):
    n_tok, d = x.shape
    tm = TOKEN_TILE
    n_tiles = n_tok // tm
    tiles_per_seq = seq_len // tm
    has_pos = pos is not None
    (wq, wkt, wv, wo, wu, wga, wgb, wgt, bg, tri) = wts
    row = pl.BlockSpec((tm, d), lambda i: (i, 0))
    in_specs = [row]
    args = [x]
    if has_pos:
        in_specs.append(pl.BlockSpec((tm, d), lambda i: (i % tiles_per_seq, 0)))
        args.append(pos)
    in_specs += [_mod_spec(d, tm, seq_len, mod_rows), _const_spec((1, d))]
    in_specs += [_const_spec(w.shape) for w in wts]
    args += [mod, g_pre1.reshape(1, d)] + list(wts)
    n_gate_rows = 4 * N_HEADS
    du = wu.shape[1]
    out_shape = [jax.ShapeDtypeStruct((n_tok, d), BF16),
                 jax.ShapeDtypeStruct((n_tiles, d, tm), BF16),
                 jax.ShapeDtypeStruct((n_tok, d), BF16),
                 jax.ShapeDtypeStruct((n_tok, d), F32),
                 jax.ShapeDtypeStruct((n_tok, du), BF16),
                 jax.ShapeDtypeStruct((n_tok, d), F32),
                 jax.ShapeDtypeStruct((n_tok, d), F32),
                 jax.ShapeDtypeStruct((n_tiles, n_gate_rows, tm), F32)]
    out_specs = [row,
                 pl.BlockSpec((None, d, tm), lambda i: (i, 0, 0)),
                 row, row,
                 pl.BlockSpec((tm, du), lambda i: (i, 0)),
                 row, row,
                 pl.BlockSpec((None, n_gate_rows, tm), lambda i: (i, 0, 0))]
    return pl.pallas_call(
        functools.partial(_proj_kernel, has_pos),
        grid=(n_tiles,),
        in_specs=in_specs,
        out_specs=out_specs,
        out_shape=out_shape,
        compiler_params=_params(("parallel",)),
        name="proj",
    )(*args)


def _mlstm_kernel(n_chunks, hp, has_state, emit_state, *refs):
    it = iter(refs)
    if has_state:
        m0_ref = next(it)
    q_ref, kt_ref, v_ref, o_ref, ghn_ref, gr_ref = (next(it) for _ in range(6))
    if has_state:
        c0_ref, n0_ref = next(it), next(it)
    ha_ref = next(it)
    if emit_state:
        cout_ref, nout_ref, mout_ref = next(it), next(it), next(it)
    hs_ref, cn_ref, m_ref = (next(it) for _ in range(3))

    L = CHUNK
    nh, dh = N_HEADS, HEAD_DIM
    head0 = pl.program_id(1) * hp
    neg_inf = -jnp.inf

    sq_r = lax.broadcasted_iota(jnp.int32, (dh, dh), 0)
    sq_c = lax.broadcasted_iota(jnp.int32, (dh, dh), 1)
    for j in range(hp):
        for d in range(2):
            if has_state:
                n_col = jnp.sum(jnp.where(sq_r == sq_c, n0_ref[d, j], 0.0), axis=1, keepdims=True)
                cn_ref[j, d, :, 0:dh] = c0_ref[d, j]
                cn_ref[j, d, :, dh:2 * dh] = jnp.broadcast_to(n_col, (dh, dh))
                m0 = m0_ref[pl.program_id(0) * (2 * nh) + d * nh + head0 + j]
                m_ref[j, d] = jnp.full((8, LANES), m0, F32)
            else:
                cn_ref[j, d] = jnp.zeros((dh, 2 * dh), F32)
                m_ref[j, d] = jnp.zeros((8, LANES), F32)

    row_id = lax.broadcasted_iota(jnp.int32, (L, L), 0)
    col_id = lax.broadcasted_iota(jnp.int32, (L, L), 1)
    ones_blk = jnp.ones((L, dh), BF16)

    def chunk(j, d, c):
        t0 = pl.multiple_of(c * L, L)
        lanes = slice(j * dh, (j + 1) * dh)
        qc = q_ref[pl.ds(t0, L), lanes]
        ktc = kt_ref[c, lanes, :]
        vc = v_ref[pl.ds(t0, L), lanes]
        ig_row = gr_ref[c, pl.ds(2 * nh * d + head0 + j, 1), :]
        b_row = gr_ref[c, pl.ds(2 * nh * d + nh + head0 + j, 1), :]
        m_prev = m_ref[j, d][0:1, 0:1]
        if d == 0:
            b_end = b_row[:, L - 1:L]
            mask = col_id <= row_id
        else:
            b_end = b_row[:, 0:1]
            mask = col_id >= row_id
        a_row = ig_row - b_row
        a_max = jnp.max(jnp.where(mask, a_row, neg_inf), axis=1, keepdims=True)
        g = jnp.maximum(jnp.broadcast_to(a_max, (L, dh)), m_prev)
        b_col = jnp.broadcast_to(
            jnp.sum(jnp.where(row_id == col_id, b_row, 0.0), axis=1, keepdims=True), (L, dh))
        g_full = jnp.concatenate([g] * (L // dh), axis=1)
        s = _dot(qc, ktc) * jnp.exp(jnp.where(mask, a_row - g_full, neg_inf))
        decay = jnp.exp(m_prev - g)
        qcn = _dot(qc, cn_ref[j, d].astype(BF16))
        num = decay * qcn[:, 0:dh] + _dot(s.astype(BF16), vc)
        den = decay * qcn[:, dh:2 * dh] + jnp.sum(s, axis=1, keepdims=True)
        hch = num / jnp.maximum(jnp.abs(den), jnp.exp(-(b_col + g)))
        w_log = b_end + a_row
        m_new = jnp.maximum(b_end + m_prev, jnp.max(w_log, axis=1, keepdims=True))
        kw = (ktc.astype(F32) * jnp.exp(w_log - m_new)).astype(BF16)
        v_aug = jnp.concatenate([vc, ones_blk], axis=1)
        cn_ref[j, d] = jnp.exp(b_end + m_prev - m_new) * cn_ref[j, d] + _dot(kw, v_aug)
        m_ref[j, d] = jnp.broadcast_to(m_new, (8, LANES))
        return t0, hch

    if n_chunks == 1:
        for j in range(hp):
            _, h_f = chunk(j, 0, 0)
            _, h_b = chunk(j, 1, 0)
            hs_ref[:, j * dh:(j + 1) * dh] = h_f + h_b
    else:
        half = n_chunks // 2

        def first(i, carry):
            for j in range(hp):
                lanes = slice(j * dh, (j + 1) * dh)
                t_f, h_f = chunk(j, 0, i)
                t_b, h_b = chunk(j, 1, n_chunks - 1 - i)
                hs_ref[pl.ds(t_f, L), lanes] = h_f
                hs_ref[pl.ds(t_b, L), lanes] = h_b
            return carry

        def second(i, carry):
            for j in range(hp):
                lanes = slice(j * dh, (j + 1) * dh)
                t_f, h_f = chunk(j, 0, i)
                t_b, h_b = chunk(j, 1, n_chunks - 1 - i)
                hs_ref[pl.ds(t_f, L), lanes] += h_f
                hs_ref[pl.ds(t_b, L), lanes] += h_b
            return carry

        lax.fori_loop(0, half, first, 0)
        lax.fori_loop(half, n_chunks, second, 0)

    def finish(c, carry):
        t0 = pl.multiple_of(c * L, L)
        for j in range(hp):
            lanes = slice(j * dh, (j + 1) * dh)
            ha = hs_ref[pl.ds(t0, L), lanes]
            ha = ha * lax.rsqrt(jnp.mean(ha * ha, axis=-1, keepdims=True) + EPS)
            ha = ha * ghn_ref[:, lanes] * _sigmoid(o_ref[pl.ds(t0, L), lanes])
            ha_ref[pl.ds(t0, L), lanes] = ha.astype(BF16)
        return carry

    if n_chunks == 1:
        finish(0, 0)
    else:
        lax.fori_loop(0, n_chunks, finish, 0)

    if emit_state:
        for j in range(hp):
            for d in range(2):
                cout_ref[d, j] = cn_ref[j, d, :, 0:dh]
                n_rep = cn_ref[j, d, :, dh:2 * dh]
                nout_ref[d, j] = jnp.sum(jnp.where(sq_r == sq_c, n_rep, 0.0), axis=0, keepdims=True)
                mout_ref[d, j] = m_ref[j, d][0:1, :]


def _mlstm(q, kt, v, o, g_hn, gr, n_seq, seq_len, states, hp):
    n_tok, d = q.shape
    L = CHUNK
    n_chunks = seq_len // L
    has_state = states is not None
    emit_state = not has_state
    nh, dh = N_HEADS, HEAD_DIM
    assert n_chunks == 1 or n_chunks % 2 == 0
    assert nh % hp == 0
    wd = hp * dh

    in_specs = []
    args = []
    if has_state:
        state_c, state_n, state_m = states
        in_specs.append(pl.BlockSpec(memory_space=pltpu.SMEM))
        args.append(state_m.reshape(-1))
    tok = pl.BlockSpec((seq_len, wd), lambda b, h: (b, h))
    in_specs += [tok,
                 pl.BlockSpec((n_chunks, wd, L), lambda b, h: (b, h, 0)),
                 tok, tok,
                 pl.BlockSpec((1, wd), lambda b, h: (0, h)),
                 pl.BlockSpec((n_chunks, 4 * nh, L), lambda b, h: (b, 0, 0))]
    args += [q, kt, v, o, g_hn.reshape(1, d), gr]
    st_c = pl.BlockSpec((None, None, 2, hp, dh, dh), lambda b, h: (b, 0, 0, h, 0, 0))
    st_v = pl.BlockSpec((None, None, 2, hp, 1, dh), lambda b, h: (b, 0, 0, h, 0, 0))
    if has_state:
        in_specs += [st_c, st_v]
        args += [state_c, state_n.reshape(n_seq, 1, 2, nh, 1, dh)]
    out_shape = [jax.ShapeDtypeStruct((n_tok, d), BF16)]
    out_specs = [tok]
    if emit_state:
        out_shape += [jax.ShapeDtypeStruct((n_seq, 1, 2, nh, dh, dh), F32),
                      jax.ShapeDtypeStruct((n_seq, 1, 2, nh, 1, dh), F32),
                      jax.ShapeDtypeStruct((n_seq, 1, 2, nh, 1, LANES), F32)]
        out_specs += [st_c, st_v, st_v]
    scratch = [pltpu.VMEM((seq_len, wd), F32),
               pltpu.VMEM((hp, 2, dh, 2 * dh), F32),
               pltpu.VMEM((hp, 2, 8, LANES), F32)]
    return pl.pallas_call(
        functools.partial(_mlstm_kernel, n_chunks, hp, has_state, emit_state),
        grid=(n_seq, nh // hp),
        in_specs=in_specs,
        out_specs=out_specs,
        out_shape=out_shape,
        scratch_shapes=scratch,
        compiler_params=_params(("parallel", "parallel")),
        name="mlstm",
    )(*args)


def _fnet_kernel(seq_len, u_ref, f_ref, cs_ref, y_ref, ab_ref):
    T = seq_len

    @pl.when(pl.program_id(1) == 0)
    def _():
        for g in range(N_FGROUPS):
            ug = u_ref[:, g * FGROUP_DIM:(g + 1) * FGROUP_DIM]
            ab_ref[0:T, g * FGROUP_DIM:(g + 1) * FGROUP_DIM] = _dot(ug, cs_ref[0]).astype(BF16)
            ab_ref[T:2 * T, g * FGROUP_DIM:(g + 1) * FGROUP_DIM] = _dot(ug, cs_ref[1]).astype(BF16)

    scale = 1.0 / math.sqrt(T * FGROUP_DIM)
    y_ref[...] = (_dot(f_ref[...], ab_ref[...]) * scale).astype(BF16)


def _fnet(u, f_mat, cs_mat, n_seq, seq_len):
    n_tok, du = u.shape
    tr = min(seq_len, 512)
    return pl.pallas_call(
        functools.partial(_fnet_kernel, seq_len),
        grid=(n_seq, seq_len // tr),
        in_specs=[pl.BlockSpec((seq_len, du), lambda b, r: (b, 0)),
                  pl.BlockSpec((tr, 2 * seq_len), lambda b, r: (r, 0)),
                  _const_spec(cs_mat.shape)],
        out_specs=pl.BlockSpec((tr, du), lambda b, r: (b * (seq_len // tr) + r, 0)),
        out_shape=jax.ShapeDtypeStruct((n_tok, du), BF16),
        scratch_shapes=[pltpu.VMEM((2 * seq_len, du), BF16)],
        compiler_params=_params(("parallel", "arbitrary")),
        name="fnet",
    )(u, f_mat, cs_mat)


def _post_kernel(has_pos, *refs):
    if has_pos:
        x_ref, pos_ref = refs[0], refs[1]
        refs = refs[2:]
    else:
        x_ref = refs[0]
        refs = refs[1:]
    (ha_ref, yf_ref, ga_ref, gb_ref, mod_ref, gp1_ref, gp2_ref, wa_ref, wf_ref, wout_ref, wr_ref, br_ref,
     x1_ref, h2_ref, lg_ref) = refs
    x = x_ref[...]
    if has_pos:
        x = x + pos_ref[...]
    ya = _dot(ha_ref[...], wa_ref[...])
    yf = _dot(yf_ref[...], wf_ref[...])
    mix_in = _sigmoid(ga_ref[...]) * ya + _sigmoid(gb_ref[...]) * yf
    mix = _dot(mix_in.astype(BF16), wout_ref[...])
    gt1 = mod_ref[2:3, :]
    sh2 = mod_ref[3:4, :]
    sc2 = mod_ref[4:5, :]
    x1 = x + gt1 * _rms(mix, gp1_ref[...])
    h2 = _rms(x1, gp2_ref[...]) * (1.0 + sc2) + sh2
    x1_ref[...] = x1
    h2_ref[...] = h2
    lg_ref[...] = _dot_f32(wr_ref[...], h2, nt=True) + br_ref[...]


def _post(x, pos, ha, yf, ga, gb, mod, mod_rows, seq_len, g_post1, g_pre2, wts):
    n_tok, d = x.shape
    tm = TOKEN_TILE
    tiles_per_seq = seq_len // tm
    has_pos = pos is not None
    (wa, wf, wout, wr, br) = wts
    du = yf.shape[1]
    row = pl.BlockSpec((tm, d), lambda i: (i, 0))
    in_specs = [row]
    args = [x]
    if has_pos:
        in_specs.append(pl.BlockSpec((tm, d), lambda i: (i % tiles_per_seq, 0)))
        args.append(pos)
    in_specs += [row, pl.BlockSpec((tm, du), lambda i: (i, 0)), row, row,
                 _mod_spec(d, tm, seq_len, mod_rows),
                 _const_spec((1, d)), _const_spec((1, d))]
    in_specs += [_const_spec(w.shape) for w in wts]
    args += [ha, yf, ga, gb, mod, g_post1.reshape(1, d), g_pre2.reshape(1, d)] + list(wts)
    return pl.pallas_call(
        functools.partial(_post_kernel, has_pos),
        grid=(n_tok // tm,),
        in_specs=in_specs,
        out_specs=[row, row, pl.BlockSpec((None, LANES, tm), lambda i: (i, 0, 0))],
        out_shape=[jax.ShapeDtypeStruct((n_tok, d), F32),
                   jax.ShapeDtypeStruct((n_tok, d), F32),
                   jax.ShapeDtypeStruct((n_tok // tm, LANES, tm), F32)],
        compiler_params=_params(("parallel",)),
        name="post",
    )(*args)


FFN_TILE = 256
PLAN_ITEMS = LANES


def _route_rows(lg):
    tm = lg.shape[1]
    sub = lax.broadcasted_iota(jnp.int32, (LANES, tm), 0).astype(F32)
    big = float(LANES)
    is_g = sub < N_GROUPS
    gl = jnp.where(is_g, lg, -jnp.inf)
    mx = jnp.max(gl, axis=0, keepdims=True)
    z = jnp.sum(jnp.where(is_g, jnp.exp(gl - mx), 0.0), axis=0, keepdims=True)
    p_sel = 1.0 / z
    g_sel = jnp.min(jnp.where(gl == mx, sub, big), axis=0, keepdims=True)
    lo = N_GROUPS + EXPERTS_PER_GROUP * g_sel
    in_grp = (sub >= lo) & (sub < lo + EXPERTS_PER_GROUP)
    le = jnp.where(in_grp, lg, -jnp.inf)
    v1 = jnp.max(le, axis=0, keepdims=True)
    i1 = jnp.min(jnp.where(le == v1, sub, big), axis=0, keepdims=True)
    le2 = jnp.where(sub == i1, -jnp.inf, le)
    v2 = jnp.max(le2, axis=0, keepdims=True)
    i2 = jnp.min(jnp.where(le2 == v2, sub, big), axis=0, keepdims=True)
    e2 = jnp.exp(v2 - v1)
    w1 = p_sel / (1.0 + e2)
    w2 = p_sel * e2 / (1.0 + e2)
    return i1 - N_GROUPS, i2 - N_GROUPS, w1, w2


def _route_kernel(n_tiles, lg_ref, etri_ref, ttri_ref, pos_ref, w_ref, plan_ref, meta_ref, cnt_ref, run_ref, base_ref):
    p = pl.program_id(0)
    i = pl.program_id(1)
    tm = lg_ref.shape[1]
    sub = lax.broadcasted_iota(jnp.int32, (LANES, tm), 0).astype(F32)
    tf = float(FFN_TILE)

    @pl.when(p == 0)
    def _():
        e_a, e_b, w_a, w_b = _route_rows(lg_ref[...])
        meta_ref[i, 0:1, :] = e_a
        meta_ref[i, 1:2, :] = e_b
        meta_ref[i, 2:3, :] = w_a
        meta_ref[i, 3:4, :] = w_b
        meta_ref[i, 4:8, :] = jnp.zeros((4, tm), F32)
        oh = jnp.where((sub == e_a) | (sub == e_b), 1.0, 0.0)

        @pl.when(i == 0)
        def _():
            cnt_ref[...] = jnp.zeros_like(cnt_ref)

        cnt_ref[...] += jnp.broadcast_to(jnp.sum(oh, axis=1, keepdims=True), cnt_ref.shape)

    @pl.when(p == 1)
    def _():
        @pl.when(i == 0)
        def _():
            c1, c2, c3 = _split3(cnt_ref[...])
            base_ref[...] = _dot(etri_ref[...], c1) + (_dot(etri_ref[...], c2) + _dot(etri_ref[...], c3))
            run_ref[...] = jnp.zeros_like(run_ref)

        e_a = meta_ref[i, 0:1, :]
        e_b = meta_ref[i, 1:2, :]
        oh_a = jnp.where(sub == e_a, 1.0, 0.0)
        oh_b = jnp.where(sub == e_b, 1.0, 0.0)
        oh = oh_a + oh_b
        before = _dot(oh.astype(BF16), ttri_ref[...]) + run_ref[...] + base_ref[...]
        pos_ref[...] = jnp.zeros(pos_ref.shape, jnp.int32)
        pos_ref[0:1, :] = jnp.sum(oh_a * before, axis=0, keepdims=True).astype(jnp.int32)
        pos_ref[1:2, :] = jnp.sum(oh_b * before, axis=0, keepdims=True).astype(jnp.int32)
        w_ref[...] = meta_ref[i]
        run_ref[...] += jnp.broadcast_to(jnp.sum(oh, axis=1, keepdims=True), run_ref.shape)

        @pl.when(i == n_tiles - 1)
        def _():
            cnt = cnt_ref[:, 0:PLAN_ITEMS]
            base = base_ref[:, 0:PLAN_ITEMS]
            e_id = lax.broadcasted_iota(jnp.int32, (LANES, PLAN_ITEMS), 0)
            is_e = e_id < N_EXPERTS
            t_lo = jnp.floor(base / tf)
            t_hi = jnp.floor((base + cnt - 1.0) / tf)
            n_items = jnp.where(is_e & (cnt > 0.0), t_hi - t_lo + 1.0, 0.0)
            i_start = _dot(etri_ref[...], n_items.astype(BF16))
            item = lax.broadcasted_iota(jnp.int32, (LANES, PLAN_ITEMS), 1).astype(F32)
            sel = (i_start <= item) & (item < i_start + n_items)
            tile_e = t_lo + item - i_start
            off = base - tile_e * tf

            def pick(v):
                return jnp.sum(jnp.where(sel, v, 0.0), axis=0, keepdims=True).astype(jnp.int32)

            plan_ref[...] = jnp.zeros(plan_ref.shape, jnp.int32)
            plan_ref[0:1, :] = pick(e_id.astype(F32))
            plan_ref[1:2, :] = pick(tile_e)
            plan_ref[2:3, :] = pick(jnp.maximum(off, 0.0))
            plan_ref[3:4, :] = pick(jnp.minimum(off + cnt, tf))
            plan_ref[4:5, :] = jnp.sum(n_items, axis=0, keepdims=True).astype(jnp.int32)


def _route(lg, tris):
    expert_tri, token_tri = tris
    n_tiles, _, tm = lg.shape
    return pl.pallas_call(
        functools.partial(_route_kernel, n_tiles),
        grid=(2, n_tiles),
        in_specs=[pl.BlockSpec((None, LANES, tm), lambda p, i: (i, 0, 0)),
                  _const_spec(expert_tri.shape), _const_spec(token_tri.shape)],
        out_specs=[pl.BlockSpec((None, 8, tm), lambda p, i: (i * p, 0, 0)),
                   pl.BlockSpec((None, 8, tm), lambda p, i: (i * p, 0, 0)),
                   pl.BlockSpec((8, PLAN_ITEMS), lambda p, i: (0, 0))],
        out_shape=[jax.ShapeDtypeStruct((n_tiles, 8, tm), jnp.int32),
                   jax.ShapeDtypeStruct((n_tiles, 8, tm), F32),
                   jax.ShapeDtypeStruct((8, PLAN_ITEMS), jnp.int32)],
        scratch_shapes=[pltpu.VMEM((n_tiles, 8, tm), F32),
                        pltpu.VMEM((LANES, tm), F32),
                        pltpu.VMEM((LANES, tm), F32),
                        pltpu.VMEM((LANES, tm), F32)],
        compiler_params=_params(("arbitrary", "arbitrary")),
        name="route",
    )(lg, expert_tri, token_tri)


ROW_UNROLL = 8


def _dispatch_kernel(tm, pos_ref, h_ref, xs_ref, sem):
    i = pl.program_id(0)
    n = pl.num_programs(0)

    def row_copy(tok, dst):
        return pltpu.make_async_copy(h_ref.at[pl.ds(tok, 1)], xs_ref.at[pl.ds(dst, 1)], sem)

    def issue(g, carry):
        for u in range(ROW_UNROLL):
            t = g * ROW_UNROLL + u
            row_copy(i * tm + t, pos_ref[0, t]).start()
            row_copy(i * tm + t, pos_ref[1, t]).start()
        return carry

    def drain(g, carry):
        for _ in range(2 * ROW_UNROLL):
            row_copy(0, 0).wait()
        return carry

    lax.fori_loop(0, tm // ROW_UNROLL, issue, 0)

    @pl.when(i > 0)
    def _():
        lax.fori_loop(0, tm // ROW_UNROLL, drain, 0)

    @pl.when(i == n - 1)
    def _():
        lax.fori_loop(0, tm // ROW_UNROLL, drain, 0)


def _dispatch(pos, h2, n_rows):
    n_tiles, _, tm = pos.shape
    d = h2.shape[1]
    return pl.pallas_call(
        functools.partial(_dispatch_kernel, tm),
        grid=(n_tiles,),
        in_specs=[pl.BlockSpec((None, 8, tm), lambda i: (i, 0, 0), memory_space=pltpu.SMEM),
                  pl.BlockSpec(memory_space=pl.ANY)],
        out_specs=pl.BlockSpec(memory_space=pl.ANY),
        out_shape=jax.ShapeDtypeStruct((n_rows, d), F32),
        scratch_shapes=[pltpu.SemaphoreType.DMA],
        compiler_params=_params(("arbitrary",)),
        name="dispatch",
    )(pos, h2)


def _ffn_kernel(owner_ref, tile_ref, lo_ref, hi_ref, used_ref, x_ref, w1_ref, w3_ref, w2_ref, y_ref):
    w = pl.program_id(0)

    @pl.when(w < used_ref[0])
    def _():
        x = x_ref[...].astype(BF16)
        a = _dot(x, w1_ref[...])
        b = _dot(x, w3_ref[...])
        hid = (a * _sigmoid(a)) * b
        y = _dot(hid.astype(BF16), w2_ref[...])
        row = lax.broadcasted_iota(jnp.int32, y.shape, 0)
        mine = (row >= lo_ref[w]) & (row < hi_ref[w])
        first = (w == 0) | (tile_ref[w] != tile_ref[jnp.maximum(w - 1, 0)])

        @pl.when(first)
        def _():
            y_ref[...] = jnp.where(mine, y, 0.0)

        @pl.when(jnp.logical_not(first))
        def _():
            y_ref[...] = jnp.where(mine, y, y_ref[...])


def _ffn(plan, xs, w1, w3, w2):
    n_rows, d = xs.shape
    tf = FFN_TILE
    de = w1.shape[1] // N_EXPERTS
    n_items = n_rows // tf + N_EXPERTS - 1
    assert n_items <= PLAN_ITEMS

    def item(w, used):
        return jnp.minimum(w, used[0] - 1)

    grid_spec = pltpu.PrefetchScalarGridSpec(
        num_scalar_prefetch=5,
        grid=(n_items,),
        in_specs=[pl.BlockSpec((tf, d), lambda w, own, til, lo, hi, used: (til[item(w, used)], 0)),
                  pl.BlockSpec((d, de), lambda w, own, til, lo, hi, used: (0, own[item(w, used)])),
                  pl.BlockSpec((d, de), lambda w, own, til, lo, hi, used: (0, own[item(w, used)])),
                  pl.BlockSpec((de, d), lambda w, own, til, lo, hi, used: (own[item(w, used)], 0))],
        out_specs=pl.BlockSpec((tf, d), lambda w, own, til, lo, hi, used: (til[item(w, used)], 0)),
    )
    return pl.pallas_call(
        _ffn_kernel,
        grid_spec=grid_spec,
        out_shape=jax.ShapeDtypeStruct((n_rows, d), F32),
        compiler_params=_params(("arbitrary",)),
        name="ffn",
    )(plan[0], plan[1], plan[2], plan[3], plan[4, 0:1], xs, w1, w3, w2)


def _combine_kernel(tm, pos_ref, nxt_ref, w_ref, x1_ref, mod_ref, gp_ref, ys_ref, y_ref, buf_ref, sem):
    i = pl.program_id(0)
    n = pl.num_programs(0)
    slot = i % 2

    def row_copy(src, k, t, s):
        return pltpu.make_async_copy(ys_ref.at[pl.ds(src, 1)], buf_ref.at[s, k, pl.ds(t, 1)], sem.at[s])

    def issue(p_ref, s):
        def body(g, carry):
            for u in range(ROW_UNROLL):
                t = g * ROW_UNROLL + u
                row_copy(p_ref[0, t], 0, t, s).start()
                row_copy(p_ref[1, t], 1, t, s).start()
            return carry
        lax.fori_loop(0, tm // ROW_UNROLL, body, 0)

    @pl.when(i == 0)
    def _():
        issue(pos_ref, 0)

    @pl.when(i + 1 < n)
    def _():
        issue(nxt_ref, 1 - slot)

    def drain(g, carry):
        for _ in range(2 * ROW_UNROLL):
            row_copy(0, 0, 0, slot).wait()
        return carry

    lax.fori_loop(0, tm // ROW_UNROLL, drain, 0)

    r = lax.broadcasted_iota(jnp.int32, (tm, tm), 0)
    c = lax.broadcasted_iota(jnp.int32, (tm, tm), 1)
    d = y_ref.shape[1]
    wa = jnp.broadcast_to(jnp.sum(jnp.where(r == c, w_ref[2:3, :], 0.0), axis=1, keepdims=True), (tm, d))
    wb = jnp.broadcast_to(jnp.sum(jnp.where(r == c, w_ref[3:4, :], 0.0), axis=1, keepdims=True), (tm, d))
    moe = wa * buf_ref[slot, 0] + wb * buf_ref[slot, 1]
    gt2 = mod_ref[5:6, :]
    y_ref[...] = x1_ref[...] + gt2 * _rms(moe, gp_ref[...])


def _combine(pos, w, x1, mod, mod_rows, seq_len, g_post2, ys):
    n_tiles, _, tm = pos.shape
    n_tok, d = x1.shape
    row = pl.BlockSpec((tm, d), lambda i: (i, 0))
    return pl.pallas_call(
        functools.partial(_combine_kernel, tm),
        grid=(n_tiles,),
        in_specs=[pl.BlockSpec((None, 8, tm), lambda i: (i, 0, 0), memory_space=pltpu.SMEM),
                  pl.BlockSpec((None, 8, tm), lambda i: (jnp.minimum(i + 1, n_tiles - 1), 0, 0),
                               memory_space=pltpu.SMEM),
                  pl.BlockSpec((None, 8, tm), lambda i: (i, 0, 0)),
                  row,
                  _mod_spec(d, tm, seq_len, mod_rows),
                  _const_spec((1, d)),
                  pl.BlockSpec(memory_space=pl.ANY)],
        out_specs=row,
        out_shape=jax.ShapeDtypeStruct((n_tok, d), F32),
        scratch_shapes=[pltpu.VMEM((2, 2, tm, d), F32), pltpu.SemaphoreType.DMA((2,))],
        compiler_params=_params(("arbitrary",)),
        name="combine",
    )(pos, pos, w, x1, mod, g_post2.reshape(1, d), ys)


def _moe(h2, lg, x1, mod, mod_rows, seq_len, g_post2, w1, w3, w2, route_tri):
    n_tok, d = x1.shape
    n_rows = 2 * n_tok
    assert n_rows % FFN_TILE == 0
    pos, w, plan = _route(lg, route_tri)
    xs = _dispatch(pos, h2, n_rows)
    ys = _ffn(plan, xs, w1, w3, w2)
    return _combine(pos, w, x1, mod, mod_rows, seq_len, g_post2, ys)


def _grid_pos(n_tokens, d_model):
    rows = n_tokens // GRID_W
    row = jnp.repeat(jnp.arange(rows, dtype=F32), GRID_W)
    col = jnp.tile(jnp.arange(GRID_W, dtype=F32), rows)
    n_freq = d_model // 4
    freq = jnp.exp(-math.log(POS_BASE) * jnp.arange(n_freq, dtype=F32) / n_freq)

    def enc(p):
        a = p[:, None] * freq[None, :]
        return jnp.concatenate([jnp.sin(a), jnp.cos(a)], axis=-1)

    return jnp.concatenate([enc(row), enc(col)], axis=-1)


def _dft_cos_sin(n):
    k = np.arange(n, dtype=np.int64)
    ang = 2.0 * np.pi * ((k[:, None] * k[None, :]) % n).astype(np.float64) / n
    return np.cos(ang), np.sin(ang)


def _fnet_consts(seq_len):
    ct, st = _dft_cos_sin(seq_len)
    f_mat = jnp.asarray(np.concatenate([ct, -st], axis=1), dtype=F32).astype(BF16)
    cc, sc = _dft_cos_sin(FGROUP_DIM)
    cs_mat = jnp.asarray(np.stack([cc, sc]), dtype=F32).astype(BF16)
    return f_mat, cs_mat


def _tri_consts():
    i = np.arange(CHUNK)
    prefix = (i[:, None] <= i[None, :]).astype(np.float32)
    suffix = (i[:, None] >= i[None, :]).astype(np.float32)
    tri = jnp.asarray(np.stack([prefix, suffix]), dtype=BF16)
    e = np.arange(LANES)
    expert_tri = jnp.asarray((e[None, :] < e[:, None]).astype(np.float32), dtype=BF16)
    t = np.arange(TOKEN_TILE)
    token_tri = jnp.asarray((t[:, None] < t[None, :]).astype(np.float32), dtype=BF16)
    return tri, (expert_tri, token_tri)


def _run_path(x, pos, n_seq, seq_len, mod, mod_rows, states, lw, consts):
    d = x.shape[-1]
    (g_pre1, proj_w, g_hn, post_w, g_post1, g_pre2, w_e1, w_e3, w_e2, g_post2) = lw
    tri, route_tris = consts
    x2 = x.reshape(n_seq * seq_len, d)
    q, kt, v, o, u, ga, gb, gr = _proj(x2, pos, mod, mod_rows, seq_len, g_pre1, proj_w + (tri,))
    heads_per_step = N_HEADS if seq_len == CHUNK else 2
    outs = _mlstm(q, kt, v, o, g_hn, gr, n_seq, seq_len, states, heads_per_step)
    ha = outs[0]
    f_mat, cs_mat = _fnet_consts(seq_len)
    yf = _fnet(u, f_mat, cs_mat, n_seq, seq_len)
    x1, h2, logits = _post(x2, pos, ha, yf, ga, gb, mod, mod_rows, seq_len, g_post1, g_pre2, post_w)
    y = _moe(h2, logits, x1, mod, mod_rows, seq_len, g_post2, w_e1, w_e3, w_e2, route_tris)
    return y.reshape(n_seq, seq_len, d), outs[1:]


def kernel(x_prompt, x_sample, c, state_C, state_n, state_m, c_ctx, w_ada, b_ada, g_pre1, w_in, b_gates, g_hn,
           w_a, w_f, w_out, g_post1, g_pre2, w_rg, b_rg, w_re, b_re, w_e1, w_e3, w_e2, g_post2):
    bp, sp, d = x_prompt.shape
    bs, ss, _ = x_sample.shape
    depth = w_in.shape[0]
    assert depth == 1
    l = 0
    dm = N_HEADS * HEAD_DIM
    du = N_FGROUPS * FGROUP_DIM
    ng = 4 * N_HEADS

    n_rows = 8
    assert 1 + bs <= n_rows
    cs = jnp.concatenate([c_ctx[None, :], c, jnp.zeros((n_rows - 1 - bs, d), F32)], axis=0)
    mod = _ada(cs, w_ada[l], b_ada[l]).reshape(n_rows, 6, d)

    wi = w_in[l]
    cuts = np.cumsum([dm] * 4 + [ng, du, d]).tolist()
    wq = wi[:, :cuts[0]].astype(BF16)
    wkt = wi[:, cuts[0]:cuts[1]].T.astype(BF16)
    wv = wi[:, cuts[1]:cuts[2]].astype(BF16)
    wo = wi[:, cuts[2]:cuts[3]].astype(BF16)
    wgt = wi[:, cuts[3]:cuts[4]].T
    wu = wi[:, cuts[4]:cuts[5]].astype(BF16)
    wga = wi[:, cuts[5]:cuts[6]].astype(BF16)
    wgb = wi[:, cuts[6]:].astype(BF16)
    bg = b_gates[l].reshape(ng, 1)
    proj_w = (wq, wkt, wv, wo, wu, wga, wgb, wgt, bg)
    n_r = N_GROUPS + N_EXPERTS
    wr = jnp.concatenate([w_rg[l], w_re[l], jnp.zeros((d, LANES - n_r), F32)], axis=1).T
    br = jnp.concatenate([b_rg[l], b_re[l], jnp.zeros((LANES - n_r,), F32)]).reshape(LANES, 1)
    post_w = (w_a[l].astype(BF16), w_f[l].astype(BF16), w_out[l].astype(BF16), wr, br)
    lw = (g_pre1[l], proj_w, g_hn[l], post_w, g_post1[l], g_pre2[l],
          w_e1[l].astype(BF16), w_e3[l].astype(BF16), w_e2[l].astype(BF16), g_post2[l])
    consts = _tri_consts()

    y_prompt, (new_c, new_n, new_m) = _run_path(x_prompt, None, bp, sp, mod, (0, 0), None, lw, consts)
    pos = _grid_pos(ss, d)
    y_sample, _ = _run_path(x_sample, pos, bs, ss, mod, (1, 1), (state_C, state_n, state_m), lw, consts)

    new_state_c = new_c
    new_state_n = new_n.reshape(bp, depth, 2, N_HEADS, HEAD_DIM)
    new_state_m = new_m[..., 0, 0]
    return (y_prompt, y_sample, new_state_c, new_state_n, new_state_m)
```

```python
import functools
import math

import numpy as np
import jax
import jax.numpy as jnp
from jax import lax
from jax.experimental import pallas as pl
from jax.experimental.pallas import tpu as pltpu

F32 = jnp.float32
BF16 = jnp.bfloat16

N_HEADS = 8
HEAD_DIM = 128
N_FGROUPS = 4
FGROUP_DIM = 128
N_GROUPS = 4
EXPERTS_PER_GROUP = 8
N_EXPERTS = N_GROUPS * EXPERTS_PER_GROUP
GRID_W = 64
POS_BASE = 10000.0
EPS = 1e-6

V7X_VMEM_LIMIT_BYTES = 56 * 1024 * 1024
LANES = 128
CHUNK = 256
TOKEN_TILE = CHUNK


def _params(sem):
    return pltpu.CompilerParams(dimension_semantics=sem, vmem_limit_bytes=V7X_VMEM_LIMIT_BYTES)


def _const_spec(shape):
    n = len(shape)
    return pl.BlockSpec(shape, lambda *_: (0,) * n, pipeline_mode=pl.Buffered(1))


def _split2(x):
    hi = x.astype(BF16)
    lo = (x - hi.astype(F32)).astype(BF16)
    return hi, lo


def _split3(x):
    hi = x.astype(BF16)
    r = x - hi.astype(F32)
    mid = r.astype(BF16)
    lo = (r - mid.astype(F32)).astype(BF16)
    return hi, mid, lo


def _dot(a, b):
    return jnp.dot(a, b, preferred_element_type=F32)


def _dot_nt(a, b):
    return lax.dot_general(a, b, (((1,), (1,)), ((), ())), preferred_element_type=F32)


def _dot_f32(a, b, nt=False):
    d = _dot_nt if nt else _dot
    a1, a2 = _split2(a)
    b1, b2 = _split2(b)
    return d(a1, b1) + (d(a1, b2) + d(a2, b1))


def _dot_exact_rhs(a, b_bf16, nt=False):
    d = _dot_nt if nt else _dot
    a1, a2, a3 = _split3(a)
    return d(a1, b_bf16) + (d(a2, b_bf16) + d(a3, b_bf16))


def _rms(x, g):
    return x * lax.rsqrt(jnp.mean(x * x, axis=-1, keepdims=True) + EPS) * g


def _sigmoid(x):
    return 1.0 / (1.0 + jnp.exp(-x))


def _log_sigmoid(x):
    return jnp.minimum(x, 0.0) - jnp.log(1.0 + jnp.exp(-jnp.abs(x)))


def _ada_kernel(c_ref, w_ref, b_ref, o_ref):
    c = c_ref[...]
    s = (c * _sigmoid(c)).astype(BF16)
    o_ref[...] = _dot(s, w_ref[...].astype(BF16)) + b_ref[...]


def _ada(cs, w_ada, b_ada):
    rows, d = cs.shape
    n = w_ada.shape[1]
    tn = 1024
    return pl.pallas_call(
        _ada_kernel,
        grid=(n // tn,),
        in_specs=[pl.BlockSpec((rows, d), lambda j: (0, 0)),
                  pl.BlockSpec((d, tn), lambda j: (0, j)),
                  pl.BlockSpec((1, tn), lambda j: (0, j))],
        out_specs=pl.BlockSpec((rows, tn), lambda j: (0, j)),
        out_shape=jax.ShapeDtypeStruct((rows, n), F32),
        compiler_params=_params(("arbitrary",)),
        name="ada",
    )(cs, w_ada, b_ada.reshape(1, n))


def _proj_kernel(has_pos, *refs):
    if has_pos:
        x_ref, pos_ref = refs[0], refs[1]
        refs = refs[2:]
    else:
        x_ref = refs[0]
        refs = refs[1:]
    (mod_ref, g_ref, wq_ref, wkt_ref, wv_ref, wo_ref, wu_ref, wga_ref, wgb_ref, wgt_ref, bg_ref,
     tri_ref, q_ref, kt_ref, v_ref, o_ref, u_ref, ga_ref, gb_ref, gr_ref) = refs
    x = x_ref[...]
    if has_pos:
        x = x + pos_ref[...]
    sh = mod_ref[0:1, :]
    sc = mod_ref[1:2, :]
    h = _rms(x, g_ref[...]) * (1.0 + sc) + sh
    hb = h.astype(BF16)
    q_ref[...] = _dot(hb, wq_ref[...]).astype(BF16)
    kt_ref[...] = (_dot_nt(wkt_ref[...], hb) * (HEAD_DIM ** -0.5)).astype(BF16)
    v_ref[...] = _dot(hb, wv_ref[...]).astype(BF16)
    o_ref[...] = _dot(hb, wo_ref[...])
    u_ref[...] = _dot(hb, wu_ref[...]).astype(BF16)
    ga_ref[...] = _dot(hb, wga_ref[...])
    gb_ref[...] = _dot(hb, wgb_ref[...])
    g = _dot_f32(wgt_ref[...], h, nt=True) + bg_ref[...]
    nh = N_HEADS
    lf_f = _log_sigmoid(g[nh:2 * nh])
    lf_b = _log_sigmoid(g[3 * nh:4 * nh])
    b_f = _dot_exact_rhs(lf_f, tri_ref[0])
    b_b = _dot_exact_rhs(lf_b, tri_ref[1])
    gr_ref[0:nh, :] = g[0:nh]
    gr_ref[nh:2 * nh, :] = b_f
    gr_ref[2 * nh:3 * nh, :] = g[2 * nh:3 * nh]
    gr_ref[3 * nh:4 * nh, :] = b_b


def _mod_spec(d, tm, seq_len, mod_rows):
    row0, stride = mod_rows
    return pl.BlockSpec((None, 6, d), lambda i, *_: (row0 + stride * ((i * tm) // seq_len), 0, 0))


def _proj(x, pos, mod, mod_rows, seq_len, g_pre1, wts):
    n_tok, d = x.shape
    tm = TOKEN_TILE
    n_tiles = n_tok // tm
    tiles_per_seq = seq_len // tm
    has_pos = pos is not None
    (wq, wkt, wv, wo, wu, wga, wgb, wgt, bg, tri) = wts
    row = pl.BlockSpec((tm, d), lambda i: (i, 0))
    in_specs = [row]
    args = [x]
    if has_pos:
        in_specs.append(pl.BlockSpec((tm, d), lambda i: (i % tiles_per_seq, 0)))
        args.append(pos)
    in_specs += [_mod_spec(d, tm, seq_len, mod_rows), _const_spec((1, d))]
    in_specs += [_const_spec(w.shape) for w in wts]
    args += [mod, g_pre1.reshape(1, d)] + list(wts)
    n_gate_rows = 4 * N_HEADS
    du = wu.shape[1]
    out_shape = [jax.ShapeDtypeStruct((n_tok, d), BF16),
                 jax.ShapeDtypeStruct((n_tiles, d, tm), BF16),
                 jax.ShapeDtypeStruct((n_tok, d), BF16),
                 jax.ShapeDtypeStruct((n_tok, d), F32),
                 jax.ShapeDtypeStruct((n_tok, du), BF16),
                 jax.ShapeDtypeStruct((n_tok, d), F32),
                 jax.ShapeDtypeStruct((n_tok, d), F32),
                 jax.ShapeDtypeStruct((n_tiles, n_gate_rows, tm), F32)]
    out_specs = [row,
                 pl.BlockSpec((None, d, tm), lambda i: (i, 0, 0)),
                 row, row,
                 pl.BlockSpec((tm, du), lambda i: (i, 0)),
                 row, row,
                 pl.BlockSpec((None, n_gate_rows, tm), lambda i: (i, 0, 0))]
    return pl.pallas_call(
        functools.partial(_proj_kernel, has_pos),
        grid=(n_tiles,),
        in_specs=in_specs,
        out_specs=out_specs,
        out_shape=out_shape,
        compiler_params=_params(("parallel",)),
        name="proj",
    )(*args)


def _mlstm_kernel(n_chunks, hp, has_state, emit_state, *refs):
    it = iter(refs)
    if has_state:
        m0_ref = next(it)
    q_ref, kt_ref, v_ref, o_ref, ghn_ref, gr_ref = (next(it) for _ in range(6))
    if has_state:
        c0_ref, n0_ref = next(it), next(it)
    ha_ref = next(it)
    if emit_state:
        cout_ref, nout_ref, mout_ref = next(it), next(it), next(it)
    hs_ref, cn_ref, m_ref = (next(it) for _ in range(3))

    L = CHUNK
    nh, dh = N_HEADS, HEAD_DIM
    head0 = pl.program_id(1) * hp
    neg_inf = -jnp.inf

    sq_r = lax.broadcasted_iota(jnp.int32, (dh, dh), 0)
    sq_c = lax.broadcasted_iota(jnp.int32, (dh, dh), 1)
    for j in range(hp):
        for d in range(2):
            if has_state:
                n_col = jnp.sum(jnp.where(sq_r == sq_c, n0_ref[d, j], 0.0), axis=1, keepdims=True)
                cn_ref[j, d, :, 0:dh] = c0_ref[d, j]
                cn_ref[j, d, :, dh:2 * dh] = jnp.broadcast_to(n_col, (dh, dh))
                m0 = m0_ref[pl.program_id(0) * (2 * nh) + d * nh + head0 + j]
                m_ref[j, d] = jnp.full((8, LANES), m0, F32)
            else:
                cn_ref[j, d] = jnp.zeros((dh, 2 * dh), F32)
                m_ref[j, d] = jnp.zeros((8, LANES), F32)

    row_id = lax.broadcasted_iota(jnp.int32, (L, L), 0)
    col_id = lax.broadcasted_iota(jnp.int32, (L, L), 1)
    ones_blk = jnp.ones((L, dh), BF16)

    def chunk(j, d, c):
        t0 = pl.multiple_of(c * L, L)
        lanes = slice(j * dh, (j + 1) * dh)
        qc = q_ref[pl.ds(t0, L), lanes]
        ktc = kt_ref[c, lanes, :]
        vc = v_ref[pl.ds(t0, L), lanes]
        ig_row = gr_ref[c, pl.ds(2 * nh * d + head0 + j, 1), :]
        b_row = gr_ref[c, pl.ds(2 * nh * d + nh + head0 + j, 1), :]
        m_prev = m_ref[j, d][0:1, 0:1]
        if d == 0:
            b_end = b_row[:, L - 1:L]
            mask = col_id <= row_id
        else:
            b_end = b_row[:, 0:1]
            mask = col_id >= row_id
        a_row = ig_row - b_row
        a_max = jnp.max(jnp.where(mask, a_row, neg_inf), axis=1, keepdims=True)
        g = jnp.maximum(jnp.broadcast_to(a_max, (L, dh)), m_prev)
        b_col = jnp.broadcast_to(
            jnp.sum(jnp.where(row_id == col_id, b_row, 0.0), axis=1, keepdims=True), (L, dh))
        g_full = jnp.concatenate([g] * (L // dh), axis=1)
        s = _dot(qc, ktc) * jnp.exp(jnp.where(mask, a_row - g_full, neg_inf))
        decay = jnp.exp(m_prev - g)
        qcn = _dot(qc, cn_ref[j, d].astype(BF16))
        num = decay * qcn[:, 0:dh] + _dot(s.astype(BF16), vc)
        den = decay * qcn[:, dh:2 * dh] + jnp.sum(s, axis=1, keepdims=True)
        hch = num / jnp.maximum(jnp.abs(den), jnp.exp(-(b_col + g)))
        w_log = b_end + a_row
        m_new = jnp.maximum(b_end + m_prev, jnp.max(w_log, axis=1, keepdims=True))
        kw = (ktc.astype(F32) * jnp.exp(w_log - m_new)).astype(BF16)
        v_aug = jnp.concatenate([vc, ones_blk], axis=1)
        cn_ref[j, d] = jnp.exp(b_end + m_prev - m_new) * cn_ref[j, d] + _dot(kw, v_aug)
        m_ref[j, d] = jnp.broadcast_to(m_new, (8, LANES))
        return t0, hch

    if n_chunks == 1:
        for j in range(hp):
            _, h_f = chunk(j, 0, 0)
            _, h_b = chunk(j, 1, 0)
            hs_ref[:, j * dh:(j + 1) * dh] = h_f + h_b
    else:
        half = n_chunks // 2

        def first(i, carry):
            for j in range(hp):
                lanes = slice(j * dh, (j + 1) * dh)
                t_f, h_f = chunk(j, 0, i)
                t_b, h_b = chunk(j, 1, n_chunks - 1 - i)
                hs_ref[pl.ds(t_f, L), lanes] = h_f
                hs_ref[pl.ds(t_b, L), lanes] = h_b
            return carry

        def second(i, carry):
            for j in range(hp):
                lanes = slice(j * dh, (j + 1) * dh)
                t_f, h_f = chunk(j, 0, i)
                t_b, h_b = chunk(j, 1, n_chunks - 1 - i)
                hs_ref[pl.ds(t_f, L), lanes] += h_f
                hs_ref[pl.ds(t_b, L), lanes] += h_b
            return carry

        lax.fori_loop(0, half, first, 0)
        lax.fori_loop(half, n_chunks, second, 0)

    def finish(c, carry):
        t0 = pl.multiple_of(c * L, L)
        for j in range(hp):
            lanes = slice(j * dh, (j + 1) * dh)
            ha = hs_ref[pl.ds(t0, L), lanes]
            ha = ha * lax.rsqrt(jnp.mean(ha * ha, axis=-1, keepdims=True) + EPS)
            ha = ha * ghn_ref[:, lanes] * _sigmoid(o_ref[pl.ds(t0, L), lanes])
            ha_ref[pl.ds(t0, L), lanes] = ha.astype(BF16)
        return carry

    if n_chunks == 1:
        finish(0, 0)
    else:
        lax.fori_loop(0, n_chunks, finish, 0)

    if emit_state:
        for j in range(hp):
            for d in range(2):
                cout_ref[d, j] = cn_ref[j, d, :, 0:dh]
                n_rep = cn_ref[j, d, :, dh:2 * dh]
                nout_ref[d, j] = jnp.sum(jnp.where(sq_r == sq_c, n_rep, 0.0), axis=0, keepdims=True)
                mout_ref[d, j] = m_ref[j, d][0:1, :]


def _mlstm(q, kt, v, o, g_hn, gr, n_seq, seq_len, states, hp):
    n_tok, d = q.shape
    L = CHUNK
    n_chunks = seq_len // L
    has_state = states is not None
    emit_state = not has_state
    nh, dh = N_HEADS, HEAD_DIM
    assert n_chunks == 1 or n_chunks % 2 == 0
    assert nh % hp == 0
    wd = hp * dh

    in_specs = []
    args = []
    if has_state:
        state_c, state_n, state_m = states
        in_specs.append(pl.BlockSpec(memory_space=pltpu.SMEM))
        args.append(state_m.reshape(-1))
    tok = pl.BlockSpec((seq_len, wd), lambda b, h: (b, h))
    in_specs += [tok,
                 pl.BlockSpec((n_chunks, wd, L), lambda b, h: (b, h, 0)),
                 tok, tok,
                 pl.BlockSpec((1, wd), lambda b, h: (0, h)),
                 pl.BlockSpec((n_chunks, 4 * nh, L), lambda b, h: (b, 0, 0))]
    args += [q, kt, v, o, g_hn.reshape(1, d), gr]
    st_c = pl.BlockSpec((None, None, 2, hp, dh, dh), lambda b, h: (b, 0, 0, h, 0, 0))
    st_v = pl.BlockSpec((None, None, 2, hp, 1, dh), lambda b, h: (b, 0, 0, h, 0, 0))
    if has_state:
        in_specs += [st_c, st_v]
        args += [state_c, state_n.reshape(n_seq, 1, 2, nh, 1, dh)]
    out_shape = [jax.ShapeDtypeStruct((n_tok, d), BF16)]
    out_specs = [tok]
    if emit_state:
        out_shape += [jax.ShapeDtypeStruct((n_seq, 1, 2, nh, dh, dh), F32),
                      jax.ShapeDtypeStruct((n_seq, 1, 2, nh, 1, dh), F32),
                      jax.ShapeDtypeStruct((n_seq, 1, 2, nh, 1, LANES), F32)]
        out_specs += [st_c, st_v, st_v]
    scratch = [pltpu.VMEM((seq_len, wd), F32),
               pltpu.VMEM((hp, 2, dh, 2 * dh), F32),
               pltpu.VMEM((hp, 2, 8, LANES), F32)]
    return pl.pallas_call(
        functools.partial(_mlstm_kernel, n_chunks, hp, has_state, emit_state),
        grid=(n_seq, nh // hp),
        in_specs=in_specs,
        out_specs=out_specs,
        out_shape=out_shape,
        scratch_shapes=scratch,
        compiler_params=_params(("parallel", "parallel")),
        name="mlstm",
    )(*args)


def _fnet_kernel(seq_len, u_ref, f_ref, cs_ref, y_ref, ab_ref):
    T = seq_len

    @pl.when(pl.program_id(1) == 0)
    def _():
        for g in range(N_FGROUPS):
            ug = u_ref[:, g * FGROUP_DIM:(g + 1) * FGROUP_DIM]
            ab_ref[0:T, g * FGROUP_DIM:(g + 1) * FGROUP_DIM] = _dot(ug, cs_ref[0]).astype(BF16)
            ab_ref[T:2 * T, g * FGROUP_DIM:(g + 1) * FGROUP_DIM] = _dot(ug, cs_ref[1]).astype(BF16)

    scale = 1.0 / math.sqrt(T * FGROUP_DIM)
    y_ref[...] = (_dot(f_ref[...], ab_ref[...]) * scale).astype(BF16)


def _fnet(u, f_mat, cs_mat, n_seq, seq_len):
    n_tok, du = u.shape
    tr = min(seq_len, 512)
    return pl.pallas_call(
        functools.partial(_fnet_kernel, seq_len),
        grid=(n_seq, seq_len // tr),
        in_specs=[pl.BlockSpec((seq_len, du), lambda b, r: (b, 0)),
                  pl.BlockSpec((tr, 2 * seq_len), lambda b, r: (r, 0)),
                  _const_spec(cs_mat.shape)],
        out_specs=pl.BlockSpec((tr, du), lambda b, r: (b * (seq_len // tr) + r, 0)),
        out_shape=jax.ShapeDtypeStruct((n_tok, du), BF16),
        scratch_shapes=[pltpu.VMEM((2 * seq_len, du), BF16)],
        compiler_params=_params(("parallel", "arbitrary")),
        name="fnet",
    )(u, f_mat, cs_mat)


def _post_kernel(has_pos, *refs):
    if has_pos:
        x_ref, pos_ref = refs[0], refs[1]
        refs = refs[2:]
    else:
        x_ref = refs[0]
        refs = refs[1:]
    (ha_ref, yf_ref, ga_ref, gb_ref, mod_ref, gp1_ref, gp2_ref, wa_ref, wf_ref, wout_ref, wr_ref, br_ref,
     x1_ref, h2_ref, lg_ref) = refs
    x = x_ref[...]
    if has_pos:
        x = x + pos_ref[...]
    ya = _dot(ha_ref[...], wa_ref[...])
    yf = _dot(yf_ref[...], wf_ref[...])
    mix_in = _sigmoid(ga_ref[...]) * ya + _sigmoid(gb_ref[...]) * yf
    mix = _dot(mix_in.astype(BF16), wout_ref[...])
    gt1 = mod_ref[2:3, :]
    sh2 = mod_ref[3:4, :]
    sc2 = mod_ref[4:5, :]
    x1 = x + gt1 * _rms(mix, gp1_ref[...])
    h2 = _rms(x1, gp2_ref[...]) * (1.0 + sc2) + sh2
    x1_ref[...] = x1
    h2_ref[...] = h2
    lg_ref[...] = _dot_f32(wr_ref[...], h2, nt=True) + br_ref[...]


def _post(x, pos, ha, yf, ga, gb, mod, mod_rows, seq_len, g_post1, g_pre2, wts):
    n_tok, d = x.shape
    tm = TOKEN_TILE
    tiles_per_seq = seq_len // tm
    has_pos = pos is not None
    (wa, wf, wout, wr, br) = wts
    du = yf.shape[1]
    row = pl.BlockSpec((tm, d), lambda i: (i, 0))
    in_specs = [row]
    args = [x]
    if has_pos:
        in_specs.append(pl.BlockSpec((tm, d), lambda i: (i % tiles_per_seq, 0)))
        args.append(pos)
    in_specs += [row, pl.BlockSpec((tm, du), lambda i: (i, 0)), row, row,
                 _mod_spec(d, tm, seq_len, mod_rows),
                 _const_spec((1, d)), _const_spec((1, d))]
    in_specs += [_const_spec(w.shape) for w in wts]
    args += [ha, yf, ga, gb, mod, g_post1.reshape(1, d), g_pre2.reshape(1, d)] + list(wts)
    return pl.pallas_call(
        functools.partial(_post_kernel, has_pos),
        grid=(n_tok // tm,),
        in_specs=in_specs,
        out_specs=[row, row, pl.BlockSpec((None, LANES, tm), lambda i: (i, 0, 0))],
        out_shape=[jax.ShapeDtypeStruct((n_tok, d), F32),
                   jax.ShapeDtypeStruct((n_tok, d), F32),
                   jax.ShapeDtypeStruct((n_tok // tm, LANES, tm), F32)],
        compiler_params=_params(("parallel",)),
        name="post",
    )(*args)


FFN_TILE = 256
PLAN_ITEMS = LANES


def _route_rows(lg):
    tm = lg.shape[1]
    sub = lax.broadcasted_iota(jnp.int32, (LANES, tm), 0).astype(F32)
    big = float(LANES)
    is_g = sub < N_GROUPS
    gl = jnp.where(is_g, lg, -jnp.inf)
    mx = jnp.max(gl, axis=0, keepdims=True)
    z = jnp.sum(jnp.where(is_g, jnp.exp(gl - mx), 0.0), axis=0, keepdims=True)
    p_sel = 1.0 / z
    g_sel = jnp.min(jnp.where(gl == mx, sub, big), axis=0, keepdims=True)
    lo = N_GROUPS + EXPERTS_PER_GROUP * g_sel
    in_grp = (sub >= lo) & (sub < lo + EXPERTS_PER_GROUP)
    le = jnp.where(in_grp, lg, -jnp.inf)
    v1 = jnp.max(le, axis=0, keepdims=True)
    i1 = jnp.min(jnp.where(le == v1, sub, big), axis=0, keepdims=True)
    le2 = jnp.where(sub == i1, -jnp.inf, le)
    v2 = jnp.max(le2, axis=0, keepdims=True)
    i2 = jnp.min(jnp.where(le2 == v2, sub, big), axis=0, keepdims=True)
    e2 = jnp.exp(v2 - v1)
    w1 = p_sel / (1.0 + e2)
    w2 = p_sel * e2 / (1.0 + e2)
    return i1 - N_GROUPS, i2 - N_GROUPS, w1, w2


def _route_kernel(n_tiles, lg_ref, etri_ref, ttri_ref, pos_ref, w_ref, plan_ref, meta_ref, cnt_ref, run_ref, base_ref):
    p = pl.program_id(0)
    i = pl.program_id(1)
    tm = lg_ref.shape[1]
    sub = lax.broadcasted_iota(jnp.int32, (LANES, tm), 0).astype(F32)
    tf = float(FFN_TILE)

    @pl.when(p == 0)
    def _():
        e_a, e_b, w_a, w_b = _route_rows(lg_ref[...])
        meta_ref[i, 0:1, :] = e_a
        meta_ref[i, 1:2, :] = e_b
        meta_ref[i, 2:3, :] = w_a
        meta_ref[i, 3:4, :] = w_b
        meta_ref[i, 4:8, :] = jnp.zeros((4, tm), F32)
        oh = jnp.where((sub == e_a) | (sub == e_b), 1.0, 0.0)

        @pl.when(i == 0)
        def _():
            cnt_ref[...] = jnp.zeros_like(cnt_ref)

        cnt_ref[...] += jnp.broadcast_to(jnp.sum(oh, axis=1, keepdims=True), cnt_ref.shape)

    @pl.when(p == 1)
    def _():
        @pl.when(i == 0)
        def _():
            c1, c2, c3 = _split3(cnt_ref[...])
            base_ref[...] = _dot(etri_ref[...], c1) + (_dot(etri_ref[...], c2) + _dot(etri_ref[...], c3))
            run_ref[...] = jnp.zeros_like(run_ref)

        e_a = meta_ref[i, 0:1, :]
        e_b = meta_ref[i, 1:2, :]
        oh_a = jnp.where(sub == e_a, 1.0, 0.0)
        oh_b = jnp.where(sub == e_b, 1.0, 0.0)
        oh = oh_a + oh_b
        before = _dot(oh.astype(BF16), ttri_ref[...]) + run_ref[...] + base_ref[...]
        pos_ref[...] = jnp.zeros(pos_ref.shape, jnp.int32)
        pos_ref[0:1, :] = jnp.sum(oh_a * before, axis=0, keepdims=True).astype(jnp.int32)
        pos_ref[1:2, :] = jnp.sum(oh_b * before, axis=0, keepdims=True).astype(jnp.int32)
        w_ref[...] = meta_ref[i]
        run_ref[...] += jnp.broadcast_to(jnp.sum(oh, axis=1, keepdims=True), run_ref.shape)

        @pl.when(i == n_tiles - 1)
        def _():
            cnt = cnt_ref[:, 0:PLAN_ITEMS]
            base = base_ref[:, 0:PLAN_ITEMS]
            e_id = lax.broadcasted_iota(jnp.int32, (LANES, PLAN_ITEMS), 0)
            is_e = e_id < N_EXPERTS
            t_lo = jnp.floor(base / tf)
            t_hi = jnp.floor((base + cnt - 1.0) / tf)
            n_items = jnp.where(is_e & (cnt > 0.0), t_hi - t_lo + 1.0, 0.0)
            i_start = _dot(etri_ref[...], n_items.astype(BF16))
            item = lax.broadcasted_iota(jnp.int32, (LANES, PLAN_ITEMS), 1).astype(F32)
            sel = (i_start <= item) & (item < i_start + n_items)
            tile_e = t_lo + item - i_start
            off = base - tile_e * tf

            def pick(v):
                return jnp.sum(jnp.where(sel, v, 0.0), axis=0, keepdims=True).astype(jnp.int32)

            plan_ref[...] = jnp.zeros(plan_ref.shape, jnp.int32)
            plan_ref[0:1, :] = pick(e_id.astype(F32))
            plan_ref[1:2, :] = pick(tile_e)
            plan_ref[2:3, :] = pick(jnp.maximum(off, 0.0))
            plan_ref[3:4, :] = pick(jnp.minimum(off + cnt, tf))
            plan_ref[4:5, :] = jnp.sum(n_items, axis=0, keepdims=True).astype(jnp.int32)


def _route(lg, tris):
    expert_tri, token_tri = tris
    n_tiles, _, tm = lg.shape
    return pl.pallas_call(
        functools.partial(_route_kernel, n_tiles),
        grid=(2, n_tiles),
        in_specs=[pl.BlockSpec((None, LANES, tm), lambda p, i: (i, 0, 0)),
                  _const_spec(expert_tri.shape), _const_spec(token_tri.shape)],
        out_specs=[pl.BlockSpec((None, 8, tm), lambda p, i: (i * p, 0, 0)),
                   pl.BlockSpec((None, 8, tm), lambda p, i: (i * p, 0, 0)),
                   pl.BlockSpec((8, PLAN_ITEMS), lambda p, i: (0, 0))],
        out_shape=[jax.ShapeDtypeStruct((n_tiles, 8, tm), jnp.int32),
                   jax.ShapeDtypeStruct((n_tiles, 8, tm), F32),
                   jax.ShapeDtypeStruct((8, PLAN_ITEMS), jnp.int32)],
        scratch_shapes=[pltpu.VMEM((n_tiles, 8, tm), F32),
                        pltpu.VMEM((LANES, tm), F32),
                        pltpu.VMEM((LANES, tm), F32),
                        pltpu.VMEM((LANES, tm), F32)],
        compiler_params=_params(("arbitrary", "arbitrary")),
        name="route",
    )(lg, expert_tri, token_tri)


ROW_UNROLL = 8


DISPATCH_BLOCKS = 4


def _dispatch_kernel(tm, pos_ref, h_ref, xs_ref, sem):
    def row_copy(t, dst):
        return pltpu.make_async_copy(h_ref.at[pl.ds(t, 1)], xs_ref.at[pl.ds(dst, 1)], sem)

    for blk in range(DISPATCH_BLOCKS):
        def issue(g, carry, blk=blk):
            for u in range(ROW_UNROLL):
                t = g * ROW_UNROLL + u
                row_copy(blk * tm + t, pos_ref[blk, 0, t]).start()
                row_copy(blk * tm + t, pos_ref[blk, 1, t]).start()
            return carry

        lax.fori_loop(0, tm // ROW_UNROLL, issue, 0)

    def drain(g, carry):
        for _ in range(2 * ROW_UNROLL):
            row_copy(0, 0).wait()
        return carry

    lax.fori_loop(0, DISPATCH_BLOCKS * tm // ROW_UNROLL, drain, 0)


def _dispatch(pos, h2, n_rows):
    n_tiles, _, tm = pos.shape
    d = h2.shape[1]
    nb = DISPATCH_BLOCKS
    assert n_tiles % nb == 0
    return pl.pallas_call(
        functools.partial(_dispatch_kernel, tm),
        grid=(n_tiles // nb,),
        in_specs=[pl.BlockSpec((nb, 8, tm), lambda i: (i, 0, 0), memory_space=pltpu.SMEM),
                  pl.BlockSpec((nb * tm, d), lambda i: (i, 0))],
        out_specs=pl.BlockSpec(memory_space=pl.ANY),
        out_shape=jax.ShapeDtypeStruct((n_rows, d), F32),
        scratch_shapes=[pltpu.SemaphoreType.DMA],
        compiler_params=_params(("arbitrary",)),
        name="dispatch",
    )(pos, h2)


def _ffn_kernel(owner_ref, tile_ref, lo_ref, hi_ref, used_ref, x_ref, w1_ref, w3_ref, w2_ref, y_ref):
    w = pl.program_id(0)

    @pl.when(w < used_ref[0])
    def _():
        x = x_ref[...].astype(BF16)
        a = _dot(x, w1_ref[...])
        b = _dot(x, w3_ref[...])
        hid = (a * _sigmoid(a)) * b
        y = _dot(hid.astype(BF16), w2_ref[...])
        row = lax.broadcasted_iota(jnp.int32, y.shape, 0)
        mine = (row >= lo_ref[w]) & (row < hi_ref[w])
        first = (w == 0) | (tile_ref[w] != tile_ref[jnp.maximum(w - 1, 0)])

        @pl.when(first)
        def _():
            y_ref[...] = jnp.where(mine, y, 0.0)

        @pl.when(jnp.logical_not(first))
        def _():
            y_ref[...] = jnp.where(mine, y, y_ref[...])


def _ffn(plan, xs, w1, w3, w2):
    n_rows, d = xs.shape
    tf = FFN_TILE
    de = w1.shape[2]
    n_items = n_rows // tf + N_EXPERTS - 1
    assert n_items <= PLAN_ITEMS

    def item(w, used):
        return jnp.minimum(w, used[0] - 1)

    grid_spec = pltpu.PrefetchScalarGridSpec(
        num_scalar_prefetch=5,
        grid=(n_items,),
        in_specs=[pl.BlockSpec((tf, d), lambda w, own, til, lo, hi, used: (til[item(w, used)], 0)),
                  pl.BlockSpec((None, d, de), lambda w, own, til, lo, hi, used: (own[item(w, used)], 0, 0)),
                  pl.BlockSpec((None, d, de), lambda w, own, til, lo, hi, used: (own[item(w, used)], 0, 0)),
                  pl.BlockSpec((None, de, d), lambda w, own, til, lo, hi, used: (own[item(w, used)], 0, 0))],
        out_specs=pl.BlockSpec((tf, d), lambda w, own, til, lo, hi, used: (til[item(w, used)], 0)),
    )
    return pl.pallas_call(
        _ffn_kernel,
        grid_spec=grid_spec,
        out_shape=jax.ShapeDtypeStruct((n_rows, d), F32),
        compiler_params=_params(("arbitrary",)),
        name="ffn",
    )(plan[0], plan[1], plan[2], plan[3], plan[4, 0:1], xs, w1, w3, w2)


def _combine_kernel(tm, pos_ref, nxt_ref, w_ref, x1_ref, mod_ref, gp_ref, ys_ref, y_ref, buf_ref, sem):
    i = pl.program_id(0)
    n = pl.num_programs(0)
    slot = i % 2

    def row_copy(src, k, t, s):
        return pltpu.make_async_copy(ys_ref.at[pl.ds(src, 1)], buf_ref.at[s, k, pl.ds(t, 1)], sem.at[s])

    def issue(p_ref, s):
        def body(g, carry):
            for u in range(ROW_UNROLL):
                t = g * ROW_UNROLL + u
                row_copy(p_ref[0, t], 0, t, s).start()
                row_copy(p_ref[1, t], 1, t, s).start()
            return carry
        lax.fori_loop(0, tm // ROW_UNROLL, body, 0)

    @pl.when(i == 0)
    def _():
        issue(pos_ref, 0)

    @pl.when(i + 1 < n)
    def _():
        issue(nxt_ref, 1 - slot)

    def drain(g, carry):
        for _ in range(2 * ROW_UNROLL):
            row_copy(0, 0, 0, slot).wait()
        return carry

    lax.fori_loop(0, tm // ROW_UNROLL, drain, 0)

    r = lax.broadcasted_iota(jnp.int32, (tm, tm), 0)
    c = lax.broadcasted_iota(jnp.int32, (tm, tm), 1)
    d = y_ref.shape[1]
    wa = jnp.broadcast_to(jnp.sum(jnp.where(r == c, w_ref[2:3, :], 0.0), axis=1, keepdims=True), (tm, d))
    wb = jnp.broadcast_to(jnp.sum(jnp.where(r == c, w_ref[3:4, :], 0.0), axis=1, keepdims=True), (tm, d))
    moe = wa * buf_ref[slot, 0] + wb * buf_ref[slot, 1]
    gt2 = mod_ref[5:6, :]
    y_ref[...] = x1_ref[...] + gt2 * _rms(moe, gp_ref[...])


def _combine(pos, w, x1, mod, mod_rows, seq_len, g_post2, ys):
    n_tiles, _, tm = pos.shape
    n_tok, d = x1.shape
    row = pl.BlockSpec((tm, d), lambda i: (i, 0))
    return pl.pallas_call(
        functools.partial(_combine_kernel, tm),
        grid=(n_tiles,),
        in_specs=[pl.BlockSpec((None, 8, tm), lambda i: (i, 0, 0), memory_space=pltpu.SMEM),
                  pl.BlockSpec((None, 8, tm), lambda i: (jnp.minimum(i + 1, n_tiles - 1), 0, 0),
                               memory_space=pltpu.SMEM),
                  pl.BlockSpec((None, 8, tm), lambda i: (i, 0, 0)),
                  row,
                  _mod_spec(d, tm, seq_len, mod_rows),
                  _const_spec((1, d)),
                  pl.BlockSpec(memory_space=pl.ANY)],
        out_specs=row,
        out_shape=jax.ShapeDtypeStruct((n_tok, d), F32),
        scratch_shapes=[pltpu.VMEM((2, 2, tm, d), F32), pltpu.SemaphoreType.DMA((2,))],
        compiler_params=_params(("arbitrary",)),
        name="combine",
    )(pos, pos, w, x1, mod, g_post2.reshape(1, d), ys)


def _moe(h2, lg, x1, mod, mod_rows, seq_len, g_post2, w1, w3, w2, route_tri):
    n_tok, d = x1.shape
    n_rows = 2 * n_tok
    assert n_rows % FFN_TILE == 0
    pos, w, plan = _route(lg, route_tri)
    xs = _dispatch(pos, h2, n_rows)
    ys = _ffn(plan, xs, w1, w3, w2)
    return _combine(pos, w, x1, mod, mod_rows, seq_len, g_post2, ys)


def _grid_pos(n_tokens, d_model):
    rows = n_tokens // GRID_W
    row = jnp.repeat(jnp.arange(rows, dtype=F32), GRID_W)
    col = jnp.tile(jnp.arange(GRID_W, dtype=F32), rows)
    n_freq = d_model // 4
    freq = jnp.exp(-math.log(POS_BASE) * jnp.arange(n_freq, dtype=F32) / n_freq)

    def enc(p):
        a = p[:, None] * freq[None, :]
        return jnp.concatenate([jnp.sin(a), jnp.cos(a)], axis=-1)

    return jnp.concatenate([enc(row), enc(col)], axis=-1)


def _dft_cos_sin(n):
    k = np.arange(n, dtype=np.int64)
    ang = 2.0 * np.pi * ((k[:, None] * k[None, :]) % n).astype(np.float64) / n
    return np.cos(ang), np.sin(ang)


def _fnet_consts(seq_len):
    ct, st = _dft_cos_sin(seq_len)
    f_mat = jnp.asarray(np.concatenate([ct, -st], axis=1), dtype=F32).astype(BF16)
    cc, sc = _dft_cos_sin(FGROUP_DIM)
    cs_mat = jnp.asarray(np.stack([cc, sc]), dtype=F32).astype(BF16)
    return f_mat, cs_mat


def _tri_consts():
    i = np.arange(CHUNK)
    prefix = (i[:, None] <= i[None, :]).astype(np.float32)
    suffix = (i[:, None] >= i[None, :]).astype(np.float32)
    tri = jnp.asarray(np.stack([prefix, suffix]), dtype=BF16)
    e = np.arange(LANES)
    expert_tri = jnp.asarray((e[None, :] < e[:, None]).astype(np.float32), dtype=BF16)
    t = np.arange(TOKEN_TILE)
    token_tri = jnp.asarray((t[:, None] < t[None, :]).astype(np.float32), dtype=BF16)
    return tri, (expert_tri, token_tri)


def _run_path(x, pos, n_seq, seq_len, mod, mod_rows, states, lw, consts):
    d = x.shape[-1]
    (g_pre1, proj_w, g_hn, post_w, g_post1, g_pre2, w_e1, w_e3, w_e2, g_post2) = lw
    tri, route_tris = consts
    x2 = x.reshape(n_seq * seq_len, d)
    q, kt, v, o, u, ga, gb, gr = _proj(x2, pos, mod, mod_rows, seq_len, g_pre1, proj_w + (tri,))
    heads_per_step = N_HEADS if seq_len == CHUNK else 2
    outs = _mlstm(q, kt, v, o, g_hn, gr, n_seq, seq_len, states, heads_per_step)
    ha = outs[0]
    f_mat, cs_mat = _fnet_consts(seq_len)
    yf = _fnet(u, f_mat, cs_mat, n_seq, seq_len)
    x1, h2, logits = _post(x2, pos, ha, yf, ga, gb, mod, mod_rows, seq_len, g_post1, g_pre2, post_w)
    y = _moe(h2, logits, x1, mod, mod_rows, seq_len, g_post2, w_e1, w_e3, w_e2, route_tris)
    return y.reshape(n_seq, seq_len, d), outs[1:]


def kernel(x_prompt, x_sample, c, state_C, state_n, state_m, c_ctx, w_ada, b_ada, g_pre1, w_in, b_gates, g_hn,
           w_a, w_f, w_out, g_post1, g_pre2, w_rg, b_rg, w_re, b_re, w_e1, w_e3, w_e2, g_post2):
    bp, sp, d = x_prompt.shape
    bs, ss, _ = x_sample.shape
    depth = w_in.shape[0]
    assert depth == 1
    l = 0
    dm = N_HEADS * HEAD_DIM
    du = N_FGROUPS * FGROUP_DIM
    ng = 4 * N_HEADS

    n_rows = 8
    assert 1 + bs <= n_rows
    cs = jnp.concatenate([c_ctx[None, :], c, jnp.zeros((n_rows - 1 - bs, d), F32)], axis=0)
    mod = _ada(cs, w_ada[l], b_ada[l]).reshape(n_rows, 6, d)

    wi = w_in[l]
    cuts = np.cumsum([dm] * 4 + [ng, du, d]).tolist()
    wq = wi[:, :cuts[0]].astype(BF16)
    wkt = wi[:, cuts[0]:cuts[1]].T.astype(BF16)
    wv = wi[:, cuts[1]:cuts[2]].astype(BF16)
    wo = wi[:, cuts[2]:cuts[3]].astype(BF16)
    wgt = wi[:, cuts[3]:cuts[4]].T
    wu = wi[:, cuts[4]:cuts[5]].astype(BF16)
    wga = wi[:, cuts[5]:cuts[6]].astype(BF16)
    wgb = wi[:, cuts[6]:].astype(BF16)
    bg = b_gates[l].reshape(ng, 1)
    proj_w = (wq, wkt, wv, wo, wu, wga, wgb, wgt, bg)
    n_r = N_GROUPS + N_EXPERTS
    wr = jnp.concatenate([w_rg[l], w_re[l], jnp.zeros((d, LANES - n_r), F32)], axis=1).T
    br = jnp.concatenate([b_rg[l], b_re[l], jnp.zeros((LANES - n_r,), F32)]).reshape(LANES, 1)
    post_w = (w_a[l].astype(BF16), w_f[l].astype(BF16), w_out[l].astype(BF16), wr, br)
    de = w_e1.shape[2] // N_EXPERTS

    def expert_major(w):
        return w.reshape(d, N_EXPERTS, de).transpose(1, 0, 2).astype(BF16)

    lw = (g_pre1[l], proj_w, g_hn[l], post_w, g_post1[l], g_pre2[l],
          expert_major(w_e1[l]), expert_major(w_e3[l]), w_e2[l].reshape(N_EXPERTS, de, d).astype(BF16), g_post2[l])
    consts = _tri_consts()

    y_prompt, (new_c, new_n, new_m) = _run_path(x_prompt, None, bp, sp, mod, (0, 0), None, lw, consts)
    pos = _grid_pos(ss, d)
    y_sample, _ = _run_path(x_sample, pos, bs, ss, mod, (1, 1), (state_C, state_n, state_m), lw, consts)

    new_state_c = new_c
    new_state_n = new_n.reshape(bp, depth, 2, N_HEADS, HEAD_DIM)
    new_state_m = new_m[..., 0, 0]
    return (y_prompt, y_sample, new_state_c, new_state_n, new_state_m)
```

```python
import functools
import math

import numpy as np
import jax
import jax.numpy as jnp
from jax import lax
from jax.experimental import pallas as pl
from jax.experimental.pallas import tpu as pltpu

F32 = jnp.float32
BF16 = jnp.bfloat16

N_HEADS = 8
HEAD_DIM = 128
N_FGROUPS = 4
FGROUP_DIM = 128
N_GROUPS = 4
EXPERTS_PER_GROUP = 8
N_EXPERTS = N_GROUPS * EXPERTS_PER_GROUP
GRID_W = 64
POS_BASE = 10000.0
EPS = 1e-6

V7X_VMEM_LIMIT_BYTES = 56 * 1024 * 1024
LANES = 128
CHUNK = 256
TOKEN_TILE = CHUNK
CHUNKS_PER_STEP = 2


def _params(sem):
    return pltpu.CompilerParams(dimension_semantics=sem, vmem_limit_bytes=V7X_VMEM_LIMIT_BYTES)


def _const_spec(shape):
    n = len(shape)
    return pl.BlockSpec(shape, lambda *_: (0,) * n, pipeline_mode=pl.Buffered(1))


def _split2(x):
    hi = x.astype(BF16)
    lo = (x - hi.astype(F32)).astype(BF16)
    return hi, lo


def _split3(x):
    hi = x.astype(BF16)
    r = x - hi.astype(F32)
    mid = r.astype(BF16)
    lo = (r - mid.astype(F32)).astype(BF16)
    return hi, mid, lo


def _dot(a, b):
    return jnp.dot(a, b, preferred_element_type=F32)


def _dot_nt(a, b):
    return lax.dot_general(a, b, (((1,), (1,)), ((), ())), preferred_element_type=F32)


def _dot_f32(a, b, nt=False):
    d = _dot_nt if nt else _dot
    a1, a2 = _split2(a)
    b1, b2 = _split2(b)
    return d(a1, b1) + (d(a1, b2) + d(a2, b1))


def _dot_exact_rhs(a, b_bf16, nt=False):
    d = _dot_nt if nt else _dot
    a1, a2, a3 = _split3(a)
    return d(a1, b_bf16) + (d(a2, b_bf16) + d(a3, b_bf16))


def _rms(x, g):
    return x * lax.rsqrt(jnp.mean(x * x, axis=-1, keepdims=True) + EPS) * g


def _sigmoid(x):
    return 1.0 / (1.0 + jnp.exp(-x))


def _log_sigmoid(x):
    return jnp.minimum(x, 0.0) - jnp.log(1.0 + jnp.exp(-jnp.abs(x)))


def _ada_kernel(c_ref, w_ref, b_ref, o_ref):
    c = c_ref[...]
    s = (c * _sigmoid(c)).astype(BF16)
    o_ref[...] = _dot(s, w_ref[...].astype(BF16)) + b_ref[...]


def _ada(cs, w_ada, b_ada):
    rows, d = cs.shape
    n = w_ada.shape[1]
    tn = 1024
    return pl.pallas_call(
        _ada_kernel,
        grid=(n // tn,),
        in_specs=[pl.BlockSpec((rows, d), lambda j: (0, 0)),
                  pl.BlockSpec((d, tn), lambda j: (0, j)),
                  pl.BlockSpec((1, tn), lambda j: (0, j))],
        out_specs=pl.BlockSpec((rows, tn), lambda j: (0, j)),
        out_shape=jax.ShapeDtypeStruct((rows, n), F32),
        compiler_params=_params(("arbitrary",)),
        name="ada",
    )(cs, w_ada, b_ada.reshape(1, n))


def _proj_kernel(has_pos, *refs):
    if has_pos:
        x_ref, pos_ref = refs[0], refs[1]
        refs = refs[2:]
    else:
        x_ref = refs[0]
        refs = refs[1:]
    (mod_ref, g_ref, wq_ref, wkt_ref, wv_ref, wo_ref, wu_ref, wga_ref, wgb_ref, wgt_ref, bg_ref,
     tri_ref, q_ref, kt_ref, v_ref, o_ref, u_ref, ga_ref, gb_ref, gr_ref) = refs
    x = x_ref[...]
    if has_pos:
        x = x + pos_ref[...]
    sh = mod_ref[0:1, :]
    sc = mod_ref[1:2, :]
    h = _rms(x, g_ref[...]) * (1.0 + sc) + sh
    hb = h.astype(BF16)
    q_ref[...] = _dot(hb, wq_ref[...]).astype(BF16)
    kt = (_dot_nt(wkt_ref[...], hb) * (HEAD_DIM ** -0.5)).astype(BF16)
    tc = TOKEN_TILE
    for s in range(CHUNKS_PER_STEP):
        kt_ref[s] = kt[:, s * tc:(s + 1) * tc]
    v_ref[...] = _dot(hb, wv_ref[...]).astype(BF16)
    o_ref[...] = _dot(hb, wo_ref[...])
    u_ref[...] = _dot(hb, wu_ref[...]).astype(BF16)
    ga_ref[...] = _dot(hb, wga_ref[...])
    gb_ref[...] = _dot(hb, wgb_ref[...])
    g = _dot_f32(wgt_ref[...], h, nt=True) + bg_ref[...]
    nh = N_HEADS
    lf_f = _log_sigmoid(g[nh:2 * nh])
    lf_b = _log_sigmoid(g[3 * nh:4 * nh])
    for s in range(CHUNKS_PER_STEP):
        cols = slice(s * tc, (s + 1) * tc)
        gr_ref[s, 0:nh, :] = g[0:nh, cols]
        gr_ref[s, nh:2 * nh, :] = _dot_exact_rhs(lf_f[:, cols], tri_ref[0])
        gr_ref[s, 2 * nh:3 * nh, :] = g[2 * nh:3 * nh, cols]
        gr_ref[s, 3 * nh:4 * nh, :] = _dot_exact_rhs(lf_b[:, cols], tri_ref[1])


def _mod_spec(d, tm, seq_len, mod_rows):
    row0, stride = mod_rows
    return pl.BlockSpec((None, 6, d), lambda i, *_: (row0 + stride * ((i * tm) // seq_len), 0, 0))


def _proj(x, pos, mod, mod_rows, seq_len, g_pre1, wts):
    n_tok, d = x.shape
    tc = TOKEN_TILE
    ns = CHUNKS_PER_STEP
    tm = ns * tc
    n_tiles = n_tok // tc
    tiles_per_seq = max(seq_len // tm, 1)
    has_pos = pos is not None
    (wq, wkt, wv, wo, wu, wga, wgb, wgt, bg, tri) = wts
    row = pl.BlockSpec((tm, d), lambda i: (i, 0))
    in_specs = [row]
    args = [x]
    if has_pos:
        in_specs.append(pl.BlockSpec((tm, d), lambda i: (i % tiles_per_seq, 0)))
        args.append(pos)
    in_specs += [_mod_spec(d, tm, seq_len, mod_rows), _const_spec((1, d))]
    in_specs += [_const_spec(w.shape) for w in wts]
    args += [mod, g_pre1.reshape(1, d)] + list(wts)
    n_gate_rows = 4 * N_HEADS
    du = wu.shape[1]
    out_shape = [jax.ShapeDtypeStruct((n_tok, d), BF16),
                 jax.ShapeDtypeStruct((n_tiles, d, tc), BF16),
                 jax.ShapeDtypeStruct((n_tok, d), BF16),
                 jax.ShapeDtypeStruct((n_tok, d), F32),
                 jax.ShapeDtypeStruct((n_tok, du), BF16),
                 jax.ShapeDtypeStruct((n_tok, d), F32),
                 jax.ShapeDtypeStruct((n_tok, d), F32),
                 jax.ShapeDtypeStruct((n_tiles, n_gate_rows, tc), F32)]
    out_specs = [row,
                 pl.BlockSpec((ns, d, tc), lambda i: (i, 0, 0)),
                 row, row,
                 pl.BlockSpec((tm, du), lambda i: (i, 0)),
                 row, row,
                 pl.BlockSpec((ns, n_gate_rows, tc), lambda i: (i, 0, 0))]
    return pl.pallas_call(
        functools.partial(_proj_kernel, has_pos),
        grid=(n_tok // tm,),
        in_specs=in_specs,
        out_specs=out_specs,
        out_shape=out_shape,
        compiler_params=_params(("parallel",)),
        name="proj",
    )(*args)


def _mlstm_kernel(n_chunks, hp, has_state, emit_state, *refs):
    it = iter(refs)
    if has_state:
        m0_ref = next(it)
    q_ref, kt_ref, v_ref, o_ref, ghn_ref, gr_ref = (next(it) for _ in range(6))
    if has_state:
        c0_ref, n0_ref = next(it), next(it)
    ha_ref = next(it)
    if emit_state:
        cout_ref, nout_ref, mout_ref = next(it), next(it), next(it)
    hs_ref, cn_ref, m_ref = (next(it) for _ in range(3))

    L = CHUNK
    nh, dh = N_HEADS, HEAD_DIM
    head0 = pl.program_id(1) * hp
    neg_inf = -jnp.inf

    sq_r = lax.broadcasted_iota(jnp.int32, (dh, dh), 0)
    sq_c = lax.broadcasted_iota(jnp.int32, (dh, dh), 1)
    for j in range(hp):
        for d in range(2):
            if has_state:
                n_col = jnp.sum(jnp.where(sq_r == sq_c, n0_ref[d, j], 0.0), axis=1, keepdims=True)
                cn_ref[j, d, :, 0:dh] = c0_ref[d, j]
                cn_ref[j, d, :, dh:2 * dh] = jnp.broadcast_to(n_col, (dh, dh))
                m0 = m0_ref[pl.program_id(0) * (2 * nh) + d * nh + head0 + j]
                m_ref[j, d] = jnp.full((8, LANES), m0, F32)
            else:
                cn_ref[j, d] = jnp.zeros((dh, 2 * dh), F32)
                m_ref[j, d] = jnp.zeros((8, LANES), F32)

    row_id = lax.broadcasted_iota(jnp.int32, (L, L), 0)
    col_id = lax.broadcasted_iota(jnp.int32, (L, L), 1)
    ones_blk = jnp.ones((L, dh), BF16)

    def chunk(j, d, c):
        t0 = pl.multiple_of(c * L, L)
        lanes = slice(j * dh, (j + 1) * dh)
        qc = q_ref[pl.ds(t0, L), lanes]
        ktc = kt_ref[c, lanes, :]
        vc = v_ref[pl.ds(t0, L), lanes]
        ig_row = gr_ref[c, pl.ds(2 * nh * d + head0 + j, 1), :]
        b_row = gr_ref[c, pl.ds(2 * nh * d + nh + head0 + j, 1), :]
        m_prev = m_ref[j, d][0:1, 0:1]
        if d == 0:
            b_end = b_row[:, L - 1:L]
            mask = col_id <= row_id
        else:
            b_end = b_row[:, 0:1]
            mask = col_id >= row_id
        a_row = ig_row - b_row
        a_max = jnp.max(jnp.where(mask, a_row, neg_inf), axis=1, keepdims=True)
        g = jnp.maximum(jnp.broadcast_to(a_max, (L, dh)), m_prev)
        b_col = jnp.broadcast_to(
            jnp.sum(jnp.where(row_id == col_id, b_row, 0.0), axis=1, keepdims=True), (L, dh))
        g_full = jnp.concatenate([g] * (L // dh), axis=1)
        s = _dot(qc, ktc) * jnp.exp(jnp.where(mask, a_row - g_full, neg_inf))
        decay = jnp.exp(m_prev - g)
        qcn = _dot(qc, cn_ref[j, d].astype(BF16))
        num = decay * qcn[:, 0:dh] + _dot(s.astype(BF16), vc)
        den = decay * qcn[:, dh:2 * dh] + jnp.sum(s, axis=1, keepdims=True)
        hch = num / jnp.maximum(jnp.abs(den), jnp.exp(-(b_col + g)))
        w_log = b_end + a_row
        m_new = jnp.maximum(b_end + m_prev, jnp.max(w_log, axis=1, keepdims=True))
        kw = (ktc.astype(F32) * jnp.exp(w_log - m_new)).astype(BF16)
        v_aug = jnp.concatenate([vc, ones_blk], axis=1)
        cn_ref[j, d] = jnp.exp(b_end + m_prev - m_new) * cn_ref[j, d] + _dot(kw, v_aug)
        m_ref[j, d] = jnp.broadcast_to(m_new, (8, LANES))
        return t0, hch

    if n_chunks == 1:
        for j in range(hp):
            _, h_f = chunk(j, 0, 0)
            _, h_b = chunk(j, 1, 0)
            hs_ref[:, j * dh:(j + 1) * dh] = h_f + h_b
    else:
        half = n_chunks // 2

        def first(i, carry):
            for j in range(hp):
                lanes = slice(j * dh, (j + 1) * dh)
                t_f, h_f = chunk(j, 0, i)
                t_b, h_b = chunk(j, 1, n_chunks - 1 - i)
                hs_ref[pl.ds(t_f, L), lanes] = h_f
                hs_ref[pl.ds(t_b, L), lanes] = h_b
            return carry

        def second(i, carry):
            for j in range(hp):
                lanes = slice(j * dh, (j + 1) * dh)
                t_f, h_f = chunk(j, 0, i)
                t_b, h_b = chunk(j, 1, n_chunks - 1 - i)
                hs_ref[pl.ds(t_f, L), lanes] += h_f
                hs_ref[pl.ds(t_b, L), lanes] += h_b
            return carry

        lax.fori_loop(0, half, first, 0)
        lax.fori_loop(half, n_chunks, second, 0)

    def finish(c, carry):
        t0 = pl.multiple_of(c * L, L)
        for j in range(hp):
            lanes = slice(j * dh, (j + 1) * dh)
            ha = hs_ref[pl.ds(t0, L), lanes]
            ha = ha * lax.rsqrt(jnp.mean(ha * ha, axis=-1, keepdims=True) + EPS)
            ha = ha * ghn_ref[:, lanes] * _sigmoid(o_ref[pl.ds(t0, L), lanes])
            ha_ref[pl.ds(t0, L), lanes] = ha.astype(BF16)
        return carry

    if n_chunks == 1:
        finish(0, 0)
    else:
        lax.fori_loop(0, n_chunks, finish, 0)

    if emit_state:
        for j in range(hp):
            for d in range(2):
                cout_ref[d, j] = cn_ref[j, d, :, 0:dh]
                n_rep = cn_ref[j, d, :, dh:2 * dh]
                nout_ref[d, j] = jnp.sum(jnp.where(sq_r == sq_c, n_rep, 0.0), axis=0, keepdims=True)
                mout_ref[d, j] = m_ref[j, d][0:1, :]


def _mlstm(q, kt, v, o, g_hn, gr, n_seq, seq_len, states, hp):
    n_tok, d = q.shape
    L = CHUNK
    n_chunks = seq_len // L
    has_state = states is not None
    emit_state = not has_state
    nh, dh = N_HEADS, HEAD_DIM
    assert n_chunks == 1 or n_chunks % 2 == 0
    assert nh % hp == 0
    wd = hp * dh

    in_specs = []
    args = []
    if has_state:
        state_c, state_n, state_m = states
        in_specs.append(pl.BlockSpec(memory_space=pltpu.SMEM))
        args.append(state_m.reshape(-1))
    tok = pl.BlockSpec((seq_len, wd), lambda b, h: (b, h))
    in_specs += [tok,
                 pl.BlockSpec((n_chunks, wd, L), lambda b, h: (b, h, 0)),
                 tok, tok,
                 pl.BlockSpec((1, wd), lambda b, h: (0, h)),
                 pl.BlockSpec((n_chunks, 4 * nh, L), lambda b, h: (b, 0, 0))]
    args += [q, kt, v, o, g_hn.reshape(1, d), gr]
    st_c = pl.BlockSpec((None, None, 2, hp, dh, dh), lambda b, h: (b, 0, 0, h, 0, 0))
    st_v = pl.BlockSpec((None, None, 2, hp, 1, dh), lambda b, h: (b, 0, 0, h, 0, 0))
    if has_state:
        in_specs += [st_c, st_v]
        args += [state_c, state_n.reshape(n_seq, 1, 2, nh, 1, dh)]
    out_shape = [jax.ShapeDtypeStruct((n_tok, d), BF16)]
    out_specs = [tok]
    if emit_state:
        out_shape += [jax.ShapeDtypeStruct((n_seq, 1, 2, nh, dh, dh), F32),
                      jax.ShapeDtypeStruct((n_seq, 1, 2, nh, 1, dh), F32),
                      jax.ShapeDtypeStruct((n_seq, 1, 2, nh, 1, LANES), F32)]
        out_specs += [st_c, st_v, st_v]
    scratch = [pltpu.VMEM((seq_len, wd), F32),
               pltpu.VMEM((hp, 2, dh, 2 * dh), F32),
               pltpu.VMEM((hp, 2, 8, LANES), F32)]
    return pl.pallas_call(
        functools.partial(_mlstm_kernel, n_chunks, hp, has_state, emit_state),
        grid=(n_seq, nh // hp),
        in_specs=in_specs,
        out_specs=out_specs,
        out_shape=out_shape,
        scratch_shapes=scratch,
        compiler_params=_params(("parallel", "parallel")),
        name="mlstm",
    )(*args)


def _fnet_kernel(seq_len, u_ref, f_ref, cs_ref, y_ref, ab_ref):
    T = seq_len

    @pl.when(pl.program_id(1) == 0)
    def _():
        for g in range(N_FGROUPS):
            ug = u_ref[:, g * FGROUP_DIM:(g + 1) * FGROUP_DIM]
            ab_ref[0:T, g * FGROUP_DIM:(g + 1) * FGROUP_DIM] = _dot(ug, cs_ref[0]).astype(BF16)
            ab_ref[T:2 * T, g * FGROUP_DIM:(g + 1) * FGROUP_DIM] = _dot(ug, cs_ref[1]).astype(BF16)

    scale = 1.0 / math.sqrt(T * FGROUP_DIM)
    y_ref[...] = (_dot(f_ref[...], ab_ref[...]) * scale).astype(BF16)


def _fnet(u, f_mat, cs_mat, n_seq, seq_len):
    n_tok, du = u.shape
    tr = min(seq_len, 512)
    return pl.pallas_call(
        functools.partial(_fnet_kernel, seq_len),
        grid=(n_seq, seq_len // tr),
        in_specs=[pl.BlockSpec((seq_len, du), lambda b, r: (b, 0)),
                  pl.BlockSpec((tr, 2 * seq_len), lambda b, r: (r, 0)),
                  _const_spec(cs_mat.shape)],
        out_specs=pl.BlockSpec((tr, du), lambda b, r: (b * (seq_len // tr) + r, 0)),
        out_shape=jax.ShapeDtypeStruct((n_tok, du), BF16),
        scratch_shapes=[pltpu.VMEM((2 * seq_len, du), BF16)],
        compiler_params=_params(("parallel", "arbitrary")),
        name="fnet",
    )(u, f_mat, cs_mat)


def _post_kernel(has_pos, *refs):
    if has_pos:
        x_ref, pos_ref = refs[0], refs[1]
        refs = refs[2:]
    else:
        x_ref = refs[0]
        refs = refs[1:]
    (ha_ref, yf_ref, ga_ref, gb_ref, mod_ref, gp1_ref, gp2_ref, wa_ref, wf_ref, wout_ref, wr_ref, br_ref,
     x1_ref, h2_ref, lg_ref) = refs
    x = x_ref[...]
    if has_pos:
        x = x + pos_ref[...]
    ya = _dot(ha_ref[...], wa_ref[...])
    yf = _dot(yf_ref[...], wf_ref[...])
    mix_in = _sigmoid(ga_ref[...]) * ya + _sigmoid(gb_ref[...]) * yf
    mix = _dot(mix_in.astype(BF16), wout_ref[...])
    gt1 = mod_ref[2:3, :]
    sh2 = mod_ref[3:4, :]
    sc2 = mod_ref[4:5, :]
    x1 = x + gt1 * _rms(mix, gp1_ref[...])
    h2 = _rms(x1, gp2_ref[...]) * (1.0 + sc2) + sh2
    x1_ref[...] = x1
    h2_ref[...] = h2
    lg = _dot_f32(wr_ref[...], h2, nt=True) + br_ref[...]
    for s in range(CHUNKS_PER_STEP):
        lg_ref[s] = lg[:, s * TOKEN_TILE:(s + 1) * TOKEN_TILE]


def _post(x, pos, ha, yf, ga, gb, mod, mod_rows, seq_len, g_post1, g_pre2, wts):
    n_tok, d = x.shape
    tc = TOKEN_TILE
    ns = CHUNKS_PER_STEP
    tm = ns * tc
    tiles_per_seq = max(seq_len // tm, 1)
    has_pos = pos is not None
    (wa, wf, wout, wr, br) = wts
    du = yf.shape[1]
    row = pl.BlockSpec((tm, d), lambda i: (i, 0))
    in_specs = [row]
    args = [x]
    if has_pos:
        in_specs.append(pl.BlockSpec((tm, d), lambda i: (i % tiles_per_seq, 0)))
        args.append(pos)
    in_specs += [row, pl.BlockSpec((tm, du), lambda i: (i, 0)), row, row,
                 _mod_spec(d, tm, seq_len, mod_rows),
                 _const_spec((1, d)), _const_spec((1, d))]
    in_specs += [_const_spec(w.shape) for w in wts]
    args += [ha, yf, ga, gb, mod, g_post1.reshape(1, d), g_pre2.reshape(1, d)] + list(wts)
    return pl.pallas_call(
        functools.partial(_post_kernel, has_pos),
        grid=(n_tok // tm,),
        in_specs=in_specs,
        out_specs=[row, row, pl.BlockSpec((ns, LANES, tc), lambda i: (i, 0, 0))],
        out_shape=[jax.ShapeDtypeStruct((n_tok, d), F32),
                   jax.ShapeDtypeStruct((n_tok, d), F32),
                   jax.ShapeDtypeStruct((n_tok // tc, LANES, tc), F32)],
        compiler_params=_params(("parallel",)),
        name="post",
    )(*args)


FFN_TILE = 256
PLAN_ITEMS = LANES


def _route_rows(lg):
    tm = lg.shape[1]
    sub = lax.broadcasted_iota(jnp.int32, (LANES, tm), 0).astype(F32)
    big = float(LANES)
    is_g = sub < N_GROUPS
    gl = jnp.where(is_g, lg, -jnp.inf)
    mx = jnp.max(gl, axis=0, keepdims=True)
    z = jnp.sum(jnp.where(is_g, jnp.exp(gl - mx), 0.0), axis=0, keepdims=True)
    p_sel = 1.0 / z
    g_sel = jnp.min(jnp.where(gl == mx, sub, big), axis=0, keepdims=True)
    lo = N_GROUPS + EXPERTS_PER_GROUP * g_sel
    in_grp = (sub >= lo) & (sub < lo + EXPERTS_PER_GROUP)
    le = jnp.where(in_grp, lg, -jnp.inf)
    v1 = jnp.max(le, axis=0, keepdims=True)
    i1 = jnp.min(jnp.where(le == v1, sub, big), axis=0, keepdims=True)
    le2 = jnp.where(sub == i1, -jnp.inf, le)
    v2 = jnp.max(le2, axis=0, keepdims=True)
    i2 = jnp.min(jnp.where(le2 == v2, sub, big), axis=0, keepdims=True)
    e2 = jnp.exp(v2 - v1)
    w1 = p_sel / (1.0 + e2)
    w2 = p_sel * e2 / (1.0 + e2)
    return i1 - N_GROUPS, i2 - N_GROUPS, w1, w2


def _route_kernel(n_tiles, lg_ref, etri_ref, ttri_ref, pos_ref, w_ref, plan_ref, meta_ref, cnt_ref, run_ref, base_ref):
    p = pl.program_id(0)
    i = pl.program_id(1)
    tm = lg_ref.shape[1]
    sub = lax.broadcasted_iota(jnp.int32, (LANES, tm), 0).astype(F32)
    tf = float(FFN_TILE)

    @pl.when(p == 0)
    def _():
        e_a, e_b, w_a, w_b = _route_rows(lg_ref[...])
        meta_ref[i, 0:1, :] = e_a
        meta_ref[i, 1:2, :] = e_b
        meta_ref[i, 2:3, :] = w_a
        meta_ref[i, 3:4, :] = w_b
        meta_ref[i, 4:8, :] = jnp.zeros((4, tm), F32)
        oh = jnp.where((sub == e_a) | (sub == e_b), 1.0, 0.0)

        @pl.when(i == 0)
        def _():
            cnt_ref[...] = jnp.zeros_like(cnt_ref)

        cnt_ref[...] += jnp.broadcast_to(jnp.sum(oh, axis=1, keepdims=True), cnt_ref.shape)

    @pl.when(p == 1)
    def _():
        @pl.when(i == 0)
        def _():
            c1, c2, c3 = _split3(cnt_ref[...])
            base_ref[...] = _dot(etri_ref[...], c1) + (_dot(etri_ref[...], c2) + _dot(etri_ref[...], c3))
            run_ref[...] = jnp.zeros_like(run_ref)

        e_a = meta_ref[i, 0:1, :]
        e_b = meta_ref[i, 1:2, :]
        oh_a = jnp.where(sub == e_a, 1.0, 0.0)
        oh_b = jnp.where(sub == e_b, 1.0, 0.0)
        oh = oh_a + oh_b
        before = _dot(oh.astype(BF16), ttri_ref[...]) + run_ref[...] + base_ref[...]
        pos_ref[...] = jnp.zeros(pos_ref.shape, jnp.int32)
        pos_ref[0:1, :] = jnp.sum(oh_a * before, axis=0, keepdims=True).astype(jnp.int32)
        pos_ref[1:2, :] = jnp.sum(oh_b * before, axis=0, keepdims=True).astype(jnp.int32)
        w_ref[...] = meta_ref[i]
        run_ref[...] += jnp.broadcast_to(jnp.sum(oh, axis=1, keepdims=True), run_ref.shape)

        @pl.when(i == n_tiles - 1)
        def _():
            cnt = cnt_ref[:, 0:PLAN_ITEMS]
            base = base_ref[:, 0:PLAN_ITEMS]
            e_id = lax.broadcasted_iota(jnp.int32, (LANES, PLAN_ITEMS), 0)
            is_e = e_id < N_EXPERTS
            t_lo = jnp.floor(base / tf)
            t_hi = jnp.floor((base + cnt - 1.0) / tf)
            n_items = jnp.where(is_e & (cnt > 0.0), t_hi - t_lo + 1.0, 0.0)
            i_start = _dot(etri_ref[...], n_items.astype(BF16))
            item = lax.broadcasted_iota(jnp.int32, (LANES, PLAN_ITEMS), 1).astype(F32)
            sel = (i_start <= item) & (item < i_start + n_items)
            tile_e = t_lo + item - i_start
            off = base - tile_e * tf

            def pick(v):
                return jnp.sum(jnp.where(sel, v, 0.0), axis=0, keepdims=True).astype(jnp.int32)

            plan_ref[...] = jnp.zeros(plan_ref.shape, jnp.int32)
            plan_ref[0:1, :] = pick(e_id.astype(F32))
            plan_ref[1:2, :] = pick(tile_e)
            plan_ref[2:3, :] = pick(jnp.maximum(off, 0.0))
            plan_ref[3:4, :] = pick(jnp.minimum(off + cnt, tf))
            plan_ref[4:5, :] = jnp.sum(n_items, axis=0, keepdims=True).astype(jnp.int32)


def _route(lg, tris):
    expert_tri, token_tri = tris
    n_tiles, _, tm = lg.shape
    return pl.pallas_call(
        functools.partial(_route_kernel, n_tiles),
        grid=(2, n_tiles),
        in_specs=[pl.BlockSpec((None, LANES, tm), lambda p, i: (i, 0, 0)),
                  _const_spec(expert_tri.shape), _const_spec(token_tri.shape)],
        out_specs=[pl.BlockSpec((None, 8, tm), lambda p, i: (i * p, 0, 0)),
                   pl.BlockSpec((None, 8, tm), lambda p, i: (i * p, 0, 0)),
                   pl.BlockSpec((8, PLAN_ITEMS), lambda p, i: (0, 0))],
        out_shape=[jax.ShapeDtypeStruct((n_tiles, 8, tm), jnp.int32),
                   jax.ShapeDtypeStruct((n_tiles, 8, tm), F32),
                   jax.ShapeDtypeStruct((8, PLAN_ITEMS), jnp.int32)],
        scratch_shapes=[pltpu.VMEM((n_tiles, 8, tm), F32),
                        pltpu.VMEM((LANES, tm), F32),
                        pltpu.VMEM((LANES, tm), F32),
                        pltpu.VMEM((LANES, tm), F32)],
        compiler_params=_params(("arbitrary", "arbitrary")),
        name="route",
    )(lg, expert_tri, token_tri)


ROW_UNROLL = 8


DISPATCH_BLOCKS = 4


def _dispatch_kernel(tm, pos_ref, h_ref, xs_ref, sem):
    def row_copy(t, dst):
        return pltpu.make_async_copy(h_ref.at[pl.ds(t, 1)], xs_ref.at[pl.ds(dst, 1)], sem)

    for blk in range(DISPATCH_BLOCKS):
        def issue(g, carry, blk=blk):
            for u in range(ROW_UNROLL):
                t = g * ROW_UNROLL + u
                row_copy(blk * tm + t, pos_ref[blk, 0, t]).start()
                row_copy(blk * tm + t, pos_ref[blk, 1, t]).start()
            return carry

        lax.fori_loop(0, tm // ROW_UNROLL, issue, 0)

    def drain(g, carry):
        for _ in range(2 * ROW_UNROLL):
            row_copy(0, 0).wait()
        return carry

    lax.fori_loop(0, DISPATCH_BLOCKS * tm // ROW_UNROLL, drain, 0)


def _dispatch(pos, h2, n_rows):
    n_tiles, _, tm = pos.shape
    d = h2.shape[1]
    nb = DISPATCH_BLOCKS
    assert n_tiles % nb == 0
    return pl.pallas_call(
        functools.partial(_dispatch_kernel, tm),
        grid=(n_tiles // nb,),
        in_specs=[pl.BlockSpec((nb, 8, tm), lambda i: (i, 0, 0), memory_space=pltpu.SMEM),
                  pl.BlockSpec((nb * tm, d), lambda i: (i, 0))],
        out_specs=pl.BlockSpec(memory_space=pl.ANY),
        out_shape=jax.ShapeDtypeStruct((n_rows, d), F32),
        scratch_shapes=[pltpu.SemaphoreType.DMA],
        compiler_params=_params(("arbitrary",)),
        name="dispatch",
    )(pos, h2)


def _ffn_kernel(owner_ref, tile_ref, lo_ref, hi_ref, used_ref, x_ref, w1_ref, w3_ref, w2_ref, y_ref):
    w = pl.program_id(0)

    @pl.when(w < used_ref[0])
    def _():
        x = x_ref[...].astype(BF16)
        a = _dot(x, w1_ref[...])
        b = _dot(x, w3_ref[...])
        hid = (a * _sigmoid(a)) * b
        y = _dot(hid.astype(BF16), w2_ref[...])
        row = lax.broadcasted_iota(jnp.int32, y.shape, 0)
        mine = (row >= lo_ref[w]) & (row < hi_ref[w])
        first = (w == 0) | (tile_ref[w] != tile_ref[jnp.maximum(w - 1, 0)])

        @pl.when(first)
        def _():
            y_ref[...] = jnp.where(mine, y, 0.0)

        @pl.when(jnp.logical_not(first))
        def _():
            y_ref[...] = jnp.where(mine, y, y_ref[...])


def _ffn(plan, xs, w1, w3, w2):
    n_rows, d = xs.shape
    tf = FFN_TILE
    de = w1.shape[2]
    n_items = n_rows // tf + N_EXPERTS - 1
    assert n_items <= PLAN_ITEMS

    def item(w, used):
        return jnp.minimum(w, used[0] - 1)

    grid_spec = pltpu.PrefetchScalarGridSpec(
        num_scalar_prefetch=5,
        grid=(n_items,),
        in_specs=[pl.BlockSpec((tf, d), lambda w, own, til, lo, hi, used: (til[item(w, used)], 0)),
                  pl.BlockSpec((None, d, de), lambda w, own, til, lo, hi, used: (own[item(w, used)], 0, 0)),
                  pl.BlockSpec((None, d, de), lambda w, own, til, lo, hi, used: (own[item(w, used)], 0, 0)),
                  pl.BlockSpec((None, de, d), lambda w, own, til, lo, hi, used: (own[item(w, used)], 0, 0))],
        out_specs=pl.BlockSpec((tf, d), lambda w, own, til, lo, hi, used: (til[item(w, used)], 0)),
    )
    return pl.pallas_call(
        _ffn_kernel,
        grid_spec=grid_spec,
        out_shape=jax.ShapeDtypeStruct((n_rows, d), F32),
        compiler_params=_params(("arbitrary",)),
        name="ffn",
    )(plan[0], plan[1], plan[2], plan[3], plan[4, 0:1], xs, w1, w3, w2)


def _combine_kernel(tm, pos_ref, nxt_ref, w_ref, x1_ref, mod_ref, gp_ref, ys_ref, y_ref, buf_ref, sem):
    i = pl.program_id(0)
    n = pl.num_programs(0)
    slot = i % 2

    def row_copy(src, k, t, s):
        return pltpu.make_async_copy(ys_ref.at[pl.ds(src, 1)], buf_ref.at[s, k, pl.ds(t, 1)], sem.at[s])

    def issue(p_ref, s):
        def body(g, carry):
            for u in range(ROW_UNROLL):
                t = g * ROW_UNROLL + u
                row_copy(p_ref[0, t], 0, t, s).start()
                row_copy(p_ref[1, t], 1, t, s).start()
            return carry
        lax.fori_loop(0, tm // ROW_UNROLL, body, 0)

    @pl.when(i == 0)
    def _():
        issue(pos_ref, 0)

    @pl.when(i + 1 < n)
    def _():
        issue(nxt_ref, 1 - slot)

    def drain(g, carry):
        for _ in range(2 * ROW_UNROLL):
            row_copy(0, 0, 0, slot).wait()
        return carry

    lax.fori_loop(0, tm // ROW_UNROLL, drain, 0)

    r = lax.broadcasted_iota(jnp.int32, (tm, tm), 0)
    c = lax.broadcasted_iota(jnp.int32, (tm, tm), 1)
    d = y_ref.shape[1]
    wa = jnp.broadcast_to(jnp.sum(jnp.where(r == c, w_ref[2:3, :], 0.0), axis=1, keepdims=True), (tm, d))
    wb = jnp.broadcast_to(jnp.sum(jnp.where(r == c, w_ref[3:4, :], 0.0), axis=1, keepdims=True), (tm, d))
    moe = wa * buf_ref[slot, 0] + wb * buf_ref[slot, 1]
    gt2 = mod_ref[5:6, :]
    y_ref[...] = x1_ref[...] + gt2 * _rms(moe, gp_ref[...])


def _combine(pos, w, x1, mod, mod_rows, seq_len, g_post2, ys):
    n_tiles, _, tm = pos.shape
    n_tok, d = x1.shape
    row = pl.BlockSpec((tm, d), lambda i: (i, 0))
    return pl.pallas_call(
        functools.partial(_combine_kernel, tm),
        grid=(n_tiles,),
        in_specs=[pl.BlockSpec((None, 8, tm), lambda i: (i, 0, 0), memory_space=pltpu.SMEM),
                  pl.BlockSpec((None, 8, tm), lambda i: (jnp.minimum(i + 1, n_tiles - 1), 0, 0),
                               memory_space=pltpu.SMEM),
                  pl.BlockSpec((None, 8, tm), lambda i: (i, 0, 0)),
                  row,
                  _mod_spec(d, tm, seq_len, mod_rows),
                  _const_spec((1, d)),
                  pl.BlockSpec(memory_space=pl.ANY)],
        out_specs=row,
        out_shape=jax.ShapeDtypeStruct((n_tok, d), F32),
        scratch_shapes=[pltpu.VMEM((2, 2, tm, d), F32), pltpu.SemaphoreType.DMA((2,))],
        compiler_params=_params(("arbitrary",)),
        name="combine",
    )(pos, pos, w, x1, mod, g_post2.reshape(1, d), ys)


def _moe(h2, lg, x1, mod, mod_rows, seq_len, g_post2, w1, w3, w2, route_tri):
    n_tok, d = x1.shape
    n_rows = 2 * n_tok
    assert n_rows % FFN_TILE == 0
    pos, w, plan = _route(lg, route_tri)
    xs = _dispatch(pos, h2, n_rows)
    ys = _ffn(plan, xs, w1, w3, w2)
    return _combine(pos, w, x1, mod, mod_rows, seq_len, g_post2, ys)


def _grid_pos(n_tokens, d_model):
    rows = n_tokens // GRID_W
    row = np.repeat(np.arange(rows, dtype=np.float64), GRID_W)
    col = np.tile(np.arange(GRID_W, dtype=np.float64), rows)
    n_freq = d_model // 4
    freq = np.exp(-math.log(POS_BASE) * np.arange(n_freq, dtype=np.float64) / n_freq)

    def enc(p):
        a = p[:, None] * freq[None, :]
        return np.concatenate([np.sin(a), np.cos(a)], axis=-1)

    return jnp.asarray(np.concatenate([enc(row), enc(col)], axis=-1), dtype=F32)


def _dft_cos_sin(n):
    k = np.arange(n, dtype=np.int64)
    ang = 2.0 * np.pi * ((k[:, None] * k[None, :]) % n).astype(np.float64) / n
    return np.cos(ang), np.sin(ang)


def _fnet_consts(seq_len):
    ct, st = _dft_cos_sin(seq_len)
    f_mat = jnp.asarray(np.concatenate([ct, -st], axis=1), dtype=F32).astype(BF16)
    cc, sc = _dft_cos_sin(FGROUP_DIM)
    cs_mat = jnp.asarray(np.stack([cc, sc]), dtype=F32).astype(BF16)
    return f_mat, cs_mat


def _tri_consts():
    i = np.arange(CHUNK)
    prefix = (i[:, None] <= i[None, :]).astype(np.float32)
    suffix = (i[:, None] >= i[None, :]).astype(np.float32)
    tri = jnp.asarray(np.stack([prefix, suffix]), dtype=BF16)
    e = np.arange(LANES)
    expert_tri = jnp.asarray((e[None, :] < e[:, None]).astype(np.float32), dtype=BF16)
    t = np.arange(TOKEN_TILE)
    token_tri = jnp.asarray((t[:, None] < t[None, :]).astype(np.float32), dtype=BF16)
    return tri, (expert_tri, token_tri)


def _run_path(x, pos, n_seq, seq_len, mod, mod_rows, states, lw, consts):
    d = x.shape[-1]
    (g_pre1, proj_w, g_hn, post_w, g_post1, g_pre2, w_e1, w_e3, w_e2, g_post2) = lw
    tri, route_tris = consts
    x2 = x.reshape(n_seq * seq_len, d)
    q, kt, v, o, u, ga, gb, gr = _proj(x2, pos, mod, mod_rows, seq_len, g_pre1, proj_w + (tri,))
    heads_per_step = N_HEADS if seq_len == CHUNK else 2
    outs = _mlstm(q, kt, v, o, g_hn, gr, n_seq, seq_len, states, heads_per_step)
    ha = outs[0]
    f_mat, cs_mat = _fnet_consts(seq_len)
    yf = _fnet(u, f_mat, cs_mat, n_seq, seq_len)
    x1, h2, logits = _post(x2, pos, ha, yf, ga, gb, mod, mod_rows, seq_len, g_post1, g_pre2, post_w)
    y = _moe(h2, logits, x1, mod, mod_rows, seq_len, g_post2, w_e1, w_e3, w_e2, route_tris)
    return y.reshape(n_seq, seq_len, d), outs[1:]


def kernel(x_prompt, x_sample, c, state_C, state_n, state_m, c_ctx, w_ada, b_ada, g_pre1, w_in, b_gates, g_hn,
           w_a, w_f, w_out, g_post1, g_pre2, w_rg, b_rg, w_re, b_re, w_e1, w_e3, w_e2, g_post2):
    bp, sp, d = x_prompt.shape
    bs, ss, _ = x_sample.shape
    depth = w_in.shape[0]
    assert depth == 1
    l = 0
    dm = N_HEADS * HEAD_DIM
    du = N_FGROUPS * FGROUP_DIM
    ng = 4 * N_HEADS

    n_rows = 8
    assert 1 + bs <= n_rows
    cs = jnp.concatenate([c_ctx[None, :], c, jnp.zeros((n_rows - 1 - bs, d), F32)], axis=0)
    mod = _ada(cs, w_ada[l], b_ada[l]).reshape(n_rows, 6, d)

    wi = w_in[l]
    cuts = np.cumsum([dm] * 4 + [ng, du, d]).tolist()
    wq = wi[:, :cuts[0]].astype(BF16)
    wkt = wi[:, cuts[0]:cuts[1]].T.astype(BF16)
    wv = wi[:, cuts[1]:cuts[2]].astype(BF16)
    wo = wi[:, cuts[2]:cuts[3]].astype(BF16)
    wgt = wi[:, cuts[3]:cuts[4]].T
    wu = wi[:, cuts[4]:cuts[5]].astype(BF16)
    wga = wi[:, cuts[5]:cuts[6]].astype(BF16)
    wgb = wi[:, cuts[6]:].astype(BF16)
    bg = b_gates[l].reshape(ng, 1)
    proj_w = (wq, wkt, wv, wo, wu, wga, wgb, wgt, bg)
    n_r = N_GROUPS + N_EXPERTS
    wr = jnp.concatenate([w_rg[l], w_re[l], jnp.zeros((d, LANES - n_r), F32)], axis=1).T
    br = jnp.concatenate([b_rg[l], b_re[l], jnp.zeros((LANES - n_r,), F32)]).reshape(LANES, 1)
    post_w = (w_a[l].astype(BF16), w_f[l].astype(BF16), w_out[l].astype(BF16), wr, br)
    de = w_e1.shape[2] // N_EXPERTS

    def expert_major(w):
        return w.reshape(d, N_EXPERTS, de).transpose(1, 0, 2).astype(BF16)

    lw = (g_pre1[l], proj_w, g_hn[l], post_w, g_post1[l], g_pre2[l],
          expert_major(w_e1[l]), expert_major(w_e3[l]), w_e2[l].reshape(N_EXPERTS, de, d).astype(BF16), g_post2[l])
    consts = _tri_consts()

    y_prompt, (new_c, new_n, new_m) = _run_path(x_prompt, None, bp, sp, mod, (0, 0), None, lw, consts)
    pos = _grid_pos(ss, d)
    y_sample, _ = _run_path(x_sample, pos, bs, ss, mod, (1, 1), (state_C, state_n, state_m), lw, consts)

    new_state_c = new_c
    new_state_n = new_n.reshape(bp, depth, 2, N_HEADS, HEAD_DIM)
    new_state_m = new_m[..., 0, 0]
    return (y_prompt, y_sample, new_state_c, new_state_n, new_state_m)
```

```python
import functools
import math

import numpy as np
import jax
import jax.numpy as jnp
from jax import lax
from jax.experimental import pallas as pl
from jax.experimental.pallas import tpu as pltpu

F32 = jnp.float32
BF16 = jnp.bfloat16

N_HEADS = 8
HEAD_DIM = 128
N_FGROUPS = 4
FGROUP_DIM = 128
N_GROUPS = 4
EXPERTS_PER_GROUP = 8
N_EXPERTS = N_GROUPS * EXPERTS_PER_GROUP
GRID_W = 64
POS_BASE = 10000.0
EPS = 1e-6

V7X_VMEM_LIMIT_BYTES = 56 * 1024 * 1024
LANES = 128
CHUNK = 256
TOKEN_TILE = CHUNK
CHUNKS_PER_STEP = 2


def _params(sem):
    return pltpu.CompilerParams(dimension_semantics=sem, vmem_limit_bytes=V7X_VMEM_LIMIT_BYTES)


def _const_spec(shape):
    n = len(shape)
    return pl.BlockSpec(shape, lambda *_: (0,) * n, pipeline_mode=pl.Buffered(1))


def _split2(x):
    hi = x.astype(BF16)
    lo = (x - hi.astype(F32)).astype(BF16)
    return hi, lo


def _split3(x):
    hi = x.astype(BF16)
    r = x - hi.astype(F32)
    mid = r.astype(BF16)
    lo = (r - mid.astype(F32)).astype(BF16)
    return hi, mid, lo


def _dot(a, b):
    return jnp.dot(a, b, preferred_element_type=F32)


def _dot_nt(a, b):
    return lax.dot_general(a, b, (((1,), (1,)), ((), ())), preferred_element_type=F32)


def _dot_f32(a, b, nt=False):
    d = _dot_nt if nt else _dot
    a1, a2 = _split2(a)
    b1, b2 = _split2(b)
    return d(a1, b1) + (d(a1, b2) + d(a2, b1))


def _dot_exact_rhs(a, b_bf16, nt=False):
    d = _dot_nt if nt else _dot
    a1, a2, a3 = _split3(a)
    return d(a1, b_bf16) + (d(a2, b_bf16) + d(a3, b_bf16))


def _rms(x, g):
    return x * lax.rsqrt(jnp.mean(x * x, axis=-1, keepdims=True) + EPS) * g


def _sigmoid(x):
    return 1.0 / (1.0 + jnp.exp(-x))


def _log_sigmoid(x):
    return jnp.minimum(x, 0.0) - jnp.log(1.0 + jnp.exp(-jnp.abs(x)))


def _ada_kernel(c_ref, w_ref, b_ref, o_ref):
    c = c_ref[...]
    s = (c * _sigmoid(c)).astype(BF16)
    o_ref[...] = _dot(s, w_ref[...].astype(BF16)) + b_ref[...]


def _ada(cs, w_ada, b_ada):
    rows, d = cs.shape
    n = w_ada.shape[1]
    tn = 1024
    return pl.pallas_call(
        _ada_kernel,
        grid=(n // tn,),
        in_specs=[pl.BlockSpec((rows, d), lambda j: (0, 0)),
                  pl.BlockSpec((d, tn), lambda j: (0, j)),
                  pl.BlockSpec((1, tn), lambda j: (0, j))],
        out_specs=pl.BlockSpec((rows, tn), lambda j: (0, j)),
        out_shape=jax.ShapeDtypeStruct((rows, n), F32),
        compiler_params=_params(("arbitrary",)),
        name="ada",
    )(cs, w_ada, b_ada.reshape(1, n))


def _proj_kernel(has_pos, *refs):
    if has_pos:
        x_ref, pos_ref = refs[0], refs[1]
        refs = refs[2:]
    else:
        x_ref = refs[0]
        refs = refs[1:]
    (mod_ref, g_ref, wq_ref, wkt_ref, wv_ref, wo_ref, wu_ref, wga_ref, wgb_ref, wgt_ref, bg_ref,
     tri_ref, q_ref, kt_ref, v_ref, o_ref, u_ref, ga_ref, gb_ref, gr_ref) = refs
    x = x_ref[...]
    if has_pos:
        x = x + pos_ref[...]
    sh = mod_ref[0:1, :]
    sc = mod_ref[1:2, :]
    h = _rms(x, g_ref[...]) * (1.0 + sc) + sh
    hb = h.astype(BF16)
    q_ref[...] = _dot(hb, wq_ref[...]).astype(BF16)
    kt = (_dot_nt(wkt_ref[...], hb) * (HEAD_DIM ** -0.5)).astype(BF16)
    tc = TOKEN_TILE
    for s in range(CHUNKS_PER_STEP):
        kt_ref[s] = kt[:, s * tc:(s + 1) * tc]
    v_ref[...] = _dot(hb, wv_ref[...]).astype(BF16)
    o_ref[...] = _dot(hb, wo_ref[...])
    u_ref[...] = _dot(hb, wu_ref[...]).astype(BF16)
    ga_ref[...] = _dot(hb, wga_ref[...])
    gb_ref[...] = _dot(hb, wgb_ref[...])
    g = _dot_f32(wgt_ref[...], h, nt=True) + bg_ref[...]
    nh = N_HEADS
    lf_f = _log_sigmoid(g[nh:2 * nh])
    lf_b = _log_sigmoid(g[3 * nh:4 * nh])
    for s in range(CHUNKS_PER_STEP):
        cols = slice(s * tc, (s + 1) * tc)
        gr_ref[s, 0:nh, :] = g[0:nh, cols]
        gr_ref[s, nh:2 * nh, :] = _dot_exact_rhs(lf_f[:, cols], tri_ref[0])
        gr_ref[s, 2 * nh:3 * nh, :] = g[2 * nh:3 * nh, cols]
        gr_ref[s, 3 * nh:4 * nh, :] = _dot_exact_rhs(lf_b[:, cols], tri_ref[1])


def _mod_spec(d, tm, seq_len, mod_rows):
    row0, stride = mod_rows
    return pl.BlockSpec((None, 6, d), lambda i, *_: (row0 + stride * ((i * tm) // seq_len), 0, 0))


def _proj(x, pos, mod, mod_rows, seq_len, g_pre1, wts):
    n_tok, d = x.shape
    tc = TOKEN_TILE
    ns = CHUNKS_PER_STEP
    tm = ns * tc
    n_tiles = n_tok // tc
    tiles_per_seq = max(seq_len // tm, 1)
    has_pos = pos is not None
    (wq, wkt, wv, wo, wu, wga, wgb, wgt, bg, tri) = wts
    row = pl.BlockSpec((tm, d), lambda i: (i, 0))
    in_specs = [row]
    args = [x]
    if has_pos:
        in_specs.append(pl.BlockSpec((tm, d), lambda i: (i % tiles_per_seq, 0)))
        args.append(pos)
    in_specs += [_mod_spec(d, tm, seq_len, mod_rows), _const_spec((1, d))]
    in_specs += [_const_spec(w.shape) for w in wts]
    args += [mod, g_pre1.reshape(1, d)] + list(wts)
    n_gate_rows = 4 * N_HEADS
    du = wu.shape[1]
    out_shape = [jax.ShapeDtypeStruct((n_tok, d), BF16),
                 jax.ShapeDtypeStruct((n_tiles, d, tc), BF16),
                 jax.ShapeDtypeStruct((n_tok, d), BF16),
                 jax.ShapeDtypeStruct((n_tok, d), F32),
                 jax.ShapeDtypeStruct((n_tok, du), BF16),
                 jax.ShapeDtypeStruct((n_tok, d), F32),
                 jax.ShapeDtypeStruct((n_tok, d), F32),
                 jax.ShapeDtypeStruct((n_tiles, n_gate_rows, tc), F32)]
    out_specs = [row,
                 pl.BlockSpec((ns, d, tc), lambda i: (i, 0, 0)),
                 row, row,
                 pl.BlockSpec((tm, du), lambda i: (i, 0)),
                 row, row,
                 pl.BlockSpec((ns, n_gate_rows, tc), lambda i: (i, 0, 0))]
    return pl.pallas_call(
        functools.partial(_proj_kernel, has_pos),
        grid=(n_tok // tm,),
        in_specs=in_specs,
        out_specs=out_specs,
        out_shape=out_shape,
        compiler_params=_params(("parallel",)),
        name="proj",
    )(*args)


def _mlstm_kernel(n_chunks, hp, has_state, emit_state, *refs):
    it = iter(refs)
    if has_state:
        m0_ref = next(it)
    q_ref, kt_ref, v_ref, o_ref, ghn_ref, gr_ref = (next(it) for _ in range(6))
    if has_state:
        c0_ref, n0_ref = next(it), next(it)
    ha_ref = next(it)
    if emit_state:
        cout_ref, nout_ref, mout_ref = next(it), next(it), next(it)
    hs_ref, cn_ref, m_ref = (next(it) for _ in range(3))

    L = CHUNK
    nh, dh = N_HEADS, HEAD_DIM
    head0 = pl.program_id(1) * hp
    neg_inf = -jnp.inf

    sq_r = lax.broadcasted_iota(jnp.int32, (dh, dh), 0)
    sq_c = lax.broadcasted_iota(jnp.int32, (dh, dh), 1)
    for j in range(hp):
        for d in range(2):
            if has_state:
                n_col = jnp.sum(jnp.where(sq_r == sq_c, n0_ref[d, j], 0.0), axis=1, keepdims=True)
                cn_ref[j, d, :, 0:dh] = c0_ref[d, j]
                cn_ref[j, d, :, dh:2 * dh] = jnp.broadcast_to(n_col, (dh, dh))
                m0 = m0_ref[pl.program_id(0) * (2 * nh) + d * nh + head0 + j]
                m_ref[j, d] = jnp.full((8, LANES), m0, F32)
            else:
                cn_ref[j, d] = jnp.zeros((dh, 2 * dh), F32)
                m_ref[j, d] = jnp.zeros((8, LANES), F32)

    row_id = lax.broadcasted_iota(jnp.int32, (L, L), 0)
    col_id = lax.broadcasted_iota(jnp.int32, (L, L), 1)
    ones_blk = jnp.ones((L, dh), BF16)

    def chunk(j, d, c):
        t0 = pl.multiple_of(c * L, L)
        lanes = slice(j * dh, (j + 1) * dh)
        qc = q_ref[pl.ds(t0, L), lanes]
        ktc = kt_ref[c, lanes, :]
        vc = v_ref[pl.ds(t0, L), lanes]
        ig_row = gr_ref[c, pl.ds(2 * nh * d + head0 + j, 1), :]
        b_row = gr_ref[c, pl.ds(2 * nh * d + nh + head0 + j, 1), :]
        m_prev = m_ref[j, d][0:1, 0:1]
        if d == 0:
            b_end = b_row[:, L - 1:L]
            mask = col_id <= row_id
        else:
            b_end = b_row[:, 0:1]
            mask = col_id >= row_id
        a_row = ig_row - b_row
        a_max = jnp.max(jnp.where(mask, a_row, neg_inf), axis=1, keepdims=True)
        g = jnp.maximum(jnp.broadcast_to(a_max, (L, dh)), m_prev)
        b_col = jnp.broadcast_to(
            jnp.sum(jnp.where(row_id == col_id, b_row, 0.0), axis=1, keepdims=True), (L, dh))
        g_full = jnp.concatenate([g] * (L // dh), axis=1)
        s = _dot(qc, ktc) * jnp.exp(jnp.where(mask, a_row - g_full, neg_inf))
        decay = jnp.exp(m_prev - g)
        qcn = _dot(qc, cn_ref[j, d].astype(BF16))
        num = decay * qcn[:, 0:dh] + _dot(s.astype(BF16), vc)
        den = decay * qcn[:, dh:2 * dh] + jnp.sum(s, axis=1, keepdims=True)
        hch = num / jnp.maximum(jnp.abs(den), jnp.exp(-(b_col + g)))
        w_log = b_end + a_row
        m_new = jnp.maximum(b_end + m_prev, jnp.max(w_log, axis=1, keepdims=True))
        kw = (ktc.astype(F32) * jnp.exp(w_log - m_new)).astype(BF16)
        v_aug = jnp.concatenate([vc, ones_blk], axis=1)
        cn_ref[j, d] = jnp.exp(b_end + m_prev - m_new) * cn_ref[j, d] + _dot(kw, v_aug)
        m_ref[j, d] = jnp.broadcast_to(m_new, (8, LANES))
        return t0, hch

    if n_chunks == 1:
        for j in range(hp):
            _, h_f = chunk(j, 0, 0)
            _, h_b = chunk(j, 1, 0)
            hs_ref[:, j * dh:(j + 1) * dh] = h_f + h_b
    else:
        half = n_chunks // 2

        def first(i, carry):
            for j in range(hp):
                lanes = slice(j * dh, (j + 1) * dh)
                t_f, h_f = chunk(j, 0, i)
                t_b, h_b = chunk(j, 1, n_chunks - 1 - i)
                hs_ref[pl.ds(t_f, L), lanes] = h_f
                hs_ref[pl.ds(t_b, L), lanes] = h_b
            return carry

        def second(i, carry):
            for j in range(hp):
                lanes = slice(j * dh, (j + 1) * dh)
                t_f, h_f = chunk(j, 0, i)
                t_b, h_b = chunk(j, 1, n_chunks - 1 - i)
                hs_ref[pl.ds(t_f, L), lanes] += h_f
                hs_ref[pl.ds(t_b, L), lanes] += h_b
            return carry

        lax.fori_loop(0, half, first, 0)
        lax.fori_loop(half, n_chunks, second, 0)

    def finish(c, carry):
        t0 = pl.multiple_of(c * L, L)
        for j in range(hp):
            lanes = slice(j * dh, (j + 1) * dh)
            ha = hs_ref[pl.ds(t0, L), lanes]
            ha = ha * lax.rsqrt(jnp.mean(ha * ha, axis=-1, keepdims=True) + EPS)
            ha = ha * ghn_ref[:, lanes] * _sigmoid(o_ref[pl.ds(t0, L), lanes])
            ha_ref[pl.ds(t0, L), lanes] = ha.astype(BF16)
        return carry

    if n_chunks == 1:
        finish(0, 0)
    else:
        lax.fori_loop(0, n_chunks, finish, 0)

    if emit_state:
        for j in range(hp):
            for d in range(2):
                cout_ref[d, j] = cn_ref[j, d, :, 0:dh]
                n_rep = cn_ref[j, d, :, dh:2 * dh]
                nout_ref[d, j] = jnp.sum(jnp.where(sq_r == sq_c, n_rep, 0.0), axis=0, keepdims=True)
                mout_ref[d, j] = m_ref[j, d][0:1, :]


def _mlstm(q, kt, v, o, g_hn, gr, n_seq, seq_len, states, hp):
    n_tok, d = q.shape
    L = CHUNK
    n_chunks = seq_len // L
    has_state = states is not None
    emit_state = not has_state
    nh, dh = N_HEADS, HEAD_DIM
    assert n_chunks == 1 or n_chunks % 2 == 0
    assert nh % hp == 0
    wd = hp * dh

    in_specs = []
    args = []
    if has_state:
        state_c, state_n, state_m = states
        in_specs.append(pl.BlockSpec(memory_space=pltpu.SMEM))
        args.append(state_m.reshape(-1))
    tok = pl.BlockSpec((seq_len, wd), lambda b, h: (b, h))
    in_specs += [tok,
                 pl.BlockSpec((n_chunks, wd, L), lambda b, h: (b, h, 0)),
                 tok, tok,
                 pl.BlockSpec((1, wd), lambda b, h: (0, h)),
                 pl.BlockSpec((n_chunks, 4 * nh, L), lambda b, h: (b, 0, 0))]
    args += [q, kt, v, o, g_hn.reshape(1, d), gr]
    st_c = pl.BlockSpec((None, None, 2, hp, dh, dh), lambda b, h: (b, 0, 0, h, 0, 0))
    st_v = pl.BlockSpec((None, None, 2, hp, 1, dh), lambda b, h: (b, 0, 0, h, 0, 0))
    if has_state:
        in_specs += [st_c, st_v]
        args += [state_c, state_n.reshape(n_seq, 1, 2, nh, 1, dh)]
    out_shape = [jax.ShapeDtypeStruct((n_tok, d), BF16)]
    out_specs = [tok]
    if emit_state:
        out_shape += [jax.ShapeDtypeStruct((n_seq, 1, 2, nh, dh, dh), F32),
                      jax.ShapeDtypeStruct((n_seq, 1, 2, nh, 1, dh), F32),
                      jax.ShapeDtypeStruct((n_seq, 1, 2, nh, 1, LANES), F32)]
        out_specs += [st_c, st_v, st_v]
    scratch = [pltpu.VMEM((seq_len, wd), F32),
               pltpu.VMEM((hp, 2, dh, 2 * dh), F32),
               pltpu.VMEM((hp, 2, 8, LANES), F32)]
    return pl.pallas_call(
        functools.partial(_mlstm_kernel, n_chunks, hp, has_state, emit_state),
        grid=(n_seq, nh // hp),
        in_specs=in_specs,
        out_specs=out_specs,
        out_shape=out_shape,
        scratch_shapes=scratch,
        compiler_params=_params(("parallel", "parallel")),
        name="mlstm",
    )(*args)


def _fnet_kernel(seq_len, u_ref, f_ref, cs_ref, y_ref, ab_ref):
    T = seq_len

    @pl.when(pl.program_id(1) == 0)
    def _():
        for g in range(N_FGROUPS):
            ug = u_ref[:, g * FGROUP_DIM:(g + 1) * FGROUP_DIM]
            ab_ref[0:T, g * FGROUP_DIM:(g + 1) * FGROUP_DIM] = _dot(ug, cs_ref[0]).astype(BF16)
            ab_ref[T:2 * T, g * FGROUP_DIM:(g + 1) * FGROUP_DIM] = _dot(ug, cs_ref[1]).astype(BF16)

    scale = 1.0 / math.sqrt(T * FGROUP_DIM)
    y_ref[...] = (_dot(f_ref[...], ab_ref[...]) * scale).astype(BF16)


def _fnet(u, f_mat, cs_mat, n_seq, seq_len):
    n_tok, du = u.shape
    tr = min(seq_len, 512)
    return pl.pallas_call(
        functools.partial(_fnet_kernel, seq_len),
        grid=(n_seq, seq_len // tr),
        in_specs=[pl.BlockSpec((seq_len, du), lambda b, r: (b, 0)),
                  pl.BlockSpec((tr, 2 * seq_len), lambda b, r: (r, 0)),
                  _const_spec(cs_mat.shape)],
        out_specs=pl.BlockSpec((tr, du), lambda b, r: (b * (seq_len // tr) + r, 0)),
        out_shape=jax.ShapeDtypeStruct((n_tok, du), BF16),
        scratch_shapes=[pltpu.VMEM((2 * seq_len, du), BF16)],
        compiler_params=_params(("parallel", "arbitrary")),
        name="fnet",
    )(u, f_mat, cs_mat)


def _post_kernel(has_pos, *refs):
    if has_pos:
        x_ref, pos_ref = refs[0], refs[1]
        refs = refs[2:]
    else:
        x_ref = refs[0]
        refs = refs[1:]
    (ha_ref, yf_ref, ga_ref, gb_ref, mod_ref, gp1_ref, gp2_ref, wa_ref, wf_ref, wout_ref, wr_ref, br_ref,
     x1_ref, h2_ref, lg_ref) = refs
    x = x_ref[...]
    if has_pos:
        x = x + pos_ref[...]
    ya = _dot(ha_ref[...], wa_ref[...])
    yf = _dot(yf_ref[...], wf_ref[...])
    mix_in = _sigmoid(ga_ref[...]) * ya + _sigmoid(gb_ref[...]) * yf
    mix = _dot(mix_in.astype(BF16), wout_ref[...])
    gt1 = mod_ref[2:3, :]
    sh2 = mod_ref[3:4, :]
    sc2 = mod_ref[4:5, :]
    x1 = x + gt1 * _rms(mix, gp1_ref[...])
    h2 = _rms(x1, gp2_ref[...]) * (1.0 + sc2) + sh2
    x1_ref[...] = x1
    h2_ref[...] = h2
    lg = _dot_f32(wr_ref[...], h2, nt=True) + br_ref[...]
    for s in range(CHUNKS_PER_STEP):
        lg_ref[s] = lg[:, s * TOKEN_TILE:(s + 1) * TOKEN_TILE]


def _post(x, pos, ha, yf, ga, gb, mod, mod_rows, seq_len, g_post1, g_pre2, wts):
    n_tok, d = x.shape
    tc = TOKEN_TILE
    ns = CHUNKS_PER_STEP
    tm = ns * tc
    tiles_per_seq = max(seq_len // tm, 1)
    has_pos = pos is not None
    (wa, wf, wout, wr, br) = wts
    du = yf.shape[1]
    row = pl.BlockSpec((tm, d), lambda i: (i, 0))
    in_specs = [row]
    args = [x]
    if has_pos:
        in_specs.append(pl.BlockSpec((tm, d), lambda i: (i % tiles_per_seq, 0)))
        args.append(pos)
    in_specs += [row, pl.BlockSpec((tm, du), lambda i: (i, 0)), row, row,
                 _mod_spec(d, tm, seq_len, mod_rows),
                 _const_spec((1, d)), _const_spec((1, d))]
    in_specs += [_const_spec(w.shape) for w in wts]
    args += [ha, yf, ga, gb, mod, g_post1.reshape(1, d), g_pre2.reshape(1, d)] + list(wts)
    return pl.pallas_call(
        functools.partial(_post_kernel, has_pos),
        grid=(n_tok // tm,),
        in_specs=in_specs,
        out_specs=[row, row, pl.BlockSpec((ns, ROUTE_ROWS, tc), lambda i: (i, 0, 0))],
        out_shape=[jax.ShapeDtypeStruct((n_tok, d), F32),
                   jax.ShapeDtypeStruct((n_tok, d), F32),
                   jax.ShapeDtypeStruct((n_tok // tc, ROUTE_ROWS, tc), F32)],
        compiler_params=_params(("parallel",)),
        name="post",
    )(*args)


ROUTE_ROWS = N_EXPERTS + 8
FFN_TILE = 256
PLAN_ITEMS = LANES


def _route_rows(lg):
    tm = lg.shape[1]
    ne = N_EXPERTS
    gl = lg[ne:ne + 8]
    g_id = lax.broadcasted_iota(jnp.int32, (8, tm), 0).astype(F32)
    is_g = g_id < N_GROUPS
    gl = jnp.where(is_g, gl, -jnp.inf)
    mx = jnp.max(gl, axis=0, keepdims=True)
    z = jnp.sum(jnp.where(is_g, jnp.exp(gl - mx), 0.0), axis=0, keepdims=True)
    p_sel = 1.0 / z
    g_sel = jnp.min(jnp.where(gl == mx, g_id, float(N_GROUPS)), axis=0, keepdims=True)
    e_id = lax.broadcasted_iota(jnp.int32, (ne, tm), 0).astype(F32)
    lo = EXPERTS_PER_GROUP * g_sel
    in_grp = (e_id >= lo) & (e_id < lo + EXPERTS_PER_GROUP)
    le = jnp.where(in_grp, lg[0:ne], -jnp.inf)
    v1 = jnp.max(le, axis=0, keepdims=True)
    i1 = jnp.min(jnp.where(le == v1, e_id, float(ne)), axis=0, keepdims=True)
    le2 = jnp.where(e_id == i1, -jnp.inf, le)
    v2 = jnp.max(le2, axis=0, keepdims=True)
    i2 = jnp.min(jnp.where(le2 == v2, e_id, float(ne)), axis=0, keepdims=True)
    e2 = jnp.exp(v2 - v1)
    w1 = p_sel / (1.0 + e2)
    w2 = p_sel * e2 / (1.0 + e2)
    return i1, i2, w1, w2


def _route_kernel(n_tiles, lg_ref, etri_ref, ttri_ref, pos_ref, w_ref, plan_ref, meta_ref, cnt_ref, run_ref, base_ref):
    p = pl.program_id(0)
    i = pl.program_id(1)
    tm = lg_ref.shape[1]
    sub = lax.broadcasted_iota(jnp.int32, (N_EXPERTS, tm), 0).astype(F32)
    tf = float(FFN_TILE)

    @pl.when(p == 0)
    def _():
        e_a, e_b, w_a, w_b = _route_rows(lg_ref[...])
        meta_ref[i, 0:1, :] = e_a
        meta_ref[i, 1:2, :] = e_b
        meta_ref[i, 2:3, :] = w_a
        meta_ref[i, 3:4, :] = w_b
        meta_ref[i, 4:8, :] = jnp.zeros((4, tm), F32)
        oh = jnp.where((sub == e_a) | (sub == e_b), 1.0, 0.0)

        @pl.when(i == 0)
        def _():
            cnt_ref[...] = jnp.zeros_like(cnt_ref)

        cnt_ref[...] += jnp.broadcast_to(jnp.sum(oh, axis=1, keepdims=True), cnt_ref.shape)

    @pl.when(p == 1)
    def _():
        @pl.when(i == 0)
        def _():
            c1, c2, c3 = _split3(cnt_ref[...])
            base_ref[...] = _dot(etri_ref[...], c1) + (_dot(etri_ref[...], c2) + _dot(etri_ref[...], c3))
            run_ref[...] = jnp.zeros_like(run_ref)

        e_a = meta_ref[i, 0:1, :]
        e_b = meta_ref[i, 1:2, :]
        oh_a = jnp.where(sub == e_a, 1.0, 0.0)
        oh_b = jnp.where(sub == e_b, 1.0, 0.0)
        oh = oh_a + oh_b
        before = _dot(oh.astype(BF16), ttri_ref[...]) + run_ref[...] + base_ref[...]
        pos_ref[...] = jnp.zeros(pos_ref.shape, jnp.int32)
        pos_ref[0:1, :] = jnp.sum(oh_a * before, axis=0, keepdims=True).astype(jnp.int32)
        pos_ref[1:2, :] = jnp.sum(oh_b * before, axis=0, keepdims=True).astype(jnp.int32)
        w_ref[...] = meta_ref[i]
        run_ref[...] += jnp.broadcast_to(jnp.sum(oh, axis=1, keepdims=True), run_ref.shape)

        @pl.when(i == n_tiles - 1)
        def _():
            cnt = cnt_ref[:, 0:PLAN_ITEMS]
            base = base_ref[:, 0:PLAN_ITEMS]
            e_id = lax.broadcasted_iota(jnp.int32, (N_EXPERTS, PLAN_ITEMS), 0)
            t_lo = jnp.floor(base / tf)
            t_hi = jnp.floor((base + cnt - 1.0) / tf)
            n_items = jnp.where(cnt > 0.0, t_hi - t_lo + 1.0, 0.0)
            i_start = _dot(etri_ref[...], n_items.astype(BF16))
            item = lax.broadcasted_iota(jnp.int32, (N_EXPERTS, PLAN_ITEMS), 1).astype(F32)
            sel = (i_start <= item) & (item < i_start + n_items)
            tile_e = t_lo + item - i_start
            off = base - tile_e * tf

            def pick(v):
                return jnp.sum(jnp.where(sel, v, 0.0), axis=0, keepdims=True).astype(jnp.int32)

            plan_ref[...] = jnp.zeros(plan_ref.shape, jnp.int32)
            plan_ref[0:1, :] = pick(e_id.astype(F32))
            plan_ref[1:2, :] = pick(tile_e)
            plan_ref[2:3, :] = pick(jnp.maximum(off, 0.0))
            plan_ref[3:4, :] = pick(jnp.minimum(off + cnt, tf))
            plan_ref[4:5, :] = jnp.sum(n_items, axis=0, keepdims=True).astype(jnp.int32)


def _route(lg, tris):
    expert_tri, token_tri = tris
    n_tiles, _, tm = lg.shape
    return pl.pallas_call(
        functools.partial(_route_kernel, n_tiles),
        grid=(2, n_tiles),
        in_specs=[pl.BlockSpec((None, ROUTE_ROWS, tm), lambda p, i: (i, 0, 0)),
                  _const_spec(expert_tri.shape), _const_spec(token_tri.shape)],
        out_specs=[pl.BlockSpec((None, 8, tm), lambda p, i: (i * p, 0, 0)),
                   pl.BlockSpec((None, 8, tm), lambda p, i: (i * p, 0, 0)),
                   pl.BlockSpec((8, PLAN_ITEMS), lambda p, i: (0, 0))],
        out_shape=[jax.ShapeDtypeStruct((n_tiles, 8, tm), jnp.int32),
                   jax.ShapeDtypeStruct((n_tiles, 8, tm), F32),
                   jax.ShapeDtypeStruct((8, PLAN_ITEMS), jnp.int32)],
        scratch_shapes=[pltpu.VMEM((n_tiles, 8, tm), F32),
                        pltpu.VMEM((N_EXPERTS, tm), F32),
                        pltpu.VMEM((N_EXPERTS, tm), F32),
                        pltpu.VMEM((N_EXPERTS, tm), F32)],
        compiler_params=_params(("arbitrary", "arbitrary")),
        name="route",
    )(lg, expert_tri, token_tri)


ROW_UNROLL = 8


DISPATCH_BLOCKS = 4


def _dispatch_kernel(tm, pos_ref, h_ref, xs_ref, sem):
    def row_copy(t, dst):
        return pltpu.make_async_copy(h_ref.at[pl.ds(t, 1)], xs_ref.at[pl.ds(dst, 1)], sem)

    for blk in range(DISPATCH_BLOCKS):
        def issue(g, carry, blk=blk):
            for u in range(ROW_UNROLL):
                t = g * ROW_UNROLL + u
                row_copy(blk * tm + t, pos_ref[blk, 0, t]).start()
                row_copy(blk * tm + t, pos_ref[blk, 1, t]).start()
            return carry

        lax.fori_loop(0, tm // ROW_UNROLL, issue, 0)

    def drain(g, carry):
        for _ in range(2 * ROW_UNROLL):
            row_copy(0, 0).wait()
        return carry

    lax.fori_loop(0, DISPATCH_BLOCKS * tm // ROW_UNROLL, drain, 0)


def _dispatch(pos, h2, n_rows):
    n_tiles, _, tm = pos.shape
    d = h2.shape[1]
    nb = DISPATCH_BLOCKS
    assert n_tiles % nb == 0
    return pl.pallas_call(
        functools.partial(_dispatch_kernel, tm),
        grid=(n_tiles // nb,),
        in_specs=[pl.BlockSpec((nb, 8, tm), lambda i: (i, 0, 0), memory_space=pltpu.SMEM),
                  pl.BlockSpec((nb * tm, d), lambda i: (i, 0))],
        out_specs=pl.BlockSpec(memory_space=pl.ANY),
        out_shape=jax.ShapeDtypeStruct((n_rows, d), F32),
        scratch_shapes=[pltpu.SemaphoreType.DMA],
        compiler_params=_params(("arbitrary",)),
        name="dispatch",
    )(pos, h2)


def _ffn_kernel(owner_ref, tile_ref, lo_ref, hi_ref, used_ref, x_ref, w1_ref, w3_ref, w2_ref, y_ref):
    w = pl.program_id(0)

    @pl.when(w < used_ref[0])
    def _():
        x = x_ref[...].astype(BF16)
        a = _dot(x, w1_ref[...].astype(BF16))
        b = _dot(x, w3_ref[...].astype(BF16))
        hid = (a * _sigmoid(a)) * b
        y = _dot(hid.astype(BF16), w2_ref[...].astype(BF16))
        row = lax.broadcasted_iota(jnp.int32, y.shape, 0)
        mine = (row >= lo_ref[w]) & (row < hi_ref[w])
        first = (w == 0) | (tile_ref[w] != tile_ref[jnp.maximum(w - 1, 0)])

        @pl.when(first)
        def _():
            y_ref[...] = jnp.where(mine, y, 0.0)

        @pl.when(jnp.logical_not(first))
        def _():
            y_ref[...] = jnp.where(mine, y, y_ref[...])


def _ffn(plan, xs, w1, w3, w2):
    n_rows, d = xs.shape
    tf = FFN_TILE
    de = w1.shape[1] // N_EXPERTS
    n_items = n_rows // tf + N_EXPERTS - 1
    assert n_items <= PLAN_ITEMS

    def item(w, used):
        return jnp.minimum(w, used[0] - 1)

    grid_spec = pltpu.PrefetchScalarGridSpec(
        num_scalar_prefetch=5,
        grid=(n_items,),
        in_specs=[pl.BlockSpec((tf, d), lambda w, own, til, lo, hi, used: (til[item(w, used)], 0)),
                  pl.BlockSpec((d, de), lambda w, own, til, lo, hi, used: (0, own[item(w, used)])),
                  pl.BlockSpec((d, de), lambda w, own, til, lo, hi, used: (0, own[item(w, used)])),
                  pl.BlockSpec((de, d), lambda w, own, til, lo, hi, used: (own[item(w, used)], 0))],
        out_specs=pl.BlockSpec((tf, d), lambda w, own, til, lo, hi, used: (til[item(w, used)], 0)),
    )
    return pl.pallas_call(
        _ffn_kernel,
        grid_spec=grid_spec,
        out_shape=jax.ShapeDtypeStruct((n_rows, d), F32),
        compiler_params=_params(("arbitrary",)),
        name="ffn",
    )(plan[0], plan[1], plan[2], plan[3], plan[4, 0:1], xs, w1, w3, w2)


def _combine_kernel(tm, pos_ref, nxt_ref, w_ref, x1_ref, mod_ref, gp_ref, ys_ref, y_ref, buf_ref, sem):
    i = pl.program_id(0)
    n = pl.num_programs(0)
    slot = i % 2

    def row_copy(src, k, t, s):
        return pltpu.make_async_copy(ys_ref.at[pl.ds(src, 1)], buf_ref.at[s, k, pl.ds(t, 1)], sem.at[s])

    def issue(p_ref, s):
        def body(g, carry):
            for u in range(ROW_UNROLL):
                t = g * ROW_UNROLL + u
                row_copy(p_ref[0, t], 0, t, s).start()
                row_copy(p_ref[1, t], 1, t, s).start()
            return carry
        lax.fori_loop(0, tm // ROW_UNROLL, body, 0)

    @pl.when(i == 0)
    def _():
        issue(pos_ref, 0)

    @pl.when(i + 1 < n)
    def _():
        issue(nxt_ref, 1 - slot)

    def drain(g, carry):
        for _ in range(2 * ROW_UNROLL):
            row_copy(0, 0, 0, slot).wait()
        return carry

    lax.fori_loop(0, tm // ROW_UNROLL, drain, 0)

    r = lax.broadcasted_iota(jnp.int32, (tm, tm), 0)
    c = lax.broadcasted_iota(jnp.int32, (tm, tm), 1)
    d = y_ref.shape[1]
    wa = jnp.broadcast_to(jnp.sum(jnp.where(r == c, w_ref[2:3, :], 0.0), axis=1, keepdims=True), (tm, d))
    wb = jnp.broadcast_to(jnp.sum(jnp.where(r == c, w_ref[3:4, :], 0.0), axis=1, keepdims=True), (tm, d))
    moe = wa * buf_ref[slot, 0] + wb * buf_ref[slot, 1]
    gt2 = mod_ref[5:6, :]
    y_ref[...] = x1_ref[...] + gt2 * _rms(moe, gp_ref[...])


def _combine(pos, w, x1, mod, mod_rows, seq_len, g_post2, ys):
    n_tiles, _, tm = pos.shape
    n_tok, d = x1.shape
    row = pl.BlockSpec((tm, d), lambda i: (i, 0))
    return pl.pallas_call(
        functools.partial(_combine_kernel, tm),
        grid=(n_tiles,),
        in_specs=[pl.BlockSpec((None, 8, tm), lambda i: (i, 0, 0), memory_space=pltpu.SMEM),
                  pl.BlockSpec((None, 8, tm), lambda i: (jnp.minimum(i + 1, n_tiles - 1), 0, 0),
                               memory_space=pltpu.SMEM),
                  pl.BlockSpec((None, 8, tm), lambda i: (i, 0, 0)),
                  row,
                  _mod_spec(d, tm, seq_len, mod_rows),
                  _const_spec((1, d)),
                  pl.BlockSpec(memory_space=pl.ANY)],
        out_specs=row,
        out_shape=jax.ShapeDtypeStruct((n_tok, d), F32),
        scratch_shapes=[pltpu.VMEM((2, 2, tm, d), F32), pltpu.SemaphoreType.DMA((2,))],
        compiler_params=_params(("arbitrary",)),
        name="combine",
    )(pos, pos, w, x1, mod, g_post2.reshape(1, d), ys)


def _moe(h2, lg, x1, mod, mod_rows, seq_len, g_post2, w1, w3, w2, route_tri):
    n_tok, d = x1.shape
    n_rows = 2 * n_tok
    assert n_rows % FFN_TILE == 0
    pos, w, plan = _route(lg, route_tri)
    xs = _dispatch(pos, h2, n_rows)
    ys = _ffn(plan, xs, w1, w3, w2)
    return _combine(pos, w, x1, mod, mod_rows, seq_len, g_post2, ys)


def _grid_pos(n_tokens, d_model):
    rows = n_tokens // GRID_W
    row = np.repeat(np.arange(rows, dtype=np.float64), GRID_W)
    col = np.tile(np.arange(GRID_W, dtype=np.float64), rows)
    n_freq = d_model // 4
    freq = np.exp(-math.log(POS_BASE) * np.arange(n_freq, dtype=np.float64) / n_freq)

    def enc(p):
        a = p[:, None] * freq[None, :]
        return np.concatenate([np.sin(a), np.cos(a)], axis=-1)

    return jnp.asarray(np.concatenate([enc(row), enc(col)], axis=-1), dtype=F32)


def _dft_cos_sin(n):
    k = np.arange(n, dtype=np.int64)
    ang = 2.0 * np.pi * ((k[:, None] * k[None, :]) % n).astype(np.float64) / n
    return np.cos(ang), np.sin(ang)


def _fnet_consts(seq_len):
    ct, st = _dft_cos_sin(seq_len)
    f_mat = jnp.asarray(np.concatenate([ct, -st], axis=1), dtype=F32).astype(BF16)
    cc, sc = _dft_cos_sin(FGROUP_DIM)
    cs_mat = jnp.asarray(np.stack([cc, sc]), dtype=F32).astype(BF16)
    return f_mat, cs_mat


def _tri_consts():
    i = np.arange(CHUNK)
    prefix = (i[:, None] <= i[None, :]).astype(np.float32)
    suffix = (i[:, None] >= i[None, :]).astype(np.float32)
    tri = jnp.asarray(np.stack([prefix, suffix]), dtype=BF16)
    e = np.arange(N_EXPERTS)
    expert_tri = jnp.asarray((e[None, :] < e[:, None]).astype(np.float32), dtype=BF16)
    t = np.arange(TOKEN_TILE)
    token_tri = jnp.asarray((t[:, None] < t[None, :]).astype(np.float32), dtype=BF16)
    return tri, (expert_tri, token_tri)


def _run_path(x, pos, n_seq, seq_len, mod, mod_rows, states, lw, consts):
    d = x.shape[-1]
    (g_pre1, proj_w, g_hn, post_w, g_post1, g_pre2, w_e1, w_e3, w_e2, g_post2) = lw
    tri, route_tris = consts
    x2 = x.reshape(n_seq * seq_len, d)
    q, kt, v, o, u, ga, gb, gr = _proj(x2, pos, mod, mod_rows, seq_len, g_pre1, proj_w + (tri,))
    heads_per_step = N_HEADS if seq_len == CHUNK else 2
    outs = _mlstm(q, kt, v, o, g_hn, gr, n_seq, seq_len, states, heads_per_step)
    ha = outs[0]
    f_mat, cs_mat = _fnet_consts(seq_len)
    yf = _fnet(u, f_mat, cs_mat, n_seq, seq_len)
    x1, h2, logits = _post(x2, pos, ha, yf, ga, gb, mod, mod_rows, seq_len, g_post1, g_pre2, post_w)
    y = _moe(h2, logits, x1, mod, mod_rows, seq_len, g_post2, w_e1, w_e3, w_e2, route_tris)
    return y.reshape(n_seq, seq_len, d), outs[1:]


def kernel(x_prompt, x_sample, c, state_C, state_n, state_m, c_ctx, w_ada, b_ada, g_pre1, w_in, b_gates, g_hn,
           w_a, w_f, w_out, g_post1, g_pre2, w_rg, b_rg, w_re, b_re, w_e1, w_e3, w_e2, g_post2):
    bp, sp, d = x_prompt.shape
    bs, ss, _ = x_sample.shape
    depth = w_in.shape[0]
    assert depth == 1
    l = 0
    dm = N_HEADS * HEAD_DIM
    du = N_FGROUPS * FGROUP_DIM
    ng = 4 * N_HEADS

    n_rows = 8
    assert 1 + bs <= n_rows
    cs = jnp.concatenate([c_ctx[None, :], c, jnp.zeros((n_rows - 1 - bs, d), F32)], axis=0)
    mod = _ada(cs, w_ada[l], b_ada[l]).reshape(n_rows, 6, d)

    wi = w_in[l]
    cuts = np.cumsum([dm] * 4 + [ng, du, d]).tolist()
    wq = wi[:, :cuts[0]].astype(BF16)
    wkt = wi[:, cuts[0]:cuts[1]].T.astype(BF16)
    wv = wi[:, cuts[1]:cuts[2]].astype(BF16)
    wo = wi[:, cuts[2]:cuts[3]].astype(BF16)
    wgt = wi[:, cuts[3]:cuts[4]].T
    wu = wi[:, cuts[4]:cuts[5]].astype(BF16)
    wga = wi[:, cuts[5]:cuts[6]].astype(BF16)
    wgb = wi[:, cuts[6]:].astype(BF16)
    bg = b_gates[l].reshape(ng, 1)
    proj_w = (wq, wkt, wv, wo, wu, wga, wgb, wgt, bg)
    n_pad = ROUTE_ROWS - N_EXPERTS - N_GROUPS
    wr = jnp.concatenate([w_re[l], w_rg[l], jnp.zeros((d, n_pad), F32)], axis=1).T
    br = jnp.concatenate([b_re[l], b_rg[l], jnp.zeros((n_pad,), F32)]).reshape(ROUTE_ROWS, 1)
    post_w = (w_a[l].astype(BF16), w_f[l].astype(BF16), w_out[l].astype(BF16), wr, br)
    lw = (g_pre1[l], proj_w, g_hn[l], post_w, g_post1[l], g_pre2[l], w_e1[l], w_e3[l], w_e2[l], g_post2[l])
    consts = _tri_consts()

    y_prompt, (new_c, new_n, new_m) = _run_path(x_prompt, None, bp, sp, mod, (0, 0), None, lw, consts)
    pos = _grid_pos(ss, d)
    y_sample, _ = _run_path(x_sample, pos, bs, ss, mod, (1, 1), (state_C, state_n, state_m), lw, consts)

    new_state_c = new_c
    new_state_n = new_n.reshape(bp, depth, 2, N_HEADS, HEAD_DIM)
    new_state_m = new_m[..., 0, 0]
    return (y_prompt, y_sample, new_state_c, new_state_n, new_state_m)
```

```python
import functools
import math
from typing import NamedTuple

import numpy as np
import jax
import jax.numpy as jnp
from jax import lax
from jax.experimental import pallas as pl
from jax.experimental.pallas import tpu as pltpu

F32 = jnp.float32
BF16 = jnp.bfloat16

N_HEADS = 8
HEAD_DIM = 128
N_FGROUPS = 4
FGROUP_DIM = 128
N_GROUPS = 4
EXPERTS_PER_GROUP = 8
N_EXPERTS = N_GROUPS * EXPERTS_PER_GROUP
GRID_W = 64
POS_BASE = 10000.0
EPS = 1e-6

V7X_VMEM_LIMIT_BYTES = 56 * 1024 * 1024
LANES = 128
CHUNK = 256
TOKEN_TILE = CHUNK
CHUNKS_PER_STEP = 2


def _params(sem):
    return pltpu.CompilerParams(dimension_semantics=sem, vmem_limit_bytes=V7X_VMEM_LIMIT_BYTES)


def _const_spec(shape):
    n = len(shape)
    return pl.BlockSpec(shape, lambda *_: (0,) * n, pipeline_mode=pl.Buffered(1))


def _split2(x):
    hi = x.astype(BF16)
    lo = (x - hi.astype(F32)).astype(BF16)
    return hi, lo


def _split3(x):
    hi = x.astype(BF16)
    r = x - hi.astype(F32)
    mid = r.astype(BF16)
    lo = (r - mid.astype(F32)).astype(BF16)
    return hi, mid, lo


def _dot(a, b):
    return jnp.dot(a, b, preferred_element_type=F32)


def _dot_nt(a, b):
    return lax.dot_general(a, b, (((1,), (1,)), ((), ())), preferred_element_type=F32)


def _dot_f32(a, b, nt=False):
    d = _dot_nt if nt else _dot
    a1, a2 = _split2(a)
    b1, b2 = _split2(b)
    return d(a1, b1) + (d(a1, b2) + d(a2, b1))


def _dot_exact_rhs(a, b_bf16, nt=False):
    d = _dot_nt if nt else _dot
    a1, a2, a3 = _split3(a)
    return d(a1, b_bf16) + (d(a2, b_bf16) + d(a3, b_bf16))


def _rms(x, g):
    return x * lax.rsqrt(jnp.mean(x * x, axis=-1, keepdims=True) + EPS) * g


def _sigmoid(x):
    return 1.0 / (1.0 + jnp.exp(-x))


def _log_sigmoid(x):
    return jnp.minimum(x, 0.0) - jnp.log(1.0 + jnp.exp(-jnp.abs(x)))


def _ada_kernel(c_ref, w_ref, b_ref, o_ref):
    c = c_ref[...]
    s = (c * _sigmoid(c)).astype(BF16)
    o_ref[...] = _dot(s, w_ref[...].astype(BF16)) + b_ref[...]


def _ada(cs, w_ada, b_ada):
    rows, d = cs.shape
    n = w_ada.shape[1]
    tn = 1024
    return pl.pallas_call(
        _ada_kernel,
        grid=(n // tn,),
        in_specs=[pl.BlockSpec((rows, d), lambda j: (0, 0)),
                  pl.BlockSpec((d, tn), lambda j: (0, j)),
                  pl.BlockSpec((1, tn), lambda j: (0, j))],
        out_specs=pl.BlockSpec((rows, tn), lambda j: (0, j)),
        out_shape=jax.ShapeDtypeStruct((rows, n), F32),
        compiler_params=_params(("arbitrary",)),
        name="ada",
    )(cs, w_ada, b_ada.reshape(1, n))


def _proj_kernel(ctx_steps, xc_ref, xl_ref, pos_ref, mod_ref, g_ref, wq_ref, wkt_ref, wv_ref, wo_ref, wu_ref,
                 wga_ref, wgb_ref, wgt_ref, bg_ref, tri_ref,
                 q_ref, kt_ref, v_ref, o_ref, u_ref, ga_ref, gb_ref, gr_ref):
    x = jnp.where(pl.program_id(0) < ctx_steps, xc_ref[...], xl_ref[...] + pos_ref[...])
    sh = mod_ref[0:1, :]
    sc = mod_ref[1:2, :]
    h = _rms(x, g_ref[...]) * (1.0 + sc) + sh
    hb = h.astype(BF16)
    q_ref[...] = _dot(hb, wq_ref[...]).astype(BF16)
    kt = (_dot_nt(wkt_ref[...], hb) * (HEAD_DIM ** -0.5)).astype(BF16)
    tc = TOKEN_TILE
    for s in range(CHUNKS_PER_STEP):
        kt_ref[s] = kt[:, s * tc:(s + 1) * tc]
    v_ref[...] = _dot(hb, wv_ref[...]).astype(BF16)
    o_ref[...] = _dot(hb, wo_ref[...])
    u_ref[...] = _dot(hb, wu_ref[...]).astype(BF16)
    ga_ref[...] = _dot(hb, wga_ref[...])
    gb_ref[...] = _dot(hb, wgb_ref[...])
    g = _dot_f32(wgt_ref[...], h, nt=True) + bg_ref[...]
    nh = N_HEADS
    lf_f = _log_sigmoid(g[nh:2 * nh])
    lf_b = _log_sigmoid(g[3 * nh:4 * nh])
    for s in range(CHUNKS_PER_STEP):
        cols = slice(s * tc, (s + 1) * tc)
        gr_ref[s, 0:nh, :] = g[0:nh, cols]
        gr_ref[s, nh:2 * nh, :] = _dot_exact_rhs(lf_f[:, cols], tri_ref[0])
        gr_ref[s, 2 * nh:3 * nh, :] = g[2 * nh:3 * nh, cols]
        gr_ref[s, 3 * nh:4 * nh, :] = _dot_exact_rhs(lf_b[:, cols], tri_ref[1])


class _Streams(NamedTuple):
    n_ctx: int
    n_lat: int
    seq_lat: int

    def ctx_steps(self, tm):
        assert self.n_ctx % tm == 0 and self.seq_lat % tm == 0
        return self.n_ctx // tm

    def ctx_spec(self, tm, width):
        c = self.ctx_steps(tm)
        return pl.BlockSpec((tm, width), lambda i, *_: (jnp.minimum(i, c - 1), 0))

    def lat_spec(self, tm, width):
        c = self.ctx_steps(tm)
        return pl.BlockSpec((tm, width), lambda i, *_: (jnp.maximum(i - c, 0), 0))

    def pos_spec(self, tm, width):
        c = self.ctx_steps(tm)
        per_seq = self.seq_lat // tm
        return pl.BlockSpec((tm, width), lambda i, *_: (jnp.maximum(i - c, 0) % per_seq, 0))

    def mod_spec(self, tm, width):
        c = self.ctx_steps(tm)
        per_seq = self.seq_lat // tm
        return pl.BlockSpec((None, 6, width),
                            lambda i, *_: (jnp.where(i < c, 0, 1 + jnp.maximum(i - c, 0) // per_seq), 0, 0))


def _proj(st, x_ctx, x_lat, pos, mod, g_pre1, wts):
    d = x_ctx.shape[1]
    n_tok = st.n_ctx + st.n_lat
    tc = TOKEN_TILE
    ns = CHUNKS_PER_STEP
    tm = ns * tc
    n_tiles = n_tok // tc
    (wq, wkt, wv, wo, wu, wga, wgb, wgt, bg, tri) = wts
    row = pl.BlockSpec((tm, d), lambda i: (i, 0))
    in_specs = [st.ctx_spec(tm, d), st.lat_spec(tm, d), st.pos_spec(tm, d), st.mod_spec(tm, d), _const_spec((1, d))]
    in_specs += [_const_spec(w.shape) for w in wts]
    args = [x_ctx, x_lat, pos, mod, g_pre1.reshape(1, d)] + list(wts)
    n_gate_rows = 4 * N_HEADS
    du = wu.shape[1]
    out_shape = [jax.ShapeDtypeStruct((n_tok, d), BF16),
                 jax.ShapeDtypeStruct((n_tiles, d, tc), BF16),
                 jax.ShapeDtypeStruct((n_tok, d), BF16),
                 jax.ShapeDtypeStruct((n_tok, d), F32),
                 jax.ShapeDtypeStruct((n_tok, du), BF16),
                 jax.ShapeDtypeStruct((n_tok, d), F32),
                 jax.ShapeDtypeStruct((n_tok, d), F32),
                 jax.ShapeDtypeStruct((n_tiles, n_gate_rows, tc), F32)]
    out_specs = [row,
                 pl.BlockSpec((ns, d, tc), lambda i: (i, 0, 0)),
                 row, row,
                 pl.BlockSpec((tm, du), lambda i: (i, 0)),
                 row, row,
                 pl.BlockSpec((ns, n_gate_rows, tc), lambda i: (i, 0, 0))]
    return pl.pallas_call(
        functools.partial(_proj_kernel, st.ctx_steps(tm)),
        grid=(n_tok // tm,),
        in_specs=in_specs,
        out_specs=out_specs,
        out_shape=out_shape,
        compiler_params=_params(("parallel",)),
        name="proj",
    )(*args)


def _mlstm_kernel(n_chunks, hp, has_state, emit_state, *refs):
    it = iter(refs)
    if has_state:
        m0_ref = next(it)
    q_ref, kt_ref, v_ref, o_ref, ghn_ref, gr_ref = (next(it) for _ in range(6))
    if has_state:
        c0_ref, n0_ref = next(it), next(it)
    ha_ref = next(it)
    if emit_state:
        cout_ref, nout_ref, mout_ref = next(it), next(it), next(it)
    hs_ref, cn_ref, m_ref = (next(it) for _ in range(3))

    L = CHUNK
    nh, dh = N_HEADS, HEAD_DIM
    head0 = pl.program_id(1) * hp
    neg_inf = -jnp.inf

    sq_r = lax.broadcasted_iota(jnp.int32, (dh, dh), 0)
    sq_c = lax.broadcasted_iota(jnp.int32, (dh, dh), 1)
    for j in range(hp):
        for d in range(2):
            if has_state:
                n_col = jnp.sum(jnp.where(sq_r == sq_c, n0_ref[d, j], 0.0), axis=1, keepdims=True)
                cn_ref[j, d, :, 0:dh] = c0_ref[d, j]
                cn_ref[j, d, :, dh:2 * dh] = jnp.broadcast_to(n_col, (dh, dh))
                m0 = m0_ref[pl.program_id(0) * (2 * nh) + d * nh + head0 + j]
                m_ref[j, d] = jnp.full((8, LANES), m0, F32)
            else:
                cn_ref[j, d] = jnp.zeros((dh, 2 * dh), F32)
                m_ref[j, d] = jnp.zeros((8, LANES), F32)

    row_id = lax.broadcasted_iota(jnp.int32, (L, L), 0)
    col_id = lax.broadcasted_iota(jnp.int32, (L, L), 1)
    ones_blk = jnp.ones((L, dh), BF16)

    def chunk(j, d, c):
        t0 = pl.multiple_of(c * L, L)
        lanes = slice(j * dh, (j + 1) * dh)
        qc = q_ref[pl.ds(t0, L), lanes]
        ktc = kt_ref[c, lanes, :]
        vc = v_ref[pl.ds(t0, L), lanes]
        ig_row = gr_ref[c, pl.ds(2 * nh * d + head0 + j, 1), :]
        b_row = gr_ref[c, pl.ds(2 * nh * d + nh + head0 + j, 1), :]
        m_prev = m_ref[j, d][0:1, 0:1]
        if d == 0:
            b_end = b_row[:, L - 1:L]
            mask = col_id <= row_id
        else:
            b_end = b_row[:, 0:1]
            mask = col_id >= row_id
        a_row = ig_row - b_row
        a_max = jnp.max(jnp.where(mask, a_row, neg_inf), axis=1, keepdims=True)
        g = jnp.maximum(jnp.broadcast_to(a_max, (L, dh)), m_prev)
        b_col = jnp.broadcast_to(
            jnp.sum(jnp.where(row_id == col_id, b_row, 0.0), axis=1, keepdims=True), (L, dh))
        g_full = jnp.concatenate([g] * (L // dh), axis=1)
        s = _dot(qc, ktc) * jnp.exp(jnp.where(mask, a_row - g_full, neg_inf))
        decay = jnp.exp(m_prev - g)
        qcn = _dot(qc, cn_ref[j, d].astype(BF16))
        num = decay * qcn[:, 0:dh] + _dot(s.astype(BF16), vc)
        den = decay * qcn[:, dh:2 * dh] + jnp.sum(s, axis=1, keepdims=True)
        hch = num / jnp.maximum(jnp.abs(den), jnp.exp(-(b_col + g)))
        w_log = b_end + a_row
        m_new = jnp.maximum(b_end + m_prev, jnp.max(w_log, axis=1, keepdims=True))
        kw = (ktc.astype(F32) * jnp.exp(w_log - m_new)).astype(BF16)
        v_aug = jnp.concatenate([vc, ones_blk], axis=1)
        cn_ref[j, d] = jnp.exp(b_end + m_prev - m_new) * cn_ref[j, d] + _dot(kw, v_aug)
        m_ref[j, d] = jnp.broadcast_to(m_new, (8, LANES))
        return t0, hch

    if n_chunks == 1:
        for j in range(hp):
            _, h_f = chunk(j, 0, 0)
            _, h_b = chunk(j, 1, 0)
            hs_ref[:, j * dh:(j + 1) * dh] = h_f + h_b
    else:
        half = n_chunks // 2

        def first(i, carry):
            for j in range(hp):
                lanes = slice(j * dh, (j + 1) * dh)
                t_f, h_f = chunk(j, 0, i)
                t_b, h_b = chunk(j, 1, n_chunks - 1 - i)
                hs_ref[pl.ds(t_f, L), lanes] = h_f
                hs_ref[pl.ds(t_b, L), lanes] = h_b
            return carry

        def second(i, carry):
            for j in range(hp):
                lanes = slice(j * dh, (j + 1) * dh)
                t_f, h_f = chunk(j, 0, i)
                t_b, h_b = chunk(j, 1, n_chunks - 1 - i)
                hs_ref[pl.ds(t_f, L), lanes] += h_f
                hs_ref[pl.ds(t_b, L), lanes] += h_b
            return carry

        lax.fori_loop(0, half, first, 0)
        lax.fori_loop(half, n_chunks, second, 0)

    def finish(c, carry):
        t0 = pl.multiple_of(c * L, L)
        for j in range(hp):
            lanes = slice(j * dh, (j + 1) * dh)
            ha = hs_ref[pl.ds(t0, L), lanes]
            ha = ha * lax.rsqrt(jnp.mean(ha * ha, axis=-1, keepdims=True) + EPS)
            ha = ha * ghn_ref[:, lanes] * _sigmoid(o_ref[pl.ds(t0, L), lanes])
            ha_ref[pl.ds(t0, L), lanes] = ha.astype(BF16)
        return carry

    if n_chunks == 1:
        finish(0, 0)
    else:
        lax.fori_loop(0, n_chunks, finish, 0)

    if emit_state:
        for j in range(hp):
            for d in range(2):
                cout_ref[d, j] = cn_ref[j, d, :, 0:dh]
                n_rep = cn_ref[j, d, :, dh:2 * dh]
                nout_ref[d, j] = jnp.sum(jnp.where(sq_r == sq_c, n_rep, 0.0), axis=0, keepdims=True)
                mout_ref[d, j] = m_ref[j, d][0:1, :]


def _mlstm(q, kt, v, o, g_hn, gr, n_seq, seq_len, tok_off, states, hp):
    d = q.shape[1]
    n_tok = n_seq * seq_len
    L = CHUNK
    assert tok_off % seq_len == 0
    sb = tok_off // seq_len
    n_chunks = seq_len // L
    has_state = states is not None
    emit_state = not has_state
    nh, dh = N_HEADS, HEAD_DIM
    assert n_chunks == 1 or n_chunks % 2 == 0
    assert nh % hp == 0
    wd = hp * dh

    in_specs = []
    args = []
    if has_state:
        state_c, state_n, state_m = states
        in_specs.append(pl.BlockSpec(memory_space=pltpu.SMEM))
        args.append(state_m.reshape(-1))
    tok = pl.BlockSpec((seq_len, wd), lambda b, h: (b + sb, h))
    in_specs += [tok,
                 pl.BlockSpec((n_chunks, wd, L), lambda b, h: (b + sb, h, 0)),
                 tok, tok,
                 pl.BlockSpec((1, wd), lambda b, h: (0, h)),
                 pl.BlockSpec((n_chunks, 4 * nh, L), lambda b, h: (b + sb, 0, 0))]
    args += [q, kt, v, o, g_hn.reshape(1, d), gr]
    st_c = pl.BlockSpec((None, None, 2, hp, dh, dh), lambda b, h: (b, 0, 0, h, 0, 0))
    st_v = pl.BlockSpec((None, None, 2, hp, 1, dh), lambda b, h: (b, 0, 0, h, 0, 0))
    if has_state:
        in_specs += [st_c, st_v]
        args += [state_c, state_n.reshape(n_seq, 1, 2, nh, 1, dh)]
    out_shape = [jax.ShapeDtypeStruct((n_tok, d), BF16)]
    out_specs = [pl.BlockSpec((seq_len, wd), lambda b, h: (b, h))]
    if emit_state:
        out_shape += [jax.ShapeDtypeStruct((n_seq, 1, 2, nh, dh, dh), F32),
                      jax.ShapeDtypeStruct((n_seq, 1, 2, nh, 1, dh), F32),
                      jax.ShapeDtypeStruct((n_seq, 1, 2, nh, 1, LANES), F32)]
        out_specs += [st_c, st_v, st_v]
    scratch = [pltpu.VMEM((seq_len, wd), F32),
               pltpu.VMEM((hp, 2, dh, 2 * dh), F32),
               pltpu.VMEM((hp, 2, 8, LANES), F32)]
    return pl.pallas_call(
        functools.partial(_mlstm_kernel, n_chunks, hp, has_state, emit_state),
        grid=(n_seq, nh // hp),
        in_specs=in_specs,
        out_specs=out_specs,
        out_shape=out_shape,
        scratch_shapes=scratch,
        compiler_params=_params(("parallel", "parallel")),
        name="mlstm",
    )(*args)


def _fnet_kernel(seq_len, u_ref, f_ref, cs_ref, y_ref, ab_ref):
    T = seq_len

    @pl.when(pl.program_id(1) == 0)
    def _():
        for g in range(N_FGROUPS):
            ug = u_ref[:, g * FGROUP_DIM:(g + 1) * FGROUP_DIM]
            ab_ref[0:T, g * FGROUP_DIM:(g + 1) * FGROUP_DIM] = _dot(ug, cs_ref[0]).astype(BF16)
            ab_ref[T:2 * T, g * FGROUP_DIM:(g + 1) * FGROUP_DIM] = _dot(ug, cs_ref[1]).astype(BF16)

    scale = 1.0 / math.sqrt(T * FGROUP_DIM)
    y_ref[...] = (_dot(f_ref[...], ab_ref[...]) * scale).astype(BF16)


def _fnet(u, f_mat, cs_mat, n_seq, seq_len, tok_off):
    du = u.shape[1]
    n_tok = n_seq * seq_len
    tr = min(seq_len, 512)
    assert tok_off % seq_len == 0
    sb = tok_off // seq_len
    return pl.pallas_call(
        functools.partial(_fnet_kernel, seq_len),
        grid=(n_seq, seq_len // tr),
        in_specs=[pl.BlockSpec((seq_len, du), lambda b, r: (b + sb, 0)),
                  pl.BlockSpec((tr, 2 * seq_len), lambda b, r: (r, 0)),
                  _const_spec(cs_mat.shape)],
        out_specs=pl.BlockSpec((tr, du), lambda b, r: (b * (seq_len // tr) + r, 0)),
        out_shape=jax.ShapeDtypeStruct((n_tok, du), BF16),
        scratch_shapes=[pltpu.VMEM((2 * seq_len, du), BF16)],
        compiler_params=_params(("parallel", "arbitrary")),
        name="fnet",
    )(u, f_mat, cs_mat)


def _post_kernel(ctx_steps, xc_ref, xl_ref, pos_ref, hac_ref, hal_ref, yfc_ref, yfl_ref, ga_ref, gb_ref, mod_ref,
                 gp1_ref, gp2_ref, wa_ref, wf_ref, wout_ref, wr_ref, br_ref, x1_ref, h2_ref, lg_ref):
    is_ctx = pl.program_id(0) < ctx_steps
    x = jnp.where(is_ctx, xc_ref[...], xl_ref[...] + pos_ref[...])
    ya = _dot(jnp.where(is_ctx, hac_ref[...], hal_ref[...]), wa_ref[...])
    yf = _dot(jnp.where(is_ctx, yfc_ref[...], yfl_ref[...]), wf_ref[...])
    mix_in = _sigmoid(ga_ref[...]) * ya + _sigmoid(gb_ref[...]) * yf
    mix = _dot(mix_in.astype(BF16), wout_ref[...])
    gt1 = mod_ref[2:3, :]
    sh2 = mod_ref[3:4, :]
    sc2 = mod_ref[4:5, :]
    x1 = x + gt1 * _rms(mix, gp1_ref[...])
    h2 = _rms(x1, gp2_ref[...]) * (1.0 + sc2) + sh2
    x1_ref[...] = x1
    h2_ref[...] = h2
    lg = _dot_f32(wr_ref[...], h2, nt=True) + br_ref[...]
    for s in range(CHUNKS_PER_STEP):
        lg_ref[s] = lg[:, s * TOKEN_TILE:(s + 1) * TOKEN_TILE]


def _post(st, x_ctx, x_lat, pos, ha_ctx, ha_lat, yf_ctx, yf_lat, ga, gb, mod, g_post1, g_pre2, wts):
    d = x_ctx.shape[1]
    n_tok = st.n_ctx + st.n_lat
    tc = TOKEN_TILE
    ns = CHUNKS_PER_STEP
    tm = ns * tc
    (wa, wf, wout, wr, br) = wts
    du = yf_ctx.shape[1]
    row = pl.BlockSpec((tm, d), lambda i: (i, 0))
    in_specs = [st.ctx_spec(tm, d), st.lat_spec(tm, d), st.pos_spec(tm, d),
                st.ctx_spec(tm, d), st.lat_spec(tm, d), st.ctx_spec(tm, du), st.lat_spec(tm, du),
                row, row, st.mod_spec(tm, d), _const_spec((1, d)), _const_spec((1, d))]
    in_specs += [_const_spec(w.shape) for w in wts]
    args = [x_ctx, x_lat, pos, ha_ctx, ha_lat, yf_ctx, yf_lat, ga, gb, mod,
            g_post1.reshape(1, d), g_pre2.reshape(1, d)] + list(wts)
    return pl.pallas_call(
        functools.partial(_post_kernel, st.ctx_steps(tm)),
        grid=(n_tok // tm,),
        in_specs=in_specs,
        out_specs=[row, row, pl.BlockSpec((ns, ROUTE_ROWS, tc), lambda i: (i, 0, 0))],
        out_shape=[jax.ShapeDtypeStruct((n_tok, d), F32),
                   jax.ShapeDtypeStruct((n_tok, d), F32),
                   jax.ShapeDtypeStruct((n_tok // tc, ROUTE_ROWS, tc), F32)],
        compiler_params=_params(("parallel",)),
        name="post",
    )(*args)


ROUTE_ROWS = N_EXPERTS + 8
FFN_TILE = 256
PLAN_ITEMS = LANES


def _route_rows(lg):
    tm = lg.shape[1]
    ne = N_EXPERTS
    gl = lg[ne:ne + 8]
    g_id = lax.broadcasted_iota(jnp.int32, (8, tm), 0).astype(F32)
    is_g = g_id < N_GROUPS
    gl = jnp.where(is_g, gl, -jnp.inf)
    mx = jnp.max(gl, axis=0, keepdims=True)
    z = jnp.sum(jnp.where(is_g, jnp.exp(gl - mx), 0.0), axis=0, keepdims=True)
    p_sel = 1.0 / z
    g_sel = jnp.min(jnp.where(gl == mx, g_id, float(N_GROUPS)), axis=0, keepdims=True)
    e_id = lax.broadcasted_iota(jnp.int32, (ne, tm), 0).astype(F32)
    lo = EXPERTS_PER_GROUP * g_sel
    in_grp = (e_id >= lo) & (e_id < lo + EXPERTS_PER_GROUP)
    le = jnp.where(in_grp, lg[0:ne], -jnp.inf)
    v1 = jnp.max(le, axis=0, keepdims=True)
    i1 = jnp.min(jnp.where(le == v1, e_id, float(ne)), axis=0, keepdims=True)
    le2 = jnp.where(e_id == i1, -jnp.inf, le)
    v2 = jnp.max(le2, axis=0, keepdims=True)
    i2 = jnp.min(jnp.where(le2 == v2, e_id, float(ne)), axis=0, keepdims=True)
    e2 = jnp.exp(v2 - v1)
    w1 = p_sel / (1.0 + e2)
    w2 = p_sel * e2 / (1.0 + e2)
    return i1, i2, w1, w2


def _route_kernel(n_tiles, lg_ref, etri_ref, ttri_ref, pos_ref, w_ref, plan_ref, meta_ref, cnt_ref, run_ref, base_ref):
    p = pl.program_id(0)
    i = pl.program_id(1)
    tm = lg_ref.shape[1]
    sub = lax.broadcasted_iota(jnp.int32, (N_EXPERTS, tm), 0).astype(F32)
    tf = float(FFN_TILE)

    @pl.when(p == 0)
    def _():
        e_a, e_b, w_a, w_b = _route_rows(lg_ref[...])
        meta_ref[i, 0:1, :] = e_a
        meta_ref[i, 1:2, :] = e_b
        meta_ref[i, 2:3, :] = w_a
        meta_ref[i, 3:4, :] = w_b
        meta_ref[i, 4:8, :] = jnp.zeros((4, tm), F32)
        oh = jnp.where((sub == e_a) | (sub == e_b), 1.0, 0.0)

        @pl.when(i == 0)
        def _():
            cnt_ref[...] = jnp.zeros_like(cnt_ref)

        cnt_ref[...] += jnp.broadcast_to(jnp.sum(oh, axis=1, keepdims=True), cnt_ref.shape)

    @pl.when(p == 1)
    def _():
        @pl.when(i == 0)
        def _():
            c1, c2, c3 = _split3(cnt_ref[...])
            base_ref[...] = _dot(etri_ref[...], c1) + (_dot(etri_ref[...], c2) + _dot(etri_ref[...], c3))
            run_ref[...] = jnp.zeros_like(run_ref)

        e_a = meta_ref[i, 0:1, :]
        e_b = meta_ref[i, 1:2, :]
        oh_a = jnp.where(sub == e_a, 1.0, 0.0)
        oh_b = jnp.where(sub == e_b, 1.0, 0.0)
        oh = oh_a + oh_b
        before = _dot(oh.astype(BF16), ttri_ref[...]) + run_ref[...] + base_ref[...]
        pos_ref[...] = jnp.zeros(pos_ref.shape, jnp.int32)
        pos_ref[0:1, :] = jnp.sum(oh_a * before, axis=0, keepdims=True).astype(jnp.int32)
        pos_ref[1:2, :] = jnp.sum(oh_b * before, axis=0, keepdims=True).astype(jnp.int32)
        w_ref[...] = meta_ref[i]
        run_ref[...] += jnp.broadcast_to(jnp.sum(oh, axis=1, keepdims=True), run_ref.shape)

        @pl.when(i == n_tiles - 1)
        def _():
            cnt = cnt_ref[:, 0:PLAN_ITEMS]
            base = base_ref[:, 0:PLAN_ITEMS]
            e_id = lax.broadcasted_iota(jnp.int32, (N_EXPERTS, PLAN_ITEMS), 0)
            t_lo = jnp.floor(base / tf)
            t_hi = jnp.floor((base + cnt - 1.0) / tf)
            n_items = jnp.where(cnt > 0.0, t_hi - t_lo + 1.0, 0.0)
            i_start = _dot(etri_ref[...], n_items.astype(BF16))
            item = lax.broadcasted_iota(jnp.int32, (N_EXPERTS, PLAN_ITEMS), 1).astype(F32)
            sel = (i_start <= item) & (item < i_start + n_items)
            tile_e = t_lo + item - i_start
            off = base - tile_e * tf

            def pick(v):
                return jnp.sum(jnp.where(sel, v, 0.0), axis=0, keepdims=True).astype(jnp.int32)

            plan_ref[...] = jnp.zeros(plan_ref.shape, jnp.int32)
            plan_ref[0:1, :] = pick(e_id.astype(F32))
            plan_ref[1:2, :] = pick(tile_e)
            plan_ref[2:3, :] = pick(jnp.maximum(off, 0.0))
            plan_ref[3:4, :] = pick(jnp.minimum(off + cnt, tf))
            plan_ref[4:5, :] = jnp.sum(n_items, axis=0, keepdims=True).astype(jnp.int32)


def _route(lg, tris):
    expert_tri, token_tri = tris
    n_tiles, _, tm = lg.shape
    return pl.pallas_call(
        functools.partial(_route_kernel, n_tiles),
        grid=(2, n_tiles),
        in_specs=[pl.BlockSpec((None, ROUTE_ROWS, tm), lambda p, i: (i, 0, 0)),
                  _const_spec(expert_tri.shape), _const_spec(token_tri.shape)],
        out_specs=[pl.BlockSpec((None, 8, tm), lambda p, i: (i * p, 0, 0)),
                   pl.BlockSpec((None, 8, tm), lambda p, i: (i * p, 0, 0)),
                   pl.BlockSpec((8, PLAN_ITEMS), lambda p, i: (0, 0))],
        out_shape=[jax.ShapeDtypeStruct((n_tiles, 8, tm), jnp.int32),
                   jax.ShapeDtypeStruct((n_tiles, 8, tm), F32),
                   jax.ShapeDtypeStruct((8, PLAN_ITEMS), jnp.int32)],
        scratch_shapes=[pltpu.VMEM((n_tiles, 8, tm), F32),
                        pltpu.VMEM((N_EXPERTS, tm), F32),
                        pltpu.VMEM((N_EXPERTS, tm), F32),
                        pltpu.VMEM((N_EXPERTS, tm), F32)],
        compiler_params=_params(("arbitrary", "arbitrary")),
        name="route",
    )(lg, expert_tri, token_tri)


ROW_UNROLL = 8


DISPATCH_BLOCKS = 4


def _dispatch_kernel(tm, pos_ref, h_ref, xs_ref, sem):
    def row_copy(t, dst):
        return pltpu.make_async_copy(h_ref.at[pl.ds(t, 1)], xs_ref.at[pl.ds(dst, 1)], sem)

    for blk in range(DISPATCH_BLOCKS):
        def issue(g, carry, blk=blk):
            for u in range(ROW_UNROLL):
                t = g * ROW_UNROLL + u
                row_copy(blk * tm + t, pos_ref[blk, 0, t]).start()
                row_copy(blk * tm + t, pos_ref[blk, 1, t]).start()
            return carry

        lax.fori_loop(0, tm // ROW_UNROLL, issue, 0)

    def drain(g, carry):
        for _ in range(2 * ROW_UNROLL):
            row_copy(0, 0).wait()
        return carry

    lax.fori_loop(0, DISPATCH_BLOCKS * tm // ROW_UNROLL, drain, 0)


def _dispatch(pos, h2, n_rows):
    n_tiles, _, tm = pos.shape
    d = h2.shape[1]
    nb = DISPATCH_BLOCKS
    assert n_tiles % nb == 0
    return pl.pallas_call(
        functools.partial(_dispatch_kernel, tm),
        grid=(n_tiles // nb,),
        in_specs=[pl.BlockSpec((nb, 8, tm), lambda i: (i, 0, 0), memory_space=pltpu.SMEM),
                  pl.BlockSpec((nb * tm, d), lambda i: (i, 0))],
        out_specs=pl.BlockSpec(memory_space=pl.ANY),
        out_shape=jax.ShapeDtypeStruct((n_rows, d), F32),
        scratch_shapes=[pltpu.SemaphoreType.DMA],
        compiler_params=_params(("arbitrary",)),
        name="dispatch",
    )(pos, h2)


def _ffn_kernel(owner_ref, tile_ref, lo_ref, hi_ref, used_ref, x_ref, w1_ref, w3_ref, w2_ref, y_ref):
    w = pl.program_id(0)

    @pl.when(w < used_ref[0])
    def _():
        x = x_ref[...].astype(BF16)
        a = _dot(x, w1_ref[...].astype(BF16))
        b = _dot(x, w3_ref[...].astype(BF16))
        hid = (a * _sigmoid(a)) * b
        y = _dot(hid.astype(BF16), w2_ref[...].astype(BF16))
        row = lax.broadcasted_iota(jnp.int32, y.shape, 0)
        mine = (row >= lo_ref[w]) & (row < hi_ref[w])
        first = (w == 0) | (tile_ref[w] != tile_ref[jnp.maximum(w - 1, 0)])

        @pl.when(first)
        def _():
            y_ref[...] = jnp.where(mine, y, 0.0)

        @pl.when(jnp.logical_not(first))
        def _():
            y_ref[...] = jnp.where(mine, y, y_ref[...])


def _ffn(plan, xs, w1, w3, w2):
    n_rows, d = xs.shape
    tf = FFN_TILE
    de = w1.shape[1] // N_EXPERTS
    n_items = n_rows // tf + N_EXPERTS - 1
    assert n_items <= PLAN_ITEMS

    def item(w, used):
        return jnp.minimum(w, used[0] - 1)

    grid_spec = pltpu.PrefetchScalarGridSpec(
        num_scalar_prefetch=5,
        grid=(n_items,),
        in_specs=[pl.BlockSpec((tf, d), lambda w, own, til, lo, hi, used: (til[item(w, used)], 0)),
                  pl.BlockSpec((d, de), lambda w, own, til, lo, hi, used: (0, own[item(w, used)])),
                  pl.BlockSpec((d, de), lambda w, own, til, lo, hi, used: (0, own[item(w, used)])),
                  pl.BlockSpec((de, d), lambda w, own, til, lo, hi, used: (own[item(w, used)], 0))],
        out_specs=pl.BlockSpec((tf, d), lambda w, own, til, lo, hi, used: (til[item(w, used)], 0)),
    )
    return pl.pallas_call(
        _ffn_kernel,
        grid_spec=grid_spec,
        out_shape=jax.ShapeDtypeStruct((n_rows, d), F32),
        compiler_params=_params(("arbitrary",)),
        name="ffn",
    )(plan[0], plan[1], plan[2], plan[3], plan[4, 0:1], xs, w1, w3, w2)


def _combine_kernel(tm, ctx_steps, pos_ref, nxt_ref, w_ref, x1_ref, mod_ref, gp_ref, ys_ref, yc_ref, yl_ref,
                    buf_ref, sem):
    i = pl.program_id(0)
    n = pl.num_programs(0)
    slot = i % 2

    def row_copy(src, k, t, s):
        return pltpu.make_async_copy(ys_ref.at[pl.ds(src, 1)], buf_ref.at[s, k, pl.ds(t, 1)], sem.at[s])

    def issue(p_ref, s):
        def body(g, carry):
            for u in range(ROW_UNROLL):
                t = g * ROW_UNROLL + u
                row_copy(p_ref[0, t], 0, t, s).start()
                row_copy(p_ref[1, t], 1, t, s).start()
            return carry
        lax.fori_loop(0, tm // ROW_UNROLL, body, 0)

    @pl.when(i == 0)
    def _():
        issue(pos_ref, 0)

    @pl.when(i + 1 < n)
    def _():
        issue(nxt_ref, 1 - slot)

    def drain(g, carry):
        for _ in range(2 * ROW_UNROLL):
            row_copy(0, 0, 0, slot).wait()
        return carry

    lax.fori_loop(0, tm // ROW_UNROLL, drain, 0)

    r = lax.broadcasted_iota(jnp.int32, (tm, tm), 0)
    c = lax.broadcasted_iota(jnp.int32, (tm, tm), 1)
    d = x1_ref.shape[1]
    wa = jnp.broadcast_to(jnp.sum(jnp.where(r == c, w_ref[2:3, :], 0.0), axis=1, keepdims=True), (tm, d))
    wb = jnp.broadcast_to(jnp.sum(jnp.where(r == c, w_ref[3:4, :], 0.0), axis=1, keepdims=True), (tm, d))
    moe = wa * buf_ref[slot, 0] + wb * buf_ref[slot, 1]
    gt2 = mod_ref[5:6, :]
    y = x1_ref[...] + gt2 * _rms(moe, gp_ref[...])

    @pl.when(i < ctx_steps)
    def _():
        yc_ref[...] = y

    @pl.when(i >= ctx_steps)
    def _():
        yl_ref[...] = y


def _combine(st, pos, w, x1, mod, g_post2, ys):
    n_tiles, _, tm = pos.shape
    d = x1.shape[1]
    row = pl.BlockSpec((tm, d), lambda i: (i, 0))
    return pl.pallas_call(
        functools.partial(_combine_kernel, tm, st.ctx_steps(tm)),
        grid=(n_tiles,),
        in_specs=[pl.BlockSpec((None, 8, tm), lambda i: (i, 0, 0), memory_space=pltpu.SMEM),
                  pl.BlockSpec((None, 8, tm), lambda i: (jnp.minimum(i + 1, n_tiles - 1), 0, 0),
                               memory_space=pltpu.SMEM),
                  pl.BlockSpec((None, 8, tm), lambda i: (i, 0, 0)),
                  row,
                  st.mod_spec(tm, d),
                  _const_spec((1, d)),
                  pl.BlockSpec(memory_space=pl.ANY)],
        out_specs=[st.ctx_spec(tm, d), st.lat_spec(tm, d)],
        out_shape=[jax.ShapeDtypeStruct((st.n_ctx, d), F32), jax.ShapeDtypeStruct((st.n_lat, d), F32)],
        scratch_shapes=[pltpu.VMEM((2, 2, tm, d), F32), pltpu.SemaphoreType.DMA((2,))],
        compiler_params=_params(("arbitrary",)),
        name="combine",
    )(pos, pos, w, x1, mod, g_post2.reshape(1, d), ys)


def _moe(st, h2, lg, x1, mod, g_post2, w1, w3, w2, route_tri):
    n_tok = x1.shape[0]
    n_rows = 2 * n_tok
    assert n_rows % FFN_TILE == 0
    pos, w, plan = _route(lg, route_tri)
    xs = _dispatch(pos, h2, n_rows)
    ys = _ffn(plan, xs, w1, w3, w2)
    return _combine(st, pos, w, x1, mod, g_post2, ys)


def _grid_pos(n_tokens, d_model):
    rows = n_tokens // GRID_W
    row = np.repeat(np.arange(rows, dtype=np.float64), GRID_W)
    col = np.tile(np.arange(GRID_W, dtype=np.float64), rows)
    n_freq = d_model // 4
    freq = np.exp(-math.log(POS_BASE) * np.arange(n_freq, dtype=np.float64) / n_freq)

    def enc(p):
        a = p[:, None] * freq[None, :]
        return np.concatenate([np.sin(a), np.cos(a)], axis=-1)

    return jnp.asarray(np.concatenate([enc(row), enc(col)], axis=-1), dtype=F32)


def _dft_cos_sin(n):
    k = np.arange(n, dtype=np.int64)
    ang = 2.0 * np.pi * ((k[:, None] * k[None, :]) % n).astype(np.float64) / n
    return np.cos(ang), np.sin(ang)


def _fnet_consts(seq_len):
    ct, st = _dft_cos_sin(seq_len)
    f_mat = jnp.asarray(np.concatenate([ct, -st], axis=1), dtype=F32).astype(BF16)
    cc, sc = _dft_cos_sin(FGROUP_DIM)
    cs_mat = jnp.asarray(np.stack([cc, sc]), dtype=F32).astype(BF16)
    return f_mat, cs_mat


def _tri_consts():
    i = np.arange(CHUNK)
    prefix = (i[:, None] <= i[None, :]).astype(np.float32)
    suffix = (i[:, None] >= i[None, :]).astype(np.float32)
    tri = jnp.asarray(np.stack([prefix, suffix]), dtype=BF16)
    e = np.arange(N_EXPERTS)
    expert_tri = jnp.asarray((e[None, :] < e[:, None]).astype(np.float32), dtype=BF16)
    t = np.arange(TOKEN_TILE)
    token_tri = jnp.asarray((t[:, None] < t[None, :]).astype(np.float32), dtype=BF16)
    return tri, (expert_tri, token_tri)


def _layer(x_prompt, x_sample, pos, mod, states, lw, consts):
    bp, sp, d = x_prompt.shape
    bs, ss, _ = x_sample.shape
    st = _Streams(n_ctx=bp * sp, n_lat=bs * ss, seq_lat=ss)
    (g_pre1, proj_w, g_hn, post_w, g_post1, g_pre2, w_e1, w_e3, w_e2, g_post2) = lw
    tri, route_tris = consts
    x_ctx = x_prompt.reshape(st.n_ctx, d)
    x_lat = x_sample.reshape(st.n_lat, d)
    q, kt, v, o, u, ga, gb, gr = _proj(st, x_ctx, x_lat, pos, mod, g_pre1, proj_w + (tri,))
    ha_ctx, new_c, new_n, new_m = _mlstm(q, kt, v, o, g_hn, gr, bp, sp, 0, None, N_HEADS if sp == CHUNK else 2)
    (ha_lat,) = _mlstm(q, kt, v, o, g_hn, gr, bs, ss, st.n_ctx, states, N_HEADS if ss == CHUNK else 2)
    yf_ctx = _fnet(u, *_fnet_consts(sp), bp, sp, 0)
    yf_lat = _fnet(u, *_fnet_consts(ss), bs, ss, st.n_ctx)
    x1, h2, logits = _post(st, x_ctx, x_lat, pos, ha_ctx, ha_lat, yf_ctx, yf_lat, ga, gb, mod, g_post1, g_pre2, post_w)
    y_ctx, y_lat = _moe(st, h2, logits, x1, mod, g_post2, w_e1, w_e3, w_e2, route_tris)
    return y_ctx.reshape(bp, sp, d), y_lat.reshape(bs, ss, d), (new_c, new_n, new_m)


def kernel(x_prompt, x_sample, c, state_C, state_n, state_m, c_ctx, w_ada, b_ada, g_pre1, w_in, b_gates, g_hn,
           w_a, w_f, w_out, g_post1, g_pre2, w_rg, b_rg, w_re, b_re, w_e1, w_e3, w_e2, g_post2):
    bp, sp, d = x_prompt.shape
    bs, ss, _ = x_sample.shape
    depth = w_in.shape[0]
    assert depth == 1
    l = 0
    dm = N_HEADS * HEAD_DIM
    du = N_FGROUPS * FGROUP_DIM
    ng = 4 * N_HEADS

    n_rows = 8
    assert 1 + bs <= n_rows
    cs = jnp.concatenate([c_ctx[None, :], c, jnp.zeros((n_rows - 1 - bs, d), F32)], axis=0)
    mod = _ada(cs, w_ada[l], b_ada[l]).reshape(n_rows, 6, d)

    wi = w_in[l]
    cuts = np.cumsum([dm] * 4 + [ng, du, d]).tolist()
    wq = wi[:, :cuts[0]].astype(BF16)
    wkt = wi[:, cuts[0]:cuts[1]].T.astype(BF16)
    wv = wi[:, cuts[1]:cuts[2]].astype(BF16)
    wo = wi[:, cuts[2]:cuts[3]].astype(BF16)
    wgt = wi[:, cuts[3]:cuts[4]].T
    wu = wi[:, cuts[4]:cuts[5]].astype(BF16)
    wga = wi[:, cuts[5]:cuts[6]].astype(BF16)
    wgb = wi[:, cuts[6]:].astype(BF16)
    bg = b_gates[l].reshape(ng, 1)
    proj_w = (wq, wkt, wv, wo, wu, wga, wgb, wgt, bg)
    n_pad = ROUTE_ROWS - N_EXPERTS - N_GROUPS
    wr = jnp.concatenate([w_re[l], w_rg[l], jnp.zeros((d, n_pad), F32)], axis=1).T
    br = jnp.concatenate([b_re[l], b_rg[l], jnp.zeros((n_pad,), F32)]).reshape(ROUTE_ROWS, 1)
    post_w = (w_a[l].astype(BF16), w_f[l].astype(BF16), w_out[l].astype(BF16), wr, br)
    lw = (g_pre1[l], proj_w, g_hn[l], post_w, g_post1[l], g_pre2[l], w_e1[l], w_e3[l], w_e2[l], g_post2[l])
    consts = _tri_consts()

    pos = _grid_pos(ss, d)
    y_prompt, y_sample, (new_c, new_n, new_m) = _layer(x_prompt, x_sample, pos, mod,
                                                       (state_C, state_n, state_m), lw, consts)

    new_state_c = new_c
    new_state_n = new_n.reshape(bp, depth, 2, N_HEADS, HEAD_DIM)
    new_state_m = new_m[..., 0, 0]
    return (y_prompt, y_sample, new_state_c, new_state_n, new_state_m)
```

```python
import functools
import math
from typing import NamedTuple

import numpy as np
import jax
import jax.numpy as jnp
from jax import lax
from jax.experimental import pallas as pl
from jax.experimental.pallas import tpu as pltpu

F32 = jnp.float32
BF16 = jnp.bfloat16

N_HEADS = 8
HEAD_DIM = 128
N_FGROUPS = 4
FGROUP_DIM = 128
N_GROUPS = 4
EXPERTS_PER_GROUP = 8
N_EXPERTS = N_GROUPS * EXPERTS_PER_GROUP
GRID_W = 64
POS_BASE = 10000.0
EPS = 1e-6

V7X_VMEM_LIMIT_BYTES = 56 * 1024 * 1024
LANES = 128
CHUNK = 256
TOKEN_TILE = CHUNK
CHUNKS_PER_STEP = 2
MLSTM_HEADS_PER_STEP = 4


def _params(sem):
    return pltpu.CompilerParams(dimension_semantics=sem, vmem_limit_bytes=V7X_VMEM_LIMIT_BYTES)


def _const_spec(shape):
    n = len(shape)
    return pl.BlockSpec(shape, lambda *_: (0,) * n, pipeline_mode=pl.Buffered(1))


def _split2(x):
    hi = x.astype(BF16)
    lo = (x - hi.astype(F32)).astype(BF16)
    return hi, lo


def _split3(x):
    hi = x.astype(BF16)
    r = x - hi.astype(F32)
    mid = r.astype(BF16)
    lo = (r - mid.astype(F32)).astype(BF16)
    return hi, mid, lo


def _dot(a, b):
    return jnp.dot(a, b, preferred_element_type=F32)


def _dot_nt(a, b):
    return lax.dot_general(a, b, (((1,), (1,)), ((), ())), preferred_element_type=F32)


def _dot_f32(a, b, nt=False):
    d = _dot_nt if nt else _dot
    a1, a2 = _split2(a)
    b1, b2 = _split2(b)
    return d(a1, b1) + (d(a1, b2) + d(a2, b1))


def _dot_exact_rhs(a, b_bf16, nt=False):
    d = _dot_nt if nt else _dot
    a1, a2, a3 = _split3(a)
    return d(a1, b_bf16) + (d(a2, b_bf16) + d(a3, b_bf16))


def _rms(x, g):
    return x * lax.rsqrt(jnp.mean(x * x, axis=-1, keepdims=True) + EPS) * g


def _sigmoid(x):
    return 1.0 / (1.0 + jnp.exp(-x))


def _log_sigmoid(x):
    return jnp.minimum(x, 0.0) - jnp.log(1.0 + jnp.exp(-jnp.abs(x)))


def _ada_kernel(c_ref, w_ref, b_ref, o_ref):
    c = c_ref[...]
    s = (c * _sigmoid(c)).astype(BF16)
    o_ref[...] = _dot(s, w_ref[...].astype(BF16)) + b_ref[...]


def _ada(cs, w_ada, b_ada):
    rows, d = cs.shape
    n = w_ada.shape[1]
    tn = 1024
    return pl.pallas_call(
        _ada_kernel,
        grid=(n // tn,),
        in_specs=[pl.BlockSpec((rows, d), lambda j: (0, 0)),
                  pl.BlockSpec((d, tn), lambda j: (0, j)),
                  pl.BlockSpec((1, tn), lambda j: (0, j))],
        out_specs=pl.BlockSpec((rows, tn), lambda j: (0, j)),
        out_shape=jax.ShapeDtypeStruct((rows, n), F32),
        compiler_params=_params(("arbitrary",)),
        name="ada",
    )(cs, w_ada, b_ada.reshape(1, n))


def _proj_kernel(ctx_steps, xc_ref, xl_ref, pos_ref, mod_ref, g_ref, wq_ref, wkt_ref, wv_ref, wo_ref, wu_ref,
                 wga_ref, wgb_ref, wgt_ref, bg_ref, tri_ref,
                 q_ref, kt_ref, v_ref, o_ref, u_ref, ga_ref, gb_ref, gr_ref):
    x = jnp.where(pl.program_id(0) < ctx_steps, xc_ref[...], xl_ref[...] + pos_ref[...])
    sh = mod_ref[0:1, :]
    sc = mod_ref[1:2, :]
    h = _rms(x, g_ref[...]) * (1.0 + sc) + sh
    hb = h.astype(BF16)
    q_ref[...] = _dot(hb, wq_ref[...]).astype(BF16)
    kt = (_dot_nt(wkt_ref[...], hb) * (HEAD_DIM ** -0.5)).astype(BF16)
    tc = TOKEN_TILE
    for s in range(CHUNKS_PER_STEP):
        kt_ref[s] = kt[:, s * tc:(s + 1) * tc]
    v_ref[...] = _dot(hb, wv_ref[...]).astype(BF16)
    o_ref[...] = _dot(hb, wo_ref[...])
    u_ref[...] = _dot(hb, wu_ref[...]).astype(BF16)
    ga_ref[...] = _dot(hb, wga_ref[...])
    gb_ref[...] = _dot(hb, wgb_ref[...])
    g = _dot_f32(wgt_ref[...], h, nt=True) + bg_ref[...]
    nh = N_HEADS
    lf_f = _log_sigmoid(g[nh:2 * nh])
    lf_b = _log_sigmoid(g[3 * nh:4 * nh])
    for s in range(CHUNKS_PER_STEP):
        cols = slice(s * tc, (s + 1) * tc)
        gr_ref[s, 0:nh, :] = g[0:nh, cols]
        gr_ref[s, nh:2 * nh, :] = _dot_exact_rhs(lf_f[:, cols], tri_ref[0])
        gr_ref[s, 2 * nh:3 * nh, :] = g[2 * nh:3 * nh, cols]
        gr_ref[s, 3 * nh:4 * nh, :] = _dot_exact_rhs(lf_b[:, cols], tri_ref[1])


class _Streams(NamedTuple):
    n_ctx: int
    n_lat: int
    seq_lat: int

    def ctx_steps(self, tm):
        assert self.n_ctx % tm == 0 and self.seq_lat % tm == 0
        return self.n_ctx // tm

    def ctx_spec(self, tm, width):
        c = self.ctx_steps(tm)
        return pl.BlockSpec((tm, width), lambda i, *_: (jnp.minimum(i, c - 1), 0))

    def lat_spec(self, tm, width):
        c = self.ctx_steps(tm)
        return pl.BlockSpec((tm, width), lambda i, *_: (jnp.maximum(i - c, 0), 0))

    def pos_spec(self, tm, width):
        c = self.ctx_steps(tm)
        per_seq = self.seq_lat // tm
        return pl.BlockSpec((tm, width), lambda i, *_: (jnp.maximum(i - c, 0) % per_seq, 0))

    def mod_spec(self, tm, width):
        c = self.ctx_steps(tm)
        per_seq = self.seq_lat // tm
        return pl.BlockSpec((None, 6, width),
                            lambda i, *_: (jnp.where(i < c, 0, 1 + jnp.maximum(i - c, 0) // per_seq), 0, 0))


def _proj(st, x_ctx, x_lat, pos, mod, g_pre1, wts):
    d = x_ctx.shape[1]
    n_tok = st.n_ctx + st.n_lat
    tc = TOKEN_TILE
    ns = CHUNKS_PER_STEP
    tm = ns * tc
    n_tiles = n_tok // tc
    (wq, wkt, wv, wo, wu, wga, wgb, wgt, bg, tri) = wts
    row = pl.BlockSpec((tm, d), lambda i: (i, 0))
    in_specs = [st.ctx_spec(tm, d), st.lat_spec(tm, d), st.pos_spec(tm, d), st.mod_spec(tm, d), _const_spec((1, d))]
    in_specs += [_const_spec(w.shape) for w in wts]
    args = [x_ctx, x_lat, pos, mod, g_pre1.reshape(1, d)] + list(wts)
    n_gate_rows = 4 * N_HEADS
    du = wu.shape[1]
    out_shape = [jax.ShapeDtypeStruct((n_tok, d), BF16),
                 jax.ShapeDtypeStruct((n_tiles, d, tc), BF16),
                 jax.ShapeDtypeStruct((n_tok, d), BF16),
                 jax.ShapeDtypeStruct((n_tok, d), F32),
                 jax.ShapeDtypeStruct((n_tok, du), BF16),
                 jax.ShapeDtypeStruct((n_tok, d), F32),
                 jax.ShapeDtypeStruct((n_tok, d), F32),
                 jax.ShapeDtypeStruct((n_tiles, n_gate_rows, tc), F32)]
    out_specs = [row,
                 pl.BlockSpec((ns, d, tc), lambda i: (i, 0, 0)),
                 row, row,
                 pl.BlockSpec((tm, du), lambda i: (i, 0)),
                 row, row,
                 pl.BlockSpec((ns, n_gate_rows, tc), lambda i: (i, 0, 0))]
    return pl.pallas_call(
        functools.partial(_proj_kernel, st.ctx_steps(tm)),
        grid=(n_tok // tm,),
        in_specs=in_specs,
        out_specs=out_specs,
        out_shape=out_shape,
        compiler_params=_params(("parallel",)),
        name="proj",
    )(*args)


def _mlstm_kernel(n_chunks, hp, has_state, emit_state, *refs):
    it = iter(refs)
    if has_state:
        m0_ref = next(it)
    q_ref, kt_ref, v_ref, o_ref, ghn_ref, gr_ref = (next(it) for _ in range(6))
    if has_state:
        c0_ref, n0_ref = next(it), next(it)
    ha_ref = next(it)
    if emit_state:
        cout_ref, nout_ref, mout_ref = next(it), next(it), next(it)
    hs_ref, cn_ref, m_ref = (next(it) for _ in range(3))

    L = CHUNK
    nh, dh = N_HEADS, HEAD_DIM
    head0 = pl.program_id(1) * hp
    neg_inf = -jnp.inf

    sq_r = lax.broadcasted_iota(jnp.int32, (dh, dh), 0)
    sq_c = lax.broadcasted_iota(jnp.int32, (dh, dh), 1)
    for j in range(hp):
        for d in range(2):
            if has_state:
                n_col = jnp.sum(jnp.where(sq_r == sq_c, n0_ref[d, j], 0.0), axis=1, keepdims=True)
                cn_ref[j, d, :, 0:dh] = c0_ref[d, j]
                cn_ref[j, d, :, dh:2 * dh] = jnp.broadcast_to(n_col, (dh, dh))
                m0 = m0_ref[pl.program_id(0) * (2 * nh) + d * nh + head0 + j]
                m_ref[j, d] = jnp.full((8, LANES), m0, F32)
            else:
                cn_ref[j, d] = jnp.zeros((dh, 2 * dh), F32)
                m_ref[j, d] = jnp.zeros((8, LANES), F32)

    row_id = lax.broadcasted_iota(jnp.int32, (L, L), 0)
    col_id = lax.broadcasted_iota(jnp.int32, (L, L), 1)
    ones_blk = jnp.ones((L, dh), BF16)

    def chunk(j, d, c):
        t0 = pl.multiple_of(c * L, L)
        lanes = slice(j * dh, (j + 1) * dh)
        qc = q_ref[pl.ds(t0, L), lanes]
        ktc = kt_ref[c, lanes, :]
        vc = v_ref[pl.ds(t0, L), lanes]
        ig_row = gr_ref[c, pl.ds(2 * nh * d + head0 + j, 1), :]
        b_row = gr_ref[c, pl.ds(2 * nh * d + nh + head0 + j, 1), :]
        m_prev = m_ref[j, d][0:1, 0:1]
        if d == 0:
            b_end = b_row[:, L - 1:L]
            mask = col_id <= row_id
        else:
            b_end = b_row[:, 0:1]
            mask = col_id >= row_id
        a_row = ig_row - b_row
        a_max = jnp.max(jnp.where(mask, a_row, neg_inf), axis=1, keepdims=True)
        g = jnp.maximum(jnp.broadcast_to(a_max, (L, dh)), m_prev)
        b_col = jnp.broadcast_to(
            jnp.sum(jnp.where(row_id == col_id, b_row, 0.0), axis=1, keepdims=True), (L, dh))
        g_full = jnp.concatenate([g] * (L // dh), axis=1)
        s = _dot(qc, ktc) * jnp.exp(jnp.where(mask, a_row - g_full, neg_inf))
        decay = jnp.exp(m_prev - g)
        qcn = _dot(qc, cn_ref[j, d].astype(BF16))
        num = decay * qcn[:, 0:dh] + _dot(s.astype(BF16), vc)
        den = decay * qcn[:, dh:2 * dh] + jnp.sum(s, axis=1, keepdims=True)
        hch = num / jnp.maximum(jnp.abs(den), jnp.exp(-(b_col + g)))
        w_log = b_end + a_row
        m_new = jnp.maximum(b_end + m_prev, jnp.max(w_log, axis=1, keepdims=True))
        kw = (ktc.astype(F32) * jnp.exp(w_log - m_new)).astype(BF16)
        v_aug = jnp.concatenate([vc, ones_blk], axis=1)
        cn_ref[j, d] = jnp.exp(b_end + m_prev - m_new) * cn_ref[j, d] + _dot(kw, v_aug)
        m_ref[j, d] = jnp.broadcast_to(m_new, (8, LANES))
        return t0, hch

    if n_chunks == 1:
        for j in range(hp):
            _, h_f = chunk(j, 0, 0)
            _, h_b = chunk(j, 1, 0)
            hs_ref[:, j * dh:(j + 1) * dh] = h_f + h_b
    else:
        half = n_chunks // 2

        def first(i, carry):
            for j in range(hp):
                lanes = slice(j * dh, (j + 1) * dh)
                t_f, h_f = chunk(j, 0, i)
                t_b, h_b = chunk(j, 1, n_chunks - 1 - i)
                hs_ref[pl.ds(t_f, L), lanes] = h_f
                hs_ref[pl.ds(t_b, L), lanes] = h_b
            return carry

        def second(i, carry):
            for j in range(hp):
                lanes = slice(j * dh, (j + 1) * dh)
                t_f, h_f = chunk(j, 0, i)
                t_b, h_b = chunk(j, 1, n_chunks - 1 - i)
                hs_ref[pl.ds(t_f, L), lanes] += h_f
                hs_ref[pl.ds(t_b, L), lanes] += h_b
            return carry

        lax.fori_loop(0, half, first, 0)
        lax.fori_loop(half, n_chunks, second, 0)

    def finish(c, carry):
        t0 = pl.multiple_of(c * L, L)
        for j in range(hp):
            lanes = slice(j * dh, (j + 1) * dh)
            ha = hs_ref[pl.ds(t0, L), lanes]
            ha = ha * lax.rsqrt(jnp.mean(ha * ha, axis=-1, keepdims=True) + EPS)
            ha = ha * ghn_ref[:, lanes] * _sigmoid(o_ref[pl.ds(t0, L), lanes])
            ha_ref[pl.ds(t0, L), lanes] = ha.astype(BF16)
        return carry

    if n_chunks == 1:
        finish(0, 0)
    else:
        lax.fori_loop(0, n_chunks, finish, 0)

    if emit_state:
        for j in range(hp):
            for d in range(2):
                cout_ref[d, j] = cn_ref[j, d, :, 0:dh]
                n_rep = cn_ref[j, d, :, dh:2 * dh]
                nout_ref[d, j] = jnp.sum(jnp.where(sq_r == sq_c, n_rep, 0.0), axis=0, keepdims=True)
                mout_ref[d, j] = m_ref[j, d][0:1, :]


def _mlstm(q, kt, v, o, g_hn, gr, n_seq, seq_len, tok_off, states, hp):
    d = q.shape[1]
    n_tok = n_seq * seq_len
    L = CHUNK
    assert tok_off % seq_len == 0
    sb = tok_off // seq_len
    n_chunks = seq_len // L
    has_state = states is not None
    emit_state = not has_state
    nh, dh = N_HEADS, HEAD_DIM
    assert n_chunks == 1 or n_chunks % 2 == 0
    assert nh % hp == 0
    wd = hp * dh

    in_specs = []
    args = []
    if has_state:
        state_c, state_n, state_m = states
        in_specs.append(pl.BlockSpec(memory_space=pltpu.SMEM))
        args.append(state_m.reshape(-1))
    tok = pl.BlockSpec((seq_len, wd), lambda b, h: (b + sb, h))
    in_specs += [tok,
                 pl.BlockSpec((n_chunks, wd, L), lambda b, h: (b + sb, h, 0)),
                 tok, tok,
                 pl.BlockSpec((1, wd), lambda b, h: (0, h)),
                 pl.BlockSpec((n_chunks, 4 * nh, L), lambda b, h: (b + sb, 0, 0))]
    args += [q, kt, v, o, g_hn.reshape(1, d), gr]
    st_c = pl.BlockSpec((None, None, 2, hp, dh, dh), lambda b, h: (b, 0, 0, h, 0, 0))
    st_v = pl.BlockSpec((None, None, 2, hp, 1, dh), lambda b, h: (b, 0, 0, h, 0, 0))
    if has_state:
        in_specs += [st_c, st_v]
        args += [state_c, state_n.reshape(n_seq, 1, 2, nh, 1, dh)]
    out_shape = [jax.ShapeDtypeStruct((n_tok, d), BF16)]
    out_specs = [pl.BlockSpec((seq_len, wd), lambda b, h: (b, h))]
    if emit_state:
        out_shape += [jax.ShapeDtypeStruct((n_seq, 1, 2, nh, dh, dh), F32),
                      jax.ShapeDtypeStruct((n_seq, 1, 2, nh, 1, dh), F32),
                      jax.ShapeDtypeStruct((n_seq, 1, 2, nh, 1, LANES), F32)]
        out_specs += [st_c, st_v, st_v]
    scratch = [pltpu.VMEM((seq_len, wd), F32),
               pltpu.VMEM((hp, 2, dh, 2 * dh), F32),
               pltpu.VMEM((hp, 2, 8, LANES), F32)]
    return pl.pallas_call(
        functools.partial(_mlstm_kernel, n_chunks, hp, has_state, emit_state),
        grid=(n_seq, nh // hp),
        in_specs=in_specs,
        out_specs=out_specs,
        out_shape=out_shape,
        scratch_shapes=scratch,
        compiler_params=_params(("parallel", "parallel")),
        name="mlstm",
    )(*args)


def _fnet_kernel(seq_len, u_ref, f_ref, cs_ref, y_ref, ab_ref):
    T = seq_len

    @pl.when(pl.program_id(1) == 0)
    def _():
        for g in range(N_FGROUPS):
            ug = u_ref[:, g * FGROUP_DIM:(g + 1) * FGROUP_DIM]
            ab_ref[0:T, g * FGROUP_DIM:(g + 1) * FGROUP_DIM] = _dot(ug, cs_ref[0]).astype(BF16)
            ab_ref[T:2 * T, g * FGROUP_DIM:(g + 1) * FGROUP_DIM] = _dot(ug, cs_ref[1]).astype(BF16)

    scale = 1.0 / math.sqrt(T * FGROUP_DIM)
    y_ref[...] = (_dot(f_ref[...], ab_ref[...]) * scale).astype(BF16)


def _fnet(u, f_mat, cs_mat, n_seq, seq_len, tok_off):
    du = u.shape[1]
    n_tok = n_seq * seq_len
    tr = min(seq_len, 512)
    assert tok_off % seq_len == 0
    sb = tok_off // seq_len
    return pl.pallas_call(
        functools.partial(_fnet_kernel, seq_len),
        grid=(n_seq, seq_len // tr),
        in_specs=[pl.BlockSpec((seq_len, du), lambda b, r: (b + sb, 0)),
                  pl.BlockSpec((tr, 2 * seq_len), lambda b, r: (r, 0)),
                  _const_spec(cs_mat.shape)],
        out_specs=pl.BlockSpec((tr, du), lambda b, r: (b * (seq_len // tr) + r, 0)),
        out_shape=jax.ShapeDtypeStruct((n_tok, du), BF16),
        scratch_shapes=[pltpu.VMEM((2 * seq_len, du), BF16)],
        compiler_params=_params(("parallel", "arbitrary")),
        name="fnet",
    )(u, f_mat, cs_mat)


def _post_kernel(ctx_steps, xc_ref, xl_ref, pos_ref, hac_ref, hal_ref, yfc_ref, yfl_ref, ga_ref, gb_ref, mod_ref,
                 gp1_ref, gp2_ref, wa_ref, wf_ref, wout_ref, wr_ref, br_ref, x1_ref, h2_ref, lg_ref):
    is_ctx = pl.program_id(0) < ctx_steps
    x = jnp.where(is_ctx, xc_ref[...], xl_ref[...] + pos_ref[...])
    ya = _dot(jnp.where(is_ctx, hac_ref[...], hal_ref[...]), wa_ref[...])
    yf = _dot(jnp.where(is_ctx, yfc_ref[...], yfl_ref[...]), wf_ref[...])
    mix_in = _sigmoid(ga_ref[...]) * ya + _sigmoid(gb_ref[...]) * yf
    mix = _dot(mix_in.astype(BF16), wout_ref[...])
    gt1 = mod_ref[2:3, :]
    sh2 = mod_ref[3:4, :]
    sc2 = mod_ref[4:5, :]
    x1 = x + gt1 * _rms(mix, gp1_ref[...])
    h2 = _rms(x1, gp2_ref[...]) * (1.0 + sc2) + sh2
    x1_ref[...] = x1
    h2_ref[...] = h2
    lg = _dot_f32(wr_ref[...], h2, nt=True) + br_ref[...]
    for s in range(CHUNKS_PER_STEP):
        lg_ref[s] = lg[:, s * TOKEN_TILE:(s + 1) * TOKEN_TILE]


def _post(st, x_ctx, x_lat, pos, ha_ctx, ha_lat, yf_ctx, yf_lat, ga, gb, mod, g_post1, g_pre2, wts):
    d = x_ctx.shape[1]
    n_tok = st.n_ctx + st.n_lat
    tc = TOKEN_TILE
    ns = CHUNKS_PER_STEP
    tm = ns * tc
    (wa, wf, wout, wr, br) = wts
    du = yf_ctx.shape[1]
    row = pl.BlockSpec((tm, d), lambda i: (i, 0))
    in_specs = [st.ctx_spec(tm, d), st.lat_spec(tm, d), st.pos_spec(tm, d),
                st.ctx_spec(tm, d), st.lat_spec(tm, d), st.ctx_spec(tm, du), st.lat_spec(tm, du),
                row, row, st.mod_spec(tm, d), _const_spec((1, d)), _const_spec((1, d))]
    in_specs += [_const_spec(w.shape) for w in wts]
    args = [x_ctx, x_lat, pos, ha_ctx, ha_lat, yf_ctx, yf_lat, ga, gb, mod,
            g_post1.reshape(1, d), g_pre2.reshape(1, d)] + list(wts)
    return pl.pallas_call(
        functools.partial(_post_kernel, st.ctx_steps(tm)),
        grid=(n_tok // tm,),
        in_specs=in_specs,
        out_specs=[row, row, pl.BlockSpec((ns, ROUTE_ROWS, tc), lambda i: (i, 0, 0))],
        out_shape=[jax.ShapeDtypeStruct((n_tok, d), F32),
                   jax.ShapeDtypeStruct((n_tok, d), F32),
                   jax.ShapeDtypeStruct((n_tok // tc, ROUTE_ROWS, tc), F32)],
        compiler_params=_params(("parallel",)),
        name="post",
    )(*args)


ROUTE_ROWS = N_EXPERTS + 8
FFN_TILE = 256
PLAN_ITEMS = LANES


def _route_rows(lg):
    tm = lg.shape[1]
    ne = N_EXPERTS
    gl = lg[ne:ne + 8]
    g_id = lax.broadcasted_iota(jnp.int32, (8, tm), 0).astype(F32)
    is_g = g_id < N_GROUPS
    gl = jnp.where(is_g, gl, -jnp.inf)
    mx = jnp.max(gl, axis=0, keepdims=True)
    z = jnp.sum(jnp.where(is_g, jnp.exp(gl - mx), 0.0), axis=0, keepdims=True)
    p_sel = 1.0 / z
    g_sel = jnp.min(jnp.where(gl == mx, g_id, float(N_GROUPS)), axis=0, keepdims=True)
    e_id = lax.broadcasted_iota(jnp.int32, (ne, tm), 0).astype(F32)
    lo = EXPERTS_PER_GROUP * g_sel
    in_grp = (e_id >= lo) & (e_id < lo + EXPERTS_PER_GROUP)
    le = jnp.where(in_grp, lg[0:ne], -jnp.inf)
    v1 = jnp.max(le, axis=0, keepdims=True)
    i1 = jnp.min(jnp.where(le == v1, e_id, float(ne)), axis=0, keepdims=True)
    le2 = jnp.where(e_id == i1, -jnp.inf, le)
    v2 = jnp.max(le2, axis=0, keepdims=True)
    i2 = jnp.min(jnp.where(le2 == v2, e_id, float(ne)), axis=0, keepdims=True)
    e2 = jnp.exp(v2 - v1)
    w1 = p_sel / (1.0 + e2)
    w2 = p_sel * e2 / (1.0 + e2)
    return i1, i2, w1, w2


def _route_kernel(n_tiles, lg_ref, etri_ref, ttri_ref, pos_ref, w_ref, plan_ref, meta_ref, cnt_ref, run_ref, base_ref):
    p = pl.program_id(0)
    i = pl.program_id(1)
    tm = lg_ref.shape[1]
    sub = lax.broadcasted_iota(jnp.int32, (N_EXPERTS, tm), 0).astype(F32)
    tf = float(FFN_TILE)

    @pl.when(p == 0)
    def _():
        e_a, e_b, w_a, w_b = _route_rows(lg_ref[...])
        meta_ref[i, 0:1, :] = e_a
        meta_ref[i, 1:2, :] = e_b
        meta_ref[i, 2:3, :] = w_a
        meta_ref[i, 3:4, :] = w_b
        meta_ref[i, 4:8, :] = jnp.zeros((4, tm), F32)
        oh = jnp.where((sub == e_a) | (sub == e_b), 1.0, 0.0)

        @pl.when(i == 0)
        def _():
            cnt_ref[...] = jnp.zeros_like(cnt_ref)

        cnt_ref[...] += jnp.broadcast_to(jnp.sum(oh, axis=1, keepdims=True), cnt_ref.shape)

    @pl.when(p == 1)
    def _():
        @pl.when(i == 0)
        def _():
            c1, c2, c3 = _split3(cnt_ref[...])
            base_ref[...] = _dot(etri_ref[...], c1) + (_dot(etri_ref[...], c2) + _dot(etri_ref[...], c3))
            run_ref[...] = jnp.zeros_like(run_ref)

        e_a = meta_ref[i, 0:1, :]
        e_b = meta_ref[i, 1:2, :]
        oh_a = jnp.where(sub == e_a, 1.0, 0.0)
        oh_b = jnp.where(sub == e_b, 1.0, 0.0)
        oh = oh_a + oh_b
        before = _dot(oh.astype(BF16), ttri_ref[...]) + run_ref[...] + base_ref[...]
        pos_ref[...] = jnp.zeros(pos_ref.shape, jnp.int32)
        pos_ref[0:1, :] = jnp.sum(oh_a * before, axis=0, keepdims=True).astype(jnp.int32)
        pos_ref[1:2, :] = jnp.sum(oh_b * before, axis=0, keepdims=True).astype(jnp.int32)
        w_ref[...] = meta_ref[i]
        run_ref[...] += jnp.broadcast_to(jnp.sum(oh, axis=1, keepdims=True), run_ref.shape)

        @pl.when(i == n_tiles - 1)
        def _():
            cnt = cnt_ref[:, 0:PLAN_ITEMS]
            base = base_ref[:, 0:PLAN_ITEMS]
            e_id = lax.broadcasted_iota(jnp.int32, (N_EXPERTS, PLAN_ITEMS), 0)
            t_lo = jnp.floor(base / tf)
            t_hi = jnp.floor((base + cnt - 1.0) / tf)
            n_items = jnp.where(cnt > 0.0, t_hi - t_lo + 1.0, 0.0)
            i_start = _dot(etri_ref[...], n_items.astype(BF16))
            item = lax.broadcasted_iota(jnp.int32, (N_EXPERTS, PLAN_ITEMS), 1).astype(F32)
            sel = (i_start <= item) & (item < i_start + n_items)
            tile_e = t_lo + item - i_start
            off = base - tile_e * tf

            def pick(v):
                return jnp.sum(jnp.where(sel, v, 0.0), axis=0, keepdims=True).astype(jnp.int32)

            plan_ref[...] = jnp.zeros(plan_ref.shape, jnp.int32)
            plan_ref[0:1, :] = pick(e_id.astype(F32))
            plan_ref[1:2, :] = pick(tile_e)
            plan_ref[2:3, :] = pick(jnp.maximum(off, 0.0))
            plan_ref[3:4, :] = pick(jnp.minimum(off + cnt, tf))
            plan_ref[4:5, :] = jnp.sum(n_items, axis=0, keepdims=True).astype(jnp.int32)


def _route(lg, tris):
    expert_tri, token_tri = tris
    n_tiles, _, tm = lg.shape
    return pl.pallas_call(
        functools.partial(_route_kernel, n_tiles),
        grid=(2, n_tiles),
        in_specs=[pl.BlockSpec((None, ROUTE_ROWS, tm), lambda p, i: (i, 0, 0)),
                  _const_spec(expert_tri.shape), _const_spec(token_tri.shape)],
        out_specs=[pl.BlockSpec((None, 8, tm), lambda p, i: (i * p, 0, 0)),
                   pl.BlockSpec((None, 8, tm), lambda p, i: (i * p, 0, 0)),
                   pl.BlockSpec((8, PLAN_ITEMS), lambda p, i: (0, 0))],
        out_shape=[jax.ShapeDtypeStruct((n_tiles, 8, tm), jnp.int32),
                   jax.ShapeDtypeStruct((n_tiles, 8, tm), F32),
                   jax.ShapeDtypeStruct((8, PLAN_ITEMS), jnp.int32)],
        scratch_shapes=[pltpu.VMEM((n_tiles, 8, tm), F32),
                        pltpu.VMEM((N_EXPERTS, tm), F32),
                        pltpu.VMEM((N_EXPERTS, tm), F32),
                        pltpu.VMEM((N_EXPERTS, tm), F32)],
        compiler_params=_params(("arbitrary", "arbitrary")),
        name="route",
    )(lg, expert_tri, token_tri)


ROW_UNROLL = 8


DISPATCH_BLOCKS = 4


def _dispatch_kernel(tm, pos_ref, h_ref, xs_ref, sem):
    def row_copy(t, dst):
        return pltpu.make_async_copy(h_ref.at[pl.ds(t, 1)], xs_ref.at[pl.ds(dst, 1)], sem)

    for blk in range(DISPATCH_BLOCKS):
        def issue(g, carry, blk=blk):
            for u in range(ROW_UNROLL):
                t = g * ROW_UNROLL + u
                row_copy(blk * tm + t, pos_ref[blk, 0, t]).start()
                row_copy(blk * tm + t, pos_ref[blk, 1, t]).start()
            return carry

        lax.fori_loop(0, tm // ROW_UNROLL, issue, 0)

    def drain(g, carry):
        for _ in range(2 * ROW_UNROLL):
            row_copy(0, 0).wait()
        return carry

    lax.fori_loop(0, DISPATCH_BLOCKS * tm // ROW_UNROLL, drain, 0)


def _dispatch(pos, h2, n_rows):
    n_tiles, _, tm = pos.shape
    d = h2.shape[1]
    nb = DISPATCH_BLOCKS
    assert n_tiles % nb == 0
    return pl.pallas_call(
        functools.partial(_dispatch_kernel, tm),
        grid=(n_tiles // nb,),
        in_specs=[pl.BlockSpec((nb, 8, tm), lambda i: (i, 0, 0), memory_space=pltpu.SMEM),
                  pl.BlockSpec((nb * tm, d), lambda i: (i, 0))],
        out_specs=pl.BlockSpec(memory_space=pl.ANY),
        out_shape=jax.ShapeDtypeStruct((n_rows, d), F32),
        scratch_shapes=[pltpu.SemaphoreType.DMA],
        compiler_params=_params(("arbitrary",)),
        name="dispatch",
    )(pos, h2)


def _ffn_kernel(owner_ref, tile_ref, lo_ref, hi_ref, used_ref, x_ref, w1_ref, w3_ref, w2_ref, y_ref,
                w1b_ref, w3b_ref, w2b_ref):
    w = pl.program_id(0)
    prev = jnp.maximum(w - 1, 0)

    @pl.when((w == 0) | (owner_ref[w] != owner_ref[prev]))
    def _():
        w1b_ref[...] = w1_ref[...].astype(BF16)
        w3b_ref[...] = w3_ref[...].astype(BF16)
        w2b_ref[...] = w2_ref[...].astype(BF16)

    @pl.when(w < used_ref[0])
    def _():
        x = x_ref[...].astype(BF16)
        a = _dot(x, w1b_ref[...])
        b = _dot(x, w3b_ref[...])
        hid = (a * _sigmoid(a)) * b
        y = _dot(hid.astype(BF16), w2b_ref[...])
        row = lax.broadcasted_iota(jnp.int32, y.shape, 0)
        mine = (row >= lo_ref[w]) & (row < hi_ref[w])
        first = (w == 0) | (tile_ref[w] != tile_ref[jnp.maximum(w - 1, 0)])

        @pl.when(first)
        def _():
            y_ref[...] = jnp.where(mine, y, 0.0)

        @pl.when(jnp.logical_not(first))
        def _():
            y_ref[...] = jnp.where(mine, y, y_ref[...])


def _ffn(plan, xs, w1, w3, w2):
    n_rows, d = xs.shape
    tf = FFN_TILE
    de = w1.shape[1] // N_EXPERTS
    n_items = n_rows // tf + N_EXPERTS - 1
    assert n_items <= PLAN_ITEMS

    def item(w, used):
        return jnp.minimum(w, used[0] - 1)

    grid_spec = pltpu.PrefetchScalarGridSpec(
        num_scalar_prefetch=5,
        grid=(n_items,),
        in_specs=[pl.BlockSpec((tf, d), lambda w, own, til, lo, hi, used: (til[item(w, used)], 0)),
                  pl.BlockSpec((d, de), lambda w, own, til, lo, hi, used: (0, own[item(w, used)])),
                  pl.BlockSpec((d, de), lambda w, own, til, lo, hi, used: (0, own[item(w, used)])),
                  pl.BlockSpec((de, d), lambda w, own, til, lo, hi, used: (own[item(w, used)], 0))],
        out_specs=pl.BlockSpec((tf, d), lambda w, own, til, lo, hi, used: (til[item(w, used)], 0)),
        scratch_shapes=[pltpu.VMEM((d, de), BF16), pltpu.VMEM((d, de), BF16), pltpu.VMEM((de, d), BF16)],
    )
    return pl.pallas_call(
        _ffn_kernel,
        grid_spec=grid_spec,
        out_shape=jax.ShapeDtypeStruct((n_rows, d), F32),
        compiler_params=_params(("arbitrary",)),
        name="ffn",
    )(plan[0], plan[1], plan[2], plan[3], plan[4, 0:1], xs, w1, w3, w2)


def _combine_kernel(tm, ctx_steps, pos_ref, nxt_ref, w_ref, x1_ref, mod_ref, gp_ref, ys_ref, yc_ref, yl_ref,
                    buf_ref, sem):
    i = pl.program_id(0)
    n = pl.num_programs(0)
    slot = i % 2

    def row_copy(src, k, t, s):
        return pltpu.make_async_copy(ys_ref.at[pl.ds(src, 1)], buf_ref.at[s, k, pl.ds(t, 1)], sem.at[s])

    def issue(p_ref, s):
        def body(g, carry):
            for u in range(ROW_UNROLL):
                t = g * ROW_UNROLL + u
                row_copy(p_ref[0, t], 0, t, s).start()
                row_copy(p_ref[1, t], 1, t, s).start()
            return carry
        lax.fori_loop(0, tm // ROW_UNROLL, body, 0)

    @pl.when(i == 0)
    def _():
        issue(pos_ref, 0)

    @pl.when(i + 1 < n)
    def _():
        issue(nxt_ref, 1 - slot)

    def drain(g, carry):
        for _ in range(2 * ROW_UNROLL):
            row_copy(0, 0, 0, slot).wait()
        return carry

    lax.fori_loop(0, tm // ROW_UNROLL, drain, 0)

    r = lax.broadcasted_iota(jnp.int32, (tm, tm), 0)
    c = lax.broadcasted_iota(jnp.int32, (tm, tm), 1)
    d = x1_ref.shape[1]
    wa = jnp.broadcast_to(jnp.sum(jnp.where(r == c, w_ref[2:3, :], 0.0), axis=1, keepdims=True), (tm, d))
    wb = jnp.broadcast_to(jnp.sum(jnp.where(r == c, w_ref[3:4, :], 0.0), axis=1, keepdims=True), (tm, d))
    moe = wa * buf_ref[slot, 0] + wb * buf_ref[slot, 1]
    gt2 = mod_ref[5:6, :]
    y = x1_ref[...] + gt2 * _rms(moe, gp_ref[...])

    @pl.when(i < ctx_steps)
    def _():
        yc_ref[...] = y

    @pl.when(i >= ctx_steps)
    def _():
        yl_ref[...] = y


def _combine(st, pos, w, x1, mod, g_post2, ys):
    n_tiles, _, tm = pos.shape
    d = x1.shape[1]
    row = pl.BlockSpec((tm, d), lambda i: (i, 0))
    return pl.pallas_call(
        functools.partial(_combine_kernel, tm, st.ctx_steps(tm)),
        grid=(n_tiles,),
        in_specs=[pl.BlockSpec((None, 8, tm), lambda i: (i, 0, 0), memory_space=pltpu.SMEM),
                  pl.BlockSpec((None, 8, tm), lambda i: (jnp.minimum(i + 1, n_tiles - 1), 0, 0),
                               memory_space=pltpu.SMEM),
                  pl.BlockSpec((None, 8, tm), lambda i: (i, 0, 0)),
                  row,
                  st.mod_spec(tm, d),
                  _const_spec((1, d)),
                  pl.BlockSpec(memory_space=pl.ANY)],
        out_specs=[st.ctx_spec(tm, d), st.lat_spec(tm, d)],
        out_shape=[jax.ShapeDtypeStruct((st.n_ctx, d), F32), jax.ShapeDtypeStruct((st.n_lat, d), F32)],
        scratch_shapes=[pltpu.VMEM((2, 2, tm, d), F32), pltpu.SemaphoreType.DMA((2,))],
        compiler_params=_params(("arbitrary",)),
        name="combine",
    )(pos, pos, w, x1, mod, g_post2.reshape(1, d), ys)


def _moe(st, h2, lg, x1, mod, g_post2, w1, w3, w2, route_tri):
    n_tok = x1.shape[0]
    n_rows = 2 * n_tok
    assert n_rows % FFN_TILE == 0
    pos, w, plan = _route(lg, route_tri)
    xs = _dispatch(pos, h2, n_rows)
    ys = _ffn(plan, xs, w1, w3, w2)
    return _combine(st, pos, w, x1, mod, g_post2, ys)


def _grid_pos(n_tokens, d_model):
    rows = n_tokens // GRID_W
    row = np.repeat(np.arange(rows, dtype=np.float64), GRID_W)
    col = np.tile(np.arange(GRID_W, dtype=np.float64), rows)
    n_freq = d_model // 4
    freq = np.exp(-math.log(POS_BASE) * np.arange(n_freq, dtype=np.float64) / n_freq)

    def enc(p):
        a = p[:, None] * freq[None, :]
        return np.concatenate([np.sin(a), np.cos(a)], axis=-1)

    return jnp.asarray(np.concatenate([enc(row), enc(col)], axis=-1), dtype=F32)


def _dft_cos_sin(n):
    k = np.arange(n, dtype=np.int64)
    ang = 2.0 * np.pi * ((k[:, None] * k[None, :]) % n).astype(np.float64) / n
    return np.cos(ang), np.sin(ang)


def _fnet_consts(seq_len):
    ct, st = _dft_cos_sin(seq_len)
    f_mat = jnp.asarray(np.concatenate([ct, -st], axis=1), dtype=F32).astype(BF16)
    cc, sc = _dft_cos_sin(FGROUP_DIM)
    cs_mat = jnp.asarray(np.stack([cc, sc]), dtype=F32).astype(BF16)
    return f_mat, cs_mat


def _tri_consts():
    i = np.arange(CHUNK)
    prefix = (i[:, None] <= i[None, :]).astype(np.float32)
    suffix = (i[:, None] >= i[None, :]).astype(np.float32)
    tri = jnp.asarray(np.stack([prefix, suffix]), dtype=BF16)
    e = np.arange(N_EXPERTS)
    expert_tri = jnp.asarray((e[None, :] < e[:, None]).astype(np.float32), dtype=BF16)
    t = np.arange(TOKEN_TILE)
    token_tri = jnp.asarray((t[:, None] < t[None, :]).astype(np.float32), dtype=BF16)
    return tri, (expert_tri, token_tri)


def _layer(x_prompt, x_sample, pos, mod, states, lw, consts):
    bp, sp, d = x_prompt.shape
    bs, ss, _ = x_sample.shape
    st = _Streams(n_ctx=bp * sp, n_lat=bs * ss, seq_lat=ss)
    (g_pre1, proj_w, g_hn, post_w, g_post1, g_pre2, w_e1, w_e3, w_e2, g_post2) = lw
    tri, route_tris = consts
    x_ctx = x_prompt.reshape(st.n_ctx, d)
    x_lat = x_sample.reshape(st.n_lat, d)
    q, kt, v, o, u, ga, gb, gr = _proj(st, x_ctx, x_lat, pos, mod, g_pre1, proj_w + (tri,))
    ha_ctx, new_c, new_n, new_m = _mlstm(q, kt, v, o, g_hn, gr, bp, sp, 0, None, MLSTM_HEADS_PER_STEP)
    (ha_lat,) = _mlstm(q, kt, v, o, g_hn, gr, bs, ss, st.n_ctx, states, MLSTM_HEADS_PER_STEP)
    yf_ctx = _fnet(u, *_fnet_consts(sp), bp, sp, 0)
    yf_lat = _fnet(u, *_fnet_consts(ss), bs, ss, st.n_ctx)
    x1, h2, logits = _post(st, x_ctx, x_lat, pos, ha_ctx, ha_lat, yf_ctx, yf_lat, ga, gb, mod, g_post1, g_pre2, post_w)
    y_ctx, y_lat = _moe(st, h2, logits, x1, mod, g_post2, w_e1, w_e3, w_e2, route_tris)
    return y_ctx.reshape(bp, sp, d), y_lat.reshape(bs, ss, d), (new_c, new_n, new_m)


def kernel(x_prompt, x_sample, c, state_C, state_n, state_m, c_ctx, w_ada, b_ada, g_pre1, w_in, b_gates, g_hn,
           w_a, w_f, w_out, g_post1, g_pre2, w_rg, b_rg, w_re, b_re, w_e1, w_e3, w_e2, g_post2):
    bp, sp, d = x_prompt.shape
    bs, ss, _ = x_sample.shape
    depth = w_in.shape[0]
    assert depth == 1
    l = 0
    dm = N_HEADS * HEAD_DIM
    du = N_FGROUPS * FGROUP_DIM
    ng = 4 * N_HEADS

    n_rows = 8
    assert 1 + bs <= n_rows
    cs = jnp.concatenate([c_ctx[None, :], c, jnp.zeros((n_rows - 1 - bs, d), F32)], axis=0)
    mod = _ada(cs, w_ada[l], b_ada[l]).reshape(n_rows, 6, d)

    wi = w_in[l]
    cuts = np.cumsum([dm] * 4 + [ng, du, d]).tolist()
    wq = wi[:, :cuts[0]].astype(BF16)
    wkt = wi[:, cuts[0]:cuts[1]].T.astype(BF16)
    wv = wi[:, cuts[1]:cuts[2]].astype(BF16)
    wo = wi[:, cuts[2]:cuts[3]].astype(BF16)
    wgt = wi[:, cuts[3]:cuts[4]].T
    wu = wi[:, cuts[4]:cuts[5]].astype(BF16)
    wga = wi[:, cuts[5]:cuts[6]].astype(BF16)
    wgb = wi[:, cuts[6]:].astype(BF16)
    bg = b_gates[l].reshape(ng, 1)
    proj_w = (wq, wkt, wv, wo, wu, wga, wgb, wgt, bg)
    n_pad = ROUTE_ROWS - N_EXPERTS - N_GROUPS
    wr = jnp.concatenate([w_re[l], w_rg[l], jnp.zeros((d, n_pad), F32)], axis=1).T
    br = jnp.concatenate([b_re[l], b_rg[l], jnp.zeros((n_pad,), F32)]).reshape(ROUTE_ROWS, 1)
    post_w = (w_a[l].astype(BF16), w_f[l].astype(BF16), w_out[l].astype(BF16), wr, br)
    lw = (g_pre1[l], proj_w, g_hn[l], post_w, g_post1[l], g_pre2[l], w_e1[l], w_e3[l], w_e2[l], g_post2[l])
    consts = _tri_consts()

    pos = _grid_pos(ss, d)
    y_prompt, y_sample, (new_c, new_n, new_m) = _layer(x_prompt, x_sample, pos, mod,
                                                       (state_C, state_n, state_m), lw, consts)

    new_state_c = new_c
    new_state_n = new_n.reshape(bp, depth, 2, N_HEADS, HEAD_DIM)
    new_state_m = new_m[..., 0, 0]
    return (y_prompt, y_sample, new_state_c, new_state_n, new_state_m)
```

```python
import functools
import math
from typing import NamedTuple

import numpy as np
import jax
import jax.numpy as jnp
from jax import lax
from jax.experimental import pallas as pl
from jax.experimental.pallas import tpu as pltpu

F32 = jnp.float32
BF16 = jnp.bfloat16

N_HEADS = 8
HEAD_DIM = 128
N_FGROUPS = 4
FGROUP_DIM = 128
N_GROUPS = 4
EXPERTS_PER_GROUP = 8
N_EXPERTS = N_GROUPS * EXPERTS_PER_GROUP
GRID_W = 64
POS_BASE = 10000.0
EPS = 1e-6

V7X_VMEM_LIMIT_BYTES = 56 * 1024 * 1024
LANES = 128
CHUNK = 256
TOKEN_TILE = CHUNK
CHUNKS_PER_STEP = 2
MLSTM_HEADS_PER_STEP = 4


def _params(sem):
    return pltpu.CompilerParams(dimension_semantics=sem, vmem_limit_bytes=V7X_VMEM_LIMIT_BYTES)


def _const_spec(shape):
    n = len(shape)
    return pl.BlockSpec(shape, lambda *_: (0,) * n, pipeline_mode=pl.Buffered(1))


def _split2(x):
    hi = x.astype(BF16)
    lo = (x - hi.astype(F32)).astype(BF16)
    return hi, lo


def _split3(x):
    hi = x.astype(BF16)
    r = x - hi.astype(F32)
    mid = r.astype(BF16)
    lo = (r - mid.astype(F32)).astype(BF16)
    return hi, mid, lo


def _dot(a, b):
    return jnp.dot(a, b, preferred_element_type=F32)


def _dot_nt(a, b):
    return lax.dot_general(a, b, (((1,), (1,)), ((), ())), preferred_element_type=F32)


def _dot_f32(a, b, nt=False):
    d = _dot_nt if nt else _dot
    a1, a2 = _split2(a)
    b1, b2 = _split2(b)
    return d(a1, b1) + (d(a1, b2) + d(a2, b1))


def _dot_exact_rhs(a, b_bf16, nt=False):
    d = _dot_nt if nt else _dot
    a1, a2, a3 = _split3(a)
    return d(a1, b_bf16) + (d(a2, b_bf16) + d(a3, b_bf16))


def _rms(x, g):
    return x * lax.rsqrt(jnp.mean(x * x, axis=-1, keepdims=True) + EPS) * g


def _sigmoid(x):
    return 1.0 / (1.0 + jnp.exp(-x))


def _log_sigmoid(x):
    return jnp.minimum(x, 0.0) - jnp.log(1.0 + jnp.exp(-jnp.abs(x)))


def _ada_kernel(c_ref, w_ref, b_ref, o_ref):
    c = c_ref[...]
    s = (c * _sigmoid(c)).astype(BF16)
    o_ref[...] = _dot(s, w_ref[...].astype(BF16)) + b_ref[...]


def _ada(cs, w_ada, b_ada):
    rows, d = cs.shape
    n = w_ada.shape[1]
    tn = 1024
    return pl.pallas_call(
        _ada_kernel,
        grid=(n // tn,),
        in_specs=[pl.BlockSpec((rows, d), lambda j: (0, 0)),
                  pl.BlockSpec((d, tn), lambda j: (0, j)),
                  pl.BlockSpec((1, tn), lambda j: (0, j))],
        out_specs=pl.BlockSpec((rows, tn), lambda j: (0, j)),
        out_shape=jax.ShapeDtypeStruct((rows, n), F32),
        compiler_params=_params(("arbitrary",)),
        name="ada",
    )(cs, w_ada, b_ada.reshape(1, n))


def _proj_kernel(ctx_steps, xc_ref, xl_ref, pos_ref, mod_ref, g_ref, wq_ref, wkt_ref, wv_ref, wo_ref, wu_ref,
                 wga_ref, wgb_ref, wgt_ref, bg_ref, tri_ref,
                 q_ref, kt_ref, v_ref, o_ref, u_ref, ga_ref, gb_ref, gr_ref):
    x = jnp.where(pl.program_id(0) < ctx_steps, xc_ref[...], xl_ref[...] + pos_ref[...])
    sh = mod_ref[0:1, :]
    sc = mod_ref[1:2, :]
    h = _rms(x, g_ref[...]) * (1.0 + sc) + sh
    hb = h.astype(BF16)
    q_ref[...] = _dot(hb, wq_ref[...]).astype(BF16)
    kt = (_dot_nt(wkt_ref[...], hb) * (HEAD_DIM ** -0.5)).astype(BF16)
    tc = TOKEN_TILE
    for s in range(CHUNKS_PER_STEP):
        kt_ref[s] = kt[:, s * tc:(s + 1) * tc]
    v_ref[...] = _dot(hb, wv_ref[...]).astype(BF16)
    o_ref[...] = _dot(hb, wo_ref[...])
    u_ref[...] = _dot(hb, wu_ref[...]).astype(BF16)
    ga_ref[...] = _dot(hb, wga_ref[...])
    gb_ref[...] = _dot(hb, wgb_ref[...])
    g = _dot_f32(wgt_ref[...], h, nt=True) + bg_ref[...]
    nh = N_HEADS
    lf_f = _log_sigmoid(g[nh:2 * nh])
    lf_b = _log_sigmoid(g[3 * nh:4 * nh])
    for s in range(CHUNKS_PER_STEP):
        cols = slice(s * tc, (s + 1) * tc)
        gr_ref[s, 0:nh, :] = g[0:nh, cols]
        gr_ref[s, nh:2 * nh, :] = _dot_exact_rhs(lf_f[:, cols], tri_ref[0])
        gr_ref[s, 2 * nh:3 * nh, :] = g[2 * nh:3 * nh, cols]
        gr_ref[s, 3 * nh:4 * nh, :] = _dot_exact_rhs(lf_b[:, cols], tri_ref[1])


class _Streams(NamedTuple):
    n_ctx: int
    n_lat: int
    seq_lat: int

    def ctx_steps(self, tm):
        assert self.n_ctx % tm == 0 and self.seq_lat % tm == 0
        return self.n_ctx // tm

    def ctx_spec(self, tm, width):
        c = self.ctx_steps(tm)
        return pl.BlockSpec((tm, width), lambda i, *_: (jnp.minimum(i, c - 1), 0))

    def lat_spec(self, tm, width):
        c = self.ctx_steps(tm)
        return pl.BlockSpec((tm, width), lambda i, *_: (jnp.maximum(i - c, 0), 0))

    def pos_spec(self, tm, width):
        c = self.ctx_steps(tm)
        per_seq = self.seq_lat // tm
        return pl.BlockSpec((tm, width), lambda i, *_: (jnp.maximum(i - c, 0) % per_seq, 0))

    def mod_spec(self, tm, width):
        c = self.ctx_steps(tm)
        per_seq = self.seq_lat // tm
        return pl.BlockSpec((None, 6, width),
                            lambda i, *_: (jnp.where(i < c, 0, 1 + jnp.maximum(i - c, 0) // per_seq), 0, 0))


def _proj(st, x_ctx, x_lat, pos, mod, g_pre1, wts):
    d = x_ctx.shape[1]
    n_tok = st.n_ctx + st.n_lat
    tc = TOKEN_TILE
    ns = CHUNKS_PER_STEP
    tm = ns * tc
    n_tiles = n_tok // tc
    (wq, wkt, wv, wo, wu, wga, wgb, wgt, bg, tri) = wts
    row = pl.BlockSpec((tm, d), lambda i: (i, 0))
    in_specs = [st.ctx_spec(tm, d), st.lat_spec(tm, d), st.pos_spec(tm, d), st.mod_spec(tm, d), _const_spec((1, d))]
    in_specs += [_const_spec(w.shape) for w in wts]
    args = [x_ctx, x_lat, pos, mod, g_pre1.reshape(1, d)] + list(wts)
    n_gate_rows = 4 * N_HEADS
    du = wu.shape[1]
    out_shape = [jax.ShapeDtypeStruct((n_tok, d), BF16),
                 jax.ShapeDtypeStruct((n_tiles, d, tc), BF16),
                 jax.ShapeDtypeStruct((n_tok, d), BF16),
                 jax.ShapeDtypeStruct((n_tok, d), F32),
                 jax.ShapeDtypeStruct((n_tok, du), BF16),
                 jax.ShapeDtypeStruct((n_tok, d), F32),
                 jax.ShapeDtypeStruct((n_tok, d), F32),
                 jax.ShapeDtypeStruct((n_tiles, n_gate_rows, tc), F32)]
    out_specs = [row,
                 pl.BlockSpec((ns, d, tc), lambda i: (i, 0, 0)),
                 row, row,
                 pl.BlockSpec((tm, du), lambda i: (i, 0)),
                 row, row,
                 pl.BlockSpec((ns, n_gate_rows, tc), lambda i: (i, 0, 0))]
    return pl.pallas_call(
        functools.partial(_proj_kernel, st.ctx_steps(tm)),
        grid=(n_tok // tm,),
        in_specs=in_specs,
        out_specs=out_specs,
        out_shape=out_shape,
        compiler_params=_params(("parallel",)),
        name="proj",
    )(*args)


def _mlstm_kernel(n_chunks, hp, has_state, emit_state, *refs):
    it = iter(refs)
    if has_state:
        m0_ref = next(it)
    q_ref, kt_ref, v_ref, o_ref, ghn_ref, gr_ref = (next(it) for _ in range(6))
    if has_state:
        c0_ref, n0_ref = next(it), next(it)
    ha_ref = next(it)
    if emit_state:
        cout_ref, nout_ref, mout_ref = next(it), next(it), next(it)
    hs_ref, cn_ref, m_ref = (next(it) for _ in range(3))

    L = CHUNK
    nh, dh = N_HEADS, HEAD_DIM
    head0 = pl.program_id(1) * hp
    neg_inf = -jnp.inf

    sq_r = lax.broadcasted_iota(jnp.int32, (dh, dh), 0)
    sq_c = lax.broadcasted_iota(jnp.int32, (dh, dh), 1)
    for j in range(hp):
        for d in range(2):
            if has_state:
                n_col = jnp.sum(jnp.where(sq_r == sq_c, n0_ref[d, j], 0.0), axis=1, keepdims=True)
                cn_ref[j, d, :, 0:dh] = c0_ref[d, j]
                cn_ref[j, d, :, dh:2 * dh] = jnp.broadcast_to(n_col, (dh, dh))
                m0 = m0_ref[pl.program_id(0) * (2 * nh) + d * nh + head0 + j]
                m_ref[j, d] = jnp.full((8, LANES), m0, F32)
            else:
                cn_ref[j, d] = jnp.zeros((dh, 2 * dh), F32)
                m_ref[j, d] = jnp.zeros((8, LANES), F32)

    row_id = lax.broadcasted_iota(jnp.int32, (L, L), 0)
    col_id = lax.broadcasted_iota(jnp.int32, (L, L), 1)
    ones_blk = jnp.ones((L, dh), BF16)

    def chunk(j, d, c):
        t0 = pl.multiple_of(c * L, L)
        lanes = slice(j * dh, (j + 1) * dh)
        qc = q_ref[pl.ds(t0, L), lanes]
        ktc = kt_ref[c, lanes, :]
        vc = v_ref[pl.ds(t0, L), lanes]
        ig_row = gr_ref[c, pl.ds(2 * nh * d + head0 + j, 1), :]
        b_row = gr_ref[c, pl.ds(2 * nh * d + nh + head0 + j, 1), :]
        m_prev = m_ref[j, d][0:1, 0:1]
        if d == 0:
            b_end = b_row[:, L - 1:L]
            mask = col_id <= row_id
        else:
            b_end = b_row[:, 0:1]
            mask = col_id >= row_id
        a_row = ig_row - b_row
        a_max = jnp.max(jnp.where(mask, a_row, neg_inf), axis=1, keepdims=True)
        g = jnp.maximum(jnp.broadcast_to(a_max, (L, dh)), m_prev)
        b_col = jnp.broadcast_to(
            jnp.sum(jnp.where(row_id == col_id, b_row, 0.0), axis=1, keepdims=True), (L, dh))
        g_full = jnp.concatenate([g] * (L // dh), axis=1)
        s = _dot(qc, ktc) * jnp.exp(jnp.where(mask, a_row - g_full, neg_inf))
        decay = jnp.exp(m_prev - g)
        qcn = _dot(qc, cn_ref[j, d].astype(BF16))
        num = decay * qcn[:, 0:dh] + _dot(s.astype(BF16), vc)
        den = decay * qcn[:, dh:2 * dh] + jnp.sum(s, axis=1, keepdims=True)
        hch = num / jnp.maximum(jnp.abs(den), jnp.exp(-(b_col + g)))
        w_log = b_end + a_row
        m_new = jnp.maximum(b_end + m_prev, jnp.max(w_log, axis=1, keepdims=True))
        kw = (ktc.astype(F32) * jnp.exp(w_log - m_new)).astype(BF16)
        v_aug = jnp.concatenate([vc, ones_blk], axis=1)
        cn_ref[j, d] = jnp.exp(b_end + m_prev - m_new) * cn_ref[j, d] + _dot(kw, v_aug)
        m_ref[j, d] = jnp.broadcast_to(m_new, (8, LANES))
        return t0, hch

    if n_chunks == 1:
        for j in range(hp):
            _, h_f = chunk(j, 0, 0)
            _, h_b = chunk(j, 1, 0)
            hs_ref[:, j * dh:(j + 1) * dh] = h_f + h_b
    else:
        half = n_chunks // 2

        def first(i, carry):
            for j in range(hp):
                lanes = slice(j * dh, (j + 1) * dh)
                t_f, h_f = chunk(j, 0, i)
                t_b, h_b = chunk(j, 1, n_chunks - 1 - i)
                hs_ref[pl.ds(t_f, L), lanes] = h_f
                hs_ref[pl.ds(t_b, L), lanes] = h_b
            return carry

        def second(i, carry):
            for j in range(hp):
                lanes = slice(j * dh, (j + 1) * dh)
                t_f, h_f = chunk(j, 0, i)
                t_b, h_b = chunk(j, 1, n_chunks - 1 - i)
                hs_ref[pl.ds(t_f, L), lanes] += h_f
                hs_ref[pl.ds(t_b, L), lanes] += h_b
            return carry

        lax.fori_loop(0, half, first, 0)
        lax.fori_loop(half, n_chunks, second, 0)

    def finish(c, carry):
        t0 = pl.multiple_of(c * L, L)
        for j in range(hp):
            lanes = slice(j * dh, (j + 1) * dh)
            ha = hs_ref[pl.ds(t0, L), lanes]
            ha = ha * lax.rsqrt(jnp.mean(ha * ha, axis=-1, keepdims=True) + EPS)
            ha = ha * ghn_ref[:, lanes] * _sigmoid(o_ref[pl.ds(t0, L), lanes])
            ha_ref[pl.ds(t0, L), lanes] = ha.astype(BF16)
        return carry

    if n_chunks == 1:
        finish(0, 0)
    else:
        lax.fori_loop(0, n_chunks, finish, 0)

    if emit_state:
        for j in range(hp):
            for d in range(2):
                cout_ref[d, j] = cn_ref[j, d, :, 0:dh]
                n_rep = cn_ref[j, d, :, dh:2 * dh]
                nout_ref[d, j] = jnp.sum(jnp.where(sq_r == sq_c, n_rep, 0.0), axis=0, keepdims=True)
                mout_ref[d, j] = m_ref[j, d][0:1, :]


def _mlstm(q, kt, v, o, g_hn, gr, n_seq, seq_len, tok_off, states, hp):
    d = q.shape[1]
    n_tok = n_seq * seq_len
    L = CHUNK
    assert tok_off % seq_len == 0
    sb = tok_off // seq_len
    n_chunks = seq_len // L
    has_state = states is not None
    emit_state = not has_state
    nh, dh = N_HEADS, HEAD_DIM
    assert n_chunks == 1 or n_chunks % 2 == 0
    assert nh % hp == 0
    wd = hp * dh

    in_specs = []
    args = []
    if has_state:
        state_c, state_n, state_m = states
        in_specs.append(pl.BlockSpec(memory_space=pltpu.SMEM))
        args.append(state_m.reshape(-1))
    tok = pl.BlockSpec((seq_len, wd), lambda b, h: (b + sb, h))
    in_specs += [tok,
                 pl.BlockSpec((n_chunks, wd, L), lambda b, h: (b + sb, h, 0)),
                 tok, tok,
                 pl.BlockSpec((1, wd), lambda b, h: (0, h)),
                 pl.BlockSpec((n_chunks, 4 * nh, L), lambda b, h: (b + sb, 0, 0))]
    args += [q, kt, v, o, g_hn.reshape(1, d), gr]
    st_c = pl.BlockSpec((None, None, 2, hp, dh, dh), lambda b, h: (b, 0, 0, h, 0, 0))
    st_v = pl.BlockSpec((None, None, 2, hp, 1, dh), lambda b, h: (b, 0, 0, h, 0, 0))
    if has_state:
        in_specs += [st_c, st_v]
        args += [state_c, state_n.reshape(n_seq, 1, 2, nh, 1, dh)]
    out_shape = [jax.ShapeDtypeStruct((n_tok, d), BF16)]
    out_specs = [pl.BlockSpec((seq_len, wd), lambda b, h: (b, h))]
    if emit_state:
        out_shape += [jax.ShapeDtypeStruct((n_seq, 1, 2, nh, dh, dh), F32),
                      jax.ShapeDtypeStruct((n_seq, 1, 2, nh, 1, dh), F32),
                      jax.ShapeDtypeStruct((n_seq, 1, 2, nh, 1, LANES), F32)]
        out_specs += [st_c, st_v, st_v]
    scratch = [pltpu.VMEM((seq_len, wd), F32),
               pltpu.VMEM((hp, 2, dh, 2 * dh), F32),
               pltpu.VMEM((hp, 2, 8, LANES), F32)]
    return pl.pallas_call(
        functools.partial(_mlstm_kernel, n_chunks, hp, has_state, emit_state),
        grid=(n_seq, nh // hp),
        in_specs=in_specs,
        out_specs=out_specs,
        out_shape=out_shape,
        scratch_shapes=scratch,
        compiler_params=_params(("parallel", "parallel")),
        name="mlstm",
    )(*args)


def _fnet_kernel(seq_len, u_ref, f_ref, cs_ref, y_ref, ab_ref):
    T = seq_len

    @pl.when(pl.program_id(1) == 0)
    def _():
        for g in range(N_FGROUPS):
            ug = u_ref[:, g * FGROUP_DIM:(g + 1) * FGROUP_DIM]
            ab_ref[0:T, g * FGROUP_DIM:(g + 1) * FGROUP_DIM] = _dot(ug, cs_ref[0]).astype(BF16)
            ab_ref[T:2 * T, g * FGROUP_DIM:(g + 1) * FGROUP_DIM] = _dot(ug, cs_ref[1]).astype(BF16)

    scale = 1.0 / math.sqrt(T * FGROUP_DIM)
    y_ref[...] = (_dot(f_ref[...], ab_ref[...]) * scale).astype(BF16)


def _fnet(u, f_mat, cs_mat, n_seq, seq_len, tok_off):
    du = u.shape[1]
    n_tok = n_seq * seq_len
    tr = min(seq_len, 512)
    assert tok_off % seq_len == 0
    sb = tok_off // seq_len
    return pl.pallas_call(
        functools.partial(_fnet_kernel, seq_len),
        grid=(n_seq, seq_len // tr),
        in_specs=[pl.BlockSpec((seq_len, du), lambda b, r: (b + sb, 0)),
                  pl.BlockSpec((tr, 2 * seq_len), lambda b, r: (r, 0)),
                  _const_spec(cs_mat.shape)],
        out_specs=pl.BlockSpec((tr, du), lambda b, r: (b * (seq_len // tr) + r, 0)),
        out_shape=jax.ShapeDtypeStruct((n_tok, du), BF16),
        scratch_shapes=[pltpu.VMEM((2 * seq_len, du), BF16)],
        compiler_params=_params(("parallel", "arbitrary")),
        name="fnet",
    )(u, f_mat, cs_mat)


def _post_kernel(ctx_steps, xc_ref, xl_ref, pos_ref, hac_ref, hal_ref, yfc_ref, yfl_ref, ga_ref, gb_ref, mod_ref,
                 gp1_ref, gp2_ref, wa_ref, wf_ref, wout_ref, wr_ref, br_ref, x1_ref, h2_ref, lg_ref):
    is_ctx = pl.program_id(0) < ctx_steps
    x = jnp.where(is_ctx, xc_ref[...], xl_ref[...] + pos_ref[...])
    ya = _dot(jnp.where(is_ctx, hac_ref[...], hal_ref[...]), wa_ref[...])
    yf = _dot(jnp.where(is_ctx, yfc_ref[...], yfl_ref[...]), wf_ref[...])
    mix_in = _sigmoid(ga_ref[...]) * ya + _sigmoid(gb_ref[...]) * yf
    mix = _dot(mix_in.astype(BF16), wout_ref[...])
    gt1 = mod_ref[2:3, :]
    sh2 = mod_ref[3:4, :]
    sc2 = mod_ref[4:5, :]
    x1 = x + gt1 * _rms(mix, gp1_ref[...])
    h2 = _rms(x1, gp2_ref[...]) * (1.0 + sc2) + sh2
    x1_ref[...] = x1
    h2_ref[...] = h2
    lg = _dot_f32(wr_ref[...], h2, nt=True) + br_ref[...]
    for s in range(CHUNKS_PER_STEP):
        lg_ref[s] = lg[:, s * TOKEN_TILE:(s + 1) * TOKEN_TILE]


def _post(st, x_ctx, x_lat, pos, ha_ctx, ha_lat, yf_ctx, yf_lat, ga, gb, mod, g_post1, g_pre2, wts):
    d = x_ctx.shape[1]
    n_tok = st.n_ctx + st.n_lat
    tc = TOKEN_TILE
    ns = CHUNKS_PER_STEP
    tm = ns * tc
    (wa, wf, wout, wr, br) = wts
    du = yf_ctx.shape[1]
    row = pl.BlockSpec((tm, d), lambda i: (i, 0))
    in_specs = [st.ctx_spec(tm, d), st.lat_spec(tm, d), st.pos_spec(tm, d),
                st.ctx_spec(tm, d), st.lat_spec(tm, d), st.ctx_spec(tm, du), st.lat_spec(tm, du),
                row, row, st.mod_spec(tm, d), _const_spec((1, d)), _const_spec((1, d))]
    in_specs += [_const_spec(w.shape) for w in wts]
    args = [x_ctx, x_lat, pos, ha_ctx, ha_lat, yf_ctx, yf_lat, ga, gb, mod,
            g_post1.reshape(1, d), g_pre2.reshape(1, d)] + list(wts)
    return pl.pallas_call(
        functools.partial(_post_kernel, st.ctx_steps(tm)),
        grid=(n_tok // tm,),
        in_specs=in_specs,
        out_specs=[row, row, pl.BlockSpec((ns, ROUTE_ROWS, tc), lambda i: (i, 0, 0))],
        out_shape=[jax.ShapeDtypeStruct((n_tok, d), F32),
                   jax.ShapeDtypeStruct((n_tok, d), F32),
                   jax.ShapeDtypeStruct((n_tok // tc, ROUTE_ROWS, tc), F32)],
        compiler_params=_params(("parallel",)),
        name="post",
    )(*args)


ROUTE_ROWS = N_EXPERTS + 8
ROUTE_BLOCKS = 4
FFN_TILE = 512
PLAN_ITEMS = LANES


def _route_rows(lg):
    tm = lg.shape[1]
    ne = N_EXPERTS
    gl = lg[ne:ne + 8]
    g_id = lax.broadcasted_iota(jnp.int32, (8, tm), 0).astype(F32)
    is_g = g_id < N_GROUPS
    gl = jnp.where(is_g, gl, -jnp.inf)
    mx = jnp.max(gl, axis=0, keepdims=True)
    z = jnp.sum(jnp.where(is_g, jnp.exp(gl - mx), 0.0), axis=0, keepdims=True)
    p_sel = 1.0 / z
    g_sel = jnp.min(jnp.where(gl == mx, g_id, float(N_GROUPS)), axis=0, keepdims=True)
    e_id = lax.broadcasted_iota(jnp.int32, (ne, tm), 0).astype(F32)
    lo = EXPERTS_PER_GROUP * g_sel
    in_grp = (e_id >= lo) & (e_id < lo + EXPERTS_PER_GROUP)
    le = jnp.where(in_grp, lg[0:ne], -jnp.inf)
    v1 = jnp.max(le, axis=0, keepdims=True)
    i1 = jnp.min(jnp.where(le == v1, e_id, float(ne)), axis=0, keepdims=True)
    le2 = jnp.where(e_id == i1, -jnp.inf, le)
    v2 = jnp.max(le2, axis=0, keepdims=True)
    i2 = jnp.min(jnp.where(le2 == v2, e_id, float(ne)), axis=0, keepdims=True)
    e2 = jnp.exp(v2 - v1)
    w1 = p_sel / (1.0 + e2)
    w2 = p_sel * e2 / (1.0 + e2)
    return i1, i2, w1, w2


def _route_kernel(n_tiles, lg_ref, etri_ref, ttri_ref, pos_ref, w_ref, plan_ref, meta_ref, cnt_ref, run_ref, base_ref):
    p = pl.program_id(0)
    i = pl.program_id(1)
    tm = lg_ref.shape[2]
    sub = lax.broadcasted_iota(jnp.int32, (N_EXPERTS, tm), 0).astype(F32)
    tf = float(FFN_TILE)

    nb = ROUTE_BLOCKS

    @pl.when(p == 0)
    def _():
        @pl.when(i == 0)
        def _():
            cnt_ref[...] = jnp.zeros_like(cnt_ref)

        count = jnp.zeros((N_EXPERTS, 1), F32)
        for s in range(nb):
            t = i * nb + s
            e_a, e_b, w_a, w_b = _route_rows(lg_ref[s])
            meta_ref[t, 0:1, :] = e_a
            meta_ref[t, 1:2, :] = e_b
            meta_ref[t, 2:3, :] = w_a
            meta_ref[t, 3:4, :] = w_b
            meta_ref[t, 4:8, :] = jnp.zeros((4, tm), F32)
            oh = jnp.where((sub == e_a) | (sub == e_b), 1.0, 0.0)
            count = count + jnp.sum(oh, axis=1, keepdims=True)
        cnt_ref[...] += jnp.broadcast_to(count, cnt_ref.shape)

    @pl.when(p == 1)
    def _():
        @pl.when(i == 0)
        def _():
            c1, c2, c3 = _split3(cnt_ref[...])
            base_ref[...] = _dot(etri_ref[...], c1) + (_dot(etri_ref[...], c2) + _dot(etri_ref[...], c3))
            run_ref[...] = jnp.zeros_like(run_ref)

        start = run_ref[...] + base_ref[...]
        for s in range(nb):
            t = i * nb + s
            e_a = meta_ref[t, 0:1, :]
            e_b = meta_ref[t, 1:2, :]
            oh_a = jnp.where(sub == e_a, 1.0, 0.0)
            oh_b = jnp.where(sub == e_b, 1.0, 0.0)
            oh = oh_a + oh_b
            before = _dot(oh.astype(BF16), ttri_ref[...]) + start
            pos_ref[s] = jnp.zeros(pos_ref.shape[1:], jnp.int32)
            pos_ref[s, 0:1, :] = jnp.sum(oh_a * before, axis=0, keepdims=True).astype(jnp.int32)
            pos_ref[s, 1:2, :] = jnp.sum(oh_b * before, axis=0, keepdims=True).astype(jnp.int32)
            w_ref[s] = meta_ref[t]
            start = start + jnp.broadcast_to(jnp.sum(oh, axis=1, keepdims=True), start.shape)
        run_ref[...] = start - base_ref[...]

        @pl.when(i == n_tiles // nb - 1)
        def _():
            cnt = cnt_ref[:, 0:PLAN_ITEMS]
            base = base_ref[:, 0:PLAN_ITEMS]
            e_id = lax.broadcasted_iota(jnp.int32, (N_EXPERTS, PLAN_ITEMS), 0)
            t_lo = jnp.floor(base / tf)
            t_hi = jnp.floor((base + cnt - 1.0) / tf)
            n_items = jnp.where(cnt > 0.0, t_hi - t_lo + 1.0, 0.0)
            i_start = _dot(etri_ref[...], n_items.astype(BF16))
            item = lax.broadcasted_iota(jnp.int32, (N_EXPERTS, PLAN_ITEMS), 1).astype(F32)
            sel = (i_start <= item) & (item < i_start + n_items)
            tile_e = t_lo + item - i_start
            off = base - tile_e * tf

            def pick(v):
                return jnp.sum(jnp.where(sel, v, 0.0), axis=0, keepdims=True).astype(jnp.int32)

            plan_ref[...] = jnp.zeros(plan_ref.shape, jnp.int32)
            plan_ref[0:1, :] = pick(e_id.astype(F32))
            plan_ref[1:2, :] = pick(tile_e)
            plan_ref[2:3, :] = pick(jnp.maximum(off, 0.0))
            plan_ref[3:4, :] = pick(jnp.minimum(off + cnt, tf))
            plan_ref[4:5, :] = jnp.sum(n_items, axis=0, keepdims=True).astype(jnp.int32)


def _route(lg, tris):
    expert_tri, token_tri = tris
    n_tiles, _, tm = lg.shape
    nb = ROUTE_BLOCKS
    assert n_tiles % nb == 0
    return pl.pallas_call(
        functools.partial(_route_kernel, n_tiles),
        grid=(2, n_tiles // nb),
        in_specs=[pl.BlockSpec((nb, ROUTE_ROWS, tm), lambda p, i: (i, 0, 0)),
                  _const_spec(expert_tri.shape), _const_spec(token_tri.shape)],
        out_specs=[pl.BlockSpec((nb, 8, tm), lambda p, i: (i * p, 0, 0)),
                   pl.BlockSpec((nb, 8, tm), lambda p, i: (i * p, 0, 0)),
                   pl.BlockSpec((8, PLAN_ITEMS), lambda p, i: (0, 0))],
        out_shape=[jax.ShapeDtypeStruct((n_tiles, 8, tm), jnp.int32),
                   jax.ShapeDtypeStruct((n_tiles, 8, tm), F32),
                   jax.ShapeDtypeStruct((8, PLAN_ITEMS), jnp.int32)],
        scratch_shapes=[pltpu.VMEM((n_tiles, 8, tm), F32),
                        pltpu.VMEM((N_EXPERTS, tm), F32),
                        pltpu.VMEM((N_EXPERTS, tm), F32),
                        pltpu.VMEM((N_EXPERTS, tm), F32)],
        compiler_params=_params(("arbitrary", "arbitrary")),
        name="route",
    )(lg, expert_tri, token_tri)


ROW_UNROLL = 8


DISPATCH_BLOCKS = 4


def _dispatch_kernel(tm, pos_ref, h_ref, xs_ref, sem):
    def row_copy(t, dst):
        return pltpu.make_async_copy(h_ref.at[pl.ds(t, 1)], xs_ref.at[pl.ds(dst, 1)], sem)

    for blk in range(DISPATCH_BLOCKS):
        def issue(g, carry, blk=blk):
            for u in range(ROW_UNROLL):
                t = g * ROW_UNROLL + u
                row_copy(blk * tm + t, pos_ref[blk, 0, t]).start()
                row_copy(blk * tm + t, pos_ref[blk, 1, t]).start()
            return carry

        lax.fori_loop(0, tm // ROW_UNROLL, issue, 0)

    def drain(g, carry):
        for _ in range(2 * ROW_UNROLL):
            row_copy(0, 0).wait()
        return carry

    lax.fori_loop(0, DISPATCH_BLOCKS * tm // ROW_UNROLL, drain, 0)


def _dispatch(pos, h2, n_rows):
    n_tiles, _, tm = pos.shape
    d = h2.shape[1]
    nb = DISPATCH_BLOCKS
    assert n_tiles % nb == 0
    return pl.pallas_call(
        functools.partial(_dispatch_kernel, tm),
        grid=(n_tiles // nb,),
        in_specs=[pl.BlockSpec((nb, 8, tm), lambda i: (i, 0, 0), memory_space=pltpu.SMEM),
                  pl.BlockSpec((nb * tm, d), lambda i: (i, 0))],
        out_specs=pl.BlockSpec(memory_space=pl.ANY),
        out_shape=jax.ShapeDtypeStruct((n_rows, d), F32),
        scratch_shapes=[pltpu.SemaphoreType.DMA],
        compiler_params=_params(("arbitrary",)),
        name="dispatch",
    )(pos, h2)


def _ffn_kernel(owner_ref, tile_ref, lo_ref, hi_ref, used_ref, x_ref, w1_ref, w3_ref, w2_ref, y_ref,
                w1b_ref, w3b_ref, w2b_ref):
    w = pl.program_id(0)
    prev = jnp.maximum(w - 1, 0)

    @pl.when((w == 0) | (owner_ref[w] != owner_ref[prev]))
    def _():
        w1b_ref[...] = w1_ref[...].astype(BF16)
        w3b_ref[...] = w3_ref[...].astype(BF16)
        w2b_ref[...] = w2_ref[...].astype(BF16)

    @pl.when(w < used_ref[0])
    def _():
        x = x_ref[...].astype(BF16)
        a = _dot(x, w1b_ref[...])
        b = _dot(x, w3b_ref[...])
        hid = (a * _sigmoid(a)) * b
        y = _dot(hid.astype(BF16), w2b_ref[...])
        row = lax.broadcasted_iota(jnp.int32, y.shape, 0)
        mine = (row >= lo_ref[w]) & (row < hi_ref[w])
        first = (w == 0) | (tile_ref[w] != tile_ref[jnp.maximum(w - 1, 0)])

        @pl.when(first)
        def _():
            y_ref[...] = jnp.where(mine, y, 0.0)

        @pl.when(jnp.logical_not(first))
        def _():
            y_ref[...] = jnp.where(mine, y, y_ref[...])


def _ffn(plan, xs, w1, w3, w2):
    n_rows, d = xs.shape
    tf = FFN_TILE
    de = w1.shape[1] // N_EXPERTS
    n_items = n_rows // tf + N_EXPERTS - 1
    assert n_items <= PLAN_ITEMS

    def item(w, used):
        return jnp.minimum(w, used[0] - 1)

    grid_spec = pltpu.PrefetchScalarGridSpec(
        num_scalar_prefetch=5,
        grid=(n_items,),
        in_specs=[pl.BlockSpec((tf, d), lambda w, own, til, lo, hi, used: (til[item(w, used)], 0)),
                  pl.BlockSpec((d, de), lambda w, own, til, lo, hi, used: (0, own[item(w, used)])),
                  pl.BlockSpec((d, de), lambda w, own, til, lo, hi, used: (0, own[item(w, used)])),
                  pl.BlockSpec((de, d), lambda w, own, til, lo, hi, used: (own[item(w, used)], 0))],
        out_specs=pl.BlockSpec((tf, d), lambda w, own, til, lo, hi, used: (til[item(w, used)], 0)),
        scratch_shapes=[pltpu.VMEM((d, de), BF16), pltpu.VMEM((d, de), BF16), pltpu.VMEM((de, d), BF16)],
    )
    return pl.pallas_call(
        _ffn_kernel,
        grid_spec=grid_spec,
        out_shape=jax.ShapeDtypeStruct((n_rows, d), F32),
        compiler_params=_params(("arbitrary",)),
        name="ffn",
    )(plan[0], plan[1], plan[2], plan[3], plan[4, 0:1], xs, w1, w3, w2)


def _combine_kernel(tm, ctx_steps, pos_ref, nxt_ref, w_ref, x1_ref, mod_ref, gp_ref, ys_ref, yc_ref, yl_ref,
                    buf_ref, sem):
    i = pl.program_id(0)
    n = pl.num_programs(0)
    slot = i % 2

    def row_copy(src, k, t, s):
        return pltpu.make_async_copy(ys_ref.at[pl.ds(src, 1)], buf_ref.at[s, k, pl.ds(t, 1)], sem.at[s])

    def issue(p_ref, s):
        def body(g, carry):
            for u in range(ROW_UNROLL):
                t = g * ROW_UNROLL + u
                row_copy(p_ref[0, t], 0, t, s).start()
                row_copy(p_ref[1, t], 1, t, s).start()
            return carry
        lax.fori_loop(0, tm // ROW_UNROLL, body, 0)

    @pl.when(i == 0)
    def _():
        issue(pos_ref, 0)

    @pl.when(i + 1 < n)
    def _():
        issue(nxt_ref, 1 - slot)

    def drain(g, carry):
        for _ in range(2 * ROW_UNROLL):
            row_copy(0, 0, 0, slot).wait()
        return carry

    lax.fori_loop(0, tm // ROW_UNROLL, drain, 0)

    r = lax.broadcasted_iota(jnp.int32, (tm, tm), 0)
    c = lax.broadcasted_iota(jnp.int32, (tm, tm), 1)
    d = x1_ref.shape[1]
    wa = jnp.broadcast_to(jnp.sum(jnp.where(r == c, w_ref[2:3, :], 0.0), axis=1, keepdims=True), (tm, d))
    wb = jnp.broadcast_to(jnp.sum(jnp.where(r == c, w_ref[3:4, :], 0.0), axis=1, keepdims=True), (tm, d))
    moe = wa * buf_ref[slot, 0] + wb * buf_ref[slot, 1]
    gt2 = mod_ref[5:6, :]
    y = x1_ref[...] + gt2 * _rms(moe, gp_ref[...])

    @pl.when(i < ctx_steps)
    def _():
        yc_ref[...] = y

    @pl.when(i >= ctx_steps)
    def _():
        yl_ref[...] = y


def _combine(st, pos, w, x1, mod, g_post2, ys):
    n_tiles, _, tm = pos.shape
    d = x1.shape[1]
    row = pl.BlockSpec((tm, d), lambda i: (i, 0))
    return pl.pallas_call(
        functools.partial(_combine_kernel, tm, st.ctx_steps(tm)),
        grid=(n_tiles,),
        in_specs=[pl.BlockSpec((None, 8, tm), lambda i: (i, 0, 0), memory_space=pltpu.SMEM),
                  pl.BlockSpec((None, 8, tm), lambda i: (jnp.minimum(i + 1, n_tiles - 1), 0, 0),
                               memory_space=pltpu.SMEM),
                  pl.BlockSpec((None, 8, tm), lambda i: (i, 0, 0)),
                  row,
                  st.mod_spec(tm, d),
                  _const_spec((1, d)),
                  pl.BlockSpec(memory_space=pl.ANY)],
        out_specs=[st.ctx_spec(tm, d), st.lat_spec(tm, d)],
        out_shape=[jax.ShapeDtypeStruct((st.n_ctx, d), F32), jax.ShapeDtypeStruct((st.n_lat, d), F32)],
        scratch_shapes=[pltpu.VMEM((2, 2, tm, d), F32), pltpu.SemaphoreType.DMA((2,))],
        compiler_params=_params(("arbitrary",)),
        name="combine",
    )(pos, pos, w, x1, mod, g_post2.reshape(1, d), ys)


def _moe(st, h2, lg, x1, mod, g_post2, w1, w3, w2, route_tri):
    n_tok = x1.shape[0]
    n_rows = 2 * n_tok
    assert n_rows % FFN_TILE == 0
    pos, w, plan = _route(lg, route_tri)
    xs = _dispatch(pos, h2, n_rows)
    ys = _ffn(plan, xs, w1, w3, w2)
    return _combine(st, pos, w, x1, mod, g_post2, ys)


def _grid_pos(n_tokens, d_model):
    rows = n_tokens // GRID_W
    row = np.repeat(np.arange(rows, dtype=np.float64), GRID_W)
    col = np.tile(np.arange(GRID_W, dtype=np.float64), rows)
    n_freq = d_model // 4
    freq = np.exp(-math.log(POS_BASE) * np.arange(n_freq, dtype=np.float64) / n_freq)

    def enc(p):
        a = p[:, None] * freq[None, :]
        return np.concatenate([np.sin(a), np.cos(a)], axis=-1)

    return jnp.asarray(np.concatenate([enc(row), enc(col)], axis=-1), dtype=F32)


def _dft_cos_sin(n):
    k = np.arange(n, dtype=np.int64)
    ang = 2.0 * np.pi * ((k[:, None] * k[None, :]) % n).astype(np.float64) / n
    return np.cos(ang), np.sin(ang)


def _fnet_consts(seq_len):
    ct, st = _dft_cos_sin(seq_len)
    f_mat = jnp.asarray(np.concatenate([ct, -st], axis=1), dtype=F32).astype(BF16)
    cc, sc = _dft_cos_sin(FGROUP_DIM)
    cs_mat = jnp.asarray(np.stack([cc, sc]), dtype=F32).astype(BF16)
    return f_mat, cs_mat


def _tri_consts():
    i = np.arange(CHUNK)
    prefix = (i[:, None] <= i[None, :]).astype(np.float32)
    suffix = (i[:, None] >= i[None, :]).astype(np.float32)
    tri = jnp.asarray(np.stack([prefix, suffix]), dtype=BF16)
    e = np.arange(N_EXPERTS)
    expert_tri = jnp.asarray((e[None, :] < e[:, None]).astype(np.float32), dtype=BF16)
    t = np.arange(TOKEN_TILE)
    token_tri = jnp.asarray((t[:, None] < t[None, :]).astype(np.float32), dtype=BF16)
    return tri, (expert_tri, token_tri)


def _layer(x_prompt, x_sample, pos, mod, states, lw, consts):
    bp, sp, d = x_prompt.shape
    bs, ss, _ = x_sample.shape
    st = _Streams(n_ctx=bp * sp, n_lat=bs * ss, seq_lat=ss)
    (g_pre1, proj_w, g_hn, post_w, g_post1, g_pre2, w_e1, w_e3, w_e2, g_post2) = lw
    tri, route_tris = consts
    x_ctx = x_prompt.reshape(st.n_ctx, d)
    x_lat = x_sample.reshape(st.n_lat, d)
    q, kt, v, o, u, ga, gb, gr = _proj(st, x_ctx, x_lat, pos, mod, g_pre1, proj_w + (tri,))
    ha_ctx, new_c, new_n, new_m = _mlstm(q, kt, v, o, g_hn, gr, bp, sp, 0, None, MLSTM_HEADS_PER_STEP)
    (ha_lat,) = _mlstm(q, kt, v, o, g_hn, gr, bs, ss, st.n_ctx, states, MLSTM_HEADS_PER_STEP)
    yf_ctx = _fnet(u, *_fnet_consts(sp), bp, sp, 0)
    yf_lat = _fnet(u, *_fnet_consts(ss), bs, ss, st.n_ctx)
    x1, h2, logits = _post(st, x_ctx, x_lat, pos, ha_ctx, ha_lat, yf_ctx, yf_lat, ga, gb, mod, g_post1, g_pre2, post_w)
    y_ctx, y_lat = _moe(st, h2, logits, x1, mod, g_post2, w_e1, w_e3, w_e2, route_tris)
    return y_ctx.reshape(bp, sp, d), y_lat.reshape(bs, ss, d), (new_c, new_n, new_m)


def kernel(x_prompt, x_sample, c, state_C, state_n, state_m, c_ctx, w_ada, b_ada, g_pre1, w_in, b_gates, g_hn,
           w_a, w_f, w_out, g_post1, g_pre2, w_rg, b_rg, w_re, b_re, w_e1, w_e3, w_e2, g_post2):
    bp, sp, d = x_prompt.shape
    bs, ss, _ = x_sample.shape
    depth = w_in.shape[0]
    assert depth == 1
    l = 0
    dm = N_HEADS * HEAD_DIM
    du = N_FGROUPS * FGROUP_DIM
    ng = 4 * N_HEADS

    n_rows = 8
    assert 1 + bs <= n_rows
    cs = jnp.concatenate([c_ctx[None, :], c, jnp.zeros((n_rows - 1 - bs, d), F32)], axis=0)
    mod = _ada(cs, w_ada[l], b_ada[l]).reshape(n_rows, 6, d)

    wi = w_in[l]
    cuts = np.cumsum([dm] * 4 + [ng, du, d]).tolist()
    wq = wi[:, :cuts[0]].astype(BF16)
    wkt = wi[:, cuts[0]:cuts[1]].T.astype(BF16)
    wv = wi[:, cuts[1]:cuts[2]].astype(BF16)
    wo = wi[:, cuts[2]:cuts[3]].astype(BF16)
    wgt = wi[:, cuts[3]:cuts[4]].T
    wu = wi[:, cuts[4]:cuts[5]].astype(BF16)
    wga = wi[:, cuts[5]:cuts[6]].astype(BF16)
    wgb = wi[:, cuts[6]:].astype(BF16)
    bg = b_gates[l].reshape(ng, 1)
    proj_w = (wq, wkt, wv, wo, wu, wga, wgb, wgt, bg)
    n_pad = ROUTE_ROWS - N_EXPERTS - N_GROUPS
    wr = jnp.concatenate([w_re[l], w_rg[l], jnp.zeros((d, n_pad), F32)], axis=1).T
    br = jnp.concatenate([b_re[l], b_rg[l], jnp.zeros((n_pad,), F32)]).reshape(ROUTE_ROWS, 1)
    post_w = (w_a[l].astype(BF16), w_f[l].astype(BF16), w_out[l].astype(BF16), wr, br)
    lw = (g_pre1[l], proj_w, g_hn[l], post_w, g_post1[l], g_pre2[l], w_e1[l], w_e3[l], w_e2[l], g_post2[l])
    consts = _tri_consts()

    pos = _grid_pos(ss, d)
    y_prompt, y_sample, (new_c, new_n, new_m) = _layer(x_prompt, x_sample, pos, mod,
                                                       (state_C, state_n, state_m), lw, consts)

    new_state_c = new_c
    new_state_n = new_n.reshape(bp, depth, 2, N_HEADS, HEAD_DIM)
    new_state_m = new_m[..., 0, 0]
    return (y_prompt, y_sample, new_state_c, new_state_n, new_state_m)
```

```python
import functools
import math
from typing import NamedTuple

import numpy as np
import jax
import jax.numpy as jnp
from jax import lax
from jax.experimental import pallas as pl
from jax.experimental.pallas import tpu as pltpu

F32 = jnp.float32
BF16 = jnp.bfloat16

N_HEADS = 8
HEAD_DIM = 128
N_FGROUPS = 4
FGROUP_DIM = 128
N_GROUPS = 4
EXPERTS_PER_GROUP = 8
N_EXPERTS = N_GROUPS * EXPERTS_PER_GROUP
GRID_W = 64
POS_BASE = 10000.0
EPS = 1e-6

V7X_VMEM_LIMIT_BYTES = 56 * 1024 * 1024
LANES = 128
CHUNK = 256
TOKEN_TILE = CHUNK
CHUNKS_PER_STEP = 2
MLSTM_HEADS_PER_STEP = 4


def _params(sem):
    return pltpu.CompilerParams(dimension_semantics=sem, vmem_limit_bytes=V7X_VMEM_LIMIT_BYTES)


def _const_spec(shape):
    n = len(shape)
    return pl.BlockSpec(shape, lambda *_: (0,) * n, pipeline_mode=pl.Buffered(1))


def _split2(x):
    hi = x.astype(BF16)
    lo = (x - hi.astype(F32)).astype(BF16)
    return hi, lo


def _split3(x):
    hi = x.astype(BF16)
    r = x - hi.astype(F32)
    mid = r.astype(BF16)
    lo = (r - mid.astype(F32)).astype(BF16)
    return hi, mid, lo


def _dot(a, b):
    return jnp.dot(a, b, preferred_element_type=F32)


def _dot_nt(a, b):
    return lax.dot_general(a, b, (((1,), (1,)), ((), ())), preferred_element_type=F32)


def _dot_f32(a, b, nt=False):
    d = _dot_nt if nt else _dot
    a1, a2 = _split2(a)
    b1, b2 = _split2(b)
    return d(a1, b1) + (d(a1, b2) + d(a2, b1))


def _dot_exact_rhs(a, b_bf16, nt=False):
    d = _dot_nt if nt else _dot
    a1, a2, a3 = _split3(a)
    return d(a1, b_bf16) + (d(a2, b_bf16) + d(a3, b_bf16))


def _rms(x, g):
    return x * lax.rsqrt(jnp.mean(x * x, axis=-1, keepdims=True) + EPS) * g


def _sigmoid(x):
    return 1.0 / (1.0 + jnp.exp(-x))


def _log_sigmoid(x):
    return jnp.minimum(x, 0.0) - jnp.log(1.0 + jnp.exp(-jnp.abs(x)))


def _ada_kernel(c_ref, w_ref, b_ref, o_ref):
    c = c_ref[...]
    s = (c * _sigmoid(c)).astype(BF16)
    o_ref[...] = _dot(s, w_ref[...].astype(BF16)) + b_ref[...]


def _ada(cs, w_ada, b_ada):
    rows, d = cs.shape
    n = w_ada.shape[1]
    tn = 1024
    return pl.pallas_call(
        _ada_kernel,
        grid=(n // tn,),
        in_specs=[pl.BlockSpec((rows, d), lambda j: (0, 0)),
                  pl.BlockSpec((d, tn), lambda j: (0, j)),
                  pl.BlockSpec((1, tn), lambda j: (0, j))],
        out_specs=pl.BlockSpec((rows, tn), lambda j: (0, j)),
        out_shape=jax.ShapeDtypeStruct((rows, n), F32),
        compiler_params=_params(("arbitrary",)),
        name="ada",
    )(cs, w_ada, b_ada.reshape(1, n))


def _proj_kernel(ctx_steps, xc_ref, xl_ref, pos_ref, mod_ref, g_ref, wq_ref, wkt_ref, wv_ref, wo_ref, wu_ref,
                 wga_ref, wgb_ref, wgt_ref, bg_ref, tri_ref,
                 q_ref, kt_ref, v_ref, o_ref, u_ref, ga_ref, gb_ref, gr_ref):
    x = jnp.where(pl.program_id(0) < ctx_steps, xc_ref[...], xl_ref[...] + pos_ref[...])
    sh = mod_ref[0:1, :]
    sc = mod_ref[1:2, :]
    h = _rms(x, g_ref[...]) * (1.0 + sc) + sh
    hb = h.astype(BF16)
    q_ref[...] = _dot(hb, wq_ref[...]).astype(BF16)
    kt = (_dot_nt(wkt_ref[...], hb) * (HEAD_DIM ** -0.5)).astype(BF16)
    tc = TOKEN_TILE
    for s in range(CHUNKS_PER_STEP):
        kt_ref[s] = kt[:, s * tc:(s + 1) * tc]
    v_ref[...] = _dot(hb, wv_ref[...]).astype(BF16)
    o_ref[...] = _dot(hb, wo_ref[...])
    u_ref[...] = _dot(hb, wu_ref[...]).astype(BF16)
    ga_ref[...] = _dot(hb, wga_ref[...])
    gb_ref[...] = _dot(hb, wgb_ref[...])
    g = _dot_f32(wgt_ref[...], h, nt=True) + bg_ref[...]
    nh = N_HEADS
    lf_f = _log_sigmoid(g[nh:2 * nh])
    lf_b = _log_sigmoid(g[3 * nh:4 * nh])
    for s in range(CHUNKS_PER_STEP):
        cols = slice(s * tc, (s + 1) * tc)
        gr_ref[s, 0:nh, :] = g[0:nh, cols]
        gr_ref[s, nh:2 * nh, :] = _dot_exact_rhs(lf_f[:, cols], tri_ref[0])
        gr_ref[s, 2 * nh:3 * nh, :] = g[2 * nh:3 * nh, cols]
        gr_ref[s, 3 * nh:4 * nh, :] = _dot_exact_rhs(lf_b[:, cols], tri_ref[1])


class _Streams(NamedTuple):
    n_ctx: int
    n_lat: int
    seq_lat: int

    def ctx_steps(self, tm):
        assert self.n_ctx % tm == 0 and self.seq_lat % tm == 0
        return self.n_ctx // tm

    def ctx_spec(self, tm, width):
        c = self.ctx_steps(tm)
        return pl.BlockSpec((tm, width), lambda i, *_: (jnp.minimum(i, c - 1), 0))

    def lat_spec(self, tm, width):
        c = self.ctx_steps(tm)
        return pl.BlockSpec((tm, width), lambda i, *_: (jnp.maximum(i - c, 0), 0))

    def pos_spec(self, tm, width):
        c = self.ctx_steps(tm)
        per_seq = self.seq_lat // tm
        return pl.BlockSpec((tm, width), lambda i, *_: (jnp.maximum(i - c, 0) % per_seq, 0))

    def mod_spec(self, tm, width):
        c = self.ctx_steps(tm)
        per_seq = self.seq_lat // tm
        return pl.BlockSpec((None, 6, width),
                            lambda i, *_: (jnp.where(i < c, 0, 1 + jnp.maximum(i - c, 0) // per_seq), 0, 0))


def _proj(st, x_ctx, x_lat, pos, mod, g_pre1, wts):
    d = x_ctx.shape[1]
    n_tok = st.n_ctx + st.n_lat
    tc = TOKEN_TILE
    ns = CHUNKS_PER_STEP
    tm = ns * tc
    n_tiles = n_tok // tc
    (wq, wkt, wv, wo, wu, wga, wgb, wgt, bg, tri) = wts
    row = pl.BlockSpec((tm, d), lambda i: (i, 0))
    in_specs = [st.ctx_spec(tm, d), st.lat_spec(tm, d), st.pos_spec(tm, d), st.mod_spec(tm, d), _const_spec((1, d))]
    in_specs += [_const_spec(w.shape) for w in wts]
    args = [x_ctx, x_lat, pos, mod, g_pre1.reshape(1, d)] + list(wts)
    n_gate_rows = 4 * N_HEADS
    du = wu.shape[1]
    out_shape = [jax.ShapeDtypeStruct((n_tok, d), BF16),
                 jax.ShapeDtypeStruct((n_tiles, d, tc), BF16),
                 jax.ShapeDtypeStruct((n_tok, d), BF16),
                 jax.ShapeDtypeStruct((n_tok, d), F32),
                 jax.ShapeDtypeStruct((n_tok, du), BF16),
                 jax.ShapeDtypeStruct((n_tok, d), F32),
                 jax.ShapeDtypeStruct((n_tok, d), F32),
                 jax.ShapeDtypeStruct((n_tiles, n_gate_rows, tc), F32)]
    out_specs = [row,
                 pl.BlockSpec((ns, d, tc), lambda i: (i, 0, 0)),
                 row, row,
                 pl.BlockSpec((tm, du), lambda i: (i, 0)),
                 row, row,
                 pl.BlockSpec((ns, n_gate_rows, tc), lambda i: (i, 0, 0))]
    return pl.pallas_call(
        functools.partial(_proj_kernel, st.ctx_steps(tm)),
        grid=(n_tok // tm,),
        in_specs=in_specs,
        out_specs=out_specs,
        out_shape=out_shape,
        compiler_params=_params(("parallel",)),
        name="proj",
    )(*args)


def _mlstm_kernel(n_chunks, hp, has_state, emit_state, *refs):
    it = iter(refs)
    if has_state:
        m0_ref = next(it)
    q_ref, kt_ref, v_ref, o_ref, ghn_ref, gr_ref = (next(it) for _ in range(6))
    if has_state:
        c0_ref, n0_ref = next(it), next(it)
    ha_ref = next(it)
    if emit_state:
        cout_ref, nout_ref, mout_ref = next(it), next(it), next(it)
    hs_ref, cn_ref, m_ref = (next(it) for _ in range(3))

    L = CHUNK
    nh, dh = N_HEADS, HEAD_DIM
    head0 = pl.program_id(1) * hp
    neg_inf = -jnp.inf

    sq_r = lax.broadcasted_iota(jnp.int32, (dh, dh), 0)
    sq_c = lax.broadcasted_iota(jnp.int32, (dh, dh), 1)
    for j in range(hp):
        for d in range(2):
            if has_state:
                n_col = jnp.sum(jnp.where(sq_r == sq_c, n0_ref[d, j], 0.0), axis=1, keepdims=True)
                cn_ref[j, d, :, 0:dh] = c0_ref[d, j]
                cn_ref[j, d, :, dh:2 * dh] = jnp.broadcast_to(n_col, (dh, dh))
                m0 = m0_ref[pl.program_id(0) * (2 * nh) + d * nh + head0 + j]
                m_ref[j, d] = jnp.full((8, LANES), m0, F32)
            else:
                cn_ref[j, d] = jnp.zeros((dh, 2 * dh), F32)
                m_ref[j, d] = jnp.zeros((8, LANES), F32)

    row_id = lax.broadcasted_iota(jnp.int32, (L, L), 0)
    col_id = lax.broadcasted_iota(jnp.int32, (L, L), 1)
    ones_blk = jnp.ones((L, dh), BF16)

    def chunk(j, d, c):
        t0 = pl.multiple_of(c * L, L)
        lanes = slice(j * dh, (j + 1) * dh)
        qc = q_ref[pl.ds(t0, L), lanes]
        ktc = kt_ref[c, lanes, :]
        vc = v_ref[pl.ds(t0, L), lanes]
        ig_row = gr_ref[c, pl.ds(2 * nh * d + head0 + j, 1), :]
        b_row = gr_ref[c, pl.ds(2 * nh * d + nh + head0 + j, 1), :]
        m_prev = m_ref[j, d][0:1, 0:1]
        if d == 0:
            b_end = b_row[:, L - 1:L]
            mask = col_id <= row_id
        else:
            b_end = b_row[:, 0:1]
            mask = col_id >= row_id
        a_row = ig_row - b_row
        a_max = jnp.max(jnp.where(mask, a_row, neg_inf), axis=1, keepdims=True)
        g = jnp.maximum(jnp.broadcast_to(a_max, (L, dh)), m_prev)
        b_col = jnp.broadcast_to(
            jnp.sum(jnp.where(row_id == col_id, b_row, 0.0), axis=1, keepdims=True), (L, dh))
        g_full = jnp.concatenate([g] * (L // dh), axis=1)
        s = _dot(qc, ktc) * jnp.exp(jnp.where(mask, a_row - g_full, neg_inf))
        decay = jnp.exp(m_prev - g)
        qcn = _dot(qc, cn_ref[j, d].astype(BF16))
        num = decay * qcn[:, 0:dh] + _dot(s.astype(BF16), vc)
        den = decay * qcn[:, dh:2 * dh] + jnp.sum(s, axis=1, keepdims=True)
        hch = num / jnp.maximum(jnp.abs(den), jnp.exp(-(b_col + g)))
        w_log = b_end + a_row
        m_new = jnp.maximum(b_end + m_prev, jnp.max(w_log, axis=1, keepdims=True))
        kw = (ktc.astype(F32) * jnp.exp(w_log - m_new)).astype(BF16)
        v_aug = jnp.concatenate([vc, ones_blk], axis=1)
        cn_ref[j, d] = jnp.exp(b_end + m_prev - m_new) * cn_ref[j, d] + _dot(kw, v_aug)
        m_ref[j, d] = jnp.broadcast_to(m_new, (8, LANES))
        return t0, hch

    if n_chunks == 1:
        for j in range(hp):
            _, h_f = chunk(j, 0, 0)
            _, h_b = chunk(j, 1, 0)
            hs_ref[:, j * dh:(j + 1) * dh] = h_f + h_b
    else:
        half = n_chunks // 2

        def first(i, carry):
            for j in range(hp):
                lanes = slice(j * dh, (j + 1) * dh)
                t_f, h_f = chunk(j, 0, i)
                t_b, h_b = chunk(j, 1, n_chunks - 1 - i)
                hs_ref[pl.ds(t_f, L), lanes] = h_f
                hs_ref[pl.ds(t_b, L), lanes] = h_b
            return carry

        def second(i, carry):
            for j in range(hp):
                lanes = slice(j * dh, (j + 1) * dh)
                t_f, h_f = chunk(j, 0, i)
                t_b, h_b = chunk(j, 1, n_chunks - 1 - i)
                hs_ref[pl.ds(t_f, L), lanes] += h_f
                hs_ref[pl.ds(t_b, L), lanes] += h_b
            return carry

        lax.fori_loop(0, half, first, 0)
        lax.fori_loop(half, n_chunks, second, 0)

    def finish(c, carry):
        t0 = pl.multiple_of(c * L, L)
        for j in range(hp):
            lanes = slice(j * dh, (j + 1) * dh)
            ha = hs_ref[pl.ds(t0, L), lanes]
            ha = ha * lax.rsqrt(jnp.mean(ha * ha, axis=-1, keepdims=True) + EPS)
            ha = ha * ghn_ref[:, lanes] * _sigmoid(o_ref[pl.ds(t0, L), lanes])
            ha_ref[pl.ds(t0, L), lanes] = ha.astype(BF16)
        return carry

    if n_chunks == 1:
        finish(0, 0)
    else:
        lax.fori_loop(0, n_chunks, finish, 0)

    if emit_state:
        for j in range(hp):
            for d in range(2):
                cout_ref[d, j] = cn_ref[j, d, :, 0:dh]
                n_rep = cn_ref[j, d, :, dh:2 * dh]
                nout_ref[d, j] = jnp.sum(jnp.where(sq_r == sq_c, n_rep, 0.0), axis=0, keepdims=True)
                mout_ref[d, j] = m_ref[j, d][0:1, :]


def _mlstm(q, kt, v, o, g_hn, gr, n_seq, seq_len, tok_off, states, hp):
    d = q.shape[1]
    n_tok = n_seq * seq_len
    L = CHUNK
    assert tok_off % seq_len == 0
    sb = tok_off // seq_len
    n_chunks = seq_len // L
    has_state = states is not None
    emit_state = not has_state
    nh, dh = N_HEADS, HEAD_DIM
    assert n_chunks == 1 or n_chunks % 2 == 0
    assert nh % hp == 0
    wd = hp * dh

    in_specs = []
    args = []
    if has_state:
        state_c, state_n, state_m = states
        in_specs.append(pl.BlockSpec(memory_space=pltpu.SMEM))
        args.append(state_m.reshape(-1))
    tok = pl.BlockSpec((seq_len, wd), lambda b, h: (b + sb, h))
    in_specs += [tok,
                 pl.BlockSpec((n_chunks, wd, L), lambda b, h: (b + sb, h, 0)),
                 tok, tok,
                 pl.BlockSpec((1, wd), lambda b, h: (0, h)),
                 pl.BlockSpec((n_chunks, 4 * nh, L), lambda b, h: (b + sb, 0, 0))]
    args += [q, kt, v, o, g_hn.reshape(1, d), gr]
    st_c = pl.BlockSpec((None, None, 2, hp, dh, dh), lambda b, h: (b, 0, 0, h, 0, 0))
    st_v = pl.BlockSpec((None, None, 2, hp, 1, dh), lambda b, h: (b, 0, 0, h, 0, 0))
    if has_state:
        in_specs += [st_c, st_v]
        args += [state_c, state_n.reshape(n_seq, 1, 2, nh, 1, dh)]
    out_shape = [jax.ShapeDtypeStruct((n_tok, d), BF16)]
    out_specs = [pl.BlockSpec((seq_len, wd), lambda b, h: (b, h))]
    if emit_state:
        out_shape += [jax.ShapeDtypeStruct((n_seq, 1, 2, nh, dh, dh), F32),
                      jax.ShapeDtypeStruct((n_seq, 1, 2, nh, 1, dh), F32),
                      jax.ShapeDtypeStruct((n_seq, 1, 2, nh, 1, LANES), F32)]
        out_specs += [st_c, st_v, st_v]
    scratch = [pltpu.VMEM((seq_len, wd), F32),
               pltpu.VMEM((hp, 2, dh, 2 * dh), F32),
               pltpu.VMEM((hp, 2, 8, LANES), F32)]
    return pl.pallas_call(
        functools.partial(_mlstm_kernel, n_chunks, hp, has_state, emit_state),
        grid=(n_seq, nh // hp),
        in_specs=in_specs,
        out_specs=out_specs,
        out_shape=out_shape,
        scratch_shapes=scratch,
        compiler_params=_params(("parallel", "parallel")),
        name="mlstm",
    )(*args)


def _fnet_kernel(seq_len, u_ref, f_ref, cs_ref, y_ref, ab_ref):
    T = seq_len

    @pl.when(pl.program_id(1) == 0)
    def _():
        for g in range(N_FGROUPS):
            ug = u_ref[:, g * FGROUP_DIM:(g + 1) * FGROUP_DIM]
            ab_ref[0:T, g * FGROUP_DIM:(g + 1) * FGROUP_DIM] = _dot(ug, cs_ref[0]).astype(BF16)
            ab_ref[T:2 * T, g * FGROUP_DIM:(g + 1) * FGROUP_DIM] = _dot(ug, cs_ref[1]).astype(BF16)

    scale = 1.0 / math.sqrt(T * FGROUP_DIM)
    y_ref[...] = (_dot(f_ref[...], ab_ref[...]) * scale).astype(BF16)


def _fnet(u, f_mat, cs_mat, n_seq, seq_len, tok_off):
    du = u.shape[1]
    n_tok = n_seq * seq_len
    tr = min(seq_len, 512)
    assert tok_off % seq_len == 0
    sb = tok_off // seq_len
    return pl.pallas_call(
        functools.partial(_fnet_kernel, seq_len),
        grid=(n_seq, seq_len // tr),
        in_specs=[pl.BlockSpec((seq_len, du), lambda b, r: (b + sb, 0)),
                  pl.BlockSpec((tr, 2 * seq_len), lambda b, r: (r, 0)),
                  _const_spec(cs_mat.shape)],
        out_specs=pl.BlockSpec((tr, du), lambda b, r: (b * (seq_len // tr) + r, 0)),
        out_shape=jax.ShapeDtypeStruct((n_tok, du), BF16),
        scratch_shapes=[pltpu.VMEM((2 * seq_len, du), BF16)],
        compiler_params=_params(("parallel", "arbitrary")),
        name="fnet",
    )(u, f_mat, cs_mat)


def _post_kernel(ctx_steps, xc_ref, xl_ref, pos_ref, hac_ref, hal_ref, yfc_ref, yfl_ref, ga_ref, gb_ref, mod_ref,
                 gp1_ref, gp2_ref, wa_ref, wf_ref, wout_ref, wr_ref, br_ref, x1_ref, h2_ref, lg_ref):
    is_ctx = pl.program_id(0) < ctx_steps
    x = jnp.where(is_ctx, xc_ref[...], xl_ref[...] + pos_ref[...])
    ya = _dot(jnp.where(is_ctx, hac_ref[...], hal_ref[...]), wa_ref[...])
    yf = _dot(jnp.where(is_ctx, yfc_ref[...], yfl_ref[...]), wf_ref[...])
    mix_in = _sigmoid(ga_ref[...]) * ya + _sigmoid(gb_ref[...]) * yf
    mix = _dot(mix_in.astype(BF16), wout_ref[...])
    gt1 = mod_ref[2:3, :]
    sh2 = mod_ref[3:4, :]
    sc2 = mod_ref[4:5, :]
    x1 = x + gt1 * _rms(mix, gp1_ref[...])
    h2 = _rms(x1, gp2_ref[...]) * (1.0 + sc2) + sh2
    x1_ref[...] = x1
    h2_ref[...] = h2
    lg = _dot_nt(wr_ref[...].astype(BF16), h2.astype(BF16)) + br_ref[...]
    for s in range(CHUNKS_PER_STEP):
        lg_ref[s] = lg[:, s * TOKEN_TILE:(s + 1) * TOKEN_TILE]


def _post(st, x_ctx, x_lat, pos, ha_ctx, ha_lat, yf_ctx, yf_lat, ga, gb, mod, g_post1, g_pre2, wts):
    d = x_ctx.shape[1]
    n_tok = st.n_ctx + st.n_lat
    tc = TOKEN_TILE
    ns = CHUNKS_PER_STEP
    tm = ns * tc
    (wa, wf, wout, wr, br) = wts
    du = yf_ctx.shape[1]
    row = pl.BlockSpec((tm, d), lambda i: (i, 0))
    in_specs = [st.ctx_spec(tm, d), st.lat_spec(tm, d), st.pos_spec(tm, d),
                st.ctx_spec(tm, d), st.lat_spec(tm, d), st.ctx_spec(tm, du), st.lat_spec(tm, du),
                row, row, st.mod_spec(tm, d), _const_spec((1, d)), _const_spec((1, d))]
    in_specs += [_const_spec(w.shape) for w in wts]
    args = [x_ctx, x_lat, pos, ha_ctx, ha_lat, yf_ctx, yf_lat, ga, gb, mod,
            g_post1.reshape(1, d), g_pre2.reshape(1, d)] + list(wts)
    return pl.pallas_call(
        functools.partial(_post_kernel, st.ctx_steps(tm)),
        grid=(n_tok // tm,),
        in_specs=in_specs,
        out_specs=[row, row, pl.BlockSpec((ns, ROUTE_ROWS, tc), lambda i: (i, 0, 0))],
        out_shape=[jax.ShapeDtypeStruct((n_tok, d), F32),
                   jax.ShapeDtypeStruct((n_tok, d), F32),
                   jax.ShapeDtypeStruct((n_tok // tc, ROUTE_ROWS, tc), F32)],
        compiler_params=_params(("parallel",)),
        name="post",
    )(*args)


ROUTE_ROWS = N_EXPERTS + 8
ROUTE_BLOCKS = 4
FFN_TILE = 512
PLAN_ITEMS = LANES


def _route_rows(lg):
    tm = lg.shape[1]
    ne = N_EXPERTS
    gl = lg[ne:ne + 8]
    g_id = lax.broadcasted_iota(jnp.int32, (8, tm), 0).astype(F32)
    is_g = g_id < N_GROUPS
    gl = jnp.where(is_g, gl, -jnp.inf)
    mx = jnp.max(gl, axis=0, keepdims=True)
    z = jnp.sum(jnp.where(is_g, jnp.exp(gl - mx), 0.0), axis=0, keepdims=True)
    p_sel = 1.0 / z
    g_sel = jnp.min(jnp.where(gl == mx, g_id, float(N_GROUPS)), axis=0, keepdims=True)
    e_id = lax.broadcasted_iota(jnp.int32, (ne, tm), 0).astype(F32)
    lo = EXPERTS_PER_GROUP * g_sel
    in_grp = (e_id >= lo) & (e_id < lo + EXPERTS_PER_GROUP)
    le = jnp.where(in_grp, lg[0:ne], -jnp.inf)
    v1 = jnp.max(le, axis=0, keepdims=True)
    i1 = jnp.min(jnp.where(le == v1, e_id, float(ne)), axis=0, keepdims=True)
    le2 = jnp.where(e_id == i1, -jnp.inf, le)
    v2 = jnp.max(le2, axis=0, keepdims=True)
    i2 = jnp.min(jnp.where(le2 == v2, e_id, float(ne)), axis=0, keepdims=True)
    e2 = jnp.exp(v2 - v1)
    w1 = p_sel / (1.0 + e2)
    w2 = p_sel * e2 / (1.0 + e2)
    return i1, i2, w1, w2


def _route_kernel(n_tiles, lg_ref, etri_ref, ttri_ref, pos_ref, w_ref, plan_ref, meta_ref, cnt_ref, run_ref, base_ref):
    p = pl.program_id(0)
    i = pl.program_id(1)
    tm = lg_ref.shape[2]
    sub = lax.broadcasted_iota(jnp.int32, (N_EXPERTS, tm), 0).astype(F32)
    tf = float(FFN_TILE)

    nb = ROUTE_BLOCKS

    @pl.when(p == 0)
    def _():
        @pl.when(i == 0)
        def _():
            cnt_ref[...] = jnp.zeros_like(cnt_ref)

        count = jnp.zeros((N_EXPERTS, 1), F32)
        for s in range(nb):
            t = i * nb + s
            e_a, e_b, w_a, w_b = _route_rows(lg_ref[s])
            meta_ref[t, 0:1, :] = e_a
            meta_ref[t, 1:2, :] = e_b
            meta_ref[t, 2:3, :] = w_a
            meta_ref[t, 3:4, :] = w_b
            meta_ref[t, 4:8, :] = jnp.zeros((4, tm), F32)
            oh = jnp.where((sub == e_a) | (sub == e_b), 1.0, 0.0)
            count = count + jnp.sum(oh, axis=1, keepdims=True)
        cnt_ref[...] += jnp.broadcast_to(count, cnt_ref.shape)

    @pl.when(p == 1)
    def _():
        @pl.when(i == 0)
        def _():
            c1, c2, c3 = _split3(cnt_ref[...])
            base_ref[...] = _dot(etri_ref[...], c1) + (_dot(etri_ref[...], c2) + _dot(etri_ref[...], c3))
            run_ref[...] = jnp.zeros_like(run_ref)

        start = run_ref[...] + base_ref[...]
        for s in range(nb):
            t = i * nb + s
            e_a = meta_ref[t, 0:1, :]
            e_b = meta_ref[t, 1:2, :]
            oh_a = jnp.where(sub == e_a, 1.0, 0.0)
            oh_b = jnp.where(sub == e_b, 1.0, 0.0)
            oh = oh_a + oh_b
            before = _dot(oh.astype(BF16), ttri_ref[...]) + start
            pos_ref[s] = jnp.zeros(pos_ref.shape[1:], jnp.int32)
            pos_ref[s, 0:1, :] = jnp.sum(oh_a * before, axis=0, keepdims=True).astype(jnp.int32)
            pos_ref[s, 1:2, :] = jnp.sum(oh_b * before, axis=0, keepdims=True).astype(jnp.int32)
            w_ref[s] = meta_ref[t]
            start = start + jnp.broadcast_to(jnp.sum(oh, axis=1, keepdims=True), start.shape)
        run_ref[...] = start - base_ref[...]

        @pl.when(i == n_tiles // nb - 1)
        def _():
            cnt = cnt_ref[:, 0:PLAN_ITEMS]
            base = base_ref[:, 0:PLAN_ITEMS]
            e_id = lax.broadcasted_iota(jnp.int32, (N_EXPERTS, PLAN_ITEMS), 0)
            t_lo = jnp.floor(base / tf)
            t_hi = jnp.floor((base + cnt - 1.0) / tf)
            n_items = jnp.where(cnt > 0.0, t_hi - t_lo + 1.0, 0.0)
            i_start = _dot(etri_ref[...], n_items.astype(BF16))
            item = lax.broadcasted_iota(jnp.int32, (N_EXPERTS, PLAN_ITEMS), 1).astype(F32)
            sel = (i_start <= item) & (item < i_start + n_items)
            tile_e = t_lo + item - i_start
            off = base - tile_e * tf

            def pick(v):
                return jnp.sum(jnp.where(sel, v, 0.0), axis=0, keepdims=True).astype(jnp.int32)

            plan_ref[...] = jnp.zeros(plan_ref.shape, jnp.int32)
            plan_ref[0:1, :] = pick(e_id.astype(F32))
            plan_ref[1:2, :] = pick(tile_e)
            plan_ref[2:3, :] = pick(jnp.maximum(off, 0.0))
            plan_ref[3:4, :] = pick(jnp.minimum(off + cnt, tf))
            plan_ref[4:5, :] = jnp.sum(n_items, axis=0, keepdims=True).astype(jnp.int32)


def _route(lg, tris):
    expert_tri, token_tri = tris
    n_tiles, _, tm = lg.shape
    nb = ROUTE_BLOCKS
    assert n_tiles % nb == 0
    return pl.pallas_call(
        functools.partial(_route_kernel, n_tiles),
        grid=(2, n_tiles // nb),
        in_specs=[pl.BlockSpec((nb, ROUTE_ROWS, tm), lambda p, i: (i, 0, 0)),
                  _const_spec(expert_tri.shape), _const_spec(token_tri.shape)],
        out_specs=[pl.BlockSpec((nb, 8, tm), lambda p, i: (i * p, 0, 0)),
                   pl.BlockSpec((nb, 8, tm), lambda p, i: (i * p, 0, 0)),
                   pl.BlockSpec((8, PLAN_ITEMS), lambda p, i: (0, 0))],
        out_shape=[jax.ShapeDtypeStruct((n_tiles, 8, tm), jnp.int32),
                   jax.ShapeDtypeStruct((n_tiles, 8, tm), F32),
                   jax.ShapeDtypeStruct((8, PLAN_ITEMS), jnp.int32)],
        scratch_shapes=[pltpu.VMEM((n_tiles, 8, tm), F32),
                        pltpu.VMEM((N_EXPERTS, tm), F32),
                        pltpu.VMEM((N_EXPERTS, tm), F32),
                        pltpu.VMEM((N_EXPERTS, tm), F32)],
        compiler_params=_params(("arbitrary", "arbitrary")),
        name="route",
    )(lg, expert_tri, token_tri)


ROW_UNROLL = 8


DISPATCH_BLOCKS = 4


def _dispatch_kernel(tm, pos_ref, h_ref, xs_ref, sem):
    def row_copy(t, dst):
        return pltpu.make_async_copy(h_ref.at[pl.ds(t, 1)], xs_ref.at[pl.ds(dst, 1)], sem)

    for blk in range(DISPATCH_BLOCKS):
        def issue(g, carry, blk=blk):
            for u in range(ROW_UNROLL):
                t = g * ROW_UNROLL + u
                row_copy(blk * tm + t, pos_ref[blk, 0, t]).start()
                row_copy(blk * tm + t, pos_ref[blk, 1, t]).start(priority=1)
            return carry

        lax.fori_loop(0, tm // ROW_UNROLL, issue, 0)

    def drain(g, carry):
        for _ in range(2 * ROW_UNROLL):
            row_copy(0, 0).wait()
        return carry

    lax.fori_loop(0, DISPATCH_BLOCKS * tm // ROW_UNROLL, drain, 0)


def _dispatch(pos, h2, n_rows):
    n_tiles, _, tm = pos.shape
    d = h2.shape[1]
    nb = DISPATCH_BLOCKS
    assert n_tiles % nb == 0
    return pl.pallas_call(
        functools.partial(_dispatch_kernel, tm),
        grid=(n_tiles // nb,),
        in_specs=[pl.BlockSpec((nb, 8, tm), lambda i: (i, 0, 0), memory_space=pltpu.SMEM),
                  pl.BlockSpec((nb * tm, d), lambda i: (i, 0))],
        out_specs=pl.BlockSpec(memory_space=pl.ANY),
        out_shape=jax.ShapeDtypeStruct((n_rows, d), F32),
        scratch_shapes=[pltpu.SemaphoreType.DMA],
        compiler_params=_params(("arbitrary",)),
        name="dispatch",
    )(pos, h2)


def _ffn_kernel(owner_ref, tile_ref, lo_ref, hi_ref, used_ref, x_ref, w1_ref, w3_ref, w2_ref, y_ref,
                w1b_ref, w3b_ref, w2b_ref):
    w = pl.program_id(0)
    prev = jnp.maximum(w - 1, 0)

    @pl.when((w == 0) | (owner_ref[w] != owner_ref[prev]))
    def _():
        w1b_ref[...] = w1_ref[...].astype(BF16)
        w3b_ref[...] = w3_ref[...].astype(BF16)
        w2b_ref[...] = w2_ref[...].astype(BF16)

    @pl.when(w < used_ref[0])
    def _():
        x = x_ref[...].astype(BF16)
        a = _dot(x, w1b_ref[...])
        b = _dot(x, w3b_ref[...])
        hid = (a * _sigmoid(a)) * b
        y = _dot(hid.astype(BF16), w2b_ref[...])
        row = lax.broadcasted_iota(jnp.int32, y.shape, 0)
        mine = (row >= lo_ref[w]) & (row < hi_ref[w])
        first = (w == 0) | (tile_ref[w] != tile_ref[jnp.maximum(w - 1, 0)])

        @pl.when(first)
        def _():
            y_ref[...] = jnp.where(mine, y, 0.0)

        @pl.when(jnp.logical_not(first))
        def _():
            y_ref[...] = jnp.where(mine, y, y_ref[...])


def _ffn(plan, xs, w1, w3, w2):
    n_rows, d = xs.shape
    tf = FFN_TILE
    de = w1.shape[1] // N_EXPERTS
    n_items = n_rows // tf + N_EXPERTS - 1
    assert n_items <= PLAN_ITEMS

    def item(w, used):
        return jnp.minimum(w, used[0] - 1)

    grid_spec = pltpu.PrefetchScalarGridSpec(
        num_scalar_prefetch=5,
        grid=(n_items,),
        in_specs=[pl.BlockSpec((tf, d), lambda w, own, til, lo, hi, used: (til[item(w, used)], 0)),
                  pl.BlockSpec((d, de), lambda w, own, til, lo, hi, used: (0, own[item(w, used)])),
                  pl.BlockSpec((d, de), lambda w, own, til, lo, hi, used: (0, own[item(w, used)])),
                  pl.BlockSpec((de, d), lambda w, own, til, lo, hi, used: (own[item(w, used)], 0))],
        out_specs=pl.BlockSpec((tf, d), lambda w, own, til, lo, hi, used: (til[item(w, used)], 0)),
        scratch_shapes=[pltpu.VMEM((d, de), BF16), pltpu.VMEM((d, de), BF16), pltpu.VMEM((de, d), BF16)],
    )
    return pl.pallas_call(
        _ffn_kernel,
        grid_spec=grid_spec,
        out_shape=jax.ShapeDtypeStruct((n_rows, d), F32),
        compiler_params=_params(("arbitrary",)),
        name="ffn",
    )(plan[0], plan[1], plan[2], plan[3], plan[4, 0:1], xs, w1, w3, w2)


def _combine_kernel(tm, ctx_steps, pos_ref, nxt_ref, w_ref, x1_ref, mod_ref, gp_ref, ys_ref, yc_ref, yl_ref,
                    buf_ref, sem):
    i = pl.program_id(0)
    n = pl.num_programs(0)
    slot = i % 2

    def row_copy(src, k, t, s):
        return pltpu.make_async_copy(ys_ref.at[pl.ds(src, 1)], buf_ref.at[s, k, pl.ds(t, 1)], sem.at[s])

    def issue(p_ref, s):
        def body(g, carry):
            for u in range(ROW_UNROLL):
                t = g * ROW_UNROLL + u
                row_copy(p_ref[0, t], 0, t, s).start()
                row_copy(p_ref[1, t], 1, t, s).start(priority=1)
            return carry
        lax.fori_loop(0, tm // ROW_UNROLL, body, 0)

    @pl.when(i == 0)
    def _():
        issue(pos_ref, 0)

    @pl.when(i + 1 < n)
    def _():
        issue(nxt_ref, 1 - slot)

    def drain(g, carry):
        for _ in range(2 * ROW_UNROLL):
            row_copy(0, 0, 0, slot).wait()
        return carry

    lax.fori_loop(0, tm // ROW_UNROLL, drain, 0)

    r = lax.broadcasted_iota(jnp.int32, (tm, tm), 0)
    c = lax.broadcasted_iota(jnp.int32, (tm, tm), 1)
    d = x1_ref.shape[1]
    wa = jnp.broadcast_to(jnp.sum(jnp.where(r == c, w_ref[2:3, :], 0.0), axis=1, keepdims=True), (tm, d))
    wb = jnp.broadcast_to(jnp.sum(jnp.where(r == c, w_ref[3:4, :], 0.0), axis=1, keepdims=True), (tm, d))
    moe = wa * buf_ref[slot, 0] + wb * buf_ref[slot, 1]
    gt2 = mod_ref[5:6, :]
    y = x1_ref[...] + gt2 * _rms(moe, gp_ref[...])

    @pl.when(i < ctx_steps)
    def _():
        yc_ref[...] = y

    @pl.when(i >= ctx_steps)
    def _():
        yl_ref[...] = y


def _combine(st, pos, w, x1, mod, g_post2, ys):
    n_tiles, _, tm = pos.shape
    d = x1.shape[1]
    row = pl.BlockSpec((tm, d), lambda i: (i, 0))
    return pl.pallas_call(
        functools.partial(_combine_kernel, tm, st.ctx_steps(tm)),
        grid=(n_tiles,),
        in_specs=[pl.BlockSpec((None, 8, tm), lambda i: (i, 0, 0), memory_space=pltpu.SMEM),
                  pl.BlockSpec((None, 8, tm), lambda i: (jnp.minimum(i + 1, n_tiles - 1), 0, 0),
                               memory_space=pltpu.SMEM),
                  pl.BlockSpec((None, 8, tm), lambda i: (i, 0, 0)),
                  row,
                  st.mod_spec(tm, d),
                  _const_spec((1, d)),
                  pl.BlockSpec(memory_space=pl.ANY)],
        out_specs=[st.ctx_spec(tm, d), st.lat_spec(tm, d)],
        out_shape=[jax.ShapeDtypeStruct((st.n_ctx, d), F32), jax.ShapeDtypeStruct((st.n_lat, d), F32)],
        scratch_shapes=[pltpu.VMEM((2, 2, tm, d), F32), pltpu.SemaphoreType.DMA((2,))],
        compiler_params=_params(("arbitrary",)),
        name="combine",
    )(pos, pos, w, x1, mod, g_post2.reshape(1, d), ys)


def _moe(st, h2, lg, x1, mod, g_post2, w1, w3, w2, route_tri):
    n_tok = x1.shape[0]
    n_rows = 2 * n_tok
    assert n_rows % FFN_TILE == 0
    pos, w, plan = _route(lg, route_tri)
    xs = _dispatch(pos, h2, n_rows)
    ys = _ffn(plan, xs, w1, w3, w2)
    return _combine(st, pos, w, x1, mod, g_post2, ys)


def _grid_pos(n_tokens, d_model):
    rows = n_tokens // GRID_W
    row = np.repeat(np.arange(rows, dtype=np.float64), GRID_W)
    col = np.tile(np.arange(GRID_W, dtype=np.float64), rows)
    n_freq = d_model // 4
    freq = np.exp(-math.log(POS_BASE) * np.arange(n_freq, dtype=np.float64) / n_freq)

    def enc(p):
        a = p[:, None] * freq[None, :]
        return np.concatenate([np.sin(a), np.cos(a)], axis=-1)

    return jnp.asarray(np.concatenate([enc(row), enc(col)], axis=-1), dtype=F32)


def _dft_cos_sin(n):
    k = np.arange(n, dtype=np.int64)
    ang = 2.0 * np.pi * ((k[:, None] * k[None, :]) % n).astype(np.float64) / n
    return np.cos(ang), np.sin(ang)


def _fnet_consts(seq_len):
    ct, st = _dft_cos_sin(seq_len)
    f_mat = jnp.asarray(np.concatenate([ct, -st], axis=1), dtype=F32).astype(BF16)
    cc, sc = _dft_cos_sin(FGROUP_DIM)
    cs_mat = jnp.asarray(np.stack([cc, sc]), dtype=F32).astype(BF16)
    return f_mat, cs_mat


def _tri_consts():
    i = np.arange(CHUNK)
    prefix = (i[:, None] <= i[None, :]).astype(np.float32)
    suffix = (i[:, None] >= i[None, :]).astype(np.float32)
    tri = jnp.asarray(np.stack([prefix, suffix]), dtype=BF16)
    e = np.arange(N_EXPERTS)
    expert_tri = jnp.asarray((e[None, :] < e[:, None]).astype(np.float32), dtype=BF16)
    t = np.arange(TOKEN_TILE)
    token_tri = jnp.asarray((t[:, None] < t[None, :]).astype(np.float32), dtype=BF16)
    return tri, (expert_tri, token_tri)


def _layer(x_prompt, x_sample, pos, mod, states, lw, consts):
    bp, sp, d = x_prompt.shape
    bs, ss, _ = x_sample.shape
    st = _Streams(n_ctx=bp * sp, n_lat=bs * ss, seq_lat=ss)
    (g_pre1, proj_w, g_hn, post_w, g_post1, g_pre2, w_e1, w_e3, w_e2, g_post2) = lw
    tri, route_tris = consts
    x_ctx = x_prompt.reshape(st.n_ctx, d)
    x_lat = x_sample.reshape(st.n_lat, d)
    q, kt, v, o, u, ga, gb, gr = _proj(st, x_ctx, x_lat, pos, mod, g_pre1, proj_w + (tri,))
    ha_ctx, new_c, new_n, new_m = _mlstm(q, kt, v, o, g_hn, gr, bp, sp, 0, None, MLSTM_HEADS_PER_STEP)
    (ha_lat,) = _mlstm(q, kt, v, o, g_hn, gr, bs, ss, st.n_ctx, states, MLSTM_HEADS_PER_STEP)
    yf_ctx = _fnet(u, *_fnet_consts(sp), bp, sp, 0)
    yf_lat = _fnet(u, *_fnet_consts(ss), bs, ss, st.n_ctx)
    x1, h2, logits = _post(st, x_ctx, x_lat, pos, ha_ctx, ha_lat, yf_ctx, yf_lat, ga, gb, mod, g_post1, g_pre2, post_w)
    y_ctx, y_lat = _moe(st, h2, logits, x1, mod, g_post2, w_e1, w_e3, w_e2, route_tris)
    return y_ctx.reshape(bp, sp, d), y_lat.reshape(bs, ss, d), (new_c, new_n, new_m)


def kernel(x_prompt, x_sample, c, state_C, state_n, state_m, c_ctx, w_ada, b_ada, g_pre1, w_in, b_gates, g_hn,
           w_a, w_f, w_out, g_post1, g_pre2, w_rg, b_rg, w_re, b_re, w_e1, w_e3, w_e2, g_post2):
    bp, sp, d = x_prompt.shape
    bs, ss, _ = x_sample.shape
    depth = w_in.shape[0]
    assert depth == 1
    l = 0
    dm = N_HEADS * HEAD_DIM
    du = N_FGROUPS * FGROUP_DIM
    ng = 4 * N_HEADS

    n_rows = 8
    assert 1 + bs <= n_rows
    cs = jnp.concatenate([c_ctx[None, :], c, jnp.zeros((n_rows - 1 - bs, d), F32)], axis=0)
    mod = _ada(cs, w_ada[l], b_ada[l]).reshape(n_rows, 6, d)

    wi = w_in[l]
    cuts = np.cumsum([dm] * 4 + [ng, du, d]).tolist()
    wq = wi[:, :cuts[0]].astype(BF16)
    wkt = wi[:, cuts[0]:cuts[1]].T.astype(BF16)
    wv = wi[:, cuts[1]:cuts[2]].astype(BF16)
    wo = wi[:, cuts[2]:cuts[3]].astype(BF16)
    wgt = wi[:, cuts[3]:cuts[4]].T
    wu = wi[:, cuts[4]:cuts[5]].astype(BF16)
    wga = wi[:, cuts[5]:cuts[6]].astype(BF16)
    wgb = wi[:, cuts[6]:].astype(BF16)
    bg = b_gates[l].reshape(ng, 1)
    proj_w = (wq, wkt, wv, wo, wu, wga, wgb, wgt, bg)
    n_pad = ROUTE_ROWS - N_EXPERTS - N_GROUPS
    wr = jnp.concatenate([w_re[l], w_rg[l], jnp.zeros((d, n_pad), F32)], axis=1).T
    br = jnp.concatenate([b_re[l], b_rg[l], jnp.zeros((n_pad,), F32)]).reshape(ROUTE_ROWS, 1)
    post_w = (w_a[l].astype(BF16), w_f[l].astype(BF16), w_out[l].astype(BF16), wr, br)
    lw = (g_pre1[l], proj_w, g_hn[l], post_w, g_post1[l], g_pre2[l], w_e1[l], w_e3[l], w_e2[l], g_post2[l])
    consts = _tri_consts()

    pos = _grid_pos(ss, d)
    y_prompt, y_sample, (new_c, new_n, new_m) = _layer(x_prompt, x_sample, pos, mod,
                                                       (state_C, state_n, state_m), lw, consts)

    new_state_c = new_c
    new_state_n = new_n.reshape(bp, depth, 2, N_HEADS, HEAD_DIM)
    new_state_m = new_m[..., 0, 0]
    return (y_prompt, y_sample, new_state_c, new_state_n, new_state_m)
```

```python
import functools
import math
from typing import NamedTuple

import numpy as np
import jax
import jax.numpy as jnp
from jax import lax
from jax.experimental import pallas as pl
from jax.experimental.pallas import tpu as pltpu

F32 = jnp.float32
BF16 = jnp.bfloat16

N_HEADS = 8
HEAD_DIM = 128
N_FGROUPS = 4
FGROUP_DIM = 128
N_GROUPS = 4
EXPERTS_PER_GROUP = 8
N_EXPERTS = N_GROUPS * EXPERTS_PER_GROUP
GRID_W = 64
POS_BASE = 10000.0
EPS = 1e-6

V7X_VMEM_LIMIT_BYTES = 56 * 1024 * 1024
LANES = 128
CHUNK = 256
TOKEN_TILE = CHUNK
CHUNKS_PER_STEP = 2
MLSTM_HEADS_PER_STEP = 4


def _params(sem):
    return pltpu.CompilerParams(dimension_semantics=sem, vmem_limit_bytes=V7X_VMEM_LIMIT_BYTES)


def _const_spec(shape):
    n = len(shape)
    return pl.BlockSpec(shape, lambda *_: (0,) * n, pipeline_mode=pl.Buffered(1))


def _split2(x):
    hi = x.astype(BF16)
    lo = (x - hi.astype(F32)).astype(BF16)
    return hi, lo


def _split3(x):
    hi = x.astype(BF16)
    r = x - hi.astype(F32)
    mid = r.astype(BF16)
    lo = (r - mid.astype(F32)).astype(BF16)
    return hi, mid, lo


def _dot(a, b):
    return jnp.dot(a, b, preferred_element_type=F32)


def _dot_nt(a, b):
    return lax.dot_general(a, b, (((1,), (1,)), ((), ())), preferred_element_type=F32)


def _dot_f32(a, b, nt=False):
    d = _dot_nt if nt else _dot
    a1, a2 = _split2(a)
    b1, b2 = _split2(b)
    return d(a1, b1) + (d(a1, b2) + d(a2, b1))


def _dot_exact_rhs(a, b_bf16, nt=False):
    d = _dot_nt if nt else _dot
    a1, a2, a3 = _split3(a)
    return d(a1, b_bf16) + (d(a2, b_bf16) + d(a3, b_bf16))


def _rms(x, g):
    return x * lax.rsqrt(jnp.mean(x * x, axis=-1, keepdims=True) + EPS) * g


def _sigmoid(x):
    return 1.0 / (1.0 + jnp.exp(-x))


def _log_sigmoid(x):
    return jnp.minimum(x, 0.0) - jnp.log(1.0 + jnp.exp(-jnp.abs(x)))


def _ada_kernel(c_ref, w_ref, b_ref, o_ref):
    c = c_ref[...]
    s = (c * _sigmoid(c)).astype(BF16)
    o_ref[...] = _dot(s, w_ref[...].astype(BF16)) + b_ref[...]


def _ada(cs, w_ada, b_ada):
    rows, d = cs.shape
    n = w_ada.shape[1]
    tn = 1024
    return pl.pallas_call(
        _ada_kernel,
        grid=(n // tn,),
        in_specs=[pl.BlockSpec((rows, d), lambda j: (0, 0)),
                  pl.BlockSpec((d, tn), lambda j: (0, j)),
                  pl.BlockSpec((1, tn), lambda j: (0, j))],
        out_specs=pl.BlockSpec((rows, tn), lambda j: (0, j)),
        out_shape=jax.ShapeDtypeStruct((rows, n), F32),
        compiler_params=_params(("arbitrary",)),
        name="ada",
    )(cs, w_ada, b_ada.reshape(1, n))


def _proj_kernel(ctx_steps, xc_ref, xl_ref, pos_ref, mod_ref, g_ref, wq_ref, wkt_ref, wv_ref, wo_ref, wu_ref,
                 wga_ref, wgb_ref, wgt_ref, bg_ref, tri_ref,
                 q_ref, kt_ref, v_ref, o_ref, u_ref, ga_ref, gb_ref, gr_ref):
    x = jnp.where(pl.program_id(0) < ctx_steps, xc_ref[...], xl_ref[...] + pos_ref[...])
    sh = mod_ref[0:1, :]
    sc = mod_ref[1:2, :]
    h = _rms(x, g_ref[...]) * (1.0 + sc) + sh
    hb = h.astype(BF16)
    q_ref[...] = _dot(hb, wq_ref[...]).astype(BF16)
    kt = (_dot_nt(wkt_ref[...], hb) * (HEAD_DIM ** -0.5)).astype(BF16)
    tc = TOKEN_TILE
    for s in range(CHUNKS_PER_STEP):
        kt_ref[s] = kt[:, s * tc:(s + 1) * tc]
    v_ref[...] = _dot(hb, wv_ref[...]).astype(BF16)
    o_ref[...] = _dot(hb, wo_ref[...])
    u_ref[...] = _dot(hb, wu_ref[...]).astype(BF16)
    ga_ref[...] = _dot(hb, wga_ref[...]).astype(BF16)
    gb_ref[...] = _dot(hb, wgb_ref[...]).astype(BF16)
    g = _dot_f32(wgt_ref[...], h, nt=True) + bg_ref[...]
    nh = N_HEADS
    lf_f = _log_sigmoid(g[nh:2 * nh])
    lf_b = _log_sigmoid(g[3 * nh:4 * nh])
    for s in range(CHUNKS_PER_STEP):
        cols = slice(s * tc, (s + 1) * tc)
        gr_ref[s, 0:nh, :] = g[0:nh, cols]
        gr_ref[s, nh:2 * nh, :] = _dot_exact_rhs(lf_f[:, cols], tri_ref[0])
        gr_ref[s, 2 * nh:3 * nh, :] = g[2 * nh:3 * nh, cols]
        gr_ref[s, 3 * nh:4 * nh, :] = _dot_exact_rhs(lf_b[:, cols], tri_ref[1])


class _Streams(NamedTuple):
    n_ctx: int
    n_lat: int
    seq_lat: int

    def ctx_steps(self, tm):
        assert self.n_ctx % tm == 0 and self.seq_lat % tm == 0
        return self.n_ctx // tm

    def ctx_spec(self, tm, width):
        c = self.ctx_steps(tm)
        return pl.BlockSpec((tm, width), lambda i, *_: (jnp.minimum(i, c - 1), 0))

    def lat_spec(self, tm, width):
        c = self.ctx_steps(tm)
        return pl.BlockSpec((tm, width), lambda i, *_: (jnp.maximum(i - c, 0), 0))

    def pos_spec(self, tm, width):
        c = self.ctx_steps(tm)
        per_seq = self.seq_lat // tm
        return pl.BlockSpec((tm, width), lambda i, *_: (jnp.maximum(i - c, 0) % per_seq, 0))

    def mod_spec(self, tm, width):
        c = self.ctx_steps(tm)
        per_seq = self.seq_lat // tm
        return pl.BlockSpec((None, 6, width),
                            lambda i, *_: (jnp.where(i < c, 0, 1 + jnp.maximum(i - c, 0) // per_seq), 0, 0))


def _proj(st, x_ctx, x_lat, pos, mod, g_pre1, wts):
    d = x_ctx.shape[1]
    n_tok = st.n_ctx + st.n_lat
    tc = TOKEN_TILE
    ns = CHUNKS_PER_STEP
    tm = ns * tc
    n_tiles = n_tok // tc
    (wq, wkt, wv, wo, wu, wga, wgb, wgt, bg, tri) = wts
    row = pl.BlockSpec((tm, d), lambda i: (i, 0))
    in_specs = [st.ctx_spec(tm, d), st.lat_spec(tm, d), st.pos_spec(tm, d), st.mod_spec(tm, d), _const_spec((1, d))]
    in_specs += [_const_spec(w.shape) for w in wts]
    args = [x_ctx, x_lat, pos, mod, g_pre1.reshape(1, d)] + list(wts)
    n_gate_rows = 4 * N_HEADS
    du = wu.shape[1]
    out_shape = [jax.ShapeDtypeStruct((n_tok, d), BF16),
                 jax.ShapeDtypeStruct((n_tiles, d, tc), BF16),
                 jax.ShapeDtypeStruct((n_tok, d), BF16),
                 jax.ShapeDtypeStruct((n_tok, d), F32),
                 jax.ShapeDtypeStruct((n_tok, du), BF16),
                 jax.ShapeDtypeStruct((n_tok, d), BF16),
                 jax.ShapeDtypeStruct((n_tok, d), BF16),
                 jax.ShapeDtypeStruct((n_tiles, n_gate_rows, tc), F32)]
    out_specs = [row,
                 pl.BlockSpec((ns, d, tc), lambda i: (i, 0, 0)),
                 row, row,
                 pl.BlockSpec((tm, du), lambda i: (i, 0)),
                 row, row,
                 pl.BlockSpec((ns, n_gate_rows, tc), lambda i: (i, 0, 0))]
    return pl.pallas_call(
        functools.partial(_proj_kernel, st.ctx_steps(tm)),
        grid=(n_tok // tm,),
        in_specs=in_specs,
        out_specs=out_specs,
        out_shape=out_shape,
        compiler_params=_params(("parallel",)),
        name="proj",
    )(*args)


def _mlstm_kernel(n_chunks, hp, has_state, emit_state, *refs):
    it = iter(refs)
    if has_state:
        m0_ref = next(it)
    q_ref, kt_ref, v_ref, o_ref, ghn_ref, gr_ref = (next(it) for _ in range(6))
    if has_state:
        c0_ref, n0_ref = next(it), next(it)
    ha_ref = next(it)
    if emit_state:
        cout_ref, nout_ref, mout_ref = next(it), next(it), next(it)
    hs_ref, cn_ref, m_ref = (next(it) for _ in range(3))

    L = CHUNK
    nh, dh = N_HEADS, HEAD_DIM
    head0 = pl.program_id(1) * hp
    neg_inf = -jnp.inf

    sq_r = lax.broadcasted_iota(jnp.int32, (dh, dh), 0)
    sq_c = lax.broadcasted_iota(jnp.int32, (dh, dh), 1)
    for j in range(hp):
        for d in range(2):
            if has_state:
                n_col = jnp.sum(jnp.where(sq_r == sq_c, n0_ref[d, j], 0.0), axis=1, keepdims=True)
                cn_ref[j, d, :, 0:dh] = c0_ref[d, j]
                cn_ref[j, d, :, dh:2 * dh] = jnp.broadcast_to(n_col, (dh, dh))
                m0 = m0_ref[pl.program_id(0) * (2 * nh) + d * nh + head0 + j]
                m_ref[j, d] = jnp.full((8, LANES), m0, F32)
            else:
                cn_ref[j, d] = jnp.zeros((dh, 2 * dh), F32)
                m_ref[j, d] = jnp.zeros((8, LANES), F32)

    row_id = lax.broadcasted_iota(jnp.int32, (L, L), 0)
    col_id = lax.broadcasted_iota(jnp.int32, (L, L), 1)
    ones_blk = jnp.ones((L, dh), BF16)

    def chunk(j, d, c):
        t0 = pl.multiple_of(c * L, L)
        lanes = slice(j * dh, (j + 1) * dh)
        qc = q_ref[pl.ds(t0, L), lanes]
        ktc = kt_ref[c, lanes, :]
        vc = v_ref[pl.ds(t0, L), lanes]
        ig_row = gr_ref[c, pl.ds(2 * nh * d + head0 + j, 1), :]
        b_row = gr_ref[c, pl.ds(2 * nh * d + nh + head0 + j, 1), :]
        m_prev = m_ref[j, d][0:1, 0:1]
        if d == 0:
            b_end = b_row[:, L - 1:L]
            mask = col_id <= row_id
        else:
            b_end = b_row[:, 0:1]
            mask = col_id >= row_id
        a_row = ig_row - b_row
        a_max = jnp.max(jnp.where(mask, a_row, neg_inf), axis=1, keepdims=True)
        g = jnp.maximum(jnp.broadcast_to(a_max, (L, dh)), m_prev)
        b_col = jnp.broadcast_to(
            jnp.sum(jnp.where(row_id == col_id, b_row, 0.0), axis=1, keepdims=True), (L, dh))
        g_full = jnp.concatenate([g] * (L // dh), axis=1)
        s = _dot(qc, ktc) * jnp.exp(jnp.where(mask, a_row - g_full, neg_inf))
        decay = jnp.exp(m_prev - g)
        qcn = _dot(qc, cn_ref[j, d].astype(BF16))
        num = decay * qcn[:, 0:dh] + _dot(s.astype(BF16), vc)
        den = decay * qcn[:, dh:2 * dh] + jnp.sum(s, axis=1, keepdims=True)
        hch = num / jnp.maximum(jnp.abs(den), jnp.exp(-(b_col + g)))
        w_log = b_end + a_row
        m_new = jnp.maximum(b_end + m_prev, jnp.max(w_log, axis=1, keepdims=True))
        kw = (ktc.astype(F32) * jnp.exp(w_log - m_new)).astype(BF16)
        v_aug = jnp.concatenate([vc, ones_blk], axis=1)
        cn_ref[j, d] = jnp.exp(b_end + m_prev - m_new) * cn_ref[j, d] + _dot(kw, v_aug)
        m_ref[j, d] = jnp.broadcast_to(m_new, (8, LANES))
        return t0, hch

    if n_chunks == 1:
        for j in range(hp):
            _, h_f = chunk(j, 0, 0)
            _, h_b = chunk(j, 1, 0)
            hs_ref[:, j * dh:(j + 1) * dh] = h_f + h_b
    else:
        half = n_chunks // 2

        def first(i, carry):
            for j in range(hp):
                lanes = slice(j * dh, (j + 1) * dh)
                t_f, h_f = chunk(j, 0, i)
                t_b, h_b = chunk(j, 1, n_chunks - 1 - i)
                hs_ref[pl.ds(t_f, L), lanes] = h_f
                hs_ref[pl.ds(t_b, L), lanes] = h_b
            return carry

        def second(i, carry):
            for j in range(hp):
                lanes = slice(j * dh, (j + 1) * dh)
                t_f, h_f = chunk(j, 0, i)
                t_b, h_b = chunk(j, 1, n_chunks - 1 - i)
                hs_ref[pl.ds(t_f, L), lanes] += h_f
                hs_ref[pl.ds(t_b, L), lanes] += h_b
            return carry

        lax.fori_loop(0, half, first, 0)
        lax.fori_loop(half, n_chunks, second, 0)

    def finish(c, carry):
        t0 = pl.multiple_of(c * L, L)
        for j in range(hp):
            lanes = slice(j * dh, (j + 1) * dh)
            ha = hs_ref[pl.ds(t0, L), lanes]
            ha = ha * lax.rsqrt(jnp.mean(ha * ha, axis=-1, keepdims=True) + EPS)
            ha = ha * ghn_ref[:, lanes] * _sigmoid(o_ref[pl.ds(t0, L), lanes])
            ha_ref[pl.ds(t0, L), lanes] = ha.astype(BF16)
        return carry

    if n_chunks == 1:
        finish(0, 0)
    else:
        lax.fori_loop(0, n_chunks, finish, 0)

    if emit_state:
        for j in range(hp):
            for d in range(2):
                cout_ref[d, j] = cn_ref[j, d, :, 0:dh]
                n_rep = cn_ref[j, d, :, dh:2 * dh]
                nout_ref[d, j] = jnp.sum(jnp.where(sq_r == sq_c, n_rep, 0.0), axis=0, keepdims=True)
                mout_ref[d, j] = m_ref[j, d][0:1, :]


def _mlstm(q, kt, v, o, g_hn, gr, n_seq, seq_len, tok_off, states, hp):
    d = q.shape[1]
    n_tok = n_seq * seq_len
    L = CHUNK
    assert tok_off % seq_len == 0
    sb = tok_off // seq_len
    n_chunks = seq_len // L
    has_state = states is not None
    emit_state = not has_state
    nh, dh = N_HEADS, HEAD_DIM
    assert n_chunks == 1 or n_chunks % 2 == 0
    assert nh % hp == 0
    wd = hp * dh

    in_specs = []
    args = []
    if has_state:
        state_c, state_n, state_m = states
        in_specs.append(pl.BlockSpec(memory_space=pltpu.SMEM))
        args.append(state_m.reshape(-1))
    tok = pl.BlockSpec((seq_len, wd), lambda b, h: (b + sb, h))
    in_specs += [tok,
                 pl.BlockSpec((n_chunks, wd, L), lambda b, h: (b + sb, h, 0)),
                 tok, tok,
                 pl.BlockSpec((1, wd), lambda b, h: (0, h)),
                 pl.BlockSpec((n_chunks, 4 * nh, L), lambda b, h: (b + sb, 0, 0))]
    args += [q, kt, v, o, g_hn.reshape(1, d), gr]
    st_c = pl.BlockSpec((None, None, 2, hp, dh, dh), lambda b, h: (b, 0, 0, h, 0, 0))
    st_v = pl.BlockSpec((None, None, 2, hp, 1, dh), lambda b, h: (b, 0, 0, h, 0, 0))
    if has_state:
        in_specs += [st_c, st_v]
        args += [state_c, state_n.reshape(n_seq, 1, 2, nh, 1, dh)]
    out_shape = [jax.ShapeDtypeStruct((n_tok, d), BF16)]
    out_specs = [pl.BlockSpec((seq_len, wd), lambda b, h: (b, h))]
    if emit_state:
        out_shape += [jax.ShapeDtypeStruct((n_seq, 1, 2, nh, dh, dh), F32),
                      jax.ShapeDtypeStruct((n_seq, 1, 2, nh, 1, dh), F32),
                      jax.ShapeDtypeStruct((n_seq, 1, 2, nh, 1, LANES), F32)]
        out_specs += [st_c, st_v, st_v]
    scratch = [pltpu.VMEM((seq_len, wd), F32),
               pltpu.VMEM((hp, 2, dh, 2 * dh), F32),
               pltpu.VMEM((hp, 2, 8, LANES), F32)]
    return pl.pallas_call(
        functools.partial(_mlstm_kernel, n_chunks, hp, has_state, emit_state),
        grid=(n_seq, nh // hp),
        in_specs=in_specs,
        out_specs=out_specs,
        out_shape=out_shape,
        scratch_shapes=scratch,
        compiler_params=_params(("parallel", "parallel")),
        name="mlstm",
    )(*args)


def _fnet_kernel(seq_len, u_ref, f_ref, cs_ref, y_ref, ab_ref):
    T = seq_len

    @pl.when(pl.program_id(1) == 0)
    def _():
        for g in range(N_FGROUPS):
            ug = u_ref[:, g * FGROUP_DIM:(g + 1) * FGROUP_DIM]
            ab_ref[0:T, g * FGROUP_DIM:(g + 1) * FGROUP_DIM] = _dot(ug, cs_ref[0]).astype(BF16)
            ab_ref[T:2 * T, g * FGROUP_DIM:(g + 1) * FGROUP_DIM] = _dot(ug, cs_ref[1]).astype(BF16)

    scale = 1.0 / math.sqrt(T * FGROUP_DIM)
    y_ref[...] = (_dot(f_ref[...], ab_ref[...]) * scale).astype(BF16)


def _fnet(u, f_mat, cs_mat, n_seq, seq_len, tok_off):
    du = u.shape[1]
    n_tok = n_seq * seq_len
    tr = min(seq_len, 512)
    assert tok_off % seq_len == 0
    sb = tok_off // seq_len
    return pl.pallas_call(
        functools.partial(_fnet_kernel, seq_len),
        grid=(n_seq, seq_len // tr),
        in_specs=[pl.BlockSpec((seq_len, du), lambda b, r: (b + sb, 0)),
                  pl.BlockSpec((tr, 2 * seq_len), lambda b, r: (r, 0)),
                  _const_spec(cs_mat.shape)],
        out_specs=pl.BlockSpec((tr, du), lambda b, r: (b * (seq_len // tr) + r, 0)),
        out_shape=jax.ShapeDtypeStruct((n_tok, du), BF16),
        scratch_shapes=[pltpu.VMEM((2 * seq_len, du), BF16)],
        compiler_params=_params(("parallel", "arbitrary")),
        name="fnet",
    )(u, f_mat, cs_mat)


def _post_kernel(ctx_steps, xc_ref, xl_ref, pos_ref, hac_ref, hal_ref, yfc_ref, yfl_ref, ga_ref, gb_ref, mod_ref,
                 gp1_ref, gp2_ref, wa_ref, wf_ref, wout_ref, wr_ref, br_ref, x1_ref, h2_ref, lg_ref):
    is_ctx = pl.program_id(0) < ctx_steps
    x = jnp.where(is_ctx, xc_ref[...], xl_ref[...] + pos_ref[...])
    ya = _dot(jnp.where(is_ctx, hac_ref[...], hal_ref[...]), wa_ref[...])
    yf = _dot(jnp.where(is_ctx, yfc_ref[...], yfl_ref[...]), wf_ref[...])
    mix_in = _sigmoid(ga_ref[...].astype(F32)) * ya + _sigmoid(gb_ref[...].astype(F32)) * yf
    mix = _dot(mix_in.astype(BF16), wout_ref[...])
    gt1 = mod_ref[2:3, :]
    sh2 = mod_ref[3:4, :]
    sc2 = mod_ref[4:5, :]
    x1 = x + gt1 * _rms(mix, gp1_ref[...])
    h2 = _rms(x1, gp2_ref[...]) * (1.0 + sc2) + sh2
    x1_ref[...] = x1
    h2_ref[...] = h2
    lg = _dot_f32(wr_ref[...], h2, nt=True) + br_ref[...]
    for s in range(CHUNKS_PER_STEP):
        lg_ref[s] = lg[:, s * TOKEN_TILE:(s + 1) * TOKEN_TILE]


def _post(st, x_ctx, x_lat, pos, ha_ctx, ha_lat, yf_ctx, yf_lat, ga, gb, mod, g_post1, g_pre2, wts):
    d = x_ctx.shape[1]
    n_tok = st.n_ctx + st.n_lat
    tc = TOKEN_TILE
    ns = CHUNKS_PER_STEP
    tm = ns * tc
    (wa, wf, wout, wr, br) = wts
    du = yf_ctx.shape[1]
    row = pl.BlockSpec((tm, d), lambda i: (i, 0))
    in_specs = [st.ctx_spec(tm, d), st.lat_spec(tm, d), st.pos_spec(tm, d),
                st.ctx_spec(tm, d), st.lat_spec(tm, d), st.ctx_spec(tm, du), st.lat_spec(tm, du),
                row, row, st.mod_spec(tm, d), _const_spec((1, d)), _const_spec((1, d))]
    in_specs += [_const_spec(w.shape) for w in wts]
    args = [x_ctx, x_lat, pos, ha_ctx, ha_lat, yf_ctx, yf_lat, ga, gb, mod,
            g_post1.reshape(1, d), g_pre2.reshape(1, d)] + list(wts)
    return pl.pallas_call(
        functools.partial(_post_kernel, st.ctx_steps(tm)),
        grid=(n_tok // tm,),
        in_specs=in_specs,
        out_specs=[row, row, pl.BlockSpec((ns, ROUTE_ROWS, tc), lambda i: (i, 0, 0))],
        out_shape=[jax.ShapeDtypeStruct((n_tok, d), F32),
                   jax.ShapeDtypeStruct((n_tok, d), F32),
                   jax.ShapeDtypeStruct((n_tok // tc, ROUTE_ROWS, tc), F32)],
        compiler_params=_params(("parallel",)),
        name="post",
    )(*args)


ROUTE_ROWS = N_EXPERTS + 8
ROUTE_BLOCKS = 4
FFN_TILE = 512
FFN_SPLIT = 2
PLAN_ITEMS = LANES


def _route_rows(lg):
    tm = lg.shape[1]
    ne = N_EXPERTS
    gl = lg[ne:ne + 8]
    g_id = lax.broadcasted_iota(jnp.int32, (8, tm), 0).astype(F32)
    is_g = g_id < N_GROUPS
    gl = jnp.where(is_g, gl, -jnp.inf)
    mx = jnp.max(gl, axis=0, keepdims=True)
    z = jnp.sum(jnp.where(is_g, jnp.exp(gl - mx), 0.0), axis=0, keepdims=True)
    p_sel = 1.0 / z
    g_sel = jnp.min(jnp.where(gl == mx, g_id, float(N_GROUPS)), axis=0, keepdims=True)
    e_id = lax.broadcasted_iota(jnp.int32, (ne, tm), 0).astype(F32)
    lo = EXPERTS_PER_GROUP * g_sel
    in_grp = (e_id >= lo) & (e_id < lo + EXPERTS_PER_GROUP)
    le = jnp.where(in_grp, lg[0:ne], -jnp.inf)
    v1 = jnp.max(le, axis=0, keepdims=True)
    i1 = jnp.min(jnp.where(le == v1, e_id, float(ne)), axis=0, keepdims=True)
    le2 = jnp.where(e_id == i1, -jnp.inf, le)
    v2 = jnp.max(le2, axis=0, keepdims=True)
    i2 = jnp.min(jnp.where(le2 == v2, e_id, float(ne)), axis=0, keepdims=True)
    e2 = jnp.exp(v2 - v1)
    w1 = p_sel / (1.0 + e2)
    w2 = p_sel * e2 / (1.0 + e2)
    return i1, i2, w1, w2


def _route_kernel(n_tiles, lg_ref, etri_ref, ttri_ref, pos_ref, w_ref, plan_ref, meta_ref, cnt_ref, run_ref, base_ref):
    p = pl.program_id(0)
    i = pl.program_id(1)
    tm = lg_ref.shape[2]
    sub = lax.broadcasted_iota(jnp.int32, (N_EXPERTS, tm), 0).astype(F32)
    tf = float(FFN_TILE)

    nb = ROUTE_BLOCKS

    @pl.when(p == 0)
    def _():
        @pl.when(i == 0)
        def _():
            cnt_ref[...] = jnp.zeros_like(cnt_ref)

        count = jnp.zeros((N_EXPERTS, 1), F32)
        for s in range(nb):
            t = i * nb + s
            e_a, e_b, w_a, w_b = _route_rows(lg_ref[s])
            meta_ref[t, 0:1, :] = e_a
            meta_ref[t, 1:2, :] = e_b
            meta_ref[t, 2:3, :] = w_a
            meta_ref[t, 3:4, :] = w_b
            meta_ref[t, 4:8, :] = jnp.zeros((4, tm), F32)
            oh = jnp.where((sub == e_a) | (sub == e_b), 1.0, 0.0)
            count = count + jnp.sum(oh, axis=1, keepdims=True)
        cnt_ref[...] += jnp.broadcast_to(count, cnt_ref.shape)

    @pl.when(p == 1)
    def _():
        @pl.when(i == 0)
        def _():
            c1, c2, c3 = _split3(cnt_ref[...])
            base_ref[...] = _dot(etri_ref[...], c1) + (_dot(etri_ref[...], c2) + _dot(etri_ref[...], c3))
            run_ref[...] = jnp.zeros_like(run_ref)

        start = run_ref[...] + base_ref[...]
        for s in range(nb):
            t = i * nb + s
            e_a = meta_ref[t, 0:1, :]
            e_b = meta_ref[t, 1:2, :]
            oh_a = jnp.where(sub == e_a, 1.0, 0.0)
            oh_b = jnp.where(sub == e_b, 1.0, 0.0)
            oh = oh_a + oh_b
            before = _dot(oh.astype(BF16), ttri_ref[...]) + start
            pos_ref[s] = jnp.zeros(pos_ref.shape[1:], jnp.int32)
            pos_ref[s, 0:1, :] = jnp.sum(oh_a * before, axis=0, keepdims=True).astype(jnp.int32)
            pos_ref[s, 1:2, :] = jnp.sum(oh_b * before, axis=0, keepdims=True).astype(jnp.int32)
            w_ref[s] = meta_ref[t]
            start = start + jnp.broadcast_to(jnp.sum(oh, axis=1, keepdims=True), start.shape)
        run_ref[...] = start - base_ref[...]

        @pl.when(i == n_tiles // nb - 1)
        def _():
            cnt = cnt_ref[:, 0:PLAN_ITEMS]
            base = base_ref[:, 0:PLAN_ITEMS]
            e_id = lax.broadcasted_iota(jnp.int32, (N_EXPERTS, PLAN_ITEMS), 0)
            t_lo = jnp.floor(base / tf)
            t_hi = jnp.floor((base + cnt - 1.0) / tf)
            n_items = jnp.where(cnt > 0.0, t_hi - t_lo + 1.0, 0.0)
            i_start = _dot(etri_ref[...], n_items.astype(BF16))
            item = lax.broadcasted_iota(jnp.int32, (N_EXPERTS, PLAN_ITEMS), 1).astype(F32)
            sel = (i_start <= item) & (item < i_start + n_items)
            tile_e = t_lo + item - i_start
            off = base - tile_e * tf

            def pick(v):
                return jnp.sum(jnp.where(sel, v, 0.0), axis=0, keepdims=True).astype(jnp.int32)

            plan_ref[...] = jnp.zeros(plan_ref.shape, jnp.int32)
            plan_ref[0:1, :] = pick(e_id.astype(F32))
            plan_ref[1:2, :] = pick(tile_e)
            plan_ref[2:3, :] = pick(jnp.maximum(off, 0.0))
            plan_ref[3:4, :] = pick(jnp.minimum(off + cnt, tf))
            plan_ref[4:5, :] = jnp.sum(n_items, axis=0, keepdims=True).astype(jnp.int32)


def _route(lg, tris):
    expert_tri, token_tri = tris
    n_tiles, _, tm = lg.shape
    nb = ROUTE_BLOCKS
    assert n_tiles % nb == 0
    return pl.pallas_call(
        functools.partial(_route_kernel, n_tiles),
        grid=(2, n_tiles // nb),
        in_specs=[pl.BlockSpec((nb, ROUTE_ROWS, tm), lambda p, i: (i, 0, 0)),
                  _const_spec(expert_tri.shape), _const_spec(token_tri.shape)],
        out_specs=[pl.BlockSpec((nb, 8, tm), lambda p, i: (i * p, 0, 0)),
                   pl.BlockSpec((nb, 8, tm), lambda p, i: (i * p, 0, 0)),
                   pl.BlockSpec((8, PLAN_ITEMS), lambda p, i: (0, 0))],
        out_shape=[jax.ShapeDtypeStruct((n_tiles, 8, tm), jnp.int32),
                   jax.ShapeDtypeStruct((n_tiles, 8, tm), F32),
                   jax.ShapeDtypeStruct((8, PLAN_ITEMS), jnp.int32)],
        scratch_shapes=[pltpu.VMEM((n_tiles, 8, tm), F32),
                        pltpu.VMEM((N_EXPERTS, tm), F32),
                        pltpu.VMEM((N_EXPERTS, tm), F32),
                        pltpu.VMEM((N_EXPERTS, tm), F32)],
        compiler_params=_params(("arbitrary", "arbitrary")),
        name="route",
    )(lg, expert_tri, token_tri)


ROW_UNROLL = 8


DISPATCH_BLOCKS = 4


def _dispatch_kernel(tm, pos_ref, h_ref, xs_ref, sem):
    def row_copy(t, dst):
        return pltpu.make_async_copy(h_ref.at[pl.ds(t, 1)], xs_ref.at[pl.ds(dst, 1)], sem)

    for blk in range(DISPATCH_BLOCKS):
        def issue(g, carry, blk=blk):
            for u in range(ROW_UNROLL):
                t = g * ROW_UNROLL + u
                row_copy(blk * tm + t, pos_ref[blk, 0, t]).start()
                row_copy(blk * tm + t, pos_ref[blk, 1, t]).start()
            return carry

        lax.fori_loop(0, tm // ROW_UNROLL, issue, 0)

    def drain(g, carry):
        for _ in range(2 * ROW_UNROLL):
            row_copy(0, 0).wait()
        return carry

    lax.fori_loop(0, DISPATCH_BLOCKS * tm // ROW_UNROLL, drain, 0)


def _dispatch(pos, h2, n_rows):
    n_tiles, _, tm = pos.shape
    d = h2.shape[1]
    nb = DISPATCH_BLOCKS
    assert n_tiles % nb == 0
    return pl.pallas_call(
        functools.partial(_dispatch_kernel, tm),
        grid=(n_tiles // nb,),
        in_specs=[pl.BlockSpec((nb, 8, tm), lambda i: (i, 0, 0), memory_space=pltpu.SMEM),
                  pl.BlockSpec((nb * tm, d), lambda i: (i, 0))],
        out_specs=pl.BlockSpec(memory_space=pl.ANY),
        out_shape=jax.ShapeDtypeStruct((n_rows, d), F32),
        scratch_shapes=[pltpu.SemaphoreType.DMA],
        compiler_params=_params(("arbitrary",)),
        name="dispatch",
    )(pos, h2)


def _ffn_kernel(owner_ref, tile_ref, lo_ref, hi_ref, used_ref, x_ref, w1_ref, w3_ref, w2_ref, y_ref,
                w1b_ref, w3b_ref, w2b_ref):
    w = pl.program_id(0)
    prev = jnp.maximum(w - 1, 0)

    @pl.when((w == 0) | (owner_ref[w] != owner_ref[prev]))
    def _():
        w1b_ref[...] = w1_ref[...].astype(BF16)
        w3b_ref[...] = w3_ref[...].astype(BF16)
        w2b_ref[...] = w2_ref[...].astype(BF16)

    @pl.when(w < used_ref[0])
    def _():
        lo = lo_ref[w]
        hi = hi_ref[w]
        first = (w == 0) | (tile_ref[w] != tile_ref[prev])
        hr = x_ref.shape[0] // FFN_SPLIT
        for s in range(FFN_SPLIT):
            rows = slice(s * hr, (s + 1) * hr)
            wanted = (lo < (s + 1) * hr) & (hi > s * hr)

            @pl.when(wanted)
            def _(rows=rows, s=s):
                x = x_ref[rows, :].astype(BF16)
                a = _dot(x, w1b_ref[...])
                b = _dot(x, w3b_ref[...])
                hid = (a * _sigmoid(a)) * b
                y = _dot(hid.astype(BF16), w2b_ref[...])
                row = lax.broadcasted_iota(jnp.int32, y.shape, 0) + s * hr
                mine = (row >= lo) & (row < hi)

                @pl.when(first)
                def _():
                    y_ref[rows, :] = jnp.where(mine, y, 0.0)

                @pl.when(jnp.logical_not(first))
                def _():
                    y_ref[rows, :] = jnp.where(mine, y, y_ref[rows, :])

            @pl.when(jnp.logical_not(wanted) & first)
            def _(rows=rows):
                y_ref[rows, :] = jnp.zeros((hr, y_ref.shape[1]), F32)


def _ffn(plan, xs, w1, w3, w2):
    n_rows, d = xs.shape
    tf = FFN_TILE
    de = w1.shape[1] // N_EXPERTS
    n_items = n_rows // tf + N_EXPERTS - 1
    assert n_items <= PLAN_ITEMS

    def item(w, used):
        return jnp.minimum(w, used[0] - 1)

    grid_spec = pltpu.PrefetchScalarGridSpec(
        num_scalar_prefetch=5,
        grid=(n_items,),
        in_specs=[pl.BlockSpec((tf, d), lambda w, own, til, lo, hi, used: (til[item(w, used)], 0)),
                  pl.BlockSpec((d, de), lambda w, own, til, lo, hi, used: (0, own[item(w, used)])),
                  pl.BlockSpec((d, de), lambda w, own, til, lo, hi, used: (0, own[item(w, used)])),
                  pl.BlockSpec((de, d), lambda w, own, til, lo, hi, used: (own[item(w, used)], 0))],
        out_specs=pl.BlockSpec((tf, d), lambda w, own, til, lo, hi, used: (til[item(w, used)], 0)),
        scratch_shapes=[pltpu.VMEM((d, de), BF16), pltpu.VMEM((d, de), BF16), pltpu.VMEM((de, d), BF16)],
    )
    return pl.pallas_call(
        _ffn_kernel,
        grid_spec=grid_spec,
        out_shape=jax.ShapeDtypeStruct((n_rows, d), F32),
        compiler_params=_params(("arbitrary",)),
        name="ffn",
    )(plan[0], plan[1], plan[2], plan[3], plan[4, 0:1], xs, w1, w3, w2)


def _combine_kernel(tm, ctx_steps, pos_ref, nxt_ref, w_ref, x1_ref, mod_ref, gp_ref, ys_ref, yc_ref, yl_ref,
                    buf_ref, sem):
    i = pl.program_id(0)
    n = pl.num_programs(0)
    slot = i % 2

    def row_copy(src, k, t, s):
        return pltpu.make_async_copy(ys_ref.at[pl.ds(src, 1)], buf_ref.at[s, k, pl.ds(t, 1)], sem.at[s])

    def issue(p_ref, s):
        def body(g, carry):
            for u in range(ROW_UNROLL):
                t = g * ROW_UNROLL + u
                row_copy(p_ref[0, t], 0, t, s).start()
                row_copy(p_ref[1, t], 1, t, s).start()
            return carry
        lax.fori_loop(0, tm // ROW_UNROLL, body, 0)

    @pl.when(i == 0)
    def _():
        issue(pos_ref, 0)

    @pl.when(i + 1 < n)
    def _():
        issue(nxt_ref, 1 - slot)

    def drain(g, carry):
        for _ in range(2 * ROW_UNROLL):
            row_copy(0, 0, 0, slot).wait()
        return carry

    lax.fori_loop(0, tm // ROW_UNROLL, drain, 0)

    r = lax.broadcasted_iota(jnp.int32, (tm, tm), 0)
    c = lax.broadcasted_iota(jnp.int32, (tm, tm), 1)
    d = x1_ref.shape[1]
    wa = jnp.broadcast_to(jnp.sum(jnp.where(r == c, w_ref[2:3, :], 0.0), axis=1, keepdims=True), (tm, d))
    wb = jnp.broadcast_to(jnp.sum(jnp.where(r == c, w_ref[3:4, :], 0.0), axis=1, keepdims=True), (tm, d))
    moe = wa * buf_ref[slot, 0] + wb * buf_ref[slot, 1]
    gt2 = mod_ref[5:6, :]
    y = x1_ref[...] + gt2 * _rms(moe, gp_ref[...])

    @pl.when(i < ctx_steps)
    def _():
        yc_ref[...] = y

    @pl.when(i >= ctx_steps)
    def _():
        yl_ref[...] = y


def _combine(st, pos, w, x1, mod, g_post2, ys):
    n_tiles, _, tm = pos.shape
    d = x1.shape[1]
    row = pl.BlockSpec((tm, d), lambda i: (i, 0))
    return pl.pallas_call(
        functools.partial(_combine_kernel, tm, st.ctx_steps(tm)),
        grid=(n_tiles,),
        in_specs=[pl.BlockSpec((None, 8, tm), lambda i: (i, 0, 0), memory_space=pltpu.SMEM),
                  pl.BlockSpec((None, 8, tm), lambda i: (jnp.minimum(i + 1, n_tiles - 1), 0, 0),
                               memory_space=pltpu.SMEM),
                  pl.BlockSpec((None, 8, tm), lambda i: (i, 0, 0)),
                  row,
                  st.mod_spec(tm, d),
                  _const_spec((1, d)),
                  pl.BlockSpec(memory_space=pl.ANY)],
        out_specs=[st.ctx_spec(tm, d), st.lat_spec(tm, d)],
        out_shape=[jax.ShapeDtypeStruct((st.n_ctx, d), F32), jax.ShapeDtypeStruct((st.n_lat, d), F32)],
        scratch_shapes=[pltpu.VMEM((2, 2, tm, d), F32), pltpu.SemaphoreType.DMA((2,))],
        compiler_params=_params(("arbitrary",)),
        name="combine",
    )(pos, pos, w, x1, mod, g_post2.reshape(1, d), ys)


def _moe(st, h2, lg, x1, mod, g_post2, w1, w3, w2, route_tri):
    n_tok = x1.shape[0]
    n_rows = 2 * n_tok
    assert n_rows % FFN_TILE == 0
    pos, w, plan = _route(lg, route_tri)
    xs = _dispatch(pos, h2, n_rows)
    ys = _ffn(plan, xs, w1, w3, w2)
    return _combine(st, pos, w, x1, mod, g_post2, ys)


def _grid_pos(n_tokens, d_model):
    rows = n_tokens // GRID_W
    row = np.repeat(np.arange(rows, dtype=np.float64), GRID_W)
    col = np.tile(np.arange(GRID_W, dtype=np.float64), rows)
    n_freq = d_model // 4
    freq = np.exp(-math.log(POS_BASE) * np.arange(n_freq, dtype=np.float64) / n_freq)

    def enc(p):
        a = p[:, None] * freq[None, :]
        return np.concatenate([np.sin(a), np.cos(a)], axis=-1)

    return jnp.asarray(np.concatenate([enc(row), enc(col)], axis=-1), dtype=F32)


def _dft_cos_sin(n):
    k = np.arange(n, dtype=np.int64)
    ang = 2.0 * np.pi * ((k[:, None] * k[None, :]) % n).astype(np.float64) / n
    return np.cos(ang), np.sin(ang)


def _fnet_consts(seq_len):
    ct, st = _dft_cos_sin(seq_len)
    f_mat = jnp.asarray(np.concatenate([ct, -st], axis=1), dtype=F32).astype(BF16)
    cc, sc = _dft_cos_sin(FGROUP_DIM)
    cs_mat = jnp.asarray(np.stack([cc, sc]), dtype=F32).astype(BF16)
    return f_mat, cs_mat


def _tri_consts():
    i = np.arange(CHUNK)
    prefix = (i[:, None] <= i[None, :]).astype(np.float32)
    suffix = (i[:, None] >= i[None, :]).astype(np.float32)
    tri = jnp.asarray(np.stack([prefix, suffix]), dtype=BF16)
    e = np.arange(N_EXPERTS)
    expert_tri = jnp.asarray((e[None, :] < e[:, None]).astype(np.float32), dtype=BF16)
    t = np.arange(TOKEN_TILE)
    token_tri = jnp.asarray((t[:, None] < t[None, :]).astype(np.float32), dtype=BF16)
    return tri, (expert_tri, token_tri)


def _layer(x_prompt, x_sample, pos, mod, states, lw, consts):
    bp, sp, d = x_prompt.shape
    bs, ss, _ = x_sample.shape
    st = _Streams(n_ctx=bp * sp, n_lat=bs * ss, seq_lat=ss)
    (g_pre1, proj_w, g_hn, post_w, g_post1, g_pre2, w_e1, w_e3, w_e2, g_post2) = lw
    tri, route_tris = consts
    x_ctx = x_prompt.reshape(st.n_ctx, d)
    x_lat = x_sample.reshape(st.n_lat, d)
    q, kt, v, o, u, ga, gb, gr = _proj(st, x_ctx, x_lat, pos, mod, g_pre1, proj_w + (tri,))
    ha_ctx, new_c, new_n, new_m = _mlstm(q, kt, v, o, g_hn, gr, bp, sp, 0, None, MLSTM_HEADS_PER_STEP)
    (ha_lat,) = _mlstm(q, kt, v, o, g_hn, gr, bs, ss, st.n_ctx, states, MLSTM_HEADS_PER_STEP)
    yf_ctx = _fnet(u, *_fnet_consts(sp), bp, sp, 0)
    yf_lat = _fnet(u, *_fnet_consts(ss), bs, ss, st.n_ctx)
    x1, h2, logits = _post(st, x_ctx, x_lat, pos, ha_ctx, ha_lat, yf_ctx, yf_lat, ga, gb, mod, g_post1, g_pre2, post_w)
    y_ctx, y_lat = _moe(st, h2, logits, x1, mod, g_post2, w_e1, w_e3, w_e2, route_tris)
    return y_ctx.reshape(bp, sp, d), y_lat.reshape(bs, ss, d), (new_c, new_n, new_m)


def kernel(x_prompt, x_sample, c, state_C, state_n, state_m, c_ctx, w_ada, b_ada, g_pre1, w_in, b_gates, g_hn,
           w_a, w_f, w_out, g_post1, g_pre2, w_rg, b_rg, w_re, b_re, w_e1, w_e3, w_e2, g_post2):
    bp, sp, d = x_prompt.shape
    bs, ss, _ = x_sample.shape
    depth = w_in.shape[0]
    assert depth == 1
    l = 0
    dm = N_HEADS * HEAD_DIM
    du = N_FGROUPS * FGROUP_DIM
    ng = 4 * N_HEADS

    n_rows = 8
    assert 1 + bs <= n_rows
    cs = jnp.concatenate([c_ctx[None, :], c, jnp.zeros((n_rows - 1 - bs, d), F32)], axis=0)
    mod = _ada(cs, w_ada[l], b_ada[l]).reshape(n_rows, 6, d)

    wi = w_in[l]
    cuts = np.cumsum([dm] * 4 + [ng, du, d]).tolist()
    wq = wi[:, :cuts[0]].astype(BF16)
    wkt = wi[:, cuts[0]:cuts[1]].T.astype(BF16)
    wv = wi[:, cuts[1]:cuts[2]].astype(BF16)
    wo = wi[:, cuts[2]:cuts[3]].astype(BF16)
    wgt = wi[:, cuts[3]:cuts[4]].T
    wu = wi[:, cuts[4]:cuts[5]].astype(BF16)
    wga = wi[:, cuts[5]:cuts[6]].astype(BF16)
    wgb = wi[:, cuts[6]:].astype(BF16)
    bg = b_gates[l].reshape(ng, 1)
    proj_w = (wq, wkt, wv, wo, wu, wga, wgb, wgt, bg)
    n_pad = ROUTE_ROWS - N_EXPERTS - N_GROUPS
    wr = jnp.concatenate([w_re[l], w_rg[l], jnp.zeros((d, n_pad), F32)], axis=1).T
    br = jnp.concatenate([b_re[l], b_rg[l], jnp.zeros((n_pad,), F32)]).reshape(ROUTE_ROWS, 1)
    post_w = (w_a[l].astype(BF16), w_f[l].astype(BF16), w_out[l].astype(BF16), wr, br)
    lw = (g_pre1[l], proj_w, g_hn[l], post_w, g_post1[l], g_pre2[l], w_e1[l], w_e3[l], w_e2[l], g_post2[l])
    consts = _tri_consts()

    pos = _grid_pos(ss, d)
    y_prompt, y_sample, (new_c, new_n, new_m) = _layer(x_prompt, x_sample, pos, mod,
                                                       (state_C, state_n, state_m), lw, consts)

    new_state_c = new_c
    new_state_n = new_n.reshape(bp, depth, 2, N_HEADS, HEAD_DIM)
    new_state_m = new_m[..., 0, 0]
    return (y_prompt, y_sample, new_state_c, new_state_n, new_state_m)
```

```python
import functools
import math
from typing import NamedTuple

import numpy as np
import jax
import jax.numpy as jnp
from jax import lax
from jax.experimental import pallas as pl
from jax.experimental.pallas import tpu as pltpu

F32 = jnp.float32
BF16 = jnp.bfloat16

N_HEADS = 8
HEAD_DIM = 128
N_FGROUPS = 4
FGROUP_DIM = 128
N_GROUPS = 4
EXPERTS_PER_GROUP = 8
N_EXPERTS = N_GROUPS * EXPERTS_PER_GROUP
GRID_W = 64
POS_BASE = 10000.0
EPS = 1e-6

V7X_VMEM_LIMIT_BYTES = 56 * 1024 * 1024
LANES = 128
CHUNK = 256
TOKEN_TILE = CHUNK
CHUNKS_PER_STEP = 2
MLSTM_HEADS_PER_STEP = 4


def _params(sem):
    return pltpu.CompilerParams(dimension_semantics=sem, vmem_limit_bytes=V7X_VMEM_LIMIT_BYTES)


def _const_spec(shape):
    n = len(shape)
    return pl.BlockSpec(shape, lambda *_: (0,) * n, pipeline_mode=pl.Buffered(1))


def _split2(x):
    hi = x.astype(BF16)
    lo = (x - hi.astype(F32)).astype(BF16)
    return hi, lo


def _split3(x):
    hi = x.astype(BF16)
    r = x - hi.astype(F32)
    mid = r.astype(BF16)
    lo = (r - mid.astype(F32)).astype(BF16)
    return hi, mid, lo


def _dot(a, b):
    return jnp.dot(a, b, preferred_element_type=F32)


def _dot_nt(a, b):
    return lax.dot_general(a, b, (((1,), (1,)), ((), ())), preferred_element_type=F32)


def _dot_f32(a, b, nt=False):
    d = _dot_nt if nt else _dot
    a1, a2 = _split2(a)
    b1, b2 = _split2(b)
    return d(a1, b1) + (d(a1, b2) + d(a2, b1))


def _dot_exact_rhs(a, b_bf16, nt=False):
    d = _dot_nt if nt else _dot
    a1, a2, a3 = _split3(a)
    return d(a1, b_bf16) + (d(a2, b_bf16) + d(a3, b_bf16))


SLABS = 8


def _load_rows(ref, n, *lead):
    return jnp.concatenate([ref[(*lead, pl.ds(j, n, stride=SLABS), slice(None))] for j in range(SLABS)], axis=1)


def _store_rows(ref, val, *lead):
    n = val.shape[0]
    for j in range(SLABS):
        ref[(*lead, pl.ds(j, n, stride=SLABS), slice(None))] = val[:, j * LANES:(j + 1) * LANES]


def _slab(ref, row):
    return ref.at[pl.ds(pl.multiple_of(row * SLABS, SLABS), SLABS)]


def _rms(x, g):
    return x * lax.rsqrt(jnp.mean(x * x, axis=-1, keepdims=True) + EPS) * g


def _sigmoid(x):
    return 1.0 / (1.0 + jnp.exp(-x))


def _log_sigmoid(x):
    return jnp.minimum(x, 0.0) - jnp.log(1.0 + jnp.exp(-jnp.abs(x)))


def _ada_kernel(c_ref, w_ref, b_ref, o_ref):
    c = c_ref[...]
    s = (c * _sigmoid(c)).astype(BF16)
    o_ref[...] = _dot(s, w_ref[...].astype(BF16)) + b_ref[...]


def _ada(cs, w_ada, b_ada):
    rows, d = cs.shape
    n = w_ada.shape[1]
    tn = 1024
    return pl.pallas_call(
        _ada_kernel,
        grid=(n // tn,),
        in_specs=[pl.BlockSpec((rows, d), lambda j: (0, 0)),
                  pl.BlockSpec((d, tn), lambda j: (0, j)),
                  pl.BlockSpec((1, tn), lambda j: (0, j))],
        out_specs=pl.BlockSpec((rows, tn), lambda j: (0, j)),
        out_shape=jax.ShapeDtypeStruct((rows, n), F32),
        compiler_params=_params(("arbitrary",)),
        name="ada",
    )(cs, w_ada, b_ada.reshape(1, n))


def _proj_kernel(ctx_steps, xc_ref, xl_ref, pos_ref, mod_ref, g_ref, wq_ref, wkt_ref, wv_ref, wo_ref, wu_ref,
                 wga_ref, wgb_ref, wgt_ref, bg_ref, tri_ref,
                 q_ref, kt_ref, v_ref, o_ref, u_ref, ga_ref, gb_ref, gr_ref):
    x = jnp.where(pl.program_id(0) < ctx_steps, xc_ref[...], xl_ref[...] + pos_ref[...])
    sh = mod_ref[0:1, :]
    sc = mod_ref[1:2, :]
    h = _rms(x, g_ref[...]) * (1.0 + sc) + sh
    hb = h.astype(BF16)
    q_ref[...] = _dot(hb, wq_ref[...]).astype(BF16)
    kt = (_dot_nt(wkt_ref[...], hb) * (HEAD_DIM ** -0.5)).astype(BF16)
    tc = TOKEN_TILE
    for s in range(CHUNKS_PER_STEP):
        kt_ref[s] = kt[:, s * tc:(s + 1) * tc]
    v_ref[...] = _dot(hb, wv_ref[...]).astype(BF16)
    o_ref[...] = _dot(hb, wo_ref[...])
    u_ref[...] = _dot(hb, wu_ref[...]).astype(BF16)
    ga_ref[...] = _dot(hb, wga_ref[...])
    gb_ref[...] = _dot(hb, wgb_ref[...])
    g = _dot_f32(wgt_ref[...], h, nt=True) + bg_ref[...]
    nh = N_HEADS
    lf_f = _log_sigmoid(g[nh:2 * nh])
    lf_b = _log_sigmoid(g[3 * nh:4 * nh])
    for s in range(CHUNKS_PER_STEP):
        cols = slice(s * tc, (s + 1) * tc)
        gr_ref[s, 0:nh, :] = g[0:nh, cols]
        gr_ref[s, nh:2 * nh, :] = _dot_exact_rhs(lf_f[:, cols], tri_ref[0])
        gr_ref[s, 2 * nh:3 * nh, :] = g[2 * nh:3 * nh, cols]
        gr_ref[s, 3 * nh:4 * nh, :] = _dot_exact_rhs(lf_b[:, cols], tri_ref[1])


class _Streams(NamedTuple):
    n_ctx: int
    n_lat: int
    seq_lat: int

    def ctx_steps(self, tm):
        assert self.n_ctx % tm == 0 and self.seq_lat % tm == 0
        return self.n_ctx // tm

    def ctx_spec(self, tm, width):
        c = self.ctx_steps(tm)
        return pl.BlockSpec((tm, width), lambda i, *_: (jnp.minimum(i, c - 1), 0))

    def lat_spec(self, tm, width):
        c = self.ctx_steps(tm)
        return pl.BlockSpec((tm, width), lambda i, *_: (jnp.maximum(i - c, 0), 0))

    def pos_spec(self, tm, width):
        c = self.ctx_steps(tm)
        per_seq = self.seq_lat // tm
        return pl.BlockSpec((tm, width), lambda i, *_: (jnp.maximum(i - c, 0) % per_seq, 0))

    def mod_spec(self, tm, width):
        c = self.ctx_steps(tm)
        per_seq = self.seq_lat // tm
        return pl.BlockSpec((None, 6, width),
                            lambda i, *_: (jnp.where(i < c, 0, 1 + jnp.maximum(i - c, 0) // per_seq), 0, 0))


def _proj(st, x_ctx, x_lat, pos, mod, g_pre1, wts):
    d = x_ctx.shape[1]
    n_tok = st.n_ctx + st.n_lat
    tc = TOKEN_TILE
    ns = CHUNKS_PER_STEP
    tm = ns * tc
    n_tiles = n_tok // tc
    (wq, wkt, wv, wo, wu, wga, wgb, wgt, bg, tri) = wts
    row = pl.BlockSpec((tm, d), lambda i: (i, 0))
    in_specs = [st.ctx_spec(tm, d), st.lat_spec(tm, d), st.pos_spec(tm, d), st.mod_spec(tm, d), _const_spec((1, d))]
    in_specs += [_const_spec(w.shape) for w in wts]
    args = [x_ctx, x_lat, pos, mod, g_pre1.reshape(1, d)] + list(wts)
    n_gate_rows = 4 * N_HEADS
    du = wu.shape[1]
    out_shape = [jax.ShapeDtypeStruct((n_tok, d), BF16),
                 jax.ShapeDtypeStruct((n_tiles, d, tc), BF16),
                 jax.ShapeDtypeStruct((n_tok, d), BF16),
                 jax.ShapeDtypeStruct((n_tok, d), F32),
                 jax.ShapeDtypeStruct((n_tok, du), BF16),
                 jax.ShapeDtypeStruct((n_tok, d), F32),
                 jax.ShapeDtypeStruct((n_tok, d), F32),
                 jax.ShapeDtypeStruct((n_tiles, n_gate_rows, tc), F32)]
    out_specs = [row,
                 pl.BlockSpec((ns, d, tc), lambda i: (i, 0, 0)),
                 row, row,
                 pl.BlockSpec((tm, du), lambda i: (i, 0)),
                 row, row,
                 pl.BlockSpec((ns, n_gate_rows, tc), lambda i: (i, 0, 0))]
    return pl.pallas_call(
        functools.partial(_proj_kernel, st.ctx_steps(tm)),
        grid=(n_tok // tm,),
        in_specs=in_specs,
        out_specs=out_specs,
        out_shape=out_shape,
        compiler_params=_params(("parallel",)),
        name="proj",
    )(*args)


def _mlstm_kernel(n_chunks, hp, has_state, emit_state, *refs):
    it = iter(refs)
    if has_state:
        m0_ref = next(it)
    q_ref, kt_ref, v_ref, o_ref, ghn_ref, gr_ref = (next(it) for _ in range(6))
    if has_state:
        c0_ref, n0_ref = next(it), next(it)
    ha_ref = next(it)
    if emit_state:
        cout_ref, nout_ref, mout_ref = next(it), next(it), next(it)
    hs_ref, cn_ref, m_ref = (next(it) for _ in range(3))

    L = CHUNK
    nh, dh = N_HEADS, HEAD_DIM
    head0 = pl.program_id(1) * hp
    neg_inf = -jnp.inf

    sq_r = lax.broadcasted_iota(jnp.int32, (dh, dh), 0)
    sq_c = lax.broadcasted_iota(jnp.int32, (dh, dh), 1)
    for j in range(hp):
        for d in range(2):
            if has_state:
                n_col = jnp.sum(jnp.where(sq_r == sq_c, n0_ref[d, j], 0.0), axis=1, keepdims=True)
                cn_ref[j, d, :, 0:dh] = c0_ref[d, j]
                cn_ref[j, d, :, dh:2 * dh] = jnp.broadcast_to(n_col, (dh, dh))
                m0 = m0_ref[pl.program_id(0) * (2 * nh) + d * nh + head0 + j]
                m_ref[j, d] = jnp.full((8, LANES), m0, F32)
            else:
                cn_ref[j, d] = jnp.zeros((dh, 2 * dh), F32)
                m_ref[j, d] = jnp.zeros((8, LANES), F32)

    row_id = lax.broadcasted_iota(jnp.int32, (L, L), 0)
    col_id = lax.broadcasted_iota(jnp.int32, (L, L), 1)
    ones_blk = jnp.ones((L, dh), BF16)

    def chunk(j, d, c):
        t0 = pl.multiple_of(c * L, L)
        lanes = slice(j * dh, (j + 1) * dh)
        qc = q_ref[pl.ds(t0, L), lanes]
        ktc = kt_ref[c, lanes, :]
        vc = v_ref[pl.ds(t0, L), lanes]
        ig_row = gr_ref[c, pl.ds(2 * nh * d + head0 + j, 1), :]
        b_row = gr_ref[c, pl.ds(2 * nh * d + nh + head0 + j, 1), :]
        m_prev = m_ref[j, d][0:1, 0:1]
        if d == 0:
            b_end = b_row[:, L - 1:L]
            mask = col_id <= row_id
        else:
            b_end = b_row[:, 0:1]
            mask = col_id >= row_id
        a_row = ig_row - b_row
        a_max = jnp.max(jnp.where(mask, a_row, neg_inf), axis=1, keepdims=True)
        g = jnp.maximum(jnp.broadcast_to(a_max, (L, dh)), m_prev)
        b_col = jnp.broadcast_to(
            jnp.sum(jnp.where(row_id == col_id, b_row, 0.0), axis=1, keepdims=True), (L, dh))
        g_full = jnp.concatenate([g] * (L // dh), axis=1)
        s = _dot(qc, ktc) * jnp.exp(jnp.where(mask, a_row - g_full, neg_inf))
        decay = jnp.exp(m_prev - g)
        qcn = _dot(qc, cn_ref[j, d].astype(BF16))
        num = decay * qcn[:, 0:dh] + _dot(s.astype(BF16), vc)
        den = decay * qcn[:, dh:2 * dh] + jnp.sum(s, axis=1, keepdims=True)
        hch = num / jnp.maximum(jnp.abs(den), jnp.exp(-(b_col + g)))
        w_log = b_end + a_row
        m_new = jnp.maximum(b_end + m_prev, jnp.max(w_log, axis=1, keepdims=True))
        kw = (ktc.astype(F32) * jnp.exp(w_log - m_new)).astype(BF16)
        v_aug = jnp.concatenate([vc, ones_blk], axis=1)
        cn_ref[j, d] = jnp.exp(b_end + m_prev - m_new) * cn_ref[j, d] + _dot(kw, v_aug)
        m_ref[j, d] = jnp.broadcast_to(m_new, (8, LANES))
        return t0, hch

    if n_chunks == 1:
        for j in range(hp):
            _, h_f = chunk(j, 0, 0)
            _, h_b = chunk(j, 1, 0)
            hs_ref[:, j * dh:(j + 1) * dh] = h_f + h_b
    else:
        half = n_chunks // 2

        def first(i, carry):
            for j in range(hp):
                lanes = slice(j * dh, (j + 1) * dh)
                t_f, h_f = chunk(j, 0, i)
                t_b, h_b = chunk(j, 1, n_chunks - 1 - i)
                hs_ref[pl.ds(t_f, L), lanes] = h_f
                hs_ref[pl.ds(t_b, L), lanes] = h_b
            return carry

        def second(i, carry):
            for j in range(hp):
                lanes = slice(j * dh, (j + 1) * dh)
                t_f, h_f = chunk(j, 0, i)
                t_b, h_b = chunk(j, 1, n_chunks - 1 - i)
                hs_ref[pl.ds(t_f, L), lanes] += h_f
                hs_ref[pl.ds(t_b, L), lanes] += h_b
            return carry

        lax.fori_loop(0, half, first, 0)
        lax.fori_loop(half, n_chunks, second, 0)

    def finish(c, carry):
        t0 = pl.multiple_of(c * L, L)
        for j in range(hp):
            lanes = slice(j * dh, (j + 1) * dh)
            ha = hs_ref[pl.ds(t0, L), lanes]
            ha = ha * lax.rsqrt(jnp.mean(ha * ha, axis=-1, keepdims=True) + EPS)
            ha = ha * ghn_ref[:, lanes] * _sigmoid(o_ref[pl.ds(t0, L), lanes])
            ha_ref[pl.ds(t0, L), lanes] = ha.astype(BF16)
        return carry

    if n_chunks == 1:
        finish(0, 0)
    else:
        lax.fori_loop(0, n_chunks, finish, 0)

    if emit_state:
        for j in range(hp):
            for d in range(2):
                cout_ref[d, j] = cn_ref[j, d, :, 0:dh]
                n_rep = cn_ref[j, d, :, dh:2 * dh]
                nout_ref[d, j] = jnp.sum(jnp.where(sq_r == sq_c, n_rep, 0.0), axis=0, keepdims=True)
                mout_ref[d, j] = m_ref[j, d][0:1, :]


def _mlstm(q, kt, v, o, g_hn, gr, n_seq, seq_len, tok_off, states, hp):
    d = q.shape[1]
    n_tok = n_seq * seq_len
    L = CHUNK
    assert tok_off % seq_len == 0
    sb = tok_off // seq_len
    n_chunks = seq_len // L
    has_state = states is not None
    emit_state = not has_state
    nh, dh = N_HEADS, HEAD_DIM
    assert n_chunks == 1 or n_chunks % 2 == 0
    assert nh % hp == 0
    wd = hp * dh

    in_specs = []
    args = []
    if has_state:
        state_c, state_n, state_m = states
        in_specs.append(pl.BlockSpec(memory_space=pltpu.SMEM))
        args.append(state_m.reshape(-1))
    tok = pl.BlockSpec((seq_len, wd), lambda b, h: (b + sb, h))
    in_specs += [tok,
                 pl.BlockSpec((n_chunks, wd, L), lambda b, h: (b + sb, h, 0)),
                 tok, tok,
                 pl.BlockSpec((1, wd), lambda b, h: (0, h)),
                 pl.BlockSpec((n_chunks, 4 * nh, L), lambda b, h: (b + sb, 0, 0))]
    args += [q, kt, v, o, g_hn.reshape(1, d), gr]
    st_c = pl.BlockSpec((None, None, 2, hp, dh, dh), lambda b, h: (b, 0, 0, h, 0, 0))
    st_v = pl.BlockSpec((None, None, 2, hp, 1, dh), lambda b, h: (b, 0, 0, h, 0, 0))
    if has_state:
        in_specs += [st_c, st_v]
        args += [state_c, state_n.reshape(n_seq, 1, 2, nh, 1, dh)]
    out_shape = [jax.ShapeDtypeStruct((n_tok, d), BF16)]
    out_specs = [pl.BlockSpec((seq_len, wd), lambda b, h: (b, h))]
    if emit_state:
        out_shape += [jax.ShapeDtypeStruct((n_seq, 1, 2, nh, dh, dh), F32),
                      jax.ShapeDtypeStruct((n_seq, 1, 2, nh, 1, dh), F32),
                      jax.ShapeDtypeStruct((n_seq, 1, 2, nh, 1, LANES), F32)]
        out_specs += [st_c, st_v, st_v]
    scratch = [pltpu.VMEM((seq_len, wd), F32),
               pltpu.VMEM((hp, 2, dh, 2 * dh), F32),
               pltpu.VMEM((hp, 2, 8, LANES), F32)]
    return pl.pallas_call(
        functools.partial(_mlstm_kernel, n_chunks, hp, has_state, emit_state),
        grid=(n_seq, nh // hp),
        in_specs=in_specs,
        out_specs=out_specs,
        out_shape=out_shape,
        scratch_shapes=scratch,
        compiler_params=_params(("parallel", "parallel")),
        name="mlstm",
    )(*args)


def _fnet_kernel(seq_len, u_ref, f_ref, cs_ref, y_ref, ab_ref):
    T = seq_len

    @pl.when(pl.program_id(1) == 0)
    def _():
        for g in range(N_FGROUPS):
            ug = u_ref[:, g * FGROUP_DIM:(g + 1) * FGROUP_DIM]
            ab_ref[0:T, g * FGROUP_DIM:(g + 1) * FGROUP_DIM] = _dot(ug, cs_ref[0]).astype(BF16)
            ab_ref[T:2 * T, g * FGROUP_DIM:(g + 1) * FGROUP_DIM] = _dot(ug, cs_ref[1]).astype(BF16)

    scale = 1.0 / math.sqrt(T * FGROUP_DIM)
    y_ref[...] = (_dot(f_ref[...], ab_ref[...]) * scale).astype(BF16)


def _fnet(u, f_mat, cs_mat, n_seq, seq_len, tok_off):
    du = u.shape[1]
    n_tok = n_seq * seq_len
    tr = min(seq_len, 512)
    assert tok_off % seq_len == 0
    sb = tok_off // seq_len
    return pl.pallas_call(
        functools.partial(_fnet_kernel, seq_len),
        grid=(n_seq, seq_len // tr),
        in_specs=[pl.BlockSpec((seq_len, du), lambda b, r: (b + sb, 0)),
                  pl.BlockSpec((tr, 2 * seq_len), lambda b, r: (r, 0)),
                  _const_spec(cs_mat.shape)],
        out_specs=pl.BlockSpec((tr, du), lambda b, r: (b * (seq_len // tr) + r, 0)),
        out_shape=jax.ShapeDtypeStruct((n_tok, du), BF16),
        scratch_shapes=[pltpu.VMEM((2 * seq_len, du), BF16)],
        compiler_params=_params(("parallel", "arbitrary")),
        name="fnet",
    )(u, f_mat, cs_mat)


def _post_kernel(ctx_steps, xc_ref, xl_ref, pos_ref, hac_ref, hal_ref, yfc_ref, yfl_ref, ga_ref, gb_ref, mod_ref,
                 gp1_ref, gp2_ref, wa_ref, wf_ref, wout_ref, wr_ref, br_ref, x1_ref, h2_ref, lg_ref):
    is_ctx = pl.program_id(0) < ctx_steps
    x = jnp.where(is_ctx, xc_ref[...], xl_ref[...] + pos_ref[...])
    ya = _dot(jnp.where(is_ctx, hac_ref[...], hal_ref[...]), wa_ref[...])
    yf = _dot(jnp.where(is_ctx, yfc_ref[...], yfl_ref[...]), wf_ref[...])
    mix_in = _sigmoid(ga_ref[...]) * ya + _sigmoid(gb_ref[...]) * yf
    mix = _dot(mix_in.astype(BF16), wout_ref[...])
    gt1 = mod_ref[2:3, :]
    sh2 = mod_ref[3:4, :]
    sc2 = mod_ref[4:5, :]
    x1 = x + gt1 * _rms(mix, gp1_ref[...])
    h2 = _rms(x1, gp2_ref[...]) * (1.0 + sc2) + sh2
    x1_ref[...] = x1
    _store_rows(h2_ref, h2)
    lg = _dot_f32(wr_ref[...], h2, nt=True) + br_ref[...]
    for s in range(CHUNKS_PER_STEP):
        lg_ref[s] = lg[:, s * TOKEN_TILE:(s + 1) * TOKEN_TILE]


def _post(st, x_ctx, x_lat, pos, ha_ctx, ha_lat, yf_ctx, yf_lat, ga, gb, mod, g_post1, g_pre2, wts):
    d = x_ctx.shape[1]
    n_tok = st.n_ctx + st.n_lat
    tc = TOKEN_TILE
    ns = CHUNKS_PER_STEP
    tm = ns * tc
    (wa, wf, wout, wr, br) = wts
    du = yf_ctx.shape[1]
    row = pl.BlockSpec((tm, d), lambda i: (i, 0))
    in_specs = [st.ctx_spec(tm, d), st.lat_spec(tm, d), st.pos_spec(tm, d),
                st.ctx_spec(tm, d), st.lat_spec(tm, d), st.ctx_spec(tm, du), st.lat_spec(tm, du),
                row, row, st.mod_spec(tm, d), _const_spec((1, d)), _const_spec((1, d))]
    in_specs += [_const_spec(w.shape) for w in wts]
    args = [x_ctx, x_lat, pos, ha_ctx, ha_lat, yf_ctx, yf_lat, ga, gb, mod,
            g_post1.reshape(1, d), g_pre2.reshape(1, d)] + list(wts)
    return pl.pallas_call(
        functools.partial(_post_kernel, st.ctx_steps(tm)),
        grid=(n_tok // tm,),
        in_specs=in_specs,
        out_specs=[row, pl.BlockSpec((tm * SLABS, LANES), lambda i: (i, 0)),
                   pl.BlockSpec((ns, ROUTE_ROWS, tc), lambda i: (i, 0, 0))],
        out_shape=[jax.ShapeDtypeStruct((n_tok, d), F32),
                   jax.ShapeDtypeStruct((n_tok * SLABS, LANES), F32),
                   jax.ShapeDtypeStruct((n_tok // tc, ROUTE_ROWS, tc), F32)],
        compiler_params=_params(("parallel",)),
        name="post",
    )(*args)


ROUTE_ROWS = N_EXPERTS + 8
ROUTE_BLOCKS = 4
FFN_TILE = 512
PLAN_ITEMS = LANES


def _route_rows(lg):
    tm = lg.shape[1]
    ne = N_EXPERTS
    gl = lg[ne:ne + 8]
    g_id = lax.broadcasted_iota(jnp.int32, (8, tm), 0).astype(F32)
    is_g = g_id < N_GROUPS
    gl = jnp.where(is_g, gl, -jnp.inf)
    mx = jnp.max(gl, axis=0, keepdims=True)
    z = jnp.sum(jnp.where(is_g, jnp.exp(gl - mx), 0.0), axis=0, keepdims=True)
    p_sel = 1.0 / z
    g_sel = jnp.min(jnp.where(gl == mx, g_id, float(N_GROUPS)), axis=0, keepdims=True)
    e_id = lax.broadcasted_iota(jnp.int32, (ne, tm), 0).astype(F32)
    lo = EXPERTS_PER_GROUP * g_sel
    in_grp = (e_id >= lo) & (e_id < lo + EXPERTS_PER_GROUP)
    le = jnp.where(in_grp, lg[0:ne], -jnp.inf)
    v1 = jnp.max(le, axis=0, keepdims=True)
    i1 = jnp.min(jnp.where(le == v1, e_id, float(ne)), axis=0, keepdims=True)
    le2 = jnp.where(e_id == i1, -jnp.inf, le)
    v2 = jnp.max(le2, axis=0, keepdims=True)
    i2 = jnp.min(jnp.where(le2 == v2, e_id, float(ne)), axis=0, keepdims=True)
    e2 = jnp.exp(v2 - v1)
    w1 = p_sel / (1.0 + e2)
    w2 = p_sel * e2 / (1.0 + e2)
    return i1, i2, w1, w2


def _route_kernel(n_tiles, lg_ref, etri_ref, ttri_ref, pos_ref, w_ref, plan_ref, meta_ref, cnt_ref, run_ref, base_ref):
    p = pl.program_id(0)
    i = pl.program_id(1)
    tm = lg_ref.shape[2]
    sub = lax.broadcasted_iota(jnp.int32, (N_EXPERTS, tm), 0).astype(F32)
    tf = float(FFN_TILE)

    nb = ROUTE_BLOCKS

    @pl.when(p == 0)
    def _():
        @pl.when(i == 0)
        def _():
            cnt_ref[...] = jnp.zeros_like(cnt_ref)

        count = jnp.zeros((N_EXPERTS, 1), F32)
        for s in range(nb):
            t = i * nb + s
            e_a, e_b, w_a, w_b = _route_rows(lg_ref[s])
            meta_ref[t, 0:1, :] = e_a
            meta_ref[t, 1:2, :] = e_b
            meta_ref[t, 2:3, :] = w_a
            meta_ref[t, 3:4, :] = w_b
            meta_ref[t, 4:8, :] = jnp.zeros((4, tm), F32)
            oh = jnp.where((sub == e_a) | (sub == e_b), 1.0, 0.0)
            count = count + jnp.sum(oh, axis=1, keepdims=True)
        cnt_ref[...] += jnp.broadcast_to(count, cnt_ref.shape)

    @pl.when(p == 1)
    def _():
        @pl.when(i == 0)
        def _():
            c1, c2, c3 = _split3(cnt_ref[...])
            base_ref[...] = _dot(etri_ref[...], c1) + (_dot(etri_ref[...], c2) + _dot(etri_ref[...], c3))
            run_ref[...] = jnp.zeros_like(run_ref)

        start = run_ref[...] + base_ref[...]
        for s in range(nb):
            t = i * nb + s
            e_a = meta_ref[t, 0:1, :]
            e_b = meta_ref[t, 1:2, :]
            oh_a = jnp.where(sub == e_a, 1.0, 0.0)
            oh_b = jnp.where(sub == e_b, 1.0, 0.0)
            oh = oh_a + oh_b
            before = _dot(oh.astype(BF16), ttri_ref[...]) + start
            pos_ref[s] = jnp.zeros(pos_ref.shape[1:], jnp.int32)
            pos_ref[s, 0:1, :] = jnp.sum(oh_a * before, axis=0, keepdims=True).astype(jnp.int32)
            pos_ref[s, 1:2, :] = jnp.sum(oh_b * before, axis=0, keepdims=True).astype(jnp.int32)
            w_ref[s] = meta_ref[t]
            start = start + jnp.broadcast_to(jnp.sum(oh, axis=1, keepdims=True), start.shape)
        run_ref[...] = start - base_ref[...]

        @pl.when(i == n_tiles // nb - 1)
        def _():
            cnt = cnt_ref[:, 0:PLAN_ITEMS]
            base = base_ref[:, 0:PLAN_ITEMS]
            e_id = lax.broadcasted_iota(jnp.int32, (N_EXPERTS, PLAN_ITEMS), 0)
            t_lo = jnp.floor(base / tf)
            t_hi = jnp.floor((base + cnt - 1.0) / tf)
            n_items = jnp.where(cnt > 0.0, t_hi - t_lo + 1.0, 0.0)
            i_start = _dot(etri_ref[...], n_items.astype(BF16))
            item = lax.broadcasted_iota(jnp.int32, (N_EXPERTS, PLAN_ITEMS), 1).astype(F32)
            sel = (i_start <= item) & (item < i_start + n_items)
            tile_e = t_lo + item - i_start
            off = base - tile_e * tf

            def pick(v):
                return jnp.sum(jnp.where(sel, v, 0.0), axis=0, keepdims=True).astype(jnp.int32)

            plan_ref[...] = jnp.zeros(plan_ref.shape, jnp.int32)
            plan_ref[0:1, :] = pick(e_id.astype(F32))
            plan_ref[1:2, :] = pick(tile_e)
            plan_ref[2:3, :] = pick(jnp.maximum(off, 0.0))
            plan_ref[3:4, :] = pick(jnp.minimum(off + cnt, tf))
            plan_ref[4:5, :] = jnp.sum(n_items, axis=0, keepdims=True).astype(jnp.int32)


def _route(lg, tris):
    expert_tri, token_tri = tris
    n_tiles, _, tm = lg.shape
    nb = ROUTE_BLOCKS
    assert n_tiles % nb == 0
    return pl.pallas_call(
        functools.partial(_route_kernel, n_tiles),
        grid=(2, n_tiles // nb),
        in_specs=[pl.BlockSpec((nb, ROUTE_ROWS, tm), lambda p, i: (i, 0, 0)),
                  _const_spec(expert_tri.shape), _const_spec(token_tri.shape)],
        out_specs=[pl.BlockSpec((nb, 8, tm), lambda p, i: (i * p, 0, 0)),
                   pl.BlockSpec((nb, 8, tm), lambda p, i: (i * p, 0, 0)),
                   pl.BlockSpec((8, PLAN_ITEMS), lambda p, i: (0, 0))],
        out_shape=[jax.ShapeDtypeStruct((n_tiles, 8, tm), jnp.int32),
                   jax.ShapeDtypeStruct((n_tiles, 8, tm), F32),
                   jax.ShapeDtypeStruct((8, PLAN_ITEMS), jnp.int32)],
        scratch_shapes=[pltpu.VMEM((n_tiles, 8, tm), F32),
                        pltpu.VMEM((N_EXPERTS, tm), F32),
                        pltpu.VMEM((N_EXPERTS, tm), F32),
                        pltpu.VMEM((N_EXPERTS, tm), F32)],
        compiler_params=_params(("arbitrary", "arbitrary")),
        name="route",
    )(lg, expert_tri, token_tri)


ROW_UNROLL = 8


DISPATCH_BLOCKS = 4


def _dispatch_kernel(tm, pos_ref, h_ref, xs_ref, sem):
    def row_copy(t, dst):
        return pltpu.make_async_copy(_slab(h_ref, t), _slab(xs_ref, dst), sem)

    for blk in range(DISPATCH_BLOCKS):
        def issue(g, carry, blk=blk):
            for u in range(ROW_UNROLL):
                t = g * ROW_UNROLL + u
                row_copy(blk * tm + t, pos_ref[blk, 0, t]).start()
                row_copy(blk * tm + t, pos_ref[blk, 1, t]).start()
            return carry

        lax.fori_loop(0, tm // ROW_UNROLL, issue, 0)

    def drain(g, carry):
        for _ in range(2 * ROW_UNROLL):
            row_copy(0, 0).wait()
        return carry

    lax.fori_loop(0, DISPATCH_BLOCKS * tm // ROW_UNROLL, drain, 0)


def _dispatch(pos, h2, n_rows):
    n_tiles, _, tm = pos.shape
    nb = DISPATCH_BLOCKS
    assert n_tiles % nb == 0
    return pl.pallas_call(
        functools.partial(_dispatch_kernel, tm),
        grid=(n_tiles // nb,),
        in_specs=[pl.BlockSpec((nb, 8, tm), lambda i: (i, 0, 0), memory_space=pltpu.SMEM),
                  pl.BlockSpec((nb * tm * SLABS, LANES), lambda i: (i, 0))],
        out_specs=pl.BlockSpec(memory_space=pl.ANY),
        out_shape=jax.ShapeDtypeStruct((n_rows * SLABS, LANES), F32),
        scratch_shapes=[pltpu.SemaphoreType.DMA],
        compiler_params=_params(("arbitrary",)),
        name="dispatch",
    )(pos, h2)


def _ffn_kernel(owner_ref, tile_ref, lo_ref, hi_ref, used_ref, x_ref, w1_ref, w3_ref, w2_ref, y_ref,
                w1b_ref, w3b_ref, w2b_ref):
    w = pl.program_id(0)
    prev = jnp.maximum(w - 1, 0)

    @pl.when((w == 0) | (owner_ref[w] != owner_ref[prev]))
    def _():
        w1b_ref[...] = w1_ref[...].astype(BF16)
        w3b_ref[...] = w3_ref[...].astype(BF16)
        w2b_ref[...] = w2_ref[...].astype(BF16)

    @pl.when(w < used_ref[0])
    def _():
        x = _load_rows(x_ref, FFN_TILE).astype(BF16)
        a = _dot(x, w1b_ref[...])
        b = _dot(x, w3b_ref[...])
        hid = (a * _sigmoid(a)) * b
        y = _dot(hid.astype(BF16), w2b_ref[...])
        row = lax.broadcasted_iota(jnp.int32, y.shape, 0)
        mine = (row >= lo_ref[w]) & (row < hi_ref[w])
        first = (w == 0) | (tile_ref[w] != tile_ref[jnp.maximum(w - 1, 0)])

        @pl.when(first)
        def _():
            _store_rows(y_ref, jnp.where(mine, y, 0.0))

        @pl.when(jnp.logical_not(first))
        def _():
            _store_rows(y_ref, jnp.where(mine, y, _load_rows(y_ref, FFN_TILE)))


def _ffn(plan, xs, w1, w3, w2):
    n_rows = xs.shape[0] // SLABS
    d = SLABS * LANES
    tf = FFN_TILE
    de = w1.shape[1] // N_EXPERTS
    n_items = n_rows // tf + N_EXPERTS - 1
    assert n_items <= PLAN_ITEMS

    def item(w, used):
        return jnp.minimum(w, used[0] - 1)

    grid_spec = pltpu.PrefetchScalarGridSpec(
        num_scalar_prefetch=5,
        grid=(n_items,),
        in_specs=[pl.BlockSpec((tf * SLABS, LANES), lambda w, own, til, lo, hi, used: (til[item(w, used)], 0)),
                  pl.BlockSpec((d, de), lambda w, own, til, lo, hi, used: (0, own[item(w, used)])),
                  pl.BlockSpec((d, de), lambda w, own, til, lo, hi, used: (0, own[item(w, used)])),
                  pl.BlockSpec((de, d), lambda w, own, til, lo, hi, used: (own[item(w, used)], 0))],
        out_specs=pl.BlockSpec((tf * SLABS, LANES), lambda w, own, til, lo, hi, used: (til[item(w, used)], 0)),
        scratch_shapes=[pltpu.VMEM((d, de), BF16), pltpu.VMEM((d, de), BF16), pltpu.VMEM((de, d), BF16)],
    )
    return pl.pallas_call(
        _ffn_kernel,
        grid_spec=grid_spec,
        out_shape=jax.ShapeDtypeStruct((n_rows * SLABS, LANES), F32),
        compiler_params=_params(("arbitrary",)),
        name="ffn",
    )(plan[0], plan[1], plan[2], plan[3], plan[4, 0:1], xs, w1, w3, w2)


def _combine_kernel(tm, ctx_steps, pos_ref, nxt_ref, w_ref, x1_ref, mod_ref, gp_ref, ys_ref, yc_ref, yl_ref,
                    buf_ref, sem):
    i = pl.program_id(0)
    n = pl.num_programs(0)
    slot = i % 2

    def row_copy(src, k, t, s):
        return pltpu.make_async_copy(_slab(ys_ref, src), _slab(buf_ref.at[s, k], t), sem.at[s])

    def issue(p_ref, s):
        def body(g, carry):
            for u in range(ROW_UNROLL):
                t = g * ROW_UNROLL + u
                row_copy(p_ref[0, t], 0, t, s).start()
                row_copy(p_ref[1, t], 1, t, s).start()
            return carry
        lax.fori_loop(0, tm // ROW_UNROLL, body, 0)

    @pl.when(i == 0)
    def _():
        issue(pos_ref, 0)

    @pl.when(i + 1 < n)
    def _():
        issue(nxt_ref, 1 - slot)

    def drain(g, carry):
        for _ in range(2 * ROW_UNROLL):
            row_copy(0, 0, 0, slot).wait()
        return carry

    lax.fori_loop(0, tm // ROW_UNROLL, drain, 0)

    r = lax.broadcasted_iota(jnp.int32, (tm, tm), 0)
    c = lax.broadcasted_iota(jnp.int32, (tm, tm), 1)
    d = x1_ref.shape[1]
    wa = jnp.broadcast_to(jnp.sum(jnp.where(r == c, w_ref[2:3, :], 0.0), axis=1, keepdims=True), (tm, d))
    wb = jnp.broadcast_to(jnp.sum(jnp.where(r == c, w_ref[3:4, :], 0.0), axis=1, keepdims=True), (tm, d))
    moe = wa * _load_rows(buf_ref, tm, slot, 0) + wb * _load_rows(buf_ref, tm, slot, 1)
    gt2 = mod_ref[5:6, :]
    y = x1_ref[...] + gt2 * _rms(moe, gp_ref[...])

    @pl.when(i < ctx_steps)
    def _():
        yc_ref[...] = y

    @pl.when(i >= ctx_steps)
    def _():
        yl_ref[...] = y


def _combine(st, pos, w, x1, mod, g_post2, ys):
    n_tiles, _, tm = pos.shape
    d = x1.shape[1]
    row = pl.BlockSpec((tm, d), lambda i: (i, 0))
    return pl.pallas_call(
        functools.partial(_combine_kernel, tm, st.ctx_steps(tm)),
        grid=(n_tiles,),
        in_specs=[pl.BlockSpec((None, 8, tm), lambda i: (i, 0, 0), memory_space=pltpu.SMEM),
                  pl.BlockSpec((None, 8, tm), lambda i: (jnp.minimum(i + 1, n_tiles - 1), 0, 0),
                               memory_space=pltpu.SMEM),
                  pl.BlockSpec((None, 8, tm), lambda i: (i, 0, 0)),
                  row,
                  st.mod_spec(tm, d),
                  _const_spec((1, d)),
                  pl.BlockSpec(memory_space=pl.ANY)],
        out_specs=[st.ctx_spec(tm, d), st.lat_spec(tm, d)],
        out_shape=[jax.ShapeDtypeStruct((st.n_ctx, d), F32), jax.ShapeDtypeStruct((st.n_lat, d), F32)],
        scratch_shapes=[pltpu.VMEM((2, 2, tm * SLABS, LANES), F32), pltpu.SemaphoreType.DMA((2,))],
        compiler_params=_params(("arbitrary",)),
        name="combine",
    )(pos, pos, w, x1, mod, g_post2.reshape(1, d), ys)


def _moe(st, h2, lg, x1, mod, g_post2, w1, w3, w2, route_tri):
    n_tok = x1.shape[0]
    n_rows = 2 * n_tok
    assert n_rows % FFN_TILE == 0
    pos, w, plan = _route(lg, route_tri)
    xs = _dispatch(pos, h2, n_rows)
    ys = _ffn(plan, xs, w1, w3, w2)
    return _combine(st, pos, w, x1, mod, g_post2, ys)


def _grid_pos(n_tokens, d_model):
    rows = n_tokens // GRID_W
    row = np.repeat(np.arange(rows, dtype=np.float64), GRID_W)
    col = np.tile(np.arange(GRID_W, dtype=np.float64), rows)
    n_freq = d_model // 4
    freq = np.exp(-math.log(POS_BASE) * np.arange(n_freq, dtype=np.float64) / n_freq)

    def enc(p):
        a = p[:, None] * freq[None, :]
        return np.concatenate([np.sin(a), np.cos(a)], axis=-1)

    return jnp.asarray(np.concatenate([enc(row), enc(col)], axis=-1), dtype=F32)


def _dft_cos_sin(n):
    k = np.arange(n, dtype=np.int64)
    ang = 2.0 * np.pi * ((k[:, None] * k[None, :]) % n).astype(np.float64) / n
    return np.cos(ang), np.sin(ang)


def _fnet_consts(seq_len):
    ct, st = _dft_cos_sin(seq_len)
    f_mat = jnp.asarray(np.concatenate([ct, -st], axis=1), dtype=F32).astype(BF16)
    cc, sc = _dft_cos_sin(FGROUP_DIM)
    cs_mat = jnp.asarray(np.stack([cc, sc]), dtype=F32).astype(BF16)
    return f_mat, cs_mat


def _tri_consts():
    i = np.arange(CHUNK)
    prefix = (i[:, None] <= i[None, :]).astype(np.float32)
    suffix = (i[:, None] >= i[None, :]).astype(np.float32)
    tri = jnp.asarray(np.stack([prefix, suffix]), dtype=BF16)
    e = np.arange(N_EXPERTS)
    expert_tri = jnp.asarray((e[None, :] < e[:, None]).astype(np.float32), dtype=BF16)
    t = np.arange(TOKEN_TILE)
    token_tri = jnp.asarray((t[:, None] < t[None, :]).astype(np.float32), dtype=BF16)
    return tri, (expert_tri, token_tri)


def _layer(x_prompt, x_sample, pos, mod, states, lw, consts):
    bp, sp, d = x_prompt.shape
    bs, ss, _ = x_sample.shape
    st = _Streams(n_ctx=bp * sp, n_lat=bs * ss, seq_lat=ss)
    (g_pre1, proj_w, g_hn, post_w, g_post1, g_pre2, w_e1, w_e3, w_e2, g_post2) = lw
    tri, route_tris = consts
    x_ctx = x_prompt.reshape(st.n_ctx, d)
    x_lat = x_sample.reshape(st.n_lat, d)
    q, kt, v, o, u, ga, gb, gr = _proj(st, x_ctx, x_lat, pos, mod, g_pre1, proj_w + (tri,))
    ha_ctx, new_c, new_n, new_m = _mlstm(q, kt, v, o, g_hn, gr, bp, sp, 0, None, MLSTM_HEADS_PER_STEP)
    (ha_lat,) = _mlstm(q, kt, v, o, g_hn, gr, bs, ss, st.n_ctx, states, MLSTM_HEADS_PER_STEP)
    yf_ctx = _fnet(u, *_fnet_consts(sp), bp, sp, 0)
    yf_lat = _fnet(u, *_fnet_consts(ss), bs, ss, st.n_ctx)
    x1, h2, logits = _post(st, x_ctx, x_lat, pos, ha_ctx, ha_lat, yf_ctx, yf_lat, ga, gb, mod, g_post1, g_pre2, post_w)
    y_ctx, y_lat = _moe(st, h2, logits, x1, mod, g_post2, w_e1, w_e3, w_e2, route_tris)
    return y_ctx.reshape(bp, sp, d), y_lat.reshape(bs, ss, d), (new_c, new_n, new_m)


def kernel(x_prompt, x_sample, c, state_C, state_n, state_m, c_ctx, w_ada, b_ada, g_pre1, w_in, b_gates, g_hn,
           w_a, w_f, w_out, g_post1, g_pre2, w_rg, b_rg, w_re, b_re, w_e1, w_e3, w_e2, g_post2):
    bp, sp, d = x_prompt.shape
    bs, ss, _ = x_sample.shape
    depth = w_in.shape[0]
    assert depth == 1
    l = 0
    dm = N_HEADS * HEAD_DIM
    du = N_FGROUPS * FGROUP_DIM
    ng = 4 * N_HEADS

    n_rows = 8
    assert 1 + bs <= n_rows
    cs = jnp.concatenate([c_ctx[None, :], c, jnp.zeros((n_rows - 1 - bs, d), F32)], axis=0)
    mod = _ada(cs, w_ada[l], b_ada[l]).reshape(n_rows, 6, d)

    wi = w_in[l]
    cuts = np.cumsum([dm] * 4 + [ng, du, d]).tolist()
    wq = wi[:, :cuts[0]].astype(BF16)
    wkt = wi[:, cuts[0]:cuts[1]].T.astype(BF16)
    wv = wi[:, cuts[1]:cuts[2]].astype(BF16)
    wo = wi[:, cuts[2]:cuts[3]].astype(BF16)
    wgt = wi[:, cuts[3]:cuts[4]].T
    wu = wi[:, cuts[4]:cuts[5]].astype(BF16)
    wga = wi[:, cuts[5]:cuts[6]].astype(BF16)
    wgb = wi[:, cuts[6]:].astype(BF16)
    bg = b_gates[l].reshape(ng, 1)
    proj_w = (wq, wkt, wv, wo, wu, wga, wgb, wgt, bg)
    n_pad = ROUTE_ROWS - N_EXPERTS - N_GROUPS
    wr = jnp.concatenate([w_re[l], w_rg[l], jnp.zeros((d, n_pad), F32)], axis=1).T
    br = jnp.concatenate([b_re[l], b_rg[l], jnp.zeros((n_pad,), F32)]).reshape(ROUTE_ROWS, 1)
    post_w = (w_a[l].astype(BF16), w_f[l].astype(BF16), w_out[l].astype(BF16), wr, br)
    lw = (g_pre1[l], proj_w, g_hn[l], post_w, g_post1[l], g_pre2[l], w_e1[l], w_e3[l], w_e2[l], g_post2[l])
    consts = _tri_consts()

    pos = _grid_pos(ss, d)
    y_prompt, y_sample, (new_c, new_n, new_m) = _layer(x_prompt, x_sample, pos, mod,
                                                       (state_C, state_n, state_m), lw, consts)

    new_state_c = new_c
    new_state_n = new_n.reshape(bp, depth, 2, N_HEADS, HEAD_DIM)
    new_state_m = new_m[..., 0, 0]
    return (y_prompt, y_sample, new_state_c, new_state_n, new_state_m)
```

```python
import functools
import math
from typing import NamedTuple

import numpy as np
import jax
import jax.numpy as jnp
from jax import lax
from jax.experimental import pallas as pl
from jax.experimental.pallas import tpu as pltpu

F32 = jnp.float32
BF16 = jnp.bfloat16

N_HEADS = 8
HEAD_DIM = 128
N_FGROUPS = 4
FGROUP_DIM = 128
N_GROUPS = 4
EXPERTS_PER_GROUP = 8
N_EXPERTS = N_GROUPS * EXPERTS_PER_GROUP
GRID_W = 64
POS_BASE = 10000.0
EPS = 1e-6

V7X_VMEM_LIMIT_BYTES = 56 * 1024 * 1024
LANES = 128
CHUNK = 256
TOKEN_TILE = CHUNK
CHUNKS_PER_STEP = 2
MLSTM_HEADS_PER_STEP = 4


def _params(sem):
    return pltpu.CompilerParams(dimension_semantics=sem, vmem_limit_bytes=V7X_VMEM_LIMIT_BYTES)


def _const_spec(shape):
    n = len(shape)
    return pl.BlockSpec(shape, lambda *_: (0,) * n, pipeline_mode=pl.Buffered(1))


def _split2(x):
    hi = x.astype(BF16)
    lo = (x - hi.astype(F32)).astype(BF16)
    return hi, lo


def _split3(x):
    hi = x.astype(BF16)
    r = x - hi.astype(F32)
    mid = r.astype(BF16)
    lo = (r - mid.astype(F32)).astype(BF16)
    return hi, mid, lo


def _dot(a, b):
    return jnp.dot(a, b, preferred_element_type=F32)


def _dot_nt(a, b):
    return lax.dot_general(a, b, (((1,), (1,)), ((), ())), preferred_element_type=F32)


def _dot_f32(a, b, nt=False):
    d = _dot_nt if nt else _dot
    a1, a2 = _split2(a)
    b1, b2 = _split2(b)
    return d(a1, b1) + (d(a1, b2) + d(a2, b1))


def _dot_exact_rhs(a, b_bf16, nt=False):
    d = _dot_nt if nt else _dot
    a1, a2, a3 = _split3(a)
    return d(a1, b_bf16) + (d(a2, b_bf16) + d(a3, b_bf16))


SLABS = 8


def _load_rows(ref, n, *lead):
    return jnp.concatenate([ref[(*lead, pl.ds(j, n, stride=SLABS), slice(None))] for j in range(SLABS)], axis=1)


def _store_rows(ref, val, *lead):
    n = val.shape[0]
    for j in range(SLABS):
        ref[(*lead, pl.ds(j, n, stride=SLABS), slice(None))] = val[:, j * LANES:(j + 1) * LANES]


def _slab(ref, row):
    return ref.at[pl.ds(pl.multiple_of(row * SLABS, SLABS), SLABS)]


def _rms(x, g):
    return x * lax.rsqrt(jnp.mean(x * x, axis=-1, keepdims=True) + EPS) * g


def _sigmoid(x):
    return 1.0 / (1.0 + jnp.exp(-x))


def _log_sigmoid(x):
    return jnp.minimum(x, 0.0) - jnp.log(1.0 + jnp.exp(-jnp.abs(x)))


def _ada_kernel(c_ref, w_ref, b_ref, o_ref):
    c = c_ref[...]
    s = (c * _sigmoid(c)).astype(BF16)
    o_ref[...] = _dot(s, w_ref[...].astype(BF16)) + b_ref[...]


def _ada(cs, w_ada, b_ada):
    rows, d = cs.shape
    n = w_ada.shape[1]
    tn = 1024
    return pl.pallas_call(
        _ada_kernel,
        grid=(n // tn,),
        in_specs=[pl.BlockSpec((rows, d), lambda j: (0, 0)),
                  pl.BlockSpec((d, tn), lambda j: (0, j)),
                  pl.BlockSpec((1, tn), lambda j: (0, j))],
        out_specs=pl.BlockSpec((rows, tn), lambda j: (0, j)),
        out_shape=jax.ShapeDtypeStruct((rows, n), F32),
        compiler_params=_params(("arbitrary",)),
        name="ada",
    )(cs, w_ada, b_ada.reshape(1, n))


def _proj_kernel(ctx_steps, xc_ref, xl_ref, pos_ref, mod_ref, g_ref, wq_ref, wkt_ref, wv_ref, wo_ref, wu_ref,
                 wga_ref, wgb_ref, wgt_ref, bg_ref, tri_ref,
                 q_ref, kt_ref, v_ref, o_ref, u_ref, ga_ref, gb_ref, gr_ref):
    x = jnp.where(pl.program_id(0) < ctx_steps, xc_ref[...], xl_ref[...] + pos_ref[...])
    sh = mod_ref[0:1, :]
    sc = mod_ref[1:2, :]
    h = _rms(x, g_ref[...]) * (1.0 + sc) + sh
    hb = h.astype(BF16)
    q_ref[...] = _dot(hb, wq_ref[...]).astype(BF16)
    kt = (_dot_nt(wkt_ref[...], hb) * (HEAD_DIM ** -0.5)).astype(BF16)
    tc = TOKEN_TILE
    for s in range(CHUNKS_PER_STEP):
        kt_ref[s] = kt[:, s * tc:(s + 1) * tc]
    v_ref[...] = _dot(hb, wv_ref[...]).astype(BF16)
    o_ref[...] = _dot(hb, wo_ref[...])
    u_ref[...] = _dot(hb, wu_ref[...]).astype(BF16)
    ga_ref[...] = _dot(hb, wga_ref[...])
    gb_ref[...] = _dot(hb, wgb_ref[...])
    g = _dot_f32(wgt_ref[...], h, nt=True) + bg_ref[...]
    nh = N_HEADS
    lf_f = _log_sigmoid(g[nh:2 * nh])
    lf_b = _log_sigmoid(g[3 * nh:4 * nh])
    for s in range(CHUNKS_PER_STEP):
        cols = slice(s * tc, (s + 1) * tc)
        gr_ref[s, 0:nh, :] = g[0:nh, cols]
        gr_ref[s, nh:2 * nh, :] = _dot_exact_rhs(lf_f[:, cols], tri_ref[0])
        gr_ref[s, 2 * nh:3 * nh, :] = g[2 * nh:3 * nh, cols]
        gr_ref[s, 3 * nh:4 * nh, :] = _dot_exact_rhs(lf_b[:, cols], tri_ref[1])


class _Streams(NamedTuple):
    n_ctx: int
    n_lat: int
    seq_lat: int

    def ctx_steps(self, tm):
        assert self.n_ctx % tm == 0 and self.seq_lat % tm == 0
        return self.n_ctx // tm

    def ctx_spec(self, tm, width):
        c = self.ctx_steps(tm)
        return pl.BlockSpec((tm, width), lambda i, *_: (jnp.minimum(i, c - 1), 0))

    def lat_spec(self, tm, width):
        c = self.ctx_steps(tm)
        return pl.BlockSpec((tm, width), lambda i, *_: (jnp.maximum(i - c, 0), 0))

    def pos_spec(self, tm, width):
        c = self.ctx_steps(tm)
        per_seq = self.seq_lat // tm
        return pl.BlockSpec((tm, width), lambda i, *_: (jnp.maximum(i - c, 0) % per_seq, 0))

    def mod_spec(self, tm, width):
        c = self.ctx_steps(tm)
        per_seq = self.seq_lat // tm
        return pl.BlockSpec((None, 6, width),
                            lambda i, *_: (jnp.where(i < c, 0, 1 + jnp.maximum(i - c, 0) // per_seq), 0, 0))


def _proj(st, x_ctx, x_lat, pos, mod, g_pre1, wts):
    d = x_ctx.shape[1]
    n_tok = st.n_ctx + st.n_lat
    tc = TOKEN_TILE
    ns = CHUNKS_PER_STEP
    tm = ns * tc
    n_tiles = n_tok // tc
    (wq, wkt, wv, wo, wu, wga, wgb, wgt, bg, tri) = wts
    row = pl.BlockSpec((tm, d), lambda i: (i, 0))
    in_specs = [st.ctx_spec(tm, d), st.lat_spec(tm, d), st.pos_spec(tm, d), st.mod_spec(tm, d), _const_spec((1, d))]
    in_specs += [_const_spec(w.shape) for w in wts]
    args = [x_ctx, x_lat, pos, mod, g_pre1.reshape(1, d)] + list(wts)
    n_gate_rows = 4 * N_HEADS
    du = wu.shape[1]
    out_shape = [jax.ShapeDtypeStruct((n_tok, d), BF16),
                 jax.ShapeDtypeStruct((n_tiles, d, tc), BF16),
                 jax.ShapeDtypeStruct((n_tok, d), BF16),
                 jax.ShapeDtypeStruct((n_tok, d), F32),
                 jax.ShapeDtypeStruct((n_tok, du), BF16),
                 jax.ShapeDtypeStruct((n_tok, d), F32),
                 jax.ShapeDtypeStruct((n_tok, d), F32),
                 jax.ShapeDtypeStruct((n_tiles, n_gate_rows, tc), F32)]
    out_specs = [row,
                 pl.BlockSpec((ns, d, tc), lambda i: (i, 0, 0)),
                 row, row,
                 pl.BlockSpec((tm, du), lambda i: (i, 0)),
                 row, row,
                 pl.BlockSpec((ns, n_gate_rows, tc), lambda i: (i, 0, 0))]
    return pl.pallas_call(
        functools.partial(_proj_kernel, st.ctx_steps(tm)),
        grid=(n_tok // tm,),
        in_specs=in_specs,
        out_specs=out_specs,
        out_shape=out_shape,
        compiler_params=_params(("parallel",)),
        name="proj",
    )(*args)


def _mlstm_kernel(n_chunks, hp, has_state, emit_state, *refs):
    it = iter(refs)
    if has_state:
        m0_ref = next(it)
    q_ref, kt_ref, v_ref, o_ref, ghn_ref, gr_ref = (next(it) for _ in range(6))
    if has_state:
        c0_ref, n0_ref = next(it), next(it)
    ha_ref = next(it)
    if emit_state:
        cout_ref, nout_ref, mout_ref = next(it), next(it), next(it)
    hs_ref, cn_ref, m_ref = (next(it) for _ in range(3))

    L = CHUNK
    nh, dh = N_HEADS, HEAD_DIM
    head0 = pl.program_id(1) * hp
    neg_inf = -jnp.inf

    sq_r = lax.broadcasted_iota(jnp.int32, (dh, dh), 0)
    sq_c = lax.broadcasted_iota(jnp.int32, (dh, dh), 1)
    for j in range(hp):
        for d in range(2):
            if has_state:
                n_col = jnp.sum(jnp.where(sq_r == sq_c, n0_ref[d, j], 0.0), axis=1, keepdims=True)
                cn_ref[j, d, :, 0:dh] = c0_ref[d, j]
                cn_ref[j, d, :, dh:2 * dh] = jnp.broadcast_to(n_col, (dh, dh))
                m0 = m0_ref[pl.program_id(0) * (2 * nh) + d * nh + head0 + j]
                m_ref[j, d] = jnp.full((8, LANES), m0, F32)
            else:
                cn_ref[j, d] = jnp.zeros((dh, 2 * dh), F32)
                m_ref[j, d] = jnp.zeros((8, LANES), F32)

    row_id = lax.broadcasted_iota(jnp.int32, (L, L), 0)
    col_id = lax.broadcasted_iota(jnp.int32, (L, L), 1)
    ones_blk = jnp.ones((L, dh), BF16)

    def chunk(j, d, c):
        t0 = pl.multiple_of(c * L, L)
        lanes = slice(j * dh, (j + 1) * dh)
        qc = q_ref[pl.ds(t0, L), lanes]
        ktc = kt_ref[c, lanes, :]
        vc = v_ref[pl.ds(t0, L), lanes]
        ig_row = gr_ref[c, pl.ds(2 * nh * d + head0 + j, 1), :]
        b_row = gr_ref[c, pl.ds(2 * nh * d + nh + head0 + j, 1), :]
        m_prev = m_ref[j, d][0:1, 0:1]
        if d == 0:
            b_end = b_row[:, L - 1:L]
            mask = col_id <= row_id
        else:
            b_end = b_row[:, 0:1]
            mask = col_id >= row_id
        a_row = ig_row - b_row
        a_max = jnp.max(jnp.where(mask, a_row, neg_inf), axis=1, keepdims=True)
        g = jnp.maximum(jnp.broadcast_to(a_max, (L, dh)), m_prev)
        b_col = jnp.broadcast_to(
            jnp.sum(jnp.where(row_id == col_id, b_row, 0.0), axis=1, keepdims=True), (L, dh))
        g_full = jnp.concatenate([g] * (L // dh), axis=1)
        s = _dot(qc, ktc) * jnp.exp(jnp.where(mask, a_row - g_full, neg_inf))
        decay = jnp.exp(m_prev - g)
        qcn = _dot(qc, cn_ref[j, d].astype(BF16))
        num = decay * qcn[:, 0:dh] + _dot(s.astype(BF16), vc)
        den = decay * qcn[:, dh:2 * dh] + jnp.sum(s, axis=1, keepdims=True)
        hch = num / jnp.maximum(jnp.abs(den), jnp.exp(-(b_col + g)))
        w_log = b_end + a_row
        m_new = jnp.maximum(b_end + m_prev, jnp.max(w_log, axis=1, keepdims=True))
        kw = (ktc.astype(F32) * jnp.exp(w_log - m_new)).astype(BF16)
        v_aug = jnp.concatenate([vc, ones_blk], axis=1)
        cn_ref[j, d] = jnp.exp(b_end + m_prev - m_new) * cn_ref[j, d] + _dot(kw, v_aug)
        m_ref[j, d] = jnp.broadcast_to(m_new, (8, LANES))
        return t0, hch

    if n_chunks == 1:
        for j in range(hp):
            _, h_f = chunk(j, 0, 0)
            _, h_b = chunk(j, 1, 0)
            hs_ref[:, j * dh:(j + 1) * dh] = h_f + h_b
    else:
        half = n_chunks // 2

        def first(i, carry):
            for j in range(hp):
                lanes = slice(j * dh, (j + 1) * dh)
                t_f, h_f = chunk(j, 0, i)
                t_b, h_b = chunk(j, 1, n_chunks - 1 - i)
                hs_ref[pl.ds(t_f, L), lanes] = h_f
                hs_ref[pl.ds(t_b, L), lanes] = h_b
            return carry

        def second(i, carry):
            for j in range(hp):
                lanes = slice(j * dh, (j + 1) * dh)
                t_f, h_f = chunk(j, 0, i)
                t_b, h_b = chunk(j, 1, n_chunks - 1 - i)
                hs_ref[pl.ds(t_f, L), lanes] += h_f
                hs_ref[pl.ds(t_b, L), lanes] += h_b
            return carry

        lax.fori_loop(0, half, first, 0)
        lax.fori_loop(half, n_chunks, second, 0)

    def finish(c, carry):
        t0 = pl.multiple_of(c * L, L)
        for j in range(hp):
            lanes = slice(j * dh, (j + 1) * dh)
            ha = hs_ref[pl.ds(t0, L), lanes]
            ha = ha * lax.rsqrt(jnp.mean(ha * ha, axis=-1, keepdims=True) + EPS)
            ha = ha * ghn_ref[:, lanes] * _sigmoid(o_ref[pl.ds(t0, L), lanes])
            ha_ref[pl.ds(t0, L), lanes] = ha.astype(BF16)
        return carry

    if n_chunks == 1:
        finish(0, 0)
    else:
        lax.fori_loop(0, n_chunks, finish, 0)

    if emit_state:
        for j in range(hp):
            for d in range(2):
                cout_ref[d, j] = cn_ref[j, d, :, 0:dh]
                n_rep = cn_ref[j, d, :, dh:2 * dh]
                nout_ref[d, j] = jnp.sum(jnp.where(sq_r == sq_c, n_rep, 0.0), axis=0, keepdims=True)
                mout_ref[d, j] = m_ref[j, d][0:1, :]


def _mlstm(q, kt, v, o, g_hn, gr, n_seq, seq_len, tok_off, states, hp):
    d = q.shape[1]
    n_tok = n_seq * seq_len
    L = CHUNK
    assert tok_off % seq_len == 0
    sb = tok_off // seq_len
    n_chunks = seq_len // L
    has_state = states is not None
    emit_state = not has_state
    nh, dh = N_HEADS, HEAD_DIM
    assert n_chunks == 1 or n_chunks % 2 == 0
    assert nh % hp == 0
    wd = hp * dh

    in_specs = []
    args = []
    if has_state:
        state_c, state_n, state_m = states
        in_specs.append(pl.BlockSpec(memory_space=pltpu.SMEM))
        args.append(state_m.reshape(-1))
    tok = pl.BlockSpec((seq_len, wd), lambda b, h: (b + sb, h))
    in_specs += [tok,
                 pl.BlockSpec((n_chunks, wd, L), lambda b, h: (b + sb, h, 0)),
                 tok, tok,
                 pl.BlockSpec((1, wd), lambda b, h: (0, h)),
                 pl.BlockSpec((n_chunks, 4 * nh, L), lambda b, h: (b + sb, 0, 0))]
    args += [q, kt, v, o, g_hn.reshape(1, d), gr]
    st_c = pl.BlockSpec((None, None, 2, hp, dh, dh), lambda b, h: (b, 0, 0, h, 0, 0))
    st_v = pl.BlockSpec((None, None, 2, hp, 1, dh), lambda b, h: (b, 0, 0, h, 0, 0))
    if has_state:
        in_specs += [st_c, st_v]
        args += [state_c, state_n.reshape(n_seq, 1, 2, nh, 1, dh)]
    out_shape = [jax.ShapeDtypeStruct((n_tok, d), BF16)]
    out_specs = [pl.BlockSpec((seq_len, wd), lambda b, h: (b, h))]
    if emit_state:
        out_shape += [jax.ShapeDtypeStruct((n_seq, 1, 2, nh, dh, dh), F32),
                      jax.ShapeDtypeStruct((n_seq, 1, 2, nh, 1, dh), F32),
                      jax.ShapeDtypeStruct((n_seq, 1, 2, nh, 1, LANES), F32)]
        out_specs += [st_c, st_v, st_v]
    scratch = [pltpu.VMEM((seq_len, wd), F32),
               pltpu.VMEM((hp, 2, dh, 2 * dh), F32),
               pltpu.VMEM((hp, 2, 8, LANES), F32)]
    return pl.pallas_call(
        functools.partial(_mlstm_kernel, n_chunks, hp, has_state, emit_state),
        grid=(n_seq, nh // hp),
        in_specs=in_specs,
        out_specs=out_specs,
        out_shape=out_shape,
        scratch_shapes=scratch,
        compiler_params=_params(("parallel", "parallel")),
        name="mlstm",
    )(*args)


def _fnet_kernel(seq_len, u_ref, f_ref, cs_ref, y_ref, ab_ref):
    T = seq_len

    @pl.when(pl.program_id(1) == 0)
    def _():
        for g in range(N_FGROUPS):
            ug = u_ref[:, g * FGROUP_DIM:(g + 1) * FGROUP_DIM]
            ab_ref[0:T, g * FGROUP_DIM:(g + 1) * FGROUP_DIM] = _dot(ug, cs_ref[0]).astype(BF16)
            ab_ref[T:2 * T, g * FGROUP_DIM:(g + 1) * FGROUP_DIM] = _dot(ug, cs_ref[1]).astype(BF16)

    scale = 1.0 / math.sqrt(T * FGROUP_DIM)
    y_ref[...] = (_dot(f_ref[...], ab_ref[...]) * scale).astype(BF16)


def _fnet(u, f_mat, cs_mat, n_seq, seq_len, tok_off):
    du = u.shape[1]
    n_tok = n_seq * seq_len
    tr = min(seq_len, 512)
    assert tok_off % seq_len == 0
    sb = tok_off // seq_len
    return pl.pallas_call(
        functools.partial(_fnet_kernel, seq_len),
        grid=(n_seq, seq_len // tr),
        in_specs=[pl.BlockSpec((seq_len, du), lambda b, r: (b + sb, 0)),
                  pl.BlockSpec((tr, 2 * seq_len), lambda b, r: (r, 0)),
                  _const_spec(cs_mat.shape)],
        out_specs=pl.BlockSpec((tr, du), lambda b, r: (b * (seq_len // tr) + r, 0)),
        out_shape=jax.ShapeDtypeStruct((n_tok, du), BF16),
        scratch_shapes=[pltpu.VMEM((2 * seq_len, du), BF16)],
        compiler_params=_params(("parallel", "arbitrary")),
        name="fnet",
    )(u, f_mat, cs_mat)


def _post_kernel(ctx_steps, xc_ref, xl_ref, pos_ref, hac_ref, hal_ref, yfc_ref, yfl_ref, ga_ref, gb_ref, mod_ref,
                 gp1_ref, gp2_ref, wa_ref, wf_ref, wout_ref, wr_ref, br_ref, x1_ref, h2_ref, lg_ref):
    is_ctx = pl.program_id(0) < ctx_steps
    x = jnp.where(is_ctx, xc_ref[...], xl_ref[...] + pos_ref[...])
    ya = _dot(jnp.where(is_ctx, hac_ref[...], hal_ref[...]), wa_ref[...])
    yf = _dot(jnp.where(is_ctx, yfc_ref[...], yfl_ref[...]), wf_ref[...])
    mix_in = _sigmoid(ga_ref[...]) * ya + _sigmoid(gb_ref[...]) * yf
    mix = _dot(mix_in.astype(BF16), wout_ref[...])
    gt1 = mod_ref[2:3, :]
    sh2 = mod_ref[3:4, :]
    sc2 = mod_ref[4:5, :]
    x1 = x + gt1 * _rms(mix, gp1_ref[...])
    h2 = _rms(x1, gp2_ref[...]) * (1.0 + sc2) + sh2
    x1_ref[...] = x1
    _store_rows(h2_ref, h2)
    lg = _dot_f32(wr_ref[...], h2, nt=True) + br_ref[...]
    for s in range(CHUNKS_PER_STEP):
        lg_ref[s] = lg[:, s * TOKEN_TILE:(s + 1) * TOKEN_TILE]


def _post(st, x_ctx, x_lat, pos, ha_ctx, ha_lat, yf_ctx, yf_lat, ga, gb, mod, g_post1, g_pre2, wts):
    d = x_ctx.shape[1]
    n_tok = st.n_ctx + st.n_lat
    tc = TOKEN_TILE
    ns = CHUNKS_PER_STEP
    tm = ns * tc
    (wa, wf, wout, wr, br) = wts
    du = yf_ctx.shape[1]
    row = pl.BlockSpec((tm, d), lambda i: (i, 0))
    in_specs = [st.ctx_spec(tm, d), st.lat_spec(tm, d), st.pos_spec(tm, d),
                st.ctx_spec(tm, d), st.lat_spec(tm, d), st.ctx_spec(tm, du), st.lat_spec(tm, du),
                row, row, st.mod_spec(tm, d), _const_spec((1, d)), _const_spec((1, d))]
    in_specs += [_const_spec(w.shape) for w in wts]
    args = [x_ctx, x_lat, pos, ha_ctx, ha_lat, yf_ctx, yf_lat, ga, gb, mod,
            g_post1.reshape(1, d), g_pre2.reshape(1, d)] + list(wts)
    return pl.pallas_call(
        functools.partial(_post_kernel, st.ctx_steps(tm)),
        grid=(n_tok // tm,),
        in_specs=in_specs,
        out_specs=[row, pl.BlockSpec((tm * SLABS, LANES), lambda i: (i, 0)),
                   pl.BlockSpec((ns, ROUTE_ROWS, tc), lambda i: (i, 0, 0))],
        out_shape=[jax.ShapeDtypeStruct((n_tok, d), F32),
                   jax.ShapeDtypeStruct((n_tok * SLABS, LANES), F32),
                   jax.ShapeDtypeStruct((n_tok // tc, ROUTE_ROWS, tc), F32)],
        compiler_params=_params(("parallel",)),
        name="post",
    )(*args)


ROUTE_ROWS = N_EXPERTS + 8
ROUTE_BLOCKS = 4
FFN_TILE = 512
PLAN_ITEMS = LANES


def _route_rows(lg):
    tm = lg.shape[1]
    ne = N_EXPERTS
    gl = lg[ne:ne + 8]
    g_id = lax.broadcasted_iota(jnp.int32, (8, tm), 0).astype(F32)
    is_g = g_id < N_GROUPS
    gl = jnp.where(is_g, gl, -jnp.inf)
    mx = jnp.max(gl, axis=0, keepdims=True)
    z = jnp.sum(jnp.where(is_g, jnp.exp(gl - mx), 0.0), axis=0, keepdims=True)
    p_sel = 1.0 / z
    g_sel = jnp.min(jnp.where(gl == mx, g_id, float(N_GROUPS)), axis=0, keepdims=True)
    e_id = lax.broadcasted_iota(jnp.int32, (ne, tm), 0).astype(F32)
    lo = EXPERTS_PER_GROUP * g_sel
    in_grp = (e_id >= lo) & (e_id < lo + EXPERTS_PER_GROUP)
    le = jnp.where(in_grp, lg[0:ne], -jnp.inf)
    v1 = jnp.max(le, axis=0, keepdims=True)
    i1 = jnp.min(jnp.where(le == v1, e_id, float(ne)), axis=0, keepdims=True)
    le2 = jnp.where(e_id == i1, -jnp.inf, le)
    v2 = jnp.max(le2, axis=0, keepdims=True)
    i2 = jnp.min(jnp.where(le2 == v2, e_id, float(ne)), axis=0, keepdims=True)
    e2 = jnp.exp(v2 - v1)
    w1 = p_sel / (1.0 + e2)
    w2 = p_sel * e2 / (1.0 + e2)
    return i1, i2, w1, w2


def _route_kernel(n_tiles, lg_ref, etri_ref, ttri_ref, pos_ref, w_ref, plan_ref, meta_ref, cnt_ref, run_ref, base_ref):
    p = pl.program_id(0)
    i = pl.program_id(1)
    tm = lg_ref.shape[2]
    sub = lax.broadcasted_iota(jnp.int32, (N_EXPERTS, tm), 0).astype(F32)
    tf = float(FFN_TILE)

    nb = ROUTE_BLOCKS

    @pl.when(p == 0)
    def _():
        @pl.when(i == 0)
        def _():
            cnt_ref[...] = jnp.zeros_like(cnt_ref)

        count = jnp.zeros((N_EXPERTS, 1), F32)
        for s in range(nb):
            t = i * nb + s
            e_a, e_b, w_a, w_b = _route_rows(lg_ref[s])
            meta_ref[t, 0:1, :] = e_a
            meta_ref[t, 1:2, :] = e_b
            meta_ref[t, 2:3, :] = w_a
            meta_ref[t, 3:4, :] = w_b
            meta_ref[t, 4:8, :] = jnp.zeros((4, tm), F32)
            oh = jnp.where((sub == e_a) | (sub == e_b), 1.0, 0.0)
            count = count + jnp.sum(oh, axis=1, keepdims=True)
        cnt_ref[...] += jnp.broadcast_to(count, cnt_ref.shape)

    @pl.when(p == 1)
    def _():
        @pl.when(i == 0)
        def _():
            c1, c2, c3 = _split3(cnt_ref[...])
            base_ref[...] = _dot(etri_ref[...], c1) + (_dot(etri_ref[...], c2) + _dot(etri_ref[...], c3))
            run_ref[...] = jnp.zeros_like(run_ref)

        start = run_ref[...] + base_ref[...]
        for s in range(nb):
            t = i * nb + s
            e_a = meta_ref[t, 0:1, :]
            e_b = meta_ref[t, 1:2, :]
            oh_a = jnp.where(sub == e_a, 1.0, 0.0)
            oh_b = jnp.where(sub == e_b, 1.0, 0.0)
            oh = oh_a + oh_b
            before = _dot(oh.astype(BF16), ttri_ref[...]) + start
            pos_ref[s] = jnp.zeros(pos_ref.shape[1:], jnp.int32)
            pos_ref[s, 0:1, :] = jnp.sum(oh_a * before, axis=0, keepdims=True).astype(jnp.int32)
            pos_ref[s, 1:2, :] = jnp.sum(oh_b * before, axis=0, keepdims=True).astype(jnp.int32)
            w_ref[s] = meta_ref[t]
            start = start + jnp.broadcast_to(jnp.sum(oh, axis=1, keepdims=True), start.shape)
        run_ref[...] = start - base_ref[...]

        @pl.when(i == n_tiles // nb - 1)
        def _():
            cnt = cnt_ref[:, 0:PLAN_ITEMS]
            base = base_ref[:, 0:PLAN_ITEMS]
            e_id = lax.broadcasted_iota(jnp.int32, (N_EXPERTS, PLAN_ITEMS), 0)
            t_lo = jnp.floor(base / tf)
            t_hi = jnp.floor((base + cnt - 1.0) / tf)
            n_items = jnp.where(cnt > 0.0, t_hi - t_lo + 1.0, 0.0)
            i_start = _dot(etri_ref[...], n_items.astype(BF16))
            item = lax.broadcasted_iota(jnp.int32, (N_EXPERTS, PLAN_ITEMS), 1).astype(F32)
            sel = (i_start <= item) & (item < i_start + n_items)
            tile_e = t_lo + item - i_start
            off = base - tile_e * tf

            def pick(v):
                return jnp.sum(jnp.where(sel, v, 0.0), axis=0, keepdims=True).astype(jnp.int32)

            plan_ref[...] = jnp.zeros(plan_ref.shape, jnp.int32)
            plan_ref[0:1, :] = pick(e_id.astype(F32))
            plan_ref[1:2, :] = pick(tile_e)
            plan_ref[2:3, :] = pick(jnp.maximum(off, 0.0))
            plan_ref[3:4, :] = pick(jnp.minimum(off + cnt, tf))
            plan_ref[4:5, :] = jnp.sum(n_items, axis=0, keepdims=True).astype(jnp.int32)


def _route(lg, tris):
    expert_tri, token_tri = tris
    n_tiles, _, tm = lg.shape
    nb = ROUTE_BLOCKS
    assert n_tiles % nb == 0
    return pl.pallas_call(
        functools.partial(_route_kernel, n_tiles),
        grid=(2, n_tiles // nb),
        in_specs=[pl.BlockSpec((nb, ROUTE_ROWS, tm), lambda p, i: (i, 0, 0)),
                  _const_spec(expert_tri.shape), _const_spec(token_tri.shape)],
        out_specs=[pl.BlockSpec((nb, 8, tm), lambda p, i: (i * p, 0, 0)),
                   pl.BlockSpec((nb, 8, tm), lambda p, i: (i * p, 0, 0)),
                   pl.BlockSpec((8, PLAN_ITEMS), lambda p, i: (0, 0))],
        out_shape=[jax.ShapeDtypeStruct((n_tiles, 8, tm), jnp.int32),
                   jax.ShapeDtypeStruct((n_tiles, 8, tm), F32),
                   jax.ShapeDtypeStruct((8, PLAN_ITEMS), jnp.int32)],
        scratch_shapes=[pltpu.VMEM((n_tiles, 8, tm), F32),
                        pltpu.VMEM((N_EXPERTS, tm), F32),
                        pltpu.VMEM((N_EXPERTS, tm), F32),
                        pltpu.VMEM((N_EXPERTS, tm), F32)],
        compiler_params=_params(("arbitrary", "arbitrary")),
        name="route",
    )(lg, expert_tri, token_tri)


ROW_UNROLL = 8


DISPATCH_BLOCKS = 4


def _dispatch_kernel(tm, pos_ref, h_ref, xs_ref, sem):
    def row_copy(t, dst):
        return pltpu.make_async_copy(_slab(h_ref, t), _slab(xs_ref, dst), sem)

    for blk in range(DISPATCH_BLOCKS):
        def issue(g, carry, blk=blk):
            for u in range(ROW_UNROLL):
                t = g * ROW_UNROLL + u
                row_copy(blk * tm + t, pos_ref[blk, 0, t]).start()
                row_copy(blk * tm + t, pos_ref[blk, 1, t]).start()
            return carry

        lax.fori_loop(0, tm // ROW_UNROLL, issue, 0)

    def drain(g, carry):
        for _ in range(2 * ROW_UNROLL):
            row_copy(0, 0).wait()
        return carry

    lax.fori_loop(0, DISPATCH_BLOCKS * tm // ROW_UNROLL, drain, 0)


def _dispatch(pos, h2, n_rows):
    n_tiles, _, tm = pos.shape
    nb = DISPATCH_BLOCKS
    assert n_tiles % nb == 0
    return pl.pallas_call(
        functools.partial(_dispatch_kernel, tm),
        grid=(n_tiles // nb,),
        in_specs=[pl.BlockSpec((nb, 8, tm), lambda i: (i, 0, 0), memory_space=pltpu.SMEM),
                  pl.BlockSpec((nb * tm * SLABS, LANES), lambda i: (i, 0))],
        out_specs=pl.BlockSpec(memory_space=pl.ANY),
        out_shape=jax.ShapeDtypeStruct((n_rows * SLABS, LANES), F32),
        scratch_shapes=[pltpu.SemaphoreType.DMA],
        compiler_params=_params(("arbitrary",)),
        name="dispatch",
    )(pos, h2)


def _ffn_kernel(owner_ref, tile_ref, lo_ref, hi_ref, used_ref, x_ref, w1_ref, w3_ref, w2_ref, y_ref,
                w1b_ref, w3b_ref, w2b_ref):
    w = pl.program_id(0)
    prev = jnp.maximum(w - 1, 0)

    @pl.when((w == 0) | (owner_ref[w] != owner_ref[prev]))
    def _():
        w1b_ref[...] = w1_ref[...].astype(BF16)
        w3b_ref[...] = w3_ref[...].astype(BF16)
        w2b_ref[...] = w2_ref[...].astype(BF16)

    @pl.when(w < used_ref[0])
    def _():
        x = _load_rows(x_ref, FFN_TILE).astype(BF16)
        a = _dot(x, w1b_ref[...])
        b = _dot(x, w3b_ref[...])
        hid = (a * _sigmoid(a)) * b
        y = _dot(hid.astype(BF16), w2b_ref[...])
        row = lax.broadcasted_iota(jnp.int32, y.shape, 0)
        mine = (row >= lo_ref[w]) & (row < hi_ref[w])
        first = (w == 0) | (tile_ref[w] != tile_ref[jnp.maximum(w - 1, 0)])
        kept = jnp.where(first, 0.0, _load_rows(y_ref, FFN_TILE))
        _store_rows(y_ref, jnp.where(mine, y, kept))


def _ffn(plan, xs, w1, w3, w2):
    n_rows = xs.shape[0] // SLABS
    d = SLABS * LANES
    tf = FFN_TILE
    de = w1.shape[1] // N_EXPERTS
    n_items = n_rows // tf + N_EXPERTS - 1
    assert n_items <= PLAN_ITEMS

    def item(w, used):
        return jnp.minimum(w, used[0] - 1)

    grid_spec = pltpu.PrefetchScalarGridSpec(
        num_scalar_prefetch=5,
        grid=(n_items,),
        in_specs=[pl.BlockSpec((tf * SLABS, LANES), lambda w, own, til, lo, hi, used: (til[item(w, used)], 0)),
                  pl.BlockSpec((d, de), lambda w, own, til, lo, hi, used: (0, own[item(w, used)])),
                  pl.BlockSpec((d, de), lambda w, own, til, lo, hi, used: (0, own[item(w, used)])),
                  pl.BlockSpec((de, d), lambda w, own, til, lo, hi, used: (own[item(w, used)], 0))],
        out_specs=pl.BlockSpec((tf * SLABS, LANES), lambda w, own, til, lo, hi, used: (til[item(w, used)], 0)),
        scratch_shapes=[pltpu.VMEM((d, de), BF16), pltpu.VMEM((d, de), BF16), pltpu.VMEM((de, d), BF16)],
    )
    return pl.pallas_call(
        _ffn_kernel,
        grid_spec=grid_spec,
        out_shape=jax.ShapeDtypeStruct((n_rows * SLABS, LANES), F32),
        compiler_params=_params(("arbitrary",)),
        name="ffn",
    )(plan[0], plan[1], plan[2], plan[3], plan[4, 0:1], xs, w1, w3, w2)


def _combine_kernel(tm, ctx_steps, pos_ref, nxt_ref, w_ref, x1_ref, mod_ref, gp_ref, ys_ref, yc_ref, yl_ref,
                    buf_ref, sem):
    i = pl.program_id(0)
    n = pl.num_programs(0)
    slot = i % 2

    def row_copy(src, k, t, s):
        return pltpu.make_async_copy(_slab(ys_ref, src), _slab(buf_ref.at[s, k], t), sem.at[s])

    def issue(p_ref, s):
        def body(g, carry):
            for u in range(ROW_UNROLL):
                t = g * ROW_UNROLL + u
                row_copy(p_ref[0, t], 0, t, s).start()
                row_copy(p_ref[1, t], 1, t, s).start()
            return carry
        lax.fori_loop(0, tm // ROW_UNROLL, body, 0)

    @pl.when(i == 0)
    def _():
        issue(pos_ref, 0)

    @pl.when(i + 1 < n)
    def _():
        issue(nxt_ref, 1 - slot)

    def drain(g, carry):
        for _ in range(2 * ROW_UNROLL):
            row_copy(0, 0, 0, slot).wait()
        return carry

    lax.fori_loop(0, tm // ROW_UNROLL, drain, 0)

    r = lax.broadcasted_iota(jnp.int32, (tm, tm), 0)
    c = lax.broadcasted_iota(jnp.int32, (tm, tm), 1)
    d = x1_ref.shape[1]
    wa = jnp.broadcast_to(jnp.sum(jnp.where(r == c, w_ref[2:3, :], 0.0), axis=1, keepdims=True), (tm, d))
    wb = jnp.broadcast_to(jnp.sum(jnp.where(r == c, w_ref[3:4, :], 0.0), axis=1, keepdims=True), (tm, d))
    moe = wa * _load_rows(buf_ref, tm, slot, 0) + wb * _load_rows(buf_ref, tm, slot, 1)
    gt2 = mod_ref[5:6, :]
    y = x1_ref[...] + gt2 * _rms(moe, gp_ref[...])

    @pl.when(i < ctx_steps)
    def _():
        yc_ref[...] = y

    @pl.when(i >= ctx_steps)
    def _():
        yl_ref[...] = y


def _combine(st, pos, w, x1, mod, g_post2, ys):
    n_tiles, _, tm = pos.shape
    d = x1.shape[1]
    row = pl.BlockSpec((tm, d), lambda i: (i, 0))
    return pl.pallas_call(
        functools.partial(_combine_kernel, tm, st.ctx_steps(tm)),
        grid=(n_tiles,),
        in_specs=[pl.BlockSpec((None, 8, tm), lambda i: (i, 0, 0), memory_space=pltpu.SMEM),
                  pl.BlockSpec((None, 8, tm), lambda i: (jnp.minimum(i + 1, n_tiles - 1), 0, 0),
                               memory_space=pltpu.SMEM),
                  pl.BlockSpec((None, 8, tm), lambda i: (i, 0, 0)),
                  row,
                  st.mod_spec(tm, d),
                  _const_spec((1, d)),
                  pl.BlockSpec(memory_space=pl.ANY)],
        out_specs=[st.ctx_spec(tm, d), st.lat_spec(tm, d)],
        out_shape=[jax.ShapeDtypeStruct((st.n_ctx, d), F32), jax.ShapeDtypeStruct((st.n_lat, d), F32)],
        scratch_shapes=[pltpu.VMEM((2, 2, tm * SLABS, LANES), F32), pltpu.SemaphoreType.DMA((2,))],
        compiler_params=_params(("arbitrary",)),
        name="combine",
    )(pos, pos, w, x1, mod, g_post2.reshape(1, d), ys)


def _moe(st, h2, lg, x1, mod, g_post2, w1, w3, w2, route_tri):
    n_tok = x1.shape[0]
    n_rows = 2 * n_tok
    assert n_rows % FFN_TILE == 0
    pos, w, plan = _route(lg, route_tri)
    xs = _dispatch(pos, h2, n_rows)
    ys = _ffn(plan, xs, w1, w3, w2)
    return _combine(st, pos, w, x1, mod, g_post2, ys)


def _grid_pos(n_tokens, d_model):
    rows = n_tokens // GRID_W
    row = np.repeat(np.arange(rows, dtype=np.float64), GRID_W)
    col = np.tile(np.arange(GRID_W, dtype=np.float64), rows)
    n_freq = d_model // 4
    freq = np.exp(-math.log(POS_BASE) * np.arange(n_freq, dtype=np.float64) / n_freq)

    def enc(p):
        a = p[:, None] * freq[None, :]
        return np.concatenate([np.sin(a), np.cos(a)], axis=-1)

    return jnp.asarray(np.concatenate([enc(row), enc(col)], axis=-1), dtype=F32)


def _dft_cos_sin(n):
    k = np.arange(n, dtype=np.int64)
    ang = 2.0 * np.pi * ((k[:, None] * k[None, :]) % n).astype(np.float64) / n
    return np.cos(ang), np.sin(ang)


def _fnet_consts(seq_len):
    ct, st = _dft_cos_sin(seq_len)
    f_mat = jnp.asarray(np.concatenate([ct, -st], axis=1), dtype=F32).astype(BF16)
    cc, sc = _dft_cos_sin(FGROUP_DIM)
    cs_mat = jnp.asarray(np.stack([cc, sc]), dtype=F32).astype(BF16)
    return f_mat, cs_mat


def _tri_consts():
    i = np.arange(CHUNK)
    prefix = (i[:, None] <= i[None, :]).astype(np.float32)
    suffix = (i[:, None] >= i[None, :]).astype(np.float32)
    tri = jnp.asarray(np.stack([prefix, suffix]), dtype=BF16)
    e = np.arange(N_EXPERTS)
    expert_tri = jnp.asarray((e[None, :] < e[:, None]).astype(np.float32), dtype=BF16)
    t = np.arange(TOKEN_TILE)
    token_tri = jnp.asarray((t[:, None] < t[None, :]).astype(np.float32), dtype=BF16)
    return tri, (expert_tri, token_tri)


def _layer(x_prompt, x_sample, pos, mod, states, lw, consts):
    bp, sp, d = x_prompt.shape
    bs, ss, _ = x_sample.shape
    st = _Streams(n_ctx=bp * sp, n_lat=bs * ss, seq_lat=ss)
    (g_pre1, proj_w, g_hn, post_w, g_post1, g_pre2, w_e1, w_e3, w_e2, g_post2) = lw
    tri, route_tris = consts
    x_ctx = x_prompt.reshape(st.n_ctx, d)
    x_lat = x_sample.reshape(st.n_lat, d)
    q, kt, v, o, u, ga, gb, gr = _proj(st, x_ctx, x_lat, pos, mod, g_pre1, proj_w + (tri,))
    ha_ctx, new_c, new_n, new_m = _mlstm(q, kt, v, o, g_hn, gr, bp, sp, 0, None, MLSTM_HEADS_PER_STEP)
    (ha_lat,) = _mlstm(q, kt, v, o, g_hn, gr, bs, ss, st.n_ctx, states, MLSTM_HEADS_PER_STEP)
    yf_ctx = _fnet(u, *_fnet_consts(sp), bp, sp, 0)
    yf_lat = _fnet(u, *_fnet_consts(ss), bs, ss, st.n_ctx)
    x1, h2, logits = _post(st, x_ctx, x_lat, pos, ha_ctx, ha_lat, yf_ctx, yf_lat, ga, gb, mod, g_post1, g_pre2, post_w)
    y_ctx, y_lat = _moe(st, h2, logits, x1, mod, g_post2, w_e1, w_e3, w_e2, route_tris)
    return y_ctx.reshape(bp, sp, d), y_lat.reshape(bs, ss, d), (new_c, new_n, new_m)


def kernel(x_prompt, x_sample, c, state_C, state_n, state_m, c_ctx, w_ada, b_ada, g_pre1, w_in, b_gates, g_hn,
           w_a, w_f, w_out, g_post1, g_pre2, w_rg, b_rg, w_re, b_re, w_e1, w_e3, w_e2, g_post2):
    bp, sp, d = x_prompt.shape
    bs, ss, _ = x_sample.shape
    depth = w_in.shape[0]
    assert depth == 1
    l = 0
    dm = N_HEADS * HEAD_DIM
    du = N_FGROUPS * FGROUP_DIM
    ng = 4 * N_HEADS

    n_rows = 8
    assert 1 + bs <= n_rows
    cs = jnp.concatenate([c_ctx[None, :], c, jnp.zeros((n_rows - 1 - bs, d), F32)], axis=0)
    mod = _ada(cs, w_ada[l], b_ada[l]).reshape(n_rows, 6, d)

    wi = w_in[l]
    cuts = np.cumsum([dm] * 4 + [ng, du, d]).tolist()
    wq = wi[:, :cuts[0]].astype(BF16)
    wkt = wi[:, cuts[0]:cuts[1]].T.astype(BF16)
    wv = wi[:, cuts[1]:cuts[2]].astype(BF16)
    wo = wi[:, cuts[2]:cuts[3]].astype(BF16)
    wgt = wi[:, cuts[3]:cuts[4]].T
    wu = wi[:, cuts[4]:cuts[5]].astype(BF16)
    wga = wi[:, cuts[5]:cuts[6]].astype(BF16)
    wgb = wi[:, cuts[6]:].astype(BF16)
    bg = b_gates[l].reshape(ng, 1)
    proj_w = (wq, wkt, wv, wo, wu, wga, wgb, wgt, bg)
    n_pad = ROUTE_ROWS - N_EXPERTS - N_GROUPS
    wr = jnp.concatenate([w_re[l], w_rg[l], jnp.zeros((d, n_pad), F32)], axis=1).T
    br = jnp.concatenate([b_re[l], b_rg[l], jnp.zeros((n_pad,), F32)]).reshape(ROUTE_ROWS, 1)
    post_w = (w_a[l].astype(BF16), w_f[l].astype(BF16), w_out[l].astype(BF16), wr, br)
    lw = (g_pre1[l], proj_w, g_hn[l], post_w, g_post1[l], g_pre2[l], w_e1[l], w_e3[l], w_e2[l], g_post2[l])
    consts = _tri_consts()

    pos = _grid_pos(ss, d)
    y_prompt, y_sample, (new_c, new_n, new_m) = _layer(x_prompt, x_sample, pos, mod,
                                                       (state_C, state_n, state_m), lw, consts)

    new_state_c = new_c
    new_state_n = new_n.reshape(bp, depth, 2, N_HEADS, HEAD_DIM)
    new_state_m = new_m[..., 0, 0]
    return (y_prompt, y_sample, new_state_c, new_state_n, new_state_m)
```

```python
import functools
import math
from typing import NamedTuple

import numpy as np
import jax
import jax.numpy as jnp
from jax import lax
from jax.experimental import pallas as pl
from jax.experimental.pallas import tpu as pltpu

F32 = jnp.float32
BF16 = jnp.bfloat16

N_HEADS = 8
HEAD_DIM = 128
N_FGROUPS = 4
FGROUP_DIM = 128
N_GROUPS = 4
EXPERTS_PER_GROUP = 8
N_EXPERTS = N_GROUPS * EXPERTS_PER_GROUP
GRID_W = 64
POS_BASE = 10000.0
EPS = 1e-6

V7X_VMEM_LIMIT_BYTES = 56 * 1024 * 1024
LANES = 128
CHUNK = 256
TOKEN_TILE = CHUNK
CHUNKS_PER_STEP = 2
MLSTM_HEADS_PER_STEP = 4


def _params(sem):
    return pltpu.CompilerParams(dimension_semantics=sem, vmem_limit_bytes=V7X_VMEM_LIMIT_BYTES)


def _const_spec(shape):
    n = len(shape)
    return pl.BlockSpec(shape, lambda *_: (0,) * n, pipeline_mode=pl.Buffered(1))


def _split2(x):
    hi = x.astype(BF16)
    lo = (x - hi.astype(F32)).astype(BF16)
    return hi, lo


def _split3(x):
    hi = x.astype(BF16)
    r = x - hi.astype(F32)
    mid = r.astype(BF16)
    lo = (r - mid.astype(F32)).astype(BF16)
    return hi, mid, lo


def _dot(a, b):
    return jnp.dot(a, b, preferred_element_type=F32)


def _dot_nt(a, b):
    return lax.dot_general(a, b, (((1,), (1,)), ((), ())), preferred_element_type=F32)


def _dot_f32(a, b, nt=False):
    d = _dot_nt if nt else _dot
    a1, a2 = _split2(a)
    b1, b2 = _split2(b)
    return d(a1, b1) + (d(a1, b2) + d(a2, b1))


def _dot_exact_rhs(a, b_bf16, nt=False):
    d = _dot_nt if nt else _dot
    a1, a2, a3 = _split3(a)
    return d(a1, b_bf16) + (d(a2, b_bf16) + d(a3, b_bf16))


SLABS = 8


def _load_rows(ref, n, *lead):
    return jnp.concatenate([ref[(*lead, pl.ds(j, n, stride=SLABS), slice(None))] for j in range(SLABS)], axis=1)


def _store_rows(ref, val, *lead):
    n = val.shape[0]
    for j in range(SLABS):
        ref[(*lead, pl.ds(j, n, stride=SLABS), slice(None))] = val[:, j * LANES:(j + 1) * LANES]


def _slab(ref, row):
    return ref.at[pl.ds(pl.multiple_of(row * SLABS, SLABS), SLABS)]


def _rms(x, g):
    return x * lax.rsqrt(jnp.mean(x * x, axis=-1, keepdims=True) + EPS) * g


def _sigmoid(x):
    return 1.0 / (1.0 + jnp.exp(-x))


def _log_sigmoid(x):
    return jnp.minimum(x, 0.0) - jnp.log(1.0 + jnp.exp(-jnp.abs(x)))


def _ada_kernel(c_ref, w_ref, b_ref, o_ref):
    c = c_ref[...]
    s = (c * _sigmoid(c)).astype(BF16)
    o_ref[...] = _dot(s, w_ref[...].astype(BF16)) + b_ref[...]


def _ada(cs, w_ada, b_ada):
    rows, d = cs.shape
    n = w_ada.shape[1]
    tn = 1024
    return pl.pallas_call(
        _ada_kernel,
        grid=(n // tn,),
        in_specs=[pl.BlockSpec((rows, d), lambda j: (0, 0)),
                  pl.BlockSpec((d, tn), lambda j: (0, j)),
                  pl.BlockSpec((1, tn), lambda j: (0, j))],
        out_specs=pl.BlockSpec((rows, tn), lambda j: (0, j)),
        out_shape=jax.ShapeDtypeStruct((rows, n), F32),
        compiler_params=_params(("arbitrary",)),
        name="ada",
    )(cs, w_ada, b_ada.reshape(1, n))


def _proj_kernel(ctx_steps, xc_ref, xl_ref, pos_ref, mod_ref, g_ref, wq_ref, wkt_ref, wv_ref, wo_ref, wu_ref,
                 wga_ref, wgb_ref, wgt_ref, bg_ref, tri_ref,
                 q_ref, kt_ref, v_ref, o_ref, u_ref, ga_ref, gb_ref, gr_ref):
    x = jnp.where(pl.program_id(0) < ctx_steps, xc_ref[...], xl_ref[...] + pos_ref[...])
    sh = mod_ref[0:1, :]
    sc = mod_ref[1:2, :]
    h = _rms(x, g_ref[...]) * (1.0 + sc) + sh
    hb = h.astype(BF16)
    q_ref[...] = _dot(hb, wq_ref[...]).astype(BF16)
    kt = (_dot_nt(wkt_ref[...], hb) * (HEAD_DIM ** -0.5)).astype(BF16)
    tc = TOKEN_TILE
    for s in range(CHUNKS_PER_STEP):
        kt_ref[s] = kt[:, s * tc:(s + 1) * tc]
    v_ref[...] = _dot(hb, wv_ref[...]).astype(BF16)
    o_ref[...] = _dot(hb, wo_ref[...])
    u_ref[...] = _dot(hb, wu_ref[...]).astype(BF16)
    ga_ref[...] = _dot(hb, wga_ref[...])
    gb_ref[...] = _dot(hb, wgb_ref[...])
    g = _dot_f32(wgt_ref[...], h, nt=True) + bg_ref[...]
    nh = N_HEADS
    lf_f = _log_sigmoid(g[nh:2 * nh])
    lf_b = _log_sigmoid(g[3 * nh:4 * nh])
    for s in range(CHUNKS_PER_STEP):
        cols = slice(s * tc, (s + 1) * tc)
        gr_ref[s, 0:nh, :] = g[0:nh, cols]
        gr_ref[s, nh:2 * nh, :] = _dot_exact_rhs(lf_f[:, cols], tri_ref[0])
        gr_ref[s, 2 * nh:3 * nh, :] = g[2 * nh:3 * nh, cols]
        gr_ref[s, 3 * nh:4 * nh, :] = _dot_exact_rhs(lf_b[:, cols], tri_ref[1])


class _Streams(NamedTuple):
    n_ctx: int
    n_lat: int
    seq_lat: int

    def ctx_steps(self, tm):
        assert self.n_ctx % tm == 0 and self.seq_lat % tm == 0
        return self.n_ctx // tm

    def ctx_spec(self, tm, width):
        c = self.ctx_steps(tm)
        return pl.BlockSpec((tm, width), lambda i, *_: (jnp.minimum(i, c - 1), 0))

    def lat_spec(self, tm, width):
        c = self.ctx_steps(tm)
        return pl.BlockSpec((tm, width), lambda i, *_: (jnp.maximum(i - c, 0), 0))

    def pos_spec(self, tm, width):
        c = self.ctx_steps(tm)
        per_seq = self.seq_lat // tm
        return pl.BlockSpec((tm, width), lambda i, *_: (jnp.maximum(i - c, 0) % per_seq, 0))

    def mod_spec(self, tm, width):
        c = self.ctx_steps(tm)
        per_seq = self.seq_lat // tm
        return pl.BlockSpec((None, 6, width),
                            lambda i, *_: (jnp.where(i < c, 0, 1 + jnp.maximum(i - c, 0) // per_seq), 0, 0))


def _proj(st, x_ctx, x_lat, pos, mod, g_pre1, wts):
    d = x_ctx.shape[1]
    n_tok = st.n_ctx + st.n_lat
    tc = TOKEN_TILE
    ns = CHUNKS_PER_STEP
    tm = ns * tc
    n_tiles = n_tok // tc
    (wq, wkt, wv, wo, wu, wga, wgb, wgt, bg, tri) = wts
    row = pl.BlockSpec((tm, d), lambda i: (i, 0))
    in_specs = [st.ctx_spec(tm, d), st.lat_spec(tm, d), st.pos_spec(tm, d), st.mod_spec(tm, d), _const_spec((1, d))]
    in_specs += [_const_spec(w.shape) for w in wts]
    args = [x_ctx, x_lat, pos, mod, g_pre1.reshape(1, d)] + list(wts)
    n_gate_rows = 4 * N_HEADS
    du = wu.shape[1]
    out_shape = [jax.ShapeDtypeStruct((n_tok, d), BF16),
                 jax.ShapeDtypeStruct((n_tiles, d, tc), BF16),
                 jax.ShapeDtypeStruct((n_tok, d), BF16),
                 jax.ShapeDtypeStruct((n_tok, d), F32),
                 jax.ShapeDtypeStruct((n_tok, du), BF16),
                 jax.ShapeDtypeStruct((n_tok, d), F32),
                 jax.ShapeDtypeStruct((n_tok, d), F32),
                 jax.ShapeDtypeStruct((n_tiles, n_gate_rows, tc), F32)]
    out_specs = [row,
                 pl.BlockSpec((ns, d, tc), lambda i: (i, 0, 0)),
                 row, row,
                 pl.BlockSpec((tm, du), lambda i: (i, 0)),
                 row, row,
                 pl.BlockSpec((ns, n_gate_rows, tc), lambda i: (i, 0, 0))]
    return pl.pallas_call(
        functools.partial(_proj_kernel, st.ctx_steps(tm)),
        grid=(n_tok // tm,),
        in_specs=in_specs,
        out_specs=out_specs,
        out_shape=out_shape,
        compiler_params=_params(("parallel",)),
        name="proj",
    )(*args)


def _mlstm_kernel(n_chunks, hp, has_state, emit_state, *refs):
    it = iter(refs)
    if has_state:
        m0_ref = next(it)
    q_ref, kt_ref, v_ref, o_ref, ghn_ref, gr_ref = (next(it) for _ in range(6))
    if has_state:
        c0_ref, n0_ref = next(it), next(it)
    ha_ref = next(it)
    if emit_state:
        cout_ref, nout_ref, mout_ref = next(it), next(it), next(it)
    hs_ref, cn_ref, m_ref = (next(it) for _ in range(3))

    L = CHUNK
    nh, dh = N_HEADS, HEAD_DIM
    head0 = pl.program_id(1) * hp
    neg_inf = -jnp.inf

    sq_r = lax.broadcasted_iota(jnp.int32, (dh, dh), 0)
    sq_c = lax.broadcasted_iota(jnp.int32, (dh, dh), 1)
    for j in range(hp):
        for d in range(2):
            if has_state:
                n_col = jnp.sum(jnp.where(sq_r == sq_c, n0_ref[d, j], 0.0), axis=1, keepdims=True)
                cn_ref[j, d, :, 0:dh] = c0_ref[d, j]
                cn_ref[j, d, :, dh:2 * dh] = jnp.broadcast_to(n_col, (dh, dh))
                m0 = m0_ref[pl.program_id(0) * (2 * nh) + d * nh + head0 + j]
                m_ref[j, d] = jnp.full((8, LANES), m0, F32)
            else:
                cn_ref[j, d] = jnp.zeros((dh, 2 * dh), F32)
                m_ref[j, d] = jnp.zeros((8, LANES), F32)

    row_id = lax.broadcasted_iota(jnp.int32, (L, L), 0)
    col_id = lax.broadcasted_iota(jnp.int32, (L, L), 1)
    ones_blk = jnp.ones((L, dh), BF16)

    def chunk(j, d, c):
        t0 = pl.multiple_of(c * L, L)
        lanes = slice(j * dh, (j + 1) * dh)
        qc = q_ref[pl.ds(t0, L), lanes]
        ktc = kt_ref[c, lanes, :]
        vc = v_ref[pl.ds(t0, L), lanes]
        ig_row = gr_ref[c, pl.ds(2 * nh * d + head0 + j, 1), :]
        b_row = gr_ref[c, pl.ds(2 * nh * d + nh + head0 + j, 1), :]
        m_prev = m_ref[j, d][0:1, 0:1]
        if d == 0:
            b_end = b_row[:, L - 1:L]
            mask = col_id <= row_id
        else:
            b_end = b_row[:, 0:1]
            mask = col_id >= row_id
        a_row = ig_row - b_row
        a_max = jnp.max(jnp.where(mask, a_row, neg_inf), axis=1, keepdims=True)
        g = jnp.maximum(jnp.broadcast_to(a_max, (L, dh)), m_prev)
        b_col = jnp.broadcast_to(
            jnp.sum(jnp.where(row_id == col_id, b_row, 0.0), axis=1, keepdims=True), (L, dh))
        g_full = jnp.concatenate([g] * (L // dh), axis=1)
        s = _dot(qc, ktc) * jnp.exp(jnp.where(mask, a_row - g_full, neg_inf))
        decay = jnp.exp(m_prev - g)
        qcn = _dot(qc, cn_ref[j, d].astype(BF16))
        num = decay * qcn[:, 0:dh] + _dot(s.astype(BF16), vc)
        den = decay * qcn[:, dh:2 * dh] + jnp.sum(s, axis=1, keepdims=True)
        hch = num / jnp.maximum(jnp.abs(den), jnp.exp(-(b_col + g)))
        w_log = b_end + a_row
        m_new = jnp.maximum(b_end + m_prev, jnp.max(w_log, axis=1, keepdims=True))
        kw = (ktc.astype(F32) * jnp.exp(w_log - m_new)).astype(BF16)
        v_aug = jnp.concatenate([vc, ones_blk], axis=1)
        cn_ref[j, d] = jnp.exp(b_end + m_prev - m_new) * cn_ref[j, d] + _dot(kw, v_aug)
        m_ref[j, d] = jnp.broadcast_to(m_new, (8, LANES))
        return t0, hch

    if n_chunks == 1:
        for j in range(hp):
            _, h_f = chunk(j, 0, 0)
            _, h_b = chunk(j, 1, 0)
            hs_ref[:, j * dh:(j + 1) * dh] = h_f + h_b
    else:
        half = n_chunks // 2

        def first(i, carry):
            for j in range(hp):
                lanes = slice(j * dh, (j + 1) * dh)
                t_f, h_f = chunk(j, 0, i)
                t_b, h_b = chunk(j, 1, n_chunks - 1 - i)
                hs_ref[pl.ds(t_f, L), lanes] = h_f
                hs_ref[pl.ds(t_b, L), lanes] = h_b
            return carry

        def second(i, carry):
            for j in range(hp):
                lanes = slice(j * dh, (j + 1) * dh)
                t_f, h_f = chunk(j, 0, i)
                t_b, h_b = chunk(j, 1, n_chunks - 1 - i)
                hs_ref[pl.ds(t_f, L), lanes] += h_f
                hs_ref[pl.ds(t_b, L), lanes] += h_b
            return carry

        lax.fori_loop(0, half, first, 0)
        lax.fori_loop(half, n_chunks, second, 0)

    def finish(c, carry):
        t0 = pl.multiple_of(c * L, L)
        for j in range(hp):
            lanes = slice(j * dh, (j + 1) * dh)
            ha = hs_ref[pl.ds(t0, L), lanes]
            ha = ha * lax.rsqrt(jnp.mean(ha * ha, axis=-1, keepdims=True) + EPS)
            ha = ha * ghn_ref[:, lanes] * _sigmoid(o_ref[pl.ds(t0, L), lanes])
            ha_ref[pl.ds(t0, L), lanes] = ha.astype(BF16)
        return carry

    if n_chunks == 1:
        finish(0, 0)
    else:
        lax.fori_loop(0, n_chunks, finish, 0)

    if emit_state:
        for j in range(hp):
            for d in range(2):
                cout_ref[d, j] = cn_ref[j, d, :, 0:dh]
                n_rep = cn_ref[j, d, :, dh:2 * dh]
                nout_ref[d, j] = jnp.sum(jnp.where(sq_r == sq_c, n_rep, 0.0), axis=0, keepdims=True)
                mout_ref[d, j] = m_ref[j, d][0:1, :]


def _mlstm(q, kt, v, o, g_hn, gr, n_seq, seq_len, tok_off, states, hp):
    d = q.shape[1]
    n_tok = n_seq * seq_len
    L = CHUNK
    assert tok_off % seq_len == 0
    sb = tok_off // seq_len
    n_chunks = seq_len // L
    has_state = states is not None
    emit_state = not has_state
    nh, dh = N_HEADS, HEAD_DIM
    assert n_chunks == 1 or n_chunks % 2 == 0
    assert nh % hp == 0
    wd = hp * dh

    in_specs = []
    args = []
    if has_state:
        state_c, state_n, state_m = states
        in_specs.append(pl.BlockSpec(memory_space=pltpu.SMEM))
        args.append(state_m.reshape(-1))
    tok = pl.BlockSpec((seq_len, wd), lambda b, h: (b + sb, h))
    in_specs += [tok,
                 pl.BlockSpec((n_chunks, wd, L), lambda b, h: (b + sb, h, 0)),
                 tok, tok,
                 pl.BlockSpec((1, wd), lambda b, h: (0, h)),
                 pl.BlockSpec((n_chunks, 4 * nh, L), lambda b, h: (b + sb, 0, 0))]
    args += [q, kt, v, o, g_hn.reshape(1, d), gr]
    st_c = pl.BlockSpec((None, None, 2, hp, dh, dh), lambda b, h: (b, 0, 0, h, 0, 0))
    st_v = pl.BlockSpec((None, None, 2, hp, 1, dh), lambda b, h: (b, 0, 0, h, 0, 0))
    if has_state:
        in_specs += [st_c, st_v]
        args += [state_c, state_n.reshape(n_seq, 1, 2, nh, 1, dh)]
    out_shape = [jax.ShapeDtypeStruct((n_tok, d), BF16)]
    out_specs = [pl.BlockSpec((seq_len, wd), lambda b, h: (b, h))]
    if emit_state:
        out_shape += [jax.ShapeDtypeStruct((n_seq, 1, 2, nh, dh, dh), F32),
                      jax.ShapeDtypeStruct((n_seq, 1, 2, nh, 1, dh), F32),
                      jax.ShapeDtypeStruct((n_seq, 1, 2, nh, 1, LANES), F32)]
        out_specs += [st_c, st_v, st_v]
    scratch = [pltpu.VMEM((seq_len, wd), F32),
               pltpu.VMEM((hp, 2, dh, 2 * dh), F32),
               pltpu.VMEM((hp, 2, 8, LANES), F32)]
    return pl.pallas_call(
        functools.partial(_mlstm_kernel, n_chunks, hp, has_state, emit_state),
        grid=(n_seq, nh // hp),
        in_specs=in_specs,
        out_specs=out_specs,
        out_shape=out_shape,
        scratch_shapes=scratch,
        compiler_params=_params(("parallel", "parallel")),
        name="mlstm",
    )(*args)


def _fnet_kernel(seq_len, u_ref, f_ref, cs_ref, y_ref, ab_ref):
    T = seq_len

    @pl.when(pl.program_id(1) == 0)
    def _():
        for g in range(N_FGROUPS):
            ug = u_ref[:, g * FGROUP_DIM:(g + 1) * FGROUP_DIM]
            ab_ref[0:T, g * FGROUP_DIM:(g + 1) * FGROUP_DIM] = _dot(ug, cs_ref[0]).astype(BF16)
            ab_ref[T:2 * T, g * FGROUP_DIM:(g + 1) * FGROUP_DIM] = _dot(ug, cs_ref[1]).astype(BF16)

    scale = 1.0 / math.sqrt(T * FGROUP_DIM)
    y_ref[...] = (_dot(f_ref[...], ab_ref[...]) * scale).astype(BF16)


def _fnet(u, f_mat, cs_mat, n_seq, seq_len, tok_off):
    du = u.shape[1]
    n_tok = n_seq * seq_len
    tr = min(seq_len, 512)
    assert tok_off % seq_len == 0
    sb = tok_off // seq_len
    return pl.pallas_call(
        functools.partial(_fnet_kernel, seq_len),
        grid=(n_seq, seq_len // tr),
        in_specs=[pl.BlockSpec((seq_len, du), lambda b, r: (b + sb, 0)),
                  pl.BlockSpec((tr, 2 * seq_len), lambda b, r: (r, 0)),
                  _const_spec(cs_mat.shape)],
        out_specs=pl.BlockSpec((tr, du), lambda b, r: (b * (seq_len // tr) + r, 0)),
        out_shape=jax.ShapeDtypeStruct((n_tok, du), BF16),
        scratch_shapes=[pltpu.VMEM((2 * seq_len, du), BF16)],
        compiler_params=_params(("parallel", "arbitrary")),
        name="fnet",
    )(u, f_mat, cs_mat)


def _post_kernel(ctx_steps, xc_ref, xl_ref, pos_ref, hac_ref, hal_ref, yfc_ref, yfl_ref, ga_ref, gb_ref, mod_ref,
                 gp1_ref, gp2_ref, wa_ref, wf_ref, wout_ref, wr_ref, br_ref, x1_ref, h2_ref, lg_ref):
    is_ctx = pl.program_id(0) < ctx_steps
    x = jnp.where(is_ctx, xc_ref[...], xl_ref[...] + pos_ref[...])
    ya = _dot(jnp.where(is_ctx, hac_ref[...], hal_ref[...]), wa_ref[...])
    yf = _dot(jnp.where(is_ctx, yfc_ref[...], yfl_ref[...]), wf_ref[...])
    mix_in = _sigmoid(ga_ref[...]) * ya + _sigmoid(gb_ref[...]) * yf
    mix = _dot(mix_in.astype(BF16), wout_ref[...])
    gt1 = mod_ref[2:3, :]
    sh2 = mod_ref[3:4, :]
    sc2 = mod_ref[4:5, :]
    x1 = x + gt1 * _rms(mix, gp1_ref[...])
    h2 = _rms(x1, gp2_ref[...]) * (1.0 + sc2) + sh2
    x1_ref[...] = x1
    _store_rows(h2_ref, h2)
    lg = _dot_f32(wr_ref[...], h2, nt=True) + br_ref[...]
    for s in range(CHUNKS_PER_STEP):
        lg_ref[s] = lg[:, s * TOKEN_TILE:(s + 1) * TOKEN_TILE]


def _post(st, x_ctx, x_lat, pos, ha_ctx, ha_lat, yf_ctx, yf_lat, ga, gb, mod, g_post1, g_pre2, wts):
    d = x_ctx.shape[1]
    n_tok = st.n_ctx + st.n_lat
    tc = TOKEN_TILE
    ns = CHUNKS_PER_STEP
    tm = ns * tc
    (wa, wf, wout, wr, br) = wts
    du = yf_ctx.shape[1]
    row = pl.BlockSpec((tm, d), lambda i: (i, 0))
    in_specs = [st.ctx_spec(tm, d), st.lat_spec(tm, d), st.pos_spec(tm, d),
                st.ctx_spec(tm, d), st.lat_spec(tm, d), st.ctx_spec(tm, du), st.lat_spec(tm, du),
                row, row, st.mod_spec(tm, d), _const_spec((1, d)), _const_spec((1, d))]
    in_specs += [_const_spec(w.shape) for w in wts]
    args = [x_ctx, x_lat, pos, ha_ctx, ha_lat, yf_ctx, yf_lat, ga, gb, mod,
            g_post1.reshape(1, d), g_pre2.reshape(1, d)] + list(wts)
    return pl.pallas_call(
        functools.partial(_post_kernel, st.ctx_steps(tm)),
        grid=(n_tok // tm,),
        in_specs=in_specs,
        out_specs=[row, pl.BlockSpec((tm * SLABS, LANES), lambda i: (i, 0)),
                   pl.BlockSpec((ns, ROUTE_ROWS, tc), lambda i: (i, 0, 0))],
        out_shape=[jax.ShapeDtypeStruct((n_tok, d), F32),
                   jax.ShapeDtypeStruct((n_tok * SLABS, LANES), F32),
                   jax.ShapeDtypeStruct((n_tok // tc, ROUTE_ROWS, tc), F32)],
        compiler_params=_params(("parallel",)),
        name="post",
    )(*args)


ROUTE_ROWS = N_EXPERTS + 8
ROUTE_BLOCKS = 4
FFN_TILE = 512
PLAN_ITEMS = LANES


def _route_rows(lg):
    tm = lg.shape[1]
    ne = N_EXPERTS
    gl = lg[ne:ne + 8]
    g_id = lax.broadcasted_iota(jnp.int32, (8, tm), 0).astype(F32)
    is_g = g_id < N_GROUPS
    gl = jnp.where(is_g, gl, -jnp.inf)
    mx = jnp.max(gl, axis=0, keepdims=True)
    z = jnp.sum(jnp.where(is_g, jnp.exp(gl - mx), 0.0), axis=0, keepdims=True)
    p_sel = 1.0 / z
    g_sel = jnp.min(jnp.where(gl == mx, g_id, float(N_GROUPS)), axis=0, keepdims=True)
    e_id = lax.broadcasted_iota(jnp.int32, (ne, tm), 0).astype(F32)
    lo = EXPERTS_PER_GROUP * g_sel
    in_grp = (e_id >= lo) & (e_id < lo + EXPERTS_PER_GROUP)
    le = jnp.where(in_grp, lg[0:ne], -jnp.inf)
    v1 = jnp.max(le, axis=0, keepdims=True)
    i1 = jnp.min(jnp.where(le == v1, e_id, float(ne)), axis=0, keepdims=True)
    le2 = jnp.where(e_id == i1, -jnp.inf, le)
    v2 = jnp.max(le2, axis=0, keepdims=True)
    i2 = jnp.min(jnp.where(le2 == v2, e_id, float(ne)), axis=0, keepdims=True)
    e2 = jnp.exp(v2 - v1)
    w1 = p_sel / (1.0 + e2)
    w2 = p_sel * e2 / (1.0 + e2)
    return i1, i2, w1, w2


def _route_kernel(n_tiles, lg_ref, etri_ref, ttri_ref, pos_ref, w_ref, plan_ref, meta_ref, cnt_ref, run_ref, base_ref):
    p = pl.program_id(0)
    i = pl.program_id(1)
    tm = lg_ref.shape[2]
    sub = lax.broadcasted_iota(jnp.int32, (N_EXPERTS, tm), 0).astype(F32)
    tf = float(FFN_TILE)

    nb = ROUTE_BLOCKS

    @pl.when(p == 0)
    def _():
        @pl.when(i == 0)
        def _():
            cnt_ref[...] = jnp.zeros_like(cnt_ref)

        count = jnp.zeros((N_EXPERTS, 1), F32)
        for s in range(nb):
            t = i * nb + s
            e_a, e_b, w_a, w_b = _route_rows(lg_ref[s])
            meta_ref[t, 0:1, :] = e_a
            meta_ref[t, 1:2, :] = e_b
            meta_ref[t, 2:3, :] = w_a
            meta_ref[t, 3:4, :] = w_b
            meta_ref[t, 4:8, :] = jnp.zeros((4, tm), F32)
            oh = jnp.where((sub == e_a) | (sub == e_b), 1.0, 0.0)
            count = count + jnp.sum(oh, axis=1, keepdims=True)
        cnt_ref[...] += jnp.broadcast_to(count, cnt_ref.shape)

    @pl.when(p == 1)
    def _():
        @pl.when(i == 0)
        def _():
            c1, c2, c3 = _split3(cnt_ref[...])
            base_ref[...] = _dot(etri_ref[...], c1) + (_dot(etri_ref[...], c2) + _dot(etri_ref[...], c3))
            run_ref[...] = jnp.zeros_like(run_ref)

        start = run_ref[...] + base_ref[...]
        for s in range(nb):
            t = i * nb + s
            e_a = meta_ref[t, 0:1, :]
            e_b = meta_ref[t, 1:2, :]
            oh_a = jnp.where(sub == e_a, 1.0, 0.0)
            oh_b = jnp.where(sub == e_b, 1.0, 0.0)
            oh = oh_a + oh_b
            before = _dot(oh.astype(BF16), ttri_ref[...]) + start
            pos_ref[s] = jnp.zeros(pos_ref.shape[1:], jnp.int32)
            pos_ref[s, 0:1, :] = jnp.sum(oh_a * before, axis=0, keepdims=True).astype(jnp.int32)
            pos_ref[s, 1:2, :] = jnp.sum(oh_b * before, axis=0, keepdims=True).astype(jnp.int32)
            w_ref[s] = meta_ref[t]
            start = start + jnp.broadcast_to(jnp.sum(oh, axis=1, keepdims=True), start.shape)
        run_ref[...] = start - base_ref[...]

        @pl.when(i == n_tiles // nb - 1)
        def _():
            cnt = cnt_ref[:, 0:PLAN_ITEMS]
            base = base_ref[:, 0:PLAN_ITEMS]
            e_id = lax.broadcasted_iota(jnp.int32, (N_EXPERTS, PLAN_ITEMS), 0)
            t_lo = jnp.floor(base / tf)
            t_hi = jnp.floor((base + cnt - 1.0) / tf)
            n_items = jnp.where(cnt > 0.0, t_hi - t_lo + 1.0, 0.0)
            i_start = _dot(etri_ref[...], n_items.astype(BF16))
            item = lax.broadcasted_iota(jnp.int32, (N_EXPERTS, PLAN_ITEMS), 1).astype(F32)
            sel = (i_start <= item) & (item < i_start + n_items)
            tile_e = t_lo + item - i_start
            off = base - tile_e * tf

            def pick(v):
                return jnp.sum(jnp.where(sel, v, 0.0), axis=0, keepdims=True).astype(jnp.int32)

            plan_ref[...] = jnp.zeros(plan_ref.shape, jnp.int32)
            plan_ref[0:1, :] = pick(e_id.astype(F32))
            plan_ref[1:2, :] = pick(tile_e)
            plan_ref[2:3, :] = pick(jnp.maximum(off, 0.0))
            plan_ref[3:4, :] = pick(jnp.minimum(off + cnt, tf))
            plan_ref[4:5, :] = jnp.sum(n_items, axis=0, keepdims=True).astype(jnp.int32)


def _route(lg, tris):
    expert_tri, token_tri = tris
    n_tiles, _, tm = lg.shape
    nb = ROUTE_BLOCKS
    assert n_tiles % nb == 0
    return pl.pallas_call(
        functools.partial(_route_kernel, n_tiles),
        grid=(2, n_tiles // nb),
        in_specs=[pl.BlockSpec((nb, ROUTE_ROWS, tm), lambda p, i: (i, 0, 0)),
                  _const_spec(expert_tri.shape), _const_spec(token_tri.shape)],
        out_specs=[pl.BlockSpec((nb, 8, tm), lambda p, i: (i * p, 0, 0)),
                   pl.BlockSpec((nb, 8, tm), lambda p, i: (i * p, 0, 0)),
                   pl.BlockSpec((8, PLAN_ITEMS), lambda p, i: (0, 0))],
        out_shape=[jax.ShapeDtypeStruct((n_tiles, 8, tm), jnp.int32),
                   jax.ShapeDtypeStruct((n_tiles, 8, tm), F32),
                   jax.ShapeDtypeStruct((8, PLAN_ITEMS), jnp.int32)],
        scratch_shapes=[pltpu.VMEM((n_tiles, 8, tm), F32),
                        pltpu.VMEM((N_EXPERTS, tm), F32),
                        pltpu.VMEM((N_EXPERTS, tm), F32),
                        pltpu.VMEM((N_EXPERTS, tm), F32)],
        compiler_params=_params(("arbitrary", "arbitrary")),
        name="route",
    )(lg, expert_tri, token_tri)


ROW_UNROLL = 8


DISPATCH_BLOCKS = 4


def _dispatch_kernel(tm, pos_ref, h_ref, xs_ref, sem):
    def row_copy(t, dst):
        return pltpu.make_async_copy(_slab(h_ref, t), _slab(xs_ref, dst), sem)

    for blk in range(DISPATCH_BLOCKS):
        def issue(g, carry, blk=blk):
            for u in range(ROW_UNROLL):
                t = g * ROW_UNROLL + u
                row_copy(blk * tm + t, pos_ref[blk, 0, t]).start()
                row_copy(blk * tm + t, pos_ref[blk, 1, t]).start()
            return carry

        lax.fori_loop(0, tm // ROW_UNROLL, issue, 0)

    def drain(g, carry):
        for _ in range(2 * ROW_UNROLL):
            row_copy(0, 0).wait()
        return carry

    lax.fori_loop(0, DISPATCH_BLOCKS * tm // ROW_UNROLL, drain, 0)


def _dispatch(pos, h2, n_rows):
    n_tiles, _, tm = pos.shape
    nb = DISPATCH_BLOCKS
    assert n_tiles % nb == 0
    return pl.pallas_call(
        functools.partial(_dispatch_kernel, tm),
        grid=(n_tiles // nb,),
        in_specs=[pl.BlockSpec((nb, 8, tm), lambda i: (i, 0, 0), memory_space=pltpu.SMEM),
                  pl.BlockSpec((nb * tm * SLABS, LANES), lambda i: (i, 0))],
        out_specs=pl.BlockSpec(memory_space=pl.ANY),
        out_shape=jax.ShapeDtypeStruct((n_rows * SLABS, LANES), F32),
        scratch_shapes=[pltpu.SemaphoreType.DMA],
        compiler_params=_params(("arbitrary",)),
        name="dispatch",
    )(pos, h2)


def _ffn_kernel(owner_ref, tile_ref, lo_ref, hi_ref, used_ref, x_ref, w1_ref, w3_ref, w2_ref, y_ref,
                w1b_ref, w3b_ref, w2b_ref):
    w = pl.program_id(0)
    prev = jnp.maximum(w - 1, 0)

    live = w < used_ref[0]

    @pl.when(live & ((w == 0) | (tile_ref[w] != tile_ref[prev])))
    def _():
        y_ref[...] = jnp.zeros(y_ref.shape, F32)

    @pl.when((w == 0) | (owner_ref[w] != owner_ref[prev]))
    def _():
        w1b_ref[...] = w1_ref[...].astype(BF16)
        w3b_ref[...] = w3_ref[...].astype(BF16)
        w2b_ref[...] = w2_ref[...].astype(BF16)

    @pl.when(live)
    def _():
        x = _load_rows(x_ref, FFN_TILE).astype(BF16)
        a = _dot(x, w1b_ref[...])
        b = _dot(x, w3b_ref[...])
        hid = (a * _sigmoid(a)) * b
        y = _dot(hid.astype(BF16), w2b_ref[...])
        row = lax.broadcasted_iota(jnp.int32, y.shape, 0)
        mine = (row >= lo_ref[w]) & (row < hi_ref[w])
        _store_rows(y_ref, jnp.where(mine, y, _load_rows(y_ref, FFN_TILE)))


def _ffn(plan, xs, w1, w3, w2):
    n_rows = xs.shape[0] // SLABS
    d = SLABS * LANES
    tf = FFN_TILE
    de = w1.shape[1] // N_EXPERTS
    n_items = n_rows // tf + N_EXPERTS - 1
    assert n_items <= PLAN_ITEMS

    def item(w, used):
        return jnp.minimum(w, used[0] - 1)

    grid_spec = pltpu.PrefetchScalarGridSpec(
        num_scalar_prefetch=5,
        grid=(n_items,),
        in_specs=[pl.BlockSpec((tf * SLABS, LANES), lambda w, own, til, lo, hi, used: (til[item(w, used)], 0)),
                  pl.BlockSpec((d, de), lambda w, own, til, lo, hi, used: (0, own[item(w, used)])),
                  pl.BlockSpec((d, de), lambda w, own, til, lo, hi, used: (0, own[item(w, used)])),
                  pl.BlockSpec((de, d), lambda w, own, til, lo, hi, used: (own[item(w, used)], 0))],
        out_specs=pl.BlockSpec((tf * SLABS, LANES), lambda w, own, til, lo, hi, used: (til[item(w, used)], 0)),
        scratch_shapes=[pltpu.VMEM((d, de), BF16), pltpu.VMEM((d, de), BF16), pltpu.VMEM((de, d), BF16)],
    )
    return pl.pallas_call(
        _ffn_kernel,
        grid_spec=grid_spec,
        out_shape=jax.ShapeDtypeStruct((n_rows * SLABS, LANES), F32),
        compiler_params=_params(("arbitrary",)),
        name="ffn",
    )(plan[0], plan[1], plan[2], plan[3], plan[4, 0:1], xs, w1, w3, w2)


def _combine_kernel(tm, ctx_steps, pos_ref, nxt_ref, w_ref, x1_ref, mod_ref, gp_ref, ys_ref, yc_ref, yl_ref,
                    buf_ref, sem):
    i = pl.program_id(0)
    n = pl.num_programs(0)
    slot = i % 2

    def row_copy(src, k, t, s):
        return pltpu.make_async_copy(_slab(ys_ref, src), _slab(buf_ref.at[s, k], t), sem.at[s])

    def issue(p_ref, s):
        def body(g, carry):
            for u in range(ROW_UNROLL):
                t = g * ROW_UNROLL + u
                row_copy(p_ref[0, t], 0, t, s).start()
                row_copy(p_ref[1, t], 1, t, s).start()
            return carry
        lax.fori_loop(0, tm // ROW_UNROLL, body, 0)

    @pl.when(i == 0)
    def _():
        issue(pos_ref, 0)

    @pl.when(i + 1 < n)
    def _():
        issue(nxt_ref, 1 - slot)

    def drain(g, carry):
        for _ in range(2 * ROW_UNROLL):
            row_copy(0, 0, 0, slot).wait()
        return carry

    lax.fori_loop(0, tm // ROW_UNROLL, drain, 0)

    r = lax.broadcasted_iota(jnp.int32, (tm, tm), 0)
    c = lax.broadcasted_iota(jnp.int32, (tm, tm), 1)
    d = x1_ref.shape[1]
    wa = jnp.broadcast_to(jnp.sum(jnp.where(r == c, w_ref[2:3, :], 0.0), axis=1, keepdims=True), (tm, d))
    wb = jnp.broadcast_to(jnp.sum(jnp.where(r == c, w_ref[3:4, :], 0.0), axis=1, keepdims=True), (tm, d))
    moe = wa * _load_rows(buf_ref, tm, slot, 0) + wb * _load_rows(buf_ref, tm, slot, 1)
    gt2 = mod_ref[5:6, :]
    y = x1_ref[...] + gt2 * _rms(moe, gp_ref[...])

    @pl.when(i < ctx_steps)
    def _():
        yc_ref[...] = y

    @pl.when(i >= ctx_steps)
    def _():
        yl_ref[...] = y


def _combine(st, pos, w, x1, mod, g_post2, ys):
    n_tiles, _, tm = pos.shape
    d = x1.shape[1]
    row = pl.BlockSpec((tm, d), lambda i: (i, 0))
    return pl.pallas_call(
        functools.partial(_combine_kernel, tm, st.ctx_steps(tm)),
        grid=(n_tiles,),
        in_specs=[pl.BlockSpec((None, 8, tm), lambda i: (i, 0, 0), memory_space=pltpu.SMEM),
                  pl.BlockSpec((None, 8, tm), lambda i: (jnp.minimum(i + 1, n_tiles - 1), 0, 0),
                               memory_space=pltpu.SMEM),
                  pl.BlockSpec((None, 8, tm), lambda i: (i, 0, 0)),
                  row,
                  st.mod_spec(tm, d),
                  _const_spec((1, d)),
                  pl.BlockSpec(memory_space=pl.ANY)],
        out_specs=[st.ctx_spec(tm, d), st.lat_spec(tm, d)],
        out_shape=[jax.ShapeDtypeStruct((st.n_ctx, d), F32), jax.ShapeDtypeStruct((st.n_lat, d), F32)],
        scratch_shapes=[pltpu.VMEM((2, 2, tm * SLABS, LANES), F32), pltpu.SemaphoreType.DMA((2,))],
        compiler_params=_params(("arbitrary",)),
        name="combine",
    )(pos, pos, w, x1, mod, g_post2.reshape(1, d), ys)


def _moe(st, h2, lg, x1, mod, g_post2, w1, w3, w2, route_tri):
    n_tok = x1.shape[0]
    n_rows = 2 * n_tok
    assert n_rows % FFN_TILE == 0
    pos, w, plan = _route(lg, route_tri)
    xs = _dispatch(pos, h2, n_rows)
    ys = _ffn(plan, xs, w1, w3, w2)
    return _combine(st, pos, w, x1, mod, g_post2, ys)


def _grid_pos(n_tokens, d_model):
    rows = n_tokens // GRID_W
    row = np.repeat(np.arange(rows, dtype=np.float64), GRID_W)
    col = np.tile(np.arange(GRID_W, dtype=np.float64), rows)
    n_freq = d_model // 4
    freq = np.exp(-math.log(POS_BASE) * np.arange(n_freq, dtype=np.float64) / n_freq)

    def enc(p):
        a = p[:, None] * freq[None, :]
        return np.concatenate([np.sin(a), np.cos(a)], axis=-1)

    return jnp.asarray(np.concatenate([enc(row), enc(col)], axis=-1), dtype=F32)


def _dft_cos_sin(n):
    k = np.arange(n, dtype=np.int64)
    ang = 2.0 * np.pi * ((k[:, None] * k[None, :]) % n).astype(np.float64) / n
    return np.cos(ang), np.sin(ang)


def _fnet_consts(seq_len):
    ct, st = _dft_cos_sin(seq_len)
    f_mat = jnp.asarray(np.concatenate([ct, -st], axis=1), dtype=F32).astype(BF16)
    cc, sc = _dft_cos_sin(FGROUP_DIM)
    cs_mat = jnp.asarray(np.stack([cc, sc]), dtype=F32).astype(BF16)
    return f_mat, cs_mat


def _tri_consts():
    i = np.arange(CHUNK)
    prefix = (i[:, None] <= i[None, :]).astype(np.float32)
    suffix = (i[:, None] >= i[None, :]).astype(np.float32)
    tri = jnp.asarray(np.stack([prefix, suffix]), dtype=BF16)
    e = np.arange(N_EXPERTS)
    expert_tri = jnp.asarray((e[None, :] < e[:, None]).astype(np.float32), dtype=BF16)
    t = np.arange(TOKEN_TILE)
    token_tri = jnp.asarray((t[:, None] < t[None, :]).astype(np.float32), dtype=BF16)
    return tri, (expert_tri, token_tri)


def _layer(x_prompt, x_sample, pos, mod, states, lw, consts):
    bp, sp, d = x_prompt.shape
    bs, ss, _ = x_sample.shape
    st = _Streams(n_ctx=bp * sp, n_lat=bs * ss, seq_lat=ss)
    (g_pre1, proj_w, g_hn, post_w, g_post1, g_pre2, w_e1, w_e3, w_e2, g_post2) = lw
    tri, route_tris = consts
    x_ctx = x_prompt.reshape(st.n_ctx, d)
    x_lat = x_sample.reshape(st.n_lat, d)
    q, kt, v, o, u, ga, gb, gr = _proj(st, x_ctx, x_lat, pos, mod, g_pre1, proj_w + (tri,))
    ha_ctx, new_c, new_n, new_m = _mlstm(q, kt, v, o, g_hn, gr, bp, sp, 0, None, MLSTM_HEADS_PER_STEP)
    (ha_lat,) = _mlstm(q, kt, v, o, g_hn, gr, bs, ss, st.n_ctx, states, MLSTM_HEADS_PER_STEP)
    yf_ctx = _fnet(u, *_fnet_consts(sp), bp, sp, 0)
    yf_lat = _fnet(u, *_fnet_consts(ss), bs, ss, st.n_ctx)
    x1, h2, logits = _post(st, x_ctx, x_lat, pos, ha_ctx, ha_lat, yf_ctx, yf_lat, ga, gb, mod, g_post1, g_pre2, post_w)
    y_ctx, y_lat = _moe(st, h2, logits, x1, mod, g_post2, w_e1, w_e3, w_e2, route_tris)
    return y_ctx.reshape(bp, sp, d), y_lat.reshape(bs, ss, d), (new_c, new_n, new_m)


def kernel(x_prompt, x_sample, c, state_C, state_n, state_m, c_ctx, w_ada, b_ada, g_pre1, w_in, b_gates, g_hn,
           w_a, w_f, w_out, g_post1, g_pre2, w_rg, b_rg, w_re, b_re, w_e1, w_e3, w_e2, g_post2):
    bp, sp, d = x_prompt.shape
    bs, ss, _ = x_sample.shape
    depth = w_in.shape[0]
    assert depth == 1
    l = 0
    dm = N_HEADS * HEAD_DIM
    du = N_FGROUPS * FGROUP_DIM
    ng = 4 * N_HEADS

    n_rows = 8
    assert 1 + bs <= n_rows
    cs = jnp.concatenate([c_ctx[None, :], c, jnp.zeros((n_rows - 1 - bs, d), F32)], axis=0)
    mod = _ada(cs, w_ada[l], b_ada[l]).reshape(n_rows, 6, d)

    wi = w_in[l]
    cuts = np.cumsum([dm] * 4 + [ng, du, d]).tolist()
    wq = wi[:, :cuts[0]].astype(BF16)
    wkt = wi[:, cuts[0]:cuts[1]].T.astype(BF16)
    wv = wi[:, cuts[1]:cuts[2]].astype(BF16)
    wo = wi[:, cuts[2]:cuts[3]].astype(BF16)
    wgt = wi[:, cuts[3]:cuts[4]].T
    wu = wi[:, cuts[4]:cuts[5]].astype(BF16)
    wga = wi[:, cuts[5]:cuts[6]].astype(BF16)
    wgb = wi[:, cuts[6]:].astype(BF16)
    bg = b_gates[l].reshape(ng, 1)
    proj_w = (wq, wkt, wv, wo, wu, wga, wgb, wgt, bg)
    n_pad = ROUTE_ROWS - N_EXPERTS - N_GROUPS
    wr = jnp.concatenate([w_re[l], w_rg[l], jnp.zeros((d, n_pad), F32)], axis=1).T
    br = jnp.concatenate([b_re[l], b_rg[l], jnp.zeros((n_pad,), F32)]).reshape(ROUTE_ROWS, 1)
    post_w = (w_a[l].astype(BF16), w_f[l].astype(BF16), w_out[l].astype(BF16), wr, br)
    lw = (g_pre1[l], proj_w, g_hn[l], post_w, g_post1[l], g_pre2[l], w_e1[l], w_e3[l], w_e2[l], g_post2[l])
    consts = _tri_consts()

    pos = _grid_pos(ss, d)
    y_prompt, y_sample, (new_c, new_n, new_m) = _layer(x_prompt, x_sample, pos, mod,
                                                       (state_C, state_n, state_m), lw, consts)

    new_state_c = new_c
    new_state_n = new_n.reshape(bp, depth, 2, N_HEADS, HEAD_DIM)
    new_state_m = new_m[..., 0, 0]
    return (y_prompt, y_sample, new_state_c, new_state_n, new_state_m)
```

```python
import functools
import math
from typing import NamedTuple

import numpy as np
import jax
import jax.numpy as jnp
from jax import lax
from jax.experimental import pallas as pl
from jax.experimental.pallas import tpu as pltpu

F32 = jnp.float32
BF16 = jnp.bfloat16

N_HEADS = 8
HEAD_DIM = 128
N_FGROUPS = 4
FGROUP_DIM = 128
N_GROUPS = 4
EXPERTS_PER_GROUP = 8
N_EXPERTS = N_GROUPS * EXPERTS_PER_GROUP
GRID_W = 64
POS_BASE = 10000.0
EPS = 1e-6

V7X_VMEM_LIMIT_BYTES = 56 * 1024 * 1024
LANES = 128
SUBLANES = 8
ADA_COL_TILE = 1024
FNET_ROW_TILE = 512
CHUNK = 256
TOKEN_TILE = CHUNK
CHUNKS_PER_STEP = 2
MLSTM_HEADS_PER_STEP = 4


def _params(sem):
    return pltpu.CompilerParams(dimension_semantics=sem, vmem_limit_bytes=V7X_VMEM_LIMIT_BYTES)


def _const_spec(shape):
    n = len(shape)
    return pl.BlockSpec(shape, lambda *_: (0,) * n, pipeline_mode=pl.Buffered(1))


def _split2(x):
    hi = x.astype(BF16)
    lo = (x - hi.astype(F32)).astype(BF16)
    return hi, lo


def _split3(x):
    hi = x.astype(BF16)
    r = x - hi.astype(F32)
    mid = r.astype(BF16)
    lo = (r - mid.astype(F32)).astype(BF16)
    return hi, mid, lo


def _dot(a, b):
    return jnp.dot(a, b, preferred_element_type=F32)


def _dot_nt(a, b):
    return lax.dot_general(a, b, (((1,), (1,)), ((), ())), preferred_element_type=F32)


def _dot_f32(a, b, nt=False):
    d = _dot_nt if nt else _dot
    a1, a2 = _split2(a)
    b1, b2 = _split2(b)
    return d(a1, b1) + (d(a1, b2) + d(a2, b1))


def _dot_exact_rhs(a, b_bf16, nt=False):
    d = _dot_nt if nt else _dot
    a1, a2, a3 = _split3(a)
    return d(a1, b_bf16) + (d(a2, b_bf16) + d(a3, b_bf16))


SLABS = SUBLANES


def _load_rows(ref, n, *lead):
    return jnp.concatenate([ref[(*lead, pl.ds(j, n, stride=SLABS), slice(None))] for j in range(SLABS)], axis=1)


def _store_rows(ref, val, *lead):
    n = val.shape[0]
    for j in range(SLABS):
        ref[(*lead, pl.ds(j, n, stride=SLABS), slice(None))] = val[:, j * LANES:(j + 1) * LANES]


def _slab(ref, row):
    return ref.at[pl.ds(pl.multiple_of(row * SLABS, SLABS), SLABS)]


def _rms(x, g):
    return x * lax.rsqrt(jnp.mean(x * x, axis=-1, keepdims=True) + EPS) * g


def _sigmoid(x):
    return 1.0 / (1.0 + jnp.exp(-x))


def _log_sigmoid(x):
    return jnp.minimum(x, 0.0) - jnp.log(1.0 + jnp.exp(-jnp.abs(x)))


def _ada_kernel(c_ref, w_ref, b_ref, o_ref):
    c = c_ref[...]
    s = (c * _sigmoid(c)).astype(BF16)
    o_ref[...] = _dot(s, w_ref[...].astype(BF16)) + b_ref[...]


def _ada(cs, w_ada, b_ada):
    rows, d = cs.shape
    n = w_ada.shape[1]
    tn = ADA_COL_TILE
    return pl.pallas_call(
        _ada_kernel,
        grid=(n // tn,),
        in_specs=[pl.BlockSpec((rows, d), lambda j: (0, 0)),
                  pl.BlockSpec((d, tn), lambda j: (0, j)),
                  pl.BlockSpec((1, tn), lambda j: (0, j))],
        out_specs=pl.BlockSpec((rows, tn), lambda j: (0, j)),
        out_shape=jax.ShapeDtypeStruct((rows, n), F32),
        compiler_params=_params(("arbitrary",)),
        name="ada",
    )(cs, w_ada, b_ada.reshape(1, n))


def _proj_kernel(ctx_steps, xc_ref, xl_ref, pos_ref, mod_ref, g_ref, wq_ref, wkt_ref, wv_ref, wo_ref, wu_ref,
                 wga_ref, wgb_ref, wgt_ref, bg_ref, tri_ref,
                 q_ref, kt_ref, v_ref, o_ref, u_ref, ga_ref, gb_ref, gr_ref):
    x = jnp.where(pl.program_id(0) < ctx_steps, xc_ref[...], xl_ref[...] + pos_ref[...])
    sh = mod_ref[0:1, :]
    sc = mod_ref[1:2, :]
    h = _rms(x, g_ref[...]) * (1.0 + sc) + sh
    hb = h.astype(BF16)
    q_ref[...] = _dot(hb, wq_ref[...]).astype(BF16)
    kt = (_dot_nt(wkt_ref[...], hb) * (HEAD_DIM ** -0.5)).astype(BF16)
    tc = TOKEN_TILE
    for s in range(CHUNKS_PER_STEP):
        kt_ref[s] = kt[:, s * tc:(s + 1) * tc]
    v_ref[...] = _dot(hb, wv_ref[...]).astype(BF16)
    o_ref[...] = _dot(hb, wo_ref[...])
    u_ref[...] = _dot(hb, wu_ref[...]).astype(BF16)
    ga_ref[...] = _dot(hb, wga_ref[...])
    gb_ref[...] = _dot(hb, wgb_ref[...])
    g = _dot_f32(wgt_ref[...], h, nt=True) + bg_ref[...]
    nh = N_HEADS
    lf_f = _log_sigmoid(g[nh:2 * nh])
    lf_b = _log_sigmoid(g[3 * nh:4 * nh])
    for s in range(CHUNKS_PER_STEP):
        cols = slice(s * tc, (s + 1) * tc)
        gr_ref[s, 0:nh, :] = g[0:nh, cols]
        gr_ref[s, nh:2 * nh, :] = _dot_exact_rhs(lf_f[:, cols], tri_ref[0])
        gr_ref[s, 2 * nh:3 * nh, :] = g[2 * nh:3 * nh, cols]
        gr_ref[s, 3 * nh:4 * nh, :] = _dot_exact_rhs(lf_b[:, cols], tri_ref[1])


class _Streams(NamedTuple):
    n_ctx: int
    n_lat: int
    seq_lat: int

    def ctx_steps(self, tm):
        assert self.n_ctx % tm == 0 and self.seq_lat % tm == 0
        return self.n_ctx // tm

    def ctx_spec(self, tm, width):
        c = self.ctx_steps(tm)
        return pl.BlockSpec((tm, width), lambda i, *_: (jnp.minimum(i, c - 1), 0))

    def lat_spec(self, tm, width):
        c = self.ctx_steps(tm)
        return pl.BlockSpec((tm, width), lambda i, *_: (jnp.maximum(i - c, 0), 0))

    def pos_spec(self, tm, width):
        c = self.ctx_steps(tm)
        per_seq = self.seq_lat // tm
        return pl.BlockSpec((tm, width), lambda i, *_: (jnp.maximum(i - c, 0) % per_seq, 0))

    def mod_spec(self, tm, width):
        c = self.ctx_steps(tm)
        per_seq = self.seq_lat // tm
        return pl.BlockSpec((None, 6, width),
                            lambda i, *_: (jnp.where(i < c, 0, 1 + jnp.maximum(i - c, 0) // per_seq), 0, 0))


def _proj(st, x_ctx, x_lat, pos, mod, g_pre1, wts):
    d = x_ctx.shape[1]
    n_tok = st.n_ctx + st.n_lat
    tc = TOKEN_TILE
    ns = CHUNKS_PER_STEP
    tm = ns * tc
    n_tiles = n_tok // tc
    wu = wts[4]
    row = pl.BlockSpec((tm, d), lambda i: (i, 0))
    in_specs = [st.ctx_spec(tm, d), st.lat_spec(tm, d), st.pos_spec(tm, d), st.mod_spec(tm, d), _const_spec((1, d))]
    in_specs += [_const_spec(w.shape) for w in wts]
    args = [x_ctx, x_lat, pos, mod, g_pre1.reshape(1, d)] + list(wts)
    n_gate_rows = 4 * N_HEADS
    du = wu.shape[1]
    out_shape = [jax.ShapeDtypeStruct((n_tok, d), BF16),
                 jax.ShapeDtypeStruct((n_tiles, d, tc), BF16),
                 jax.ShapeDtypeStruct((n_tok, d), BF16),
                 jax.ShapeDtypeStruct((n_tok, d), F32),
                 jax.ShapeDtypeStruct((n_tok, du), BF16),
                 jax.ShapeDtypeStruct((n_tok, d), F32),
                 jax.ShapeDtypeStruct((n_tok, d), F32),
                 jax.ShapeDtypeStruct((n_tiles, n_gate_rows, tc), F32)]
    out_specs = [row,
                 pl.BlockSpec((ns, d, tc), lambda i: (i, 0, 0)),
                 row, row,
                 pl.BlockSpec((tm, du), lambda i: (i, 0)),
                 row, row,
                 pl.BlockSpec((ns, n_gate_rows, tc), lambda i: (i, 0, 0))]
    return pl.pallas_call(
        functools.partial(_proj_kernel, st.ctx_steps(tm)),
        grid=(n_tok // tm,),
        in_specs=in_specs,
        out_specs=out_specs,
        out_shape=out_shape,
        compiler_params=_params(("parallel",)),
        name="proj",
    )(*args)


def _mlstm_kernel(n_chunks, hp, has_state, emit_state, *refs):
    it = iter(refs)
    if has_state:
        m0_ref = next(it)
    q_ref, kt_ref, v_ref, o_ref, ghn_ref, gr_ref = (next(it) for _ in range(6))
    if has_state:
        c0_ref, n0_ref = next(it), next(it)
    ha_ref = next(it)
    if emit_state:
        cout_ref, nout_ref, mout_ref = next(it), next(it), next(it)
    hs_ref, cn_ref, m_ref = (next(it) for _ in range(3))

    L = CHUNK
    nh, dh = N_HEADS, HEAD_DIM
    head0 = pl.program_id(1) * hp
    neg_inf = -jnp.inf

    sq_r = lax.broadcasted_iota(jnp.int32, (dh, dh), 0)
    sq_c = lax.broadcasted_iota(jnp.int32, (dh, dh), 1)
    for j in range(hp):
        for d in range(2):
            if has_state:
                n_col = jnp.sum(jnp.where(sq_r == sq_c, n0_ref[d, j], 0.0), axis=1, keepdims=True)
                cn_ref[j, d, :, 0:dh] = c0_ref[d, j]
                cn_ref[j, d, :, dh:2 * dh] = jnp.broadcast_to(n_col, (dh, dh))
                m0 = m0_ref[pl.program_id(0) * (2 * nh) + d * nh + head0 + j]
                m_ref[j, d] = jnp.full((8, LANES), m0, F32)
            else:
                cn_ref[j, d] = jnp.zeros((dh, 2 * dh), F32)
                m_ref[j, d] = jnp.zeros((8, LANES), F32)

    row_id = lax.broadcasted_iota(jnp.int32, (L, L), 0)
    col_id = lax.broadcasted_iota(jnp.int32, (L, L), 1)
    ones_blk = jnp.ones((L, dh), BF16)

    def chunk(j, d, c):
        t0 = pl.multiple_of(c * L, L)
        lanes = slice(j * dh, (j + 1) * dh)
        qc = q_ref[pl.ds(t0, L), lanes]
        ktc = kt_ref[c, lanes, :]
        vc = v_ref[pl.ds(t0, L), lanes]
        ig_row = gr_ref[c, pl.ds(2 * nh * d + head0 + j, 1), :]
        b_row = gr_ref[c, pl.ds(2 * nh * d + nh + head0 + j, 1), :]
        m_prev = m_ref[j, d][0:1, 0:1]
        if d == 0:
            b_end = b_row[:, L - 1:L]
            mask = col_id <= row_id
        else:
            b_end = b_row[:, 0:1]
            mask = col_id >= row_id
        a_row = ig_row - b_row
        a_max = jnp.max(jnp.where(mask, a_row, neg_inf), axis=1, keepdims=True)
        g = jnp.maximum(jnp.broadcast_to(a_max, (L, dh)), m_prev)
        b_col = jnp.broadcast_to(
            jnp.sum(jnp.where(row_id == col_id, b_row, 0.0), axis=1, keepdims=True), (L, dh))
        g_full = jnp.concatenate([g] * (L // dh), axis=1)
        s = _dot(qc, ktc) * jnp.exp(jnp.where(mask, a_row - g_full, neg_inf))
        decay = jnp.exp(m_prev - g)
        qcn = _dot(qc, cn_ref[j, d].astype(BF16))
        num = decay * qcn[:, 0:dh] + _dot(s.astype(BF16), vc)
        den = decay * qcn[:, dh:2 * dh] + jnp.sum(s, axis=1, keepdims=True)
        hch = num / jnp.maximum(jnp.abs(den), jnp.exp(-(b_col + g)))
        w_log = b_end + a_row
        m_new = jnp.maximum(b_end + m_prev, jnp.max(w_log, axis=1, keepdims=True))
        kw = (ktc.astype(F32) * jnp.exp(w_log - m_new)).astype(BF16)
        v_aug = jnp.concatenate([vc, ones_blk], axis=1)
        cn_ref[j, d] = jnp.exp(b_end + m_prev - m_new) * cn_ref[j, d] + _dot(kw, v_aug)
        m_ref[j, d] = jnp.broadcast_to(m_new, (8, LANES))
        return t0, hch

    if n_chunks == 1:
        for j in range(hp):
            _, h_f = chunk(j, 0, 0)
            _, h_b = chunk(j, 1, 0)
            hs_ref[:, j * dh:(j + 1) * dh] = h_f + h_b
    else:
        half = n_chunks // 2

        def first(i, carry):
            for j in range(hp):
                lanes = slice(j * dh, (j + 1) * dh)
                t_f, h_f = chunk(j, 0, i)
                t_b, h_b = chunk(j, 1, n_chunks - 1 - i)
                hs_ref[pl.ds(t_f, L), lanes] = h_f
                hs_ref[pl.ds(t_b, L), lanes] = h_b
            return carry

        def second(i, carry):
            for j in range(hp):
                lanes = slice(j * dh, (j + 1) * dh)
                t_f, h_f = chunk(j, 0, i)
                t_b, h_b = chunk(j, 1, n_chunks - 1 - i)
                hs_ref[pl.ds(t_f, L), lanes] += h_f
                hs_ref[pl.ds(t_b, L), lanes] += h_b
            return carry

        lax.fori_loop(0, half, first, 0)
        lax.fori_loop(half, n_chunks, second, 0)

    def finish(c, carry):
        t0 = pl.multiple_of(c * L, L)
        for j in range(hp):
            lanes = slice(j * dh, (j + 1) * dh)
            ha = hs_ref[pl.ds(t0, L), lanes]
            ha = ha * lax.rsqrt(jnp.mean(ha * ha, axis=-1, keepdims=True) + EPS)
            ha = ha * ghn_ref[:, lanes] * _sigmoid(o_ref[pl.ds(t0, L), lanes])
            ha_ref[pl.ds(t0, L), lanes] = ha.astype(BF16)
        return carry

    if n_chunks == 1:
        finish(0, 0)
    else:
        lax.fori_loop(0, n_chunks, finish, 0)

    if emit_state:
        for j in range(hp):
            for d in range(2):
                cout_ref[d, j] = cn_ref[j, d, :, 0:dh]
                n_rep = cn_ref[j, d, :, dh:2 * dh]
                nout_ref[d, j] = jnp.sum(jnp.where(sq_r == sq_c, n_rep, 0.0), axis=0, keepdims=True)
                mout_ref[d, j] = m_ref[j, d][0:1, :]


def _mlstm(q, kt, v, o, g_hn, gr, n_seq, seq_len, tok_off, states, hp):
    d = q.shape[1]
    n_tok = n_seq * seq_len
    L = CHUNK
    assert tok_off % seq_len == 0
    sb = tok_off // seq_len
    n_chunks = seq_len // L
    has_state = states is not None
    emit_state = not has_state
    nh, dh = N_HEADS, HEAD_DIM
    assert n_chunks == 1 or n_chunks % 2 == 0
    assert nh % hp == 0
    wd = hp * dh

    in_specs = []
    args = []
    if has_state:
        state_c, state_n, state_m = states
        in_specs.append(pl.BlockSpec(memory_space=pltpu.SMEM))
        args.append(state_m.reshape(-1))
    tok = pl.BlockSpec((seq_len, wd), lambda b, h: (b + sb, h))
    in_specs += [tok,
                 pl.BlockSpec((n_chunks, wd, L), lambda b, h: (b + sb, h, 0)),
                 tok, tok,
                 pl.BlockSpec((1, wd), lambda b, h: (0, h)),
                 pl.BlockSpec((n_chunks, 4 * nh, L), lambda b, h: (b + sb, 0, 0))]
    args += [q, kt, v, o, g_hn.reshape(1, d), gr]
    st_c = pl.BlockSpec((None, None, 2, hp, dh, dh), lambda b, h: (b, 0, 0, h, 0, 0))
    st_v = pl.BlockSpec((None, None, 2, hp, 1, dh), lambda b, h: (b, 0, 0, h, 0, 0))
    if has_state:
        in_specs += [st_c, st_v]
        args += [state_c, state_n.reshape(n_seq, 1, 2, nh, 1, dh)]
    out_shape = [jax.ShapeDtypeStruct((n_tok, d), BF16)]
    out_specs = [pl.BlockSpec((seq_len, wd), lambda b, h: (b, h))]
    if emit_state:
        out_shape += [jax.ShapeDtypeStruct((n_seq, 1, 2, nh, dh, dh), F32),
                      jax.ShapeDtypeStruct((n_seq, 1, 2, nh, 1, dh), F32),
                      jax.ShapeDtypeStruct((n_seq, 1, 2, nh, 1, LANES), F32)]
        out_specs += [st_c, st_v, st_v]
    scratch = [pltpu.VMEM((seq_len, wd), F32),
               pltpu.VMEM((hp, 2, dh, 2 * dh), F32),
               pltpu.VMEM((hp, 2, 8, LANES), F32)]
    return pl.pallas_call(
        functools.partial(_mlstm_kernel, n_chunks, hp, has_state, emit_state),
        grid=(n_seq, nh // hp),
        in_specs=in_specs,
        out_specs=out_specs,
        out_shape=out_shape,
        scratch_shapes=scratch,
        compiler_params=_params(("parallel", "parallel")),
        name="mlstm",
    )(*args)


def _fnet_kernel(seq_len, u_ref, f_ref, cs_ref, y_ref, ab_ref):
    T = seq_len

    @pl.when(pl.program_id(1) == 0)
    def _():
        for g in range(N_FGROUPS):
            ug = u_ref[:, g * FGROUP_DIM:(g + 1) * FGROUP_DIM]
            ab_ref[0:T, g * FGROUP_DIM:(g + 1) * FGROUP_DIM] = _dot(ug, cs_ref[0]).astype(BF16)
            ab_ref[T:2 * T, g * FGROUP_DIM:(g + 1) * FGROUP_DIM] = _dot(ug, cs_ref[1]).astype(BF16)

    scale = 1.0 / math.sqrt(T * FGROUP_DIM)
    y_ref[...] = (_dot(f_ref[...], ab_ref[...]) * scale).astype(BF16)


def _fnet(u, f_mat, cs_mat, n_seq, seq_len, tok_off):
    du = u.shape[1]
    n_tok = n_seq * seq_len
    tr = min(seq_len, FNET_ROW_TILE)
    assert tok_off % seq_len == 0
    sb = tok_off // seq_len
    return pl.pallas_call(
        functools.partial(_fnet_kernel, seq_len),
        grid=(n_seq, seq_len // tr),
        in_specs=[pl.BlockSpec((seq_len, du), lambda b, r: (b + sb, 0)),
                  pl.BlockSpec((tr, 2 * seq_len), lambda b, r: (r, 0)),
                  _const_spec(cs_mat.shape)],
        out_specs=pl.BlockSpec((tr, du), lambda b, r: (b * (seq_len // tr) + r, 0)),
        out_shape=jax.ShapeDtypeStruct((n_tok, du), BF16),
        scratch_shapes=[pltpu.VMEM((2 * seq_len, du), BF16)],
        compiler_params=_params(("parallel", "arbitrary")),
        name="fnet",
    )(u, f_mat, cs_mat)


def _post_kernel(ctx_steps, xc_ref, xl_ref, pos_ref, hac_ref, hal_ref, yfc_ref, yfl_ref, ga_ref, gb_ref, mod_ref,
                 gp1_ref, gp2_ref, wa_ref, wf_ref, wout_ref, wr_ref, br_ref, x1_ref, h2_ref, lg_ref):
    is_ctx = pl.program_id(0) < ctx_steps
    x = jnp.where(is_ctx, xc_ref[...], xl_ref[...] + pos_ref[...])
    ya = _dot(jnp.where(is_ctx, hac_ref[...], hal_ref[...]), wa_ref[...])
    yf = _dot(jnp.where(is_ctx, yfc_ref[...], yfl_ref[...]), wf_ref[...])
    mix_in = _sigmoid(ga_ref[...]) * ya + _sigmoid(gb_ref[...]) * yf
    mix = _dot(mix_in.astype(BF16), wout_ref[...])
    gt1 = mod_ref[2:3, :]
    sh2 = mod_ref[3:4, :]
    sc2 = mod_ref[4:5, :]
    x1 = x + gt1 * _rms(mix, gp1_ref[...])
    h2 = _rms(x1, gp2_ref[...]) * (1.0 + sc2) + sh2
    x1_ref[...] = x1
    _store_rows(h2_ref, h2)
    lg = _dot_f32(wr_ref[...], h2, nt=True) + br_ref[...]
    for s in range(CHUNKS_PER_STEP):
        lg_ref[s] = lg[:, s * TOKEN_TILE:(s + 1) * TOKEN_TILE]


def _post(st, x_ctx, x_lat, pos, ha_ctx, ha_lat, yf_ctx, yf_lat, ga, gb, mod, g_post1, g_pre2, wts):
    d = x_ctx.shape[1]
    n_tok = st.n_ctx + st.n_lat
    tc = TOKEN_TILE
    ns = CHUNKS_PER_STEP
    tm = ns * tc
    du = yf_ctx.shape[1]
    row = pl.BlockSpec((tm, d), lambda i: (i, 0))
    in_specs = [st.ctx_spec(tm, d), st.lat_spec(tm, d), st.pos_spec(tm, d),
                st.ctx_spec(tm, d), st.lat_spec(tm, d), st.ctx_spec(tm, du), st.lat_spec(tm, du),
                row, row, st.mod_spec(tm, d), _const_spec((1, d)), _const_spec((1, d))]
    in_specs += [_const_spec(w.shape) for w in wts]
    args = [x_ctx, x_lat, pos, ha_ctx, ha_lat, yf_ctx, yf_lat, ga, gb, mod,
            g_post1.reshape(1, d), g_pre2.reshape(1, d)] + list(wts)
    return pl.pallas_call(
        functools.partial(_post_kernel, st.ctx_steps(tm)),
        grid=(n_tok // tm,),
        in_specs=in_specs,
        out_specs=[row, pl.BlockSpec((tm * SLABS, LANES), lambda i: (i, 0)),
                   pl.BlockSpec((ns, ROUTE_ROWS, tc), lambda i: (i, 0, 0))],
        out_shape=[jax.ShapeDtypeStruct((n_tok, d), F32),
                   jax.ShapeDtypeStruct((n_tok * SLABS, LANES), F32),
                   jax.ShapeDtypeStruct((n_tok // tc, ROUTE_ROWS, tc), F32)],
        compiler_params=_params(("parallel",)),
        name="post",
    )(*args)


ROUTE_ROWS = N_EXPERTS + 8
ROUTE_BLOCKS = 4
FFN_TILE = 512
PLAN_ITEMS = LANES


def _route_rows(lg):
    tm = lg.shape[1]
    ne = N_EXPERTS
    gl = lg[ne:ne + 8]
    g_id = lax.broadcasted_iota(jnp.int32, (8, tm), 0).astype(F32)
    is_g = g_id < N_GROUPS
    gl = jnp.where(is_g, gl, -jnp.inf)
    mx = jnp.max(gl, axis=0, keepdims=True)
    z = jnp.sum(jnp.where(is_g, jnp.exp(gl - mx), 0.0), axis=0, keepdims=True)
    p_sel = 1.0 / z
    g_sel = jnp.min(jnp.where(gl == mx, g_id, float(N_GROUPS)), axis=0, keepdims=True)
    e_id = lax.broadcasted_iota(jnp.int32, (ne, tm), 0).astype(F32)
    lo = EXPERTS_PER_GROUP * g_sel
    in_grp = (e_id >= lo) & (e_id < lo + EXPERTS_PER_GROUP)
    le = jnp.where(in_grp, lg[0:ne], -jnp.inf)
    v1 = jnp.max(le, axis=0, keepdims=True)
    i1 = jnp.min(jnp.where(le == v1, e_id, float(ne)), axis=0, keepdims=True)
    le2 = jnp.where(e_id == i1, -jnp.inf, le)
    v2 = jnp.max(le2, axis=0, keepdims=True)
    i2 = jnp.min(jnp.where(le2 == v2, e_id, float(ne)), axis=0, keepdims=True)
    e2 = jnp.exp(v2 - v1)
    w1 = p_sel / (1.0 + e2)
    w2 = p_sel * e2 / (1.0 + e2)
    return i1, i2, w1, w2


def _route_kernel(n_tiles, lg_ref, etri_ref, ttri_ref, pos_ref, w_ref, plan_ref, meta_ref, cnt_ref, run_ref, base_ref):
    p = pl.program_id(0)
    i = pl.program_id(1)
    tm = lg_ref.shape[2]
    sub = lax.broadcasted_iota(jnp.int32, (N_EXPERTS, tm), 0).astype(F32)
    tf = float(FFN_TILE)

    nb = ROUTE_BLOCKS

    @pl.when(p == 0)
    def _():
        @pl.when(i == 0)
        def _():
            cnt_ref[...] = jnp.zeros_like(cnt_ref)

        count = jnp.zeros((N_EXPERTS, 1), F32)
        for s in range(nb):
            t = i * nb + s
            e_a, e_b, w_a, w_b = _route_rows(lg_ref[s])
            meta_ref[t, 0:1, :] = e_a
            meta_ref[t, 1:2, :] = e_b
            meta_ref[t, 2:3, :] = w_a
            meta_ref[t, 3:4, :] = w_b
            meta_ref[t, 4:8, :] = jnp.zeros((4, tm), F32)
            oh = jnp.where((sub == e_a) | (sub == e_b), 1.0, 0.0)
            count = count + jnp.sum(oh, axis=1, keepdims=True)
        cnt_ref[...] += jnp.broadcast_to(count, cnt_ref.shape)

    @pl.when(p == 1)
    def _():
        @pl.when(i == 0)
        def _():
            c1, c2, c3 = _split3(cnt_ref[...])
            base_ref[...] = _dot(etri_ref[...], c1) + (_dot(etri_ref[...], c2) + _dot(etri_ref[...], c3))
            run_ref[...] = jnp.zeros_like(run_ref)

        start = run_ref[...] + base_ref[...]
        for s in range(nb):
            t = i * nb + s
            e_a = meta_ref[t, 0:1, :]
            e_b = meta_ref[t, 1:2, :]
            oh_a = jnp.where(sub == e_a, 1.0, 0.0)
            oh_b = jnp.where(sub == e_b, 1.0, 0.0)
            oh = oh_a + oh_b
            before = _dot(oh.astype(BF16), ttri_ref[...]) + start
            pos_ref[s] = jnp.zeros(pos_ref.shape[1:], jnp.int32)
            pos_ref[s, 0:1, :] = jnp.sum(oh_a * before, axis=0, keepdims=True).astype(jnp.int32)
            pos_ref[s, 1:2, :] = jnp.sum(oh_b * before, axis=0, keepdims=True).astype(jnp.int32)
            w_ref[s] = meta_ref[t]
            start = start + jnp.broadcast_to(jnp.sum(oh, axis=1, keepdims=True), start.shape)
        run_ref[...] = start - base_ref[...]

        @pl.when(i == n_tiles // nb - 1)
        def _():
            cnt = cnt_ref[:, 0:PLAN_ITEMS]
            base = base_ref[:, 0:PLAN_ITEMS]
            e_id = lax.broadcasted_iota(jnp.int32, (N_EXPERTS, PLAN_ITEMS), 0)
            t_lo = jnp.floor(base / tf)
            t_hi = jnp.floor((base + cnt - 1.0) / tf)
            n_items = jnp.where(cnt > 0.0, t_hi - t_lo + 1.0, 0.0)
            i_start = _dot(etri_ref[...], n_items.astype(BF16))
            item = lax.broadcasted_iota(jnp.int32, (N_EXPERTS, PLAN_ITEMS), 1).astype(F32)
            sel = (i_start <= item) & (item < i_start + n_items)
            tile_e = t_lo + item - i_start
            off = base - tile_e * tf

            def pick(v):
                return jnp.sum(jnp.where(sel, v, 0.0), axis=0, keepdims=True).astype(jnp.int32)

            plan_ref[...] = jnp.zeros(plan_ref.shape, jnp.int32)
            plan_ref[0:1, :] = pick(e_id.astype(F32))
            plan_ref[1:2, :] = pick(tile_e)
            plan_ref[2:3, :] = pick(jnp.maximum(off, 0.0))
            plan_ref[3:4, :] = pick(jnp.minimum(off + cnt, tf))
            plan_ref[4:5, :] = jnp.sum(n_items, axis=0, keepdims=True).astype(jnp.int32)


def _route(lg, tris):
    expert_tri, token_tri = tris
    n_tiles, _, tm = lg.shape
    nb = ROUTE_BLOCKS
    assert n_tiles % nb == 0
    return pl.pallas_call(
        functools.partial(_route_kernel, n_tiles),
        grid=(2, n_tiles // nb),
        in_specs=[pl.BlockSpec((nb, ROUTE_ROWS, tm), lambda p, i: (i, 0, 0)),
                  _const_spec(expert_tri.shape), _const_spec(token_tri.shape)],
        out_specs=[pl.BlockSpec((nb, 8, tm), lambda p, i: (i * p, 0, 0)),
                   pl.BlockSpec((nb, 8, tm), lambda p, i: (i * p, 0, 0)),
                   pl.BlockSpec((8, PLAN_ITEMS), lambda p, i: (0, 0))],
        out_shape=[jax.ShapeDtypeStruct((n_tiles, 8, tm), jnp.int32),
                   jax.ShapeDtypeStruct((n_tiles, 8, tm), F32),
                   jax.ShapeDtypeStruct((8, PLAN_ITEMS), jnp.int32)],
        scratch_shapes=[pltpu.VMEM((n_tiles, 8, tm), F32),
                        pltpu.VMEM((N_EXPERTS, tm), F32),
                        pltpu.VMEM((N_EXPERTS, tm), F32),
                        pltpu.VMEM((N_EXPERTS, tm), F32)],
        compiler_params=_params(("arbitrary", "arbitrary")),
        name="route",
    )(lg, expert_tri, token_tri)


ROW_UNROLL = 8


DISPATCH_BLOCKS = 4


def _dispatch_kernel(tm, pos_ref, h_ref, xs_ref, sem):
    def row_copy(t, dst):
        return pltpu.make_async_copy(_slab(h_ref, t), _slab(xs_ref, dst), sem)

    for blk in range(DISPATCH_BLOCKS):
        def issue(g, carry, blk=blk):
            for u in range(ROW_UNROLL):
                t = g * ROW_UNROLL + u
                row_copy(blk * tm + t, pos_ref[blk, 0, t]).start()
                row_copy(blk * tm + t, pos_ref[blk, 1, t]).start()
            return carry

        lax.fori_loop(0, tm // ROW_UNROLL, issue, 0)

    def drain(g, carry):
        for _ in range(2 * ROW_UNROLL):
            row_copy(0, 0).wait()
        return carry

    lax.fori_loop(0, DISPATCH_BLOCKS * tm // ROW_UNROLL, drain, 0)


def _dispatch(pos, h2, n_rows):
    n_tiles, _, tm = pos.shape
    nb = DISPATCH_BLOCKS
    assert n_tiles % nb == 0
    return pl.pallas_call(
        functools.partial(_dispatch_kernel, tm),
        grid=(n_tiles // nb,),
        in_specs=[pl.BlockSpec((nb, 8, tm), lambda i: (i, 0, 0), memory_space=pltpu.SMEM),
                  pl.BlockSpec((nb * tm * SLABS, LANES), lambda i: (i, 0))],
        out_specs=pl.BlockSpec(memory_space=pl.ANY),
        out_shape=jax.ShapeDtypeStruct((n_rows * SLABS, LANES), F32),
        scratch_shapes=[pltpu.SemaphoreType.DMA],
        compiler_params=_params(("arbitrary",)),
        name="dispatch",
    )(pos, h2)


def _ffn_kernel(owner_ref, tile_ref, lo_ref, hi_ref, used_ref, x_ref, w1_ref, w3_ref, w2_ref, y_ref,
                w1b_ref, w3b_ref, w2b_ref):
    w = pl.program_id(0)
    prev = jnp.maximum(w - 1, 0)

    live = w < used_ref[0]

    @pl.when(live & ((w == 0) | (tile_ref[w] != tile_ref[prev])))
    def _():
        y_ref[...] = jnp.zeros(y_ref.shape, F32)

    @pl.when((w == 0) | (owner_ref[w] != owner_ref[prev]))
    def _():
        w1b_ref[...] = w1_ref[...].astype(BF16)
        w3b_ref[...] = w3_ref[...].astype(BF16)
        w2b_ref[...] = w2_ref[...].astype(BF16)

    @pl.when(live)
    def _():
        x = _load_rows(x_ref, FFN_TILE).astype(BF16)
        a = _dot(x, w1b_ref[...])
        b = _dot(x, w3b_ref[...])
        hid = (a * _sigmoid(a)) * b
        y = _dot(hid.astype(BF16), w2b_ref[...])
        row = lax.broadcasted_iota(jnp.int32, y.shape, 0)
        mine = (row >= lo_ref[w]) & (row < hi_ref[w])
        _store_rows(y_ref, jnp.where(mine, y, _load_rows(y_ref, FFN_TILE)))


def _ffn(plan, xs, w1, w3, w2):
    n_rows = xs.shape[0] // SLABS
    d = SLABS * LANES
    tf = FFN_TILE
    de = w1.shape[1] // N_EXPERTS
    n_items = n_rows // tf + N_EXPERTS - 1
    assert n_items <= PLAN_ITEMS

    def item(w, used):
        return jnp.minimum(w, used[0] - 1)

    grid_spec = pltpu.PrefetchScalarGridSpec(
        num_scalar_prefetch=5,
        grid=(n_items,),
        in_specs=[pl.BlockSpec((tf * SLABS, LANES), lambda w, own, til, lo, hi, used: (til[item(w, used)], 0)),
                  pl.BlockSpec((d, de), lambda w, own, til, lo, hi, used: (0, own[item(w, used)])),
                  pl.BlockSpec((d, de), lambda w, own, til, lo, hi, used: (0, own[item(w, used)])),
                  pl.BlockSpec((de, d), lambda w, own, til, lo, hi, used: (own[item(w, used)], 0))],
        out_specs=pl.BlockSpec((tf * SLABS, LANES), lambda w, own, til, lo, hi, used: (til[item(w, used)], 0)),
        scratch_shapes=[pltpu.VMEM((d, de), BF16), pltpu.VMEM((d, de), BF16), pltpu.VMEM((de, d), BF16)],
    )
    return pl.pallas_call(
        _ffn_kernel,
        grid_spec=grid_spec,
        out_shape=jax.ShapeDtypeStruct((n_rows * SLABS, LANES), F32),
        compiler_params=_params(("arbitrary",)),
        name="ffn",
    )(plan[0], plan[1], plan[2], plan[3], plan[4, 0:1], xs, w1, w3, w2)


def _combine_kernel(tm, ctx_steps, pos_ref, nxt_ref, w_ref, x1_ref, mod_ref, gp_ref, ys_ref, yc_ref, yl_ref,
                    buf_ref, sem):
    i = pl.program_id(0)
    n = pl.num_programs(0)
    slot = i % 2

    def row_copy(src, k, t, s):
        return pltpu.make_async_copy(_slab(ys_ref, src), _slab(buf_ref.at[s, k], t), sem.at[s])

    def issue(p_ref, s):
        def body(g, carry):
            for u in range(ROW_UNROLL):
                t = g * ROW_UNROLL + u
                row_copy(p_ref[0, t], 0, t, s).start()
                row_copy(p_ref[1, t], 1, t, s).start()
            return carry
        lax.fori_loop(0, tm // ROW_UNROLL, body, 0)

    @pl.when(i == 0)
    def _():
        issue(pos_ref, 0)

    @pl.when(i + 1 < n)
    def _():
        issue(nxt_ref, 1 - slot)

    def drain(g, carry):
        for _ in range(2 * ROW_UNROLL):
            row_copy(0, 0, 0, slot).wait()
        return carry

    lax.fori_loop(0, tm // ROW_UNROLL, drain, 0)

    r = lax.broadcasted_iota(jnp.int32, (tm, tm), 0)
    c = lax.broadcasted_iota(jnp.int32, (tm, tm), 1)
    d = x1_ref.shape[1]
    wa = jnp.broadcast_to(jnp.sum(jnp.where(r == c, w_ref[2:3, :], 0.0), axis=1, keepdims=True), (tm, d))
    wb = jnp.broadcast_to(jnp.sum(jnp.where(r == c, w_ref[3:4, :], 0.0), axis=1, keepdims=True), (tm, d))
    moe = wa * _load_rows(buf_ref, tm, slot, 0) + wb * _load_rows(buf_ref, tm, slot, 1)
    gt2 = mod_ref[5:6, :]
    y = x1_ref[...] + gt2 * _rms(moe, gp_ref[...])

    @pl.when(i < ctx_steps)
    def _():
        yc_ref[...] = y

    @pl.when(i >= ctx_steps)
    def _():
        yl_ref[...] = y


def _combine(st, pos, w, x1, mod, g_post2, ys):
    n_tiles, _, tm = pos.shape
    d = x1.shape[1]
    row = pl.BlockSpec((tm, d), lambda i: (i, 0))
    return pl.pallas_call(
        functools.partial(_combine_kernel, tm, st.ctx_steps(tm)),
        grid=(n_tiles,),
        in_specs=[pl.BlockSpec((None, 8, tm), lambda i: (i, 0, 0), memory_space=pltpu.SMEM),
                  pl.BlockSpec((None, 8, tm), lambda i: (jnp.minimum(i + 1, n_tiles - 1), 0, 0),
                               memory_space=pltpu.SMEM),
                  pl.BlockSpec((None, 8, tm), lambda i: (i, 0, 0)),
                  row,
                  st.mod_spec(tm, d),
                  _const_spec((1, d)),
                  pl.BlockSpec(memory_space=pl.ANY)],
        out_specs=[st.ctx_spec(tm, d), st.lat_spec(tm, d)],
        out_shape=[jax.ShapeDtypeStruct((st.n_ctx, d), F32), jax.ShapeDtypeStruct((st.n_lat, d), F32)],
        scratch_shapes=[pltpu.VMEM((2, 2, tm * SLABS, LANES), F32), pltpu.SemaphoreType.DMA((2,))],
        compiler_params=_params(("arbitrary",)),
        name="combine",
    )(pos, pos, w, x1, mod, g_post2.reshape(1, d), ys)


def _moe(st, h2, lg, x1, mod, g_post2, w1, w3, w2, route_tri):
    n_tok = x1.shape[0]
    n_rows = 2 * n_tok
    assert n_rows % FFN_TILE == 0
    pos, w, plan = _route(lg, route_tri)
    xs = _dispatch(pos, h2, n_rows)
    ys = _ffn(plan, xs, w1, w3, w2)
    return _combine(st, pos, w, x1, mod, g_post2, ys)


def _grid_pos(n_tokens, d_model):
    rows = n_tokens // GRID_W
    row = np.repeat(np.arange(rows, dtype=np.float64), GRID_W)
    col = np.tile(np.arange(GRID_W, dtype=np.float64), rows)
    n_freq = d_model // 4
    freq = np.exp(-math.log(POS_BASE) * np.arange(n_freq, dtype=np.float64) / n_freq)

    def enc(p):
        a = p[:, None] * freq[None, :]
        return np.concatenate([np.sin(a), np.cos(a)], axis=-1)

    return jnp.asarray(np.concatenate([enc(row), enc(col)], axis=-1), dtype=F32)


def _dft_cos_sin(n):
    k = np.arange(n, dtype=np.int64)
    ang = 2.0 * np.pi * ((k[:, None] * k[None, :]) % n).astype(np.float64) / n
    return np.cos(ang), np.sin(ang)


def _fnet_consts(seq_len):
    ct, st = _dft_cos_sin(seq_len)
    f_mat = jnp.asarray(np.concatenate([ct, -st], axis=1), dtype=F32).astype(BF16)
    cc, sc = _dft_cos_sin(FGROUP_DIM)
    cs_mat = jnp.asarray(np.stack([cc, sc]), dtype=F32).astype(BF16)
    return f_mat, cs_mat


def _tri_consts():
    i = np.arange(CHUNK)
    prefix = (i[:, None] <= i[None, :]).astype(np.float32)
    suffix = (i[:, None] >= i[None, :]).astype(np.float32)
    tri = jnp.asarray(np.stack([prefix, suffix]), dtype=BF16)
    e = np.arange(N_EXPERTS)
    expert_tri = jnp.asarray((e[None, :] < e[:, None]).astype(np.float32), dtype=BF16)
    t = np.arange(TOKEN_TILE)
    token_tri = jnp.asarray((t[:, None] < t[None, :]).astype(np.float32), dtype=BF16)
    return tri, (expert_tri, token_tri)


def _layer(x_prompt, x_sample, pos, mod, states, lw, consts):
    bp, sp, d = x_prompt.shape
    bs, ss, _ = x_sample.shape
    st = _Streams(n_ctx=bp * sp, n_lat=bs * ss, seq_lat=ss)
    (g_pre1, proj_w, g_hn, post_w, g_post1, g_pre2, w_e1, w_e3, w_e2, g_post2) = lw
    tri, route_tris = consts
    x_ctx = x_prompt.reshape(st.n_ctx, d)
    x_lat = x_sample.reshape(st.n_lat, d)
    q, kt, v, o, u, ga, gb, gr = _proj(st, x_ctx, x_lat, pos, mod, g_pre1, proj_w + (tri,))
    ha_ctx, new_c, new_n, new_m = _mlstm(q, kt, v, o, g_hn, gr, bp, sp, 0, None, MLSTM_HEADS_PER_STEP)
    (ha_lat,) = _mlstm(q, kt, v, o, g_hn, gr, bs, ss, st.n_ctx, states, MLSTM_HEADS_PER_STEP)
    yf_ctx = _fnet(u, *_fnet_consts(sp), bp, sp, 0)
    yf_lat = _fnet(u, *_fnet_consts(ss), bs, ss, st.n_ctx)
    x1, h2, logits = _post(st, x_ctx, x_lat, pos, ha_ctx, ha_lat, yf_ctx, yf_lat, ga, gb, mod, g_post1, g_pre2, post_w)
    y_ctx, y_lat = _moe(st, h2, logits, x1, mod, g_post2, w_e1, w_e3, w_e2, route_tris)
    return y_ctx.reshape(bp, sp, d), y_lat.reshape(bs, ss, d), (new_c, new_n, new_m)


def kernel(x_prompt, x_sample, c, state_C, state_n, state_m, c_ctx, w_ada, b_ada, g_pre1, w_in, b_gates, g_hn,
           w_a, w_f, w_out, g_post1, g_pre2, w_rg, b_rg, w_re, b_re, w_e1, w_e3, w_e2, g_post2):
    bp, _, d = x_prompt.shape
    bs, ss, _ = x_sample.shape
    depth = w_in.shape[0]
    assert depth == 1
    l = 0
    dm = N_HEADS * HEAD_DIM
    du = N_FGROUPS * FGROUP_DIM
    ng = 4 * N_HEADS

    n_rows = SUBLANES
    assert 1 + bs <= n_rows
    cs = jnp.concatenate([c_ctx[None, :], c, jnp.zeros((n_rows - 1 - bs, d), F32)], axis=0)
    mod = _ada(cs, w_ada[l], b_ada[l]).reshape(n_rows, 6, d)

    wi = w_in[l]
    cuts = np.cumsum([dm] * 4 + [ng, du, d]).tolist()
    wq = wi[:, :cuts[0]].astype(BF16)
    wkt = wi[:, cuts[0]:cuts[1]].T.astype(BF16)
    wv = wi[:, cuts[1]:cuts[2]].astype(BF16)
    wo = wi[:, cuts[2]:cuts[3]].astype(BF16)
    wgt = wi[:, cuts[3]:cuts[4]].T
    wu = wi[:, cuts[4]:cuts[5]].astype(BF16)
    wga = wi[:, cuts[5]:cuts[6]].astype(BF16)
    wgb = wi[:, cuts[6]:].astype(BF16)
    bg = b_gates[l].reshape(ng, 1)
    proj_w = (wq, wkt, wv, wo, wu, wga, wgb, wgt, bg)
    n_pad = ROUTE_ROWS - N_EXPERTS - N_GROUPS
    wr = jnp.concatenate([w_re[l], w_rg[l], jnp.zeros((d, n_pad), F32)], axis=1).T
    br = jnp.concatenate([b_re[l], b_rg[l], jnp.zeros((n_pad,), F32)]).reshape(ROUTE_ROWS, 1)
    post_w = (w_a[l].astype(BF16), w_f[l].astype(BF16), w_out[l].astype(BF16), wr, br)
    lw = (g_pre1[l], proj_w, g_hn[l], post_w, g_post1[l], g_pre2[l], w_e1[l], w_e3[l], w_e2[l], g_post2[l])
    consts = _tri_consts()

    pos = _grid_pos(ss, d)
    y_prompt, y_sample, (new_c, new_n, new_m) = _layer(x_prompt, x_sample, pos, mod,
                                                       (state_C, state_n, state_m), lw, consts)

    new_state_c = new_c
    new_state_n = new_n.reshape(bp, depth, 2, N_HEADS, HEAD_DIM)
    new_state_m = new_m[..., 0, 0]
    return (y_prompt, y_sample, new_state_c, new_state_n, new_state_m)
```
